```python
import jax
import jax.numpy as jnp
from jax import lax
import numpy as np

D_MODEL = 1024
BATCH = 4
SEQ = 4096
DEPTH = 1
DEC_BATCH = 32
DEC_SEQ = 16
PAST_LEN = 1024

CHUNK = 64
EPS = 1e-6
MLA_HEADS = 16
QK_NOPE = 64
QK_ROPE = 32
V_HEAD = 64
Q_LORA = 384
KV_LORA = 256
ROPE_THETA = 10000.0
MLA_SCALE = (QK_NOPE + QK_ROPE) ** -0.5
Q_BLOCK = 128
GM_CHUNK = 128
GM_GROUPS = 8
GM_WIDTH = 1024
GM_GROUP_DIM = GM_WIDTH // GM_GROUPS
N_MEM = 256
MEM_HEADS = 4
MEM_HEAD_DIM = 128
MEM_SCALE = MEM_HEAD_DIM ** -0.5
PEER_HEADS = 8
N_KEYS = 128
N_EXPERTS = N_KEYS * N_KEYS
PEER_TOPK = 16
PEER_QDIM = 256
PEER_HALF = PEER_QDIM // 2
PEER_BLOCK = 128
IN_TOTAL = Q_LORA + KV_LORA + QK_ROPE + 2 * GM_WIDTH + 2 * D_MODEL

kernel_name = 'hybrid_mla_gmlp_peer_stream_step'


def rmsnorm(x, g):
    xf = x.astype(jnp.float32)
    y = xf * lax.rsqrt(jnp.mean(xf * xf, axis=-1, keepdims=True) + EPS)
    return (y * g.astype(jnp.float32)).astype(x.dtype)


def rope(x, pos):
    d = x.shape[-1]
    half = d // 2
    inv = ROPE_THETA ** (-jnp.arange(half, dtype=jnp.float32) / half)
    ang = pos.astype(jnp.float32)[:, None] * inv[None, :]
    ang = ang.reshape((1, pos.shape[0]) + (1,) * (x.ndim - 3) + (half,))
    cos, sin = jnp.cos(ang), jnp.sin(ang)
    xf = x.astype(jnp.float32)
    x1, x2 = xf[..., :half], xf[..., half:]
    return jnp.concatenate([x1 * cos - x2 * sin, x1 * sin + x2 * cos], axis=-1).astype(x.dtype)


def mla_scores(q_nope, q_rope, k_nope, k_rope):
    s = jnp.einsum('bqhd,bkhd->bhqk', q_nope, k_nope) + jnp.einsum('bqhd,bkd->bhqk', q_rope, k_rope)
    return s.astype(jnp.float32) * MLA_SCALE


def mla_attend_prompt(q_nope, q_rope, k_nope, k_rope, v):
    B, S, H, _ = q_nope.shape
    nb = S // Q_BLOCK
    qn = q_nope.reshape(B, nb, Q_BLOCK, H, QK_NOPE).transpose(1, 0, 2, 3, 4)
    qr = q_rope.reshape(B, nb, Q_BLOCK, H, QK_ROPE).transpose(1, 0, 2, 3, 4)
    key_chunk = jnp.arange(S) // CHUNK

    def block(args):
        qn_b, qr_b, bi = args
        s = mla_scores(qn_b, qr_b, k_nope, k_rope)
        q_chunk = (bi * Q_BLOCK + jnp.arange(Q_BLOCK)) // CHUNK
        mask = key_chunk[None, :] <= q_chunk[:, None]
        s = jnp.where(mask[None, None], s, jnp.float32(-1e30))
        pr = jax.nn.softmax(s, axis=-1).astype(v.dtype)
        return jnp.einsum('bhqk,bkhd->bqhd', pr, v)

    out = lax.map(block, (qn, qr, jnp.arange(nb)))
    return out.transpose(1, 0, 2, 3, 4).reshape(B, S, H, V_HEAD)


def mla_attend_all(q_nope, q_rope, k_nope, k_rope, v):
    pr = jax.nn.softmax(mla_scores(q_nope, q_rope, k_nope, k_rope), axis=-1).astype(v.dtype)
    return jnp.einsum('bhqk,bkhd->bqhd', pr, v)


def mixer_block(x, p, pos, past):
    B, S, _ = x.shape
    h = rmsnorm(x, p['g_mix'])
    z = h @ p['w_in']
    splits = [Q_LORA, Q_LORA + KV_LORA, Q_LORA + KV_LORA + QK_ROPE,
              Q_LORA + KV_LORA + QK_ROPE + GM_WIDTH,
              Q_LORA + KV_LORA + QK_ROPE + 2 * GM_WIDTH,
              Q_LORA + KV_LORA + QK_ROPE + 2 * GM_WIDTH + D_MODEL]
    c_q, c_kv, k_r, z_u, z_v, z_ga, z_gb = jnp.split(z, splits, axis=-1)
    q = (rmsnorm(c_q, p['g_q_lat']) @ p['w_uq']).reshape(B, S, MLA_HEADS, QK_NOPE + QK_ROPE)
    q_nope = rmsnorm(q[..., :QK_NOPE], p['g_qn'])
    q_rope = rope(rmsnorm(q[..., QK_NOPE:], p['g_qr']), pos)
    c_kv = rmsnorm(c_kv, p['g_kv_lat'])
    k_r = rope(rmsnorm(k_r, p['g_kr']), pos)
    if past is None:
        ckv_all, kr_all = c_kv, k_r
    else:
        ckv_all = jnp.concatenate([past[0], c_kv], axis=1)
        kr_all = jnp.concatenate([past[1], k_r], axis=1)
    Sk = ckv_all.shape[1]
    k_nope = rmsnorm((ckv_all @ p['w_uk']).reshape(B, Sk, MLA_HEADS, QK_NOPE), p['g_kn'])
    v = (ckv_all @ p['w_uv']).reshape(B, Sk, MLA_HEADS, V_HEAD)
    if past is None:
        o = mla_attend_prompt(q_nope, q_rope, k_nope, kr_all, v)
    else:
        o = mla_attend_all(q_nope, q_rope, k_nope, kr_all, v)
    o_a = o.reshape(B, S, MLA_HEADS * V_HEAD) @ p['w_oa']
    u = jax.nn.gelu(z_u, approximate=False)
    v_g = rmsnorm(jax.nn.gelu(z_v, approximate=False), p['g_gm'])
    L = min(S, GM_CHUNK)
    C = S // L
    w_mask = jnp.tril(p['w_s'][:, :L, :L])
    v5 = v_g.reshape(B, C, L, GM_GROUPS, GM_GROUP_DIM)
    mixed = jnp.einsum('gts,bcsgd->bctgd', w_mask, v5) + p['b_s'][:, :L].T[None, None, :, :, None]
    o_b = (u * mixed.reshape(B, S, GM_WIDTH)) @ p['w_ob']
    y = (jax.nn.sigmoid(z_ga) * o_a + jax.nn.sigmoid(z_gb) * o_b) @ p['w_o']
    return x + y, c_kv, k_r, v_g


def memory_kv(mem, g_mem, w_ck, g_ck, w_cv):
    B, N, _ = mem.shape
    m = rmsnorm(mem, g_mem)
    k = rmsnorm((m @ w_ck).reshape(B, N, MEM_HEADS, MEM_HEAD_DIM), g_ck)
    v = (m @ w_cv).reshape(B, N, MEM_HEADS, MEM_HEAD_DIM)
    return k, v


def cross_attend(h, w_cq, g_cq, mem_k, mem_v, w_co):
    B, S, _ = h.shape
    q = rmsnorm((h @ w_cq).reshape(B, S, MEM_HEADS, MEM_HEAD_DIM), g_cq)
    s = jnp.einsum('bqhd,bkhd->bhqk', q, mem_k).astype(jnp.float32) * MEM_SCALE
    pr = jax.nn.softmax(s, axis=-1).astype(mem_v.dtype)
    o = jnp.einsum('bhqk,bkhd->bqhd', pr, mem_v).reshape(B, S, MEM_HEADS * MEM_HEAD_DIM)
    return o @ w_co


def peer_ffn(h, w_pq, sub_keys, peer_u, peer_v):
    B, S, D = h.shape
    T = B * S
    pad = (-T) % PEER_BLOCK
    xb = jnp.pad(h.reshape(T, D), ((0, pad), (0, 0))).reshape(-1, PEER_BLOCK, D)

    def block(xt):
        q = (xt @ w_pq).reshape(PEER_BLOCK, PEER_HEADS, 2, PEER_HALF)
        s = jnp.einsum('thpd,hpnd->thpn', q, sub_keys).astype(jnp.float32)
        sv, si = lax.top_k(s, PEER_TOPK)
        cand = sv[:, :, 0, :, None] + sv[:, :, 1, None, :]
        cidx = si[:, :, 0, :, None] * N_KEYS + si[:, :, 1, None, :]
        cand = cand.reshape(PEER_BLOCK, PEER_HEADS, PEER_TOPK * PEER_TOPK)
        cidx = cidx.reshape(PEER_BLOCK, PEER_HEADS, PEER_TOPK * PEER_TOPK)
        top_s, sel = lax.top_k(cand, PEER_TOPK)
        eidx = jnp.take_along_axis(cidx, sel, axis=-1)
        g = jax.nn.softmax(top_s, axis=-1).astype(xt.dtype)
        act = jax.nn.gelu(jnp.einsum('thkd,td->thk', peer_u[eidx], xt), approximate=False)
        return jnp.einsum('thk,thkd->td', g * act, peer_v[eidx])

    out = lax.map(block, xb).reshape(-1, D)[:T]
    return out.reshape(B, S, D)


def setup_inputs(seed: int = 0) -> dict:
    key = jax.random.key(seed)
    ks = iter(jax.random.split(key, 64))
    f32 = jnp.float32
    L = DEPTH

    def nrm(shape, scale):
        return jax.random.normal(next(ks), shape, f32) * scale

    def gain(n):
        return 1.0 + 0.02 * jax.random.normal(next(ks), (L, n), f32)

    return {
        'x_prompt': nrm((BATCH, SEQ, D_MODEL), 1.0),
        'x_sample': nrm((DEC_BATCH, DEC_SEQ, D_MODEL), 1.0),
        'cache_mla_ckv': nrm((L, DEC_BATCH, PAST_LEN, KV_LORA), 1.0),
        'cache_mla_krope': nrm((L, DEC_BATCH, PAST_LEN, QK_ROPE), 1.0),
        'cache_mem_k': nrm((L, DEC_BATCH, N_MEM, MEM_HEADS, MEM_HEAD_DIM), 1.0),
        'cache_mem_v': nrm((L, DEC_BATCH, N_MEM, MEM_HEADS, MEM_HEAD_DIM), 1.0),
        'mem_prompt': nrm((BATCH, N_MEM, D_MODEL), 1.0),
        'g_mix': gain(D_MODEL),
        'w_in': nrm((L, D_MODEL, IN_TOTAL), D_MODEL ** -0.5),
        'g_q_lat': gain(Q_LORA),
        'w_uq': nrm((L, Q_LORA, MLA_HEADS * (QK_NOPE + QK_ROPE)), Q_LORA ** -0.5),
        'g_qn': gain(QK_NOPE),
        'g_qr': gain(QK_ROPE),
        'g_kv_lat': gain(KV_LORA),
        'g_kr': gain(QK_ROPE),
        'w_uk': nrm((L, KV_LORA, MLA_HEADS * QK_NOPE), KV_LORA ** -0.5),
        'w_uv': nrm((L, KV_LORA, MLA_HEADS * V_HEAD), KV_LORA ** -0.5),
        'g_kn': gain(QK_NOPE),
        'w_oa': nrm((L, MLA_HEADS * V_HEAD, D_MODEL), (MLA_HEADS * V_HEAD) ** -0.5),
        'g_gm': gain(GM_WIDTH),
        'w_s': nrm((L, GM_GROUPS, GM_CHUNK, GM_CHUNK), 0.5 * GM_CHUNK ** -0.5),
        'b_s': 1.0 + 0.01 * jax.random.normal(next(ks), (L, GM_GROUPS, GM_CHUNK), f32),
        'w_ob': nrm((L, GM_WIDTH, D_MODEL), GM_WIDTH ** -0.5),
        'w_o': nrm((L, D_MODEL, D_MODEL), D_MODEL ** -0.5),
        'g_xattn': gain(D_MODEL),
        'g_mem': gain(D_MODEL),
        'w_cq': nrm((L, D_MODEL, MEM_HEADS * MEM_HEAD_DIM), D_MODEL ** -0.5),
        'g_cq': gain(MEM_HEAD_DIM),
        'w_ck': nrm((L, D_MODEL, MEM_HEADS * MEM_HEAD_DIM), D_MODEL ** -0.5),
        'g_ck': gain(MEM_HEAD_DIM),
        'w_cv': nrm((L, D_MODEL, MEM_HEADS * MEM_HEAD_DIM), D_MODEL ** -0.5),
        'w_co': nrm((L, MEM_HEADS * MEM_HEAD_DIM, D_MODEL), (MEM_HEADS * MEM_HEAD_DIM) ** -0.5),
        'g_ffn': gain(D_MODEL),
        'w_pq': nrm((L, D_MODEL, PEER_HEADS * PEER_QDIM), D_MODEL ** -0.5),
        'sub_keys': nrm((L, PEER_HEADS, 2, N_KEYS, PEER_HALF), PEER_HALF ** -0.5),
        'peer_u': nrm((L, N_EXPERTS, D_MODEL), D_MODEL ** -0.5),
        'peer_v': nrm((L, N_EXPERTS, D_MODEL), D_MODEL ** -0.5),
    }


def reference(x_prompt, x_sample, cache_mla_ckv, cache_mla_krope, cache_mem_k, cache_mem_v, mem_prompt,
              g_mix, w_in, g_q_lat, w_uq, g_qn, g_qr, g_kv_lat, g_kr, w_uk, w_uv, g_kn, w_oa,
              g_gm, w_s, b_s, w_ob, w_o,
              g_xattn, g_mem, w_cq, g_cq, w_ck, g_ck, w_cv, w_co,
              g_ffn, w_pq, sub_keys, peer_u, peer_v):
    S_p = x_prompt.shape[1]
    S_s = x_sample.shape[1]
    past_len = cache_mla_ckv.shape[2]
    pos_p = jnp.arange(S_p)
    pos_s = past_len + jnp.arange(S_s)
    xp, xs = x_prompt, x_sample
    ckv_p_l, kr_p_l, mk_p_l, mv_p_l, ckv_s_l, kr_s_l, vg_s_l = [], [], [], [], [], [], []
    for i in range(DEPTH):
        p = {'g_mix': g_mix[i], 'w_in': w_in[i], 'g_q_lat': g_q_lat[i], 'w_uq': w_uq[i],
             'g_qn': g_qn[i], 'g_qr': g_qr[i], 'g_kv_lat': g_kv_lat[i], 'g_kr': g_kr[i],
             'w_uk': w_uk[i], 'w_uv': w_uv[i], 'g_kn': g_kn[i], 'w_oa': w_oa[i],
             'g_gm': g_gm[i], 'w_s': w_s[i], 'b_s': b_s[i], 'w_ob': w_ob[i], 'w_o': w_o[i]}
        xp, ckv_p, kr_p, _ = mixer_block(xp, p, pos_p, None)
        xs, ckv_s, kr_s, vg_s = mixer_block(xs, p, pos_s, (cache_mla_ckv[i], cache_mla_krope[i]))
        mk_p, mv_p = memory_kv(mem_prompt, g_mem[i], w_ck[i], g_ck[i], w_cv[i])
        xp = xp + cross_attend(rmsnorm(xp, g_xattn[i]), w_cq[i], g_cq[i], mk_p, mv_p, w_co[i])
        xs = xs + cross_attend(rmsnorm(xs, g_xattn[i]), w_cq[i], g_cq[i], cache_mem_k[i], cache_mem_v[i], w_co[i])
        xp = xp + peer_ffn(rmsnorm(xp, g_ffn[i]), w_pq[i], sub_keys[i], peer_u[i], peer_v[i])
        xs = xs + peer_ffn(rmsnorm(xs, g_ffn[i]), w_pq[i], sub_keys[i], peer_u[i], peer_v[i])
        ckv_p_l.append(ckv_p)
        kr_p_l.append(kr_p)
        mk_p_l.append(mk_p)
        mv_p_l.append(mv_p)
        ckv_s_l.append(ckv_s)
        kr_s_l.append(kr_s)
        vg_s_l.append(vg_s)
    return (xp, xs, jnp.stack(ckv_p_l), jnp.stack(kr_p_l), jnp.stack(mk_p_l), jnp.stack(mv_p_l),
            jnp.stack(ckv_s_l), jnp.stack(kr_s_l), jnp.stack(vg_s_l))
```

```python
import functools

import jax
import jax.numpy as jnp
import numpy as np
from jax import lax
from jax.experimental import pallas as pl
from jax.experimental.pallas import tpu as pltpu

CHUNK = 64
EPS = 1e-6
MLA_HEADS = 16
QK_NOPE = 64
QK_ROPE = 32
V_HEAD = 64
ROPE_THETA = 10000.0
MLA_SCALE = (QK_NOPE + QK_ROPE) ** -0.5
GM_CHUNK = 128
GM_GROUPS = 8
MEM_HEADS = 4
MEM_HEAD_DIM = 128
MEM_SCALE = MEM_HEAD_DIM ** -0.5
PEER_HEADS = 8
N_KEYS = 128
PEER_TOPK = 16
PEER_HALF = 128

LANES = 128
SLOT = LANES
ROPE_LO = QK_NOPE
ROPE_HALF = QK_ROPE // 2
ROW_TILE = 512
ATTN_TILE = 256
PEER_ROWS = 512
PEER_EXPERTS = 512
MASK_PITCH = N_KEYS + 8
VMEM_LIMIT = 48 * 1024 * 1024
PEER_VMEM_LIMIT = 60 * 1024 * 1024

F32 = jnp.float32
BF16 = jnp.bfloat16
_NT = (((1,), (1,)), ((), ()))


def _dot(a, b):
    return jnp.dot(a, b, preferred_element_type=F32)


def _dot_nt(a, b):
    return lax.dot_general(a, b, _NT, preferred_element_type=F32)


def _rms(xf, g):
    return xf * lax.rsqrt(jnp.mean(xf * xf, axis=-1, keepdims=True) + EPS) * g


def _gelu(x):
    return 0.5 * x * (1.0 + lax.erf(x * np.float32(np.sqrt(0.5))))


def _group_mean(sq, m_ref):
    hi = sq.astype(BF16)
    lo = (sq - hi.astype(F32)).astype(BF16)
    m = m_ref[...]
    return _dot(hi, m) + _dot(lo, m)


def _rope_slot(y, cos, sin_up, sin_dn):
    return (y * cos + pltpu.roll(y, ROPE_HALF, 1) * sin_up
            + pltpu.roll(y, SLOT - ROPE_HALF, 1) * sin_dn)


def _const_spec(shape):
    nd = len(shape)
    return pl.BlockSpec(shape, lambda *_: (0,) * nd, pipeline_mode=pl.Buffered(1))


def _row_spec(tm, width):
    return pl.BlockSpec((tm, width), lambda i: (i, 0))


def _params(sem, limit=VMEM_LIMIT):
    return pltpu.CompilerParams(dimension_semantics=sem, vmem_limit_bytes=limit)


def _in_proj_kernel(x_ref, gmix_ref, wcq_ref, wckv_ref, wkr_ref, wu_ref, wv_ref, wga_ref, wgb_ref,
                    gq_ref, gkv_ref, gkr_ref, ggm_ref, cos_ref, sup_ref, sdn_ref,
                    cq_out, ckv_out, kr_out, u_out, vg_out, siga_out, sigb_out):
    h = _rms(x_ref[...], gmix_ref[...]).astype(BF16)
    cq_out[...] = _rms(_dot(h, wcq_ref[...]), gq_ref[...]).astype(BF16)
    ckv_out[...] = _rms(_dot(h, wckv_ref[...]), gkv_ref[...])
    kr = _dot(h, wkr_ref[...])
    ms = jnp.sum(kr * kr, axis=-1, keepdims=True) * (1.0 / QK_ROPE)
    kr = kr * lax.rsqrt(ms + EPS) * gkr_ref[...]
    kr_out[...] = _rope_slot(kr, cos_ref[...], sup_ref[...], sdn_ref[...])
    u_out[...] = _gelu(_dot(h, wu_ref[...])).astype(BF16)
    vg_out[...] = _rms(_gelu(_dot(h, wv_ref[...])), ggm_ref[...])
    siga_out[...] = jax.nn.sigmoid(_dot(h, wga_ref[...])).astype(BF16)
    sigb_out[...] = jax.nn.sigmoid(_dot(h, wgb_ref[...])).astype(BF16)


def _in_proj(x2d, w, tabs, tm):
    t, d = x2d.shape
    cos, sup, sdn = tabs
    ntab = cos.shape[0] // tm
    tab_spec = pl.BlockSpec((tm, SLOT), lambda i: (i % ntab, 0))
    consts = [w['g_mix'], w['w_cq'], w['w_ckv'], w['w_kr'], w['w_u'], w['w_v'], w['w_ga'], w['w_gb'],
              w['g_q_lat'], w['g_kv_lat'], w['g_kr'], w['g_gm']]
    gw = w['w_u'].shape[1]
    widths = [w['w_cq'].shape[1], w['w_ckv'].shape[1], SLOT, gw, gw, d, d]
    dtypes = [BF16, F32, F32, BF16, F32, BF16, BF16]
    return pl.pallas_call(
        _in_proj_kernel,
        grid=(t // tm,),
        in_specs=[_row_spec(tm, d)] + [_const_spec(c.shape) for c in consts] + [tab_spec] * 3,
        out_specs=[_row_spec(tm, n) for n in widths],
        out_shape=[jax.ShapeDtypeStruct((t, n), dt) for n, dt in zip(widths, dtypes)],
        compiler_params=_params(("parallel",)),
        name="in_proj",
    )(x2d, *consts, cos, sup, sdn)


def _gmlp_kernel(u_ref, vg_ref, sigb_ref, mix_ref, bias_ref, wob_ref, out_ref):
    vgb = vg_ref[...].astype(BF16)
    parts = []
    for g in range(GM_GROUPS):
        sl = slice(g * LANES, (g + 1) * LANES)
        mixed = _dot(mix_ref[g], vgb[:, sl]) + bias_ref[:, sl]
        parts.append((u_ref[:, sl].astype(F32) * mixed).astype(BF16))
    ob = _dot(jnp.concatenate(parts, axis=1), wob_ref[...])
    out_ref[...] = sigb_ref[...].astype(F32) * ob


def _gmlp(u, vg, sigb, mix, bias, w_ob, tm):
    t, gw = u.shape
    d = w_ob.shape[1]
    return pl.pallas_call(
        _gmlp_kernel,
        grid=(t // tm,),
        in_specs=[_row_spec(tm, gw), _row_spec(tm, gw), _row_spec(tm, d),
                  _const_spec(mix.shape), _const_spec(bias.shape), _const_spec(w_ob.shape)],
        out_specs=_row_spec(tm, d),
        out_shape=jax.ShapeDtypeStruct((t, d), F32),
        compiler_params=_params(("parallel",)),
        name="gmlp",
    )(u, vg, sigb, mix, bias, w_ob)


def _q_proj_kernel(cq_ref, wuq_ref, gq_ref, mavg_ref, cos_ref, sup_ref, sdn_ref, q_out):
    q = _dot(cq_ref[...], wuq_ref[...])
    cos, sup, sdn = cos_ref[...], sup_ref[...], sdn_ref[...]
    for h in range(MLA_HEADS):
        sl = slice(h * SLOT, (h + 1) * SLOT)
        qs = q[:, sl]
        y = qs * lax.rsqrt(_group_mean(qs * qs, mavg_ref) + EPS) * gq_ref[...]
        q_out[:, sl] = _rope_slot(y, cos, sup, sdn).astype(BF16)


def _q_proj(cq, w, tabs, tm):
    t, ql = cq.shape
    cos, sup, sdn = tabs
    ntab = cos.shape[0] // tm
    tab_spec = pl.BlockSpec((tm, SLOT), lambda i: (i % ntab, 0))
    n = MLA_HEADS * SLOT
    return pl.pallas_call(
        _q_proj_kernel,
        grid=(t // tm,),
        in_specs=[_row_spec(tm, ql), _const_spec(w['w_uq'].shape), _const_spec(w['g_q'].shape),
                  _const_spec(w['m_avg'].shape)] + [tab_spec] * 3,
        out_specs=_row_spec(tm, n),
        out_shape=jax.ShapeDtypeStruct((t, n), BF16),
        compiler_params=_params(("parallel",)),
        name="q_proj",
    )(cq, w['w_uq'], w['g_q'], w['m_avg'], cos, sup, sdn)


def _kv_proj_kernel(ckv_ref, kr_ref, wuk_ref, wuv_ref, gkn_ref, mavg_ref, k_out, v_out):
    c = ckv_ref[...].astype(BF16)
    kn = _dot(c, wuk_ref[...])
    kr = kr_ref[...]
    for h in range(MLA_HEADS):
        sl = slice(h * SLOT, (h + 1) * SLOT)
        ks = kn[:, sl]
        y = ks * lax.rsqrt(_group_mean(ks * ks, mavg_ref) + EPS) * gkn_ref[...]
        k_out[:, sl] = (y + kr).astype(BF16)
    v_out[...] = _dot(c, wuv_ref[...]).astype(BF16)


def _kv_proj(ckv, kr, w, tm):
    t, kl = ckv.shape
    n = MLA_HEADS * SLOT
    return pl.pallas_call(
        _kv_proj_kernel,
        grid=(t // tm,),
        in_specs=[_row_spec(tm, kl), _row_spec(tm, SLOT), _const_spec(w['w_uk'].shape),
                  _const_spec(w['w_uv'].shape), _const_spec(w['g_kn'].shape),
                  _const_spec(w['m_avg'].shape)],
        out_specs=[_row_spec(tm, n), _row_spec(tm, n)],
        out_shape=[jax.ShapeDtypeStruct((t, n), BF16)] * 2,
        compiler_params=_params(("parallel",)),
        name="kv_proj",
    )(ckv, kr, w['w_uk'], w['w_uv'], w['g_kn'], w['m_avg'])


def _attn_prompt_kernel(q_ref, k_ref, v_ref, o_ref):
    tile = q_ref.shape[1]
    qi = pl.program_id(2)
    q = q_ref[0]
    first = lax.broadcasted_iota(jnp.int32, (1, LANES), 1) < V_HEAD
    row_chunk = lax.broadcasted_iota(jnp.int32, (tile, tile), 0) // CHUNK
    col_chunk = lax.broadcasted_iota(jnp.int32, (tile, tile), 1) // CHUNK
    visible = col_chunk <= row_chunk

    def step(kstart, carry, masked):
        ms, ls, acc = carry
        kstart = pl.multiple_of(kstart, tile)
        k = k_ref[0, pl.ds(kstart, tile), :]
        v = v_ref[0, pl.ds(kstart, tile), :]
        new_ms, new_ls, alphas, pvs = [], [], [], []
        for h in range(2):
            sl = slice(h * SLOT, (h + 1) * SLOT)
            s = _dot_nt(q[:, sl], k[:, sl]) * MLA_SCALE
            if masked:
                s = jnp.where(visible, s, F32(-1e30))
            m_new = jnp.maximum(ms[h], jnp.max(s, axis=-1, keepdims=True))
            alpha = jnp.exp(ms[h] - m_new)
            p = jnp.exp(s - m_new)
            new_ls.append(alpha * ls[h] + jnp.sum(p, axis=-1, keepdims=True))
            new_ms.append(m_new)
            alphas.append(alpha)
            pvs.append(_dot(p.astype(BF16), v[:, sl]))
        acc = acc * jnp.where(first, alphas[0], alphas[1]) + pvs[0] + pvs[1]
        return tuple(new_ms), tuple(new_ls), acc

    neg = jnp.full((tile, 1), -1e30, F32)
    zero = jnp.zeros((tile, 1), F32)
    init = ((neg, neg), (zero, zero), jnp.zeros((tile, LANES), F32))
    carry = lax.fori_loop(0, qi, lambda i, c: step(i * tile, c, False), init)
    _, ls, acc = step(qi * tile, carry, True)
    o_ref[0] = (acc / jnp.where(first, ls[0], ls[1])).astype(BF16)


def _attn_prompt(q, k, v, tile):
    b, s, _ = q.shape
    pairs = MLA_HEADS // 2
    return pl.pallas_call(
        _attn_prompt_kernel,
        grid=(b, pairs, s // tile),
        in_specs=[pl.BlockSpec((1, tile, 2 * SLOT), lambda bi, p, i: (bi, i, p)),
                  pl.BlockSpec((1, s, 2 * SLOT), lambda bi, p, i: (bi, 0, p)),
                  pl.BlockSpec((1, s, 2 * SLOT), lambda bi, p, i: (bi, 0, p))],
        out_specs=pl.BlockSpec((1, tile, LANES), lambda bi, p, i: (bi, i, p)),
        out_shape=jax.ShapeDtypeStruct((b, s, pairs * LANES), BF16),
        compiler_params=_params(("parallel", "parallel", "arbitrary")),
        name="attn_prompt",
    )(q, k, v)


def _attn_sample_kernel(q_ref, k_ref, v_ref, o_ref):
    q, k, v = q_ref[0], k_ref[0], v_ref[0]
    outs = []
    for h in range(2):
        sl = slice(h * SLOT, (h + 1) * SLOT)
        s = _dot_nt(q[:, sl], k[:, sl]) * MLA_SCALE
        p = jnp.exp(s - jnp.max(s, axis=-1, keepdims=True))
        p = p / jnp.sum(p, axis=-1, keepdims=True)
        outs.append(_dot(p.astype(BF16), v[:, sl]))
    o_ref[0] = (outs[0] + outs[1]).astype(BF16)


def _attn_sample(q, k, v):
    b, sq, _ = q.shape
    sk = k.shape[1]
    pairs = MLA_HEADS // 2
    return pl.pallas_call(
        _attn_sample_kernel,
        grid=(b, pairs),
        in_specs=[pl.BlockSpec((1, sq, 2 * SLOT), lambda bi, p: (bi, 0, p)),
                  pl.BlockSpec((1, sk, 2 * SLOT), lambda bi, p: (bi, 0, p)),
                  pl.BlockSpec((1, sk, 2 * SLOT), lambda bi, p: (bi, 0, p))],
        out_specs=pl.BlockSpec((1, sq, LANES), lambda bi, p: (bi, 0, p)),
        out_shape=jax.ShapeDtypeStruct((b, sq, pairs * LANES), BF16),
        compiler_params=_params(("parallel", "parallel")),
        name="attn_sample",
    )(q, k, v)


def _merge_kernel(x_ref, o_ref, siga_ref, gbob_ref, woa_ref, wo_ref, gx_ref, wcq_ref, gcq_ref,
                  x1_out, qc_out):
    oa = _dot(o_ref[...], woa_ref[...])
    merged = siga_ref[...].astype(F32) * oa + gbob_ref[...]
    x1 = x_ref[...] + _dot(merged.astype(BF16), wo_ref[...])
    x1_out[...] = x1
    qc = _dot(_rms(x1, gx_ref[...]).astype(BF16), wcq_ref[...])
    for h in range(MEM_HEADS):
        sl = slice(h * MEM_HEAD_DIM, (h + 1) * MEM_HEAD_DIM)
        qc_out[:, sl] = _rms(qc[:, sl], gcq_ref[...]).astype(BF16)


def _merge(x2d, o, siga, gbob, w, tm):
    t, d = x2d.shape
    consts = [w['w_oa'], w['w_o'], w['g_xattn'], w['w_cq_mem'], w['g_cq']]
    nq = w['w_cq_mem'].shape[1]
    return pl.pallas_call(
        _merge_kernel,
        grid=(t // tm,),
        in_specs=[_row_spec(tm, d), _row_spec(tm, o.shape[1]), _row_spec(tm, d), _row_spec(tm, d)]
        + [_const_spec(c.shape) for c in consts],
        out_specs=[_row_spec(tm, d), _row_spec(tm, nq)],
        out_shape=[jax.ShapeDtypeStruct((t, d), F32), jax.ShapeDtypeStruct((t, nq), BF16)],
        compiler_params=_params(("parallel",)),
        name="merge",
    )(x2d, o, siga, gbob, *consts)


def _mem_kv_kernel(mem_ref, gmem_ref, wck_ref, gck_ref, wcv_ref, k_out, v_out):
    m = _rms(mem_ref[...], gmem_ref[...]).astype(BF16)
    k = _dot(m, wck_ref[...])
    for h in range(MEM_HEADS):
        sl = slice(h * MEM_HEAD_DIM, (h + 1) * MEM_HEAD_DIM)
        k_out[:, sl] = _rms(k[:, sl], gck_ref[...])
    v_out[...] = _dot(m, wcv_ref[...])


def _mem_kv(mem2d, w, tm):
    t, d = mem2d.shape
    consts = [w['g_mem'], w['w_ck'], w['g_ck'], w['w_cv']]
    n = w['w_ck'].shape[1]
    return pl.pallas_call(
        _mem_kv_kernel,
        grid=(t // tm,),
        in_specs=[_row_spec(tm, d)] + [_const_spec(c.shape) for c in consts],
        out_specs=[_row_spec(tm, n)] * 2,
        out_shape=[jax.ShapeDtypeStruct((t, n), F32)] * 2,
        compiler_params=_params(("parallel",)),
        name="mem_kv",
    )(mem2d, *consts)


def _cross_kernel(q_ref, k_ref, v_ref, o_ref):
    q = q_ref[0]
    k = k_ref[0].astype(BF16)
    v = v_ref[0].astype(BF16)
    for h in range(MEM_HEADS):
        sl = slice(h * MEM_HEAD_DIM, (h + 1) * MEM_HEAD_DIM)
        s = _dot_nt(q[:, sl], k[:, sl]) * MEM_SCALE
        p = jnp.exp(s - jnp.max(s, axis=-1, keepdims=True))
        p = p / jnp.sum(p, axis=-1, keepdims=True)
        o_ref[0, :, sl] = _dot(p.astype(BF16), v[:, sl]).astype(BF16)


def _cross(qc, mk, mv, tc):
    b, s, n = qc.shape
    nm = mk.shape[1]
    return pl.pallas_call(
        _cross_kernel,
        grid=(b, s // tc),
        in_specs=[pl.BlockSpec((1, tc, n), lambda bi, i: (bi, i, 0)),
                  pl.BlockSpec((1, nm, n), lambda bi, i: (bi, 0, 0)),
                  pl.BlockSpec((1, nm, n), lambda bi, i: (bi, 0, 0))],
        out_specs=pl.BlockSpec((1, tc, n), lambda bi, i: (bi, i, 0)),
        out_shape=jax.ShapeDtypeStruct((b, s, n), BF16),
        compiler_params=_params(("parallel", "parallel")),
        name="cross_attn",
    )(qc, mk, mv)


def _cross_out_kernel(x1_ref, oc_ref, wco_ref, gffn_ref, x2_out, h_out):
    x2 = x1_ref[...] + _dot(oc_ref[...], wco_ref[...])
    x2_out[...] = x2
    h_out[...] = _rms(x2, gffn_ref[...]).astype(BF16)


def _cross_out(x1, oc, w, tm):
    t, d = x1.shape
    return pl.pallas_call(
        _cross_out_kernel,
        grid=(t // tm,),
        in_specs=[_row_spec(tm, d), _row_spec(tm, oc.shape[1]), _const_spec(w['w_co'].shape),
                  _const_spec(w['g_ffn'].shape)],
        out_specs=[_row_spec(tm, d)] * 2,
        out_shape=[jax.ShapeDtypeStruct((t, d), F32), jax.ShapeDtypeStruct((t, d), BF16)],
        compiler_params=_params(("parallel",)),
        name="cross_out",
    )(x1, oc, w['w_co'], w['g_ffn'])


def _peer_route_kernel(h_ref, wpq_ref, keys_ref, i_out, j_out, g_out,
                       q_scr, sv_scr, si_scr, ts_scr, te_scr, it_scr, jt_scr, gt_scr):
    tm = h_ref.shape[0]
    q_scr[...] = _dot(h_ref[...], wpq_ref[...]).astype(BF16)
    rows_keys = lax.broadcasted_iota(jnp.int32, (N_KEYS, LANES), 0).astype(F32)
    ncand = PEER_TOPK * PEER_TOPK
    rows_cand = lax.broadcasted_iota(jnp.int32, (ncand, LANES), 0).astype(F32)
    neg_inf = F32(-jnp.inf)

    def group_body(c, _):
        t0 = pl.multiple_of(c * LANES, LANES)

        def slot_body(s, _):
            d0 = pl.multiple_of(s * PEER_HALF, PEER_HALF)
            a = _dot_nt(keys_ref[s], q_scr[pl.ds(t0, LANES), pl.ds(d0, PEER_HALF)])
            for kk in range(PEER_TOPK):
                m = jnp.max(a, axis=0, keepdims=True)
                idx = jnp.min(jnp.where(a == m, rows_keys, F32(N_KEYS)), axis=0, keepdims=True)
                a = jnp.where(rows_keys == idx, neg_inf, a)
                sv_scr[s, kk:kk + 1, :] = m
                si_scr[s, kk:kk + 1, :] = idx
            return 0

        lax.fori_loop(0, 2 * PEER_HEADS, slot_body, 0)

        def head_body(hd, _):
            sv0, sv1 = sv_scr[2 * hd], sv_scr[2 * hd + 1]
            si0, si1 = si_scr[2 * hd], si_scr[2 * hd + 1]
            cand = jnp.concatenate([sv0[a:a + 1, :] + sv1 for a in range(PEER_TOPK)], axis=0)
            cidx = jnp.concatenate([si0[a:a + 1, :] * F32(N_KEYS) + si1 for a in range(PEER_TOPK)], axis=0)
            for kk in range(PEER_TOPK):
                m = jnp.max(cand, axis=0, keepdims=True)
                pos = jnp.min(jnp.where(cand == m, rows_cand, F32(ncand)), axis=0, keepdims=True)
                sel = rows_cand == pos
                ts_scr[kk:kk + 1, :] = m
                te_scr[kk:kk + 1, :] = jnp.max(jnp.where(sel, cidx, F32(-1.0)), axis=0, keepdims=True)
                cand = jnp.where(sel, neg_inf, cand)
            top = ts_scr[...]
            ex = jnp.exp(top - top[0:1, :])
            gate = ex / jnp.sum(ex, axis=0, keepdims=True)
            eidx = te_scr[...]
            irow = jnp.floor(eidx * F32(1.0 / N_KEYS))
            r0 = pl.multiple_of(hd * PEER_TOPK, PEER_TOPK)
            it_scr[pl.ds(r0, PEER_TOPK), :] = irow
            jt_scr[pl.ds(r0, PEER_TOPK), :] = eidx - irow * F32(N_KEYS)
            gt_scr[pl.ds(r0, PEER_TOPK), :] = gate
            return 0

        lax.fori_loop(0, PEER_HEADS, head_body, 0)
        i_out[pl.ds(t0, LANES), :] = it_scr[...].T
        j_out[pl.ds(t0, LANES), :] = jt_scr[...].T
        g_out[pl.ds(t0, LANES), :] = gt_scr[...].T
        return 0

    lax.fori_loop(0, tm // LANES, group_body, 0)


def _peer_route(h, w, tm):
    t, d = h.shape
    nsel = PEER_HEADS * PEER_TOPK
    return pl.pallas_call(
        _peer_route_kernel,
        grid=(t // tm,),
        in_specs=[_row_spec(tm, d), _const_spec(w['w_pq'].shape), _const_spec(w['sub_keys'].shape)],
        out_specs=[_row_spec(tm, nsel)] * 3,
        out_shape=[jax.ShapeDtypeStruct((t, nsel), F32)] * 3,
        scratch_shapes=[pltpu.VMEM((tm, w['w_pq'].shape[1]), BF16),
                        pltpu.VMEM((2 * PEER_HEADS, PEER_TOPK, LANES), F32),
                        pltpu.VMEM((2 * PEER_HEADS, PEER_TOPK, LANES), F32),
                        pltpu.VMEM((PEER_TOPK, LANES), F32),
                        pltpu.VMEM((PEER_TOPK, LANES), F32),
                        pltpu.VMEM((nsel, LANES), F32),
                        pltpu.VMEM((nsel, LANES), F32),
                        pltpu.VMEM((nsel, LANES), F32)],
        compiler_params=_params(("parallel",)),
        name="peer_route",
    )(h, w['w_pq'], w['sub_keys'])


def _peer_dense_kernel(h_ref, i_ref, j_ref, g_ref, x_ref, ut_ref, v_ref, out_ref, mask_scr):
    tm = h_ref.shape[0]
    te = ut_ref.shape[1]
    e = pl.program_id(1)

    @pl.when(e == 0)
    def _():
        out_ref[...] = x_ref[...]
        key_rows = lax.broadcasted_iota(jnp.int32, (N_KEYS, LANES), 0).astype(F32)

        def token_body(t, _):
            irow = i_ref[pl.ds(t, 1), :]
            jrow = j_ref[pl.ds(t, 1), :]
            grow = g_ref[pl.ds(t, 1), :]
            rw = jnp.where(key_rows == irow, grow, 0.0).astype(BF16)
            cw = jnp.where(key_rows == jrow, 1.0, 0.0).astype(BF16)
            r0 = pl.multiple_of(t * MASK_PITCH, 8)
            mask_scr[pl.ds(r0, N_KEYS), :] = _dot_nt(rw, cw)
            return 0

        lax.fori_loop(0, tm, token_body, 0, unroll=4)

    act = _gelu(_dot(h_ref[...], ut_ref[...]))
    row0 = e * (te // N_KEYS)
    mask = jnp.concatenate(
        [mask_scr[pl.ds(row0 + r, tm, stride=MASK_PITCH), :] for r in range(te // N_KEYS)], axis=1)
    out_ref[...] += _dot((act * mask).astype(BF16), v_ref[...])


def _peer_dense(h, isel, jsel, gate, x2, ut, v, tm, te):
    t, d = h.shape
    nexp = v.shape[0]
    nsel = isel.shape[1]
    row = lambda width: pl.BlockSpec((tm, width), lambda i, e: (i, 0), pipeline_mode=pl.Buffered(1))
    return pl.pallas_call(
        _peer_dense_kernel,
        grid=(t // tm, nexp // te),
        in_specs=[row(d), row(nsel), row(nsel), row(nsel), row(d),
                  pl.BlockSpec((d, te), lambda i, e: (0, e)),
                  pl.BlockSpec((te, d), lambda i, e: (e, 0))],
        out_specs=pl.BlockSpec((tm, d), lambda i, e: (i, 0)),
        out_shape=jax.ShapeDtypeStruct((t, d), F32),
        scratch_shapes=[pltpu.VMEM((tm * MASK_PITCH, LANES), F32)],
        compiler_params=_params(("parallel", "arbitrary"), PEER_VMEM_LIMIT),
        name="peer_dense",
    )(h, isel, jsel, gate, x2, ut, v)


def _slot_gain(parts):
    row = jnp.zeros((SLOT,), F32)
    for off, g in parts:
        row = lax.dynamic_update_slice(row, g.astype(F32), (off,))
    return row[None, :]


def _group_avg_matrix():
    m = np.zeros((SLOT, SLOT), np.float32)
    m[:QK_NOPE, :QK_NOPE] = 1.0 / QK_NOPE
    m[ROPE_LO:ROPE_LO + QK_ROPE, ROPE_LO:ROPE_LO + QK_ROPE] = 1.0 / QK_ROPE
    return jnp.asarray(m, BF16)


def _rope_tables(pos):
    inv = ROPE_THETA ** (-jnp.arange(ROPE_HALF, dtype=F32) / ROPE_HALF)
    ang = pos.astype(F32)[:, None] * inv[None, :]
    cos, sin = jnp.cos(ang), jnp.sin(ang)
    n = pos.shape[0]
    tail = SLOT - ROPE_LO - QK_ROPE
    cos_t = jnp.concatenate([jnp.ones((n, ROPE_LO), F32), cos, cos, jnp.zeros((n, tail), F32)], axis=1)
    sin_up = jnp.concatenate([jnp.zeros((n, ROPE_LO + ROPE_HALF), F32), sin, jnp.zeros((n, tail), F32)], axis=1)
    sin_dn = jnp.concatenate([jnp.zeros((n, ROPE_LO), F32), -sin, jnp.zeros((n, tail + ROPE_HALF), F32)], axis=1)
    return cos_t, sin_up, sin_dn


def _prep_weights(p, i):
    w_in = p['w_in'][i]
    q_lora = p['g_q_lat'].shape[1]
    kv_lora = p['g_kv_lat'].shape[1]
    gw = p['g_gm'].shape[1]
    d = w_in.shape[0]
    o = np.cumsum([0, q_lora, kv_lora, QK_ROPE, gw, gw, d, d])
    seg = lambda k: w_in[:, o[k]:o[k + 1]]
    row = lambda a: a[i][None, :].astype(F32)
    w = {}
    w['g_mix'] = row(p['g_mix'])
    w['w_cq'] = seg(0).astype(BF16)
    w['w_ckv'] = seg(1).astype(BF16)
    w['w_kr'] = jnp.pad(seg(2), ((0, 0), (ROPE_LO, SLOT - ROPE_LO - QK_ROPE))).astype(BF16)
    w['w_u'] = seg(3).astype(BF16)
    w['w_v'] = seg(4).astype(BF16)
    w['w_ga'] = seg(5).astype(BF16)
    w['w_gb'] = seg(6).astype(BF16)
    w['g_q_lat'] = row(p['g_q_lat'])
    w['g_kv_lat'] = row(p['g_kv_lat'])
    w['g_kr'] = _slot_gain([(ROPE_LO, p['g_kr'][i])])
    w['g_gm'] = row(p['g_gm'])
    head_dim = QK_NOPE + QK_ROPE
    w_uq = p['w_uq'][i].reshape(q_lora, MLA_HEADS, head_dim)
    w['w_uq'] = jnp.pad(w_uq, ((0, 0), (0, 0), (0, SLOT - head_dim))).reshape(q_lora, -1).astype(BF16)
    w['g_q'] = _slot_gain([(0, p['g_qn'][i]), (ROPE_LO, p['g_qr'][i])])
    w['m_avg'] = _group_avg_matrix()
    w_uk = p['w_uk'][i].reshape(kv_lora, MLA_HEADS, QK_NOPE)
    w['w_uk'] = jnp.pad(w_uk, ((0, 0), (0, 0), (0, SLOT - QK_NOPE))).reshape(kv_lora, -1).astype(BF16)
    w['g_kn'] = _slot_gain([(0, p['g_kn'][i])])
    w_uv = p['w_uv'][i].reshape(kv_lora, MLA_HEADS // 2, 2, V_HEAD)
    eye2 = jnp.eye(2, dtype=w_uv.dtype)
    w['w_uv'] = jnp.einsum('cpjd,jk->cpjkd', w_uv, eye2).reshape(kv_lora, -1).astype(BF16)
    w['w_oa'] = p['w_oa'][i].astype(BF16)
    w['w_ob'] = p['w_ob'][i].astype(BF16)
    w['w_o'] = p['w_o'][i].astype(BF16)
    w['g_xattn'] = row(p['g_xattn'])
    w['g_mem'] = row(p['g_mem'])
    w['w_cq_mem'] = p['w_cq'][i].astype(BF16)
    w['g_cq'] = row(p['g_cq'])
    w['w_ck'] = p['w_ck'][i].astype(BF16)
    w['g_ck'] = row(p['g_ck'])
    w['w_cv'] = p['w_cv'][i].astype(BF16)
    w['w_co'] = p['w_co'][i].astype(BF16)
    w['g_ffn'] = row(p['g_ffn'])
    w['w_pq'] = p['w_pq'][i].astype(BF16)
    w['sub_keys'] = p['sub_keys'][i].reshape(2 * PEER_HEADS, N_KEYS, PEER_HALF).astype(BF16)
    w['peer_ut'] = p['peer_u'][i].astype(BF16).T
    w['peer_v'] = p['peer_v'][i].astype(BF16)
    w['w_s'] = p['w_s'][i]
    w['b_s'] = p['b_s'][i]
    return w


def _spatial_operands(w, seq, tm):
    chunk = min(seq, GM_CHUNK)
    reps = tm // chunk
    w_mask = jnp.tril(w['w_s'][:, :chunk, :chunk])
    mix = jnp.einsum('ab,gts->gatbs', jnp.eye(reps, dtype=F32), w_mask).reshape(GM_GROUPS, tm, tm)
    bias = jnp.tile(w['b_s'][:, :chunk].T, (reps, 1))
    bias = jnp.repeat(bias, LANES, axis=1)
    return mix.astype(BF16), bias.astype(F32)


def _token_mixer(x2d, w, seq, pos, past, tm):
    t, d = x2d.shape
    nb = t // seq
    period = max(seq, tm)
    tabs = _rope_tables(jnp.tile(pos, period // seq))
    cq, ckv, kr, u, vg, siga, sigb = _in_proj(x2d, w, tabs, tm)
    mix, bias = _spatial_operands(w, seq, tm)
    gbob = _gmlp(u, vg, sigb, mix, bias, w['w_ob'], tm)
    q = _q_proj(cq, w, tabs, tm)
    if past is None:
        k, v = _kv_proj(ckv, kr, w, tm)
        o = _attn_prompt(q.reshape(nb, seq, -1), k.reshape(nb, seq, -1), v.reshape(nb, seq, -1), ATTN_TILE)
    else:
        past_ckv, past_kr = past
        ckv_all = jnp.concatenate([past_ckv, ckv.reshape(nb, seq, -1)], axis=1)
        past_kr = jnp.pad(past_kr, ((0, 0), (0, 0), (ROPE_LO, SLOT - ROPE_LO - QK_ROPE)))
        kr_all = jnp.concatenate([past_kr, kr.reshape(nb, seq, SLOT)], axis=1)
        sk = ckv_all.shape[1]
        k, v = _kv_proj(ckv_all.reshape(nb * sk, -1), kr_all.reshape(nb * sk, SLOT), w, tm)
        o = _attn_sample(q.reshape(nb, seq, -1), k.reshape(nb, sk, -1), v.reshape(nb, sk, -1))
    x1, qc = _merge(x2d, o.reshape(t, -1), siga, gbob, w, tm)
    return x1, qc, ckv, kr[:, ROPE_LO:ROPE_LO + QK_ROPE], vg


def _tail(x1, qc, mk, mv, w, nb, seq, tm):
    t, d = x1.shape
    oc = _cross(qc.reshape(nb, seq, -1), mk, mv, min(seq, tm))
    x2, h = _cross_out(x1, oc.reshape(t, -1), w, tm)
    isel, jsel, gate = _peer_route(h, w, tm)
    return _peer_dense(h, isel, jsel, gate, x2, w['peer_ut'], w['peer_v'], PEER_ROWS, PEER_EXPERTS)


def kernel(x_prompt, x_sample, cache_mla_ckv, cache_mla_krope, cache_mem_k, cache_mem_v, mem_prompt, g_mix, w_in, g_q_lat, w_uq, g_qn, g_qr, g_kv_lat, g_kr, w_uk, w_uv, g_kn, w_oa, g_gm, w_s, b_s, w_ob, w_o, g_xattn, g_mem, w_cq, g_cq, w_ck, g_ck, w_cv, w_co, g_ffn, w_pq, sub_keys, peer_u, peer_v):
    params = dict(g_mix=g_mix, w_in=w_in, g_q_lat=g_q_lat, w_uq=w_uq, g_qn=g_qn, g_qr=g_qr,
                  g_kv_lat=g_kv_lat, g_kr=g_kr, w_uk=w_uk, w_uv=w_uv, g_kn=g_kn, w_oa=w_oa, g_gm=g_gm,
                  w_s=w_s, b_s=b_s, w_ob=w_ob, w_o=w_o, g_xattn=g_xattn, g_mem=g_mem, w_cq=w_cq,
                  g_cq=g_cq, w_ck=w_ck, g_ck=g_ck, w_cv=w_cv, w_co=w_co, g_ffn=g_ffn, w_pq=w_pq,
                  sub_keys=sub_keys, peer_u=peer_u, peer_v=peer_v)
    bp, sp, d = x_prompt.shape
    bs, ss, _ = x_sample.shape
    depth = w_in.shape[0]
    past_len = cache_mla_ckv.shape[2]
    n_mem = mem_prompt.shape[1]
    tm = ROW_TILE
    assert (bp * sp) % tm == 0 and (bs * ss) % tm == 0 and sp % ATTN_TILE == 0
    pos_p = jnp.arange(sp)
    pos_s = past_len + jnp.arange(ss)
    xp = x_prompt.reshape(bp * sp, d)
    xs = x_sample.reshape(bs * ss, d)
    outs = [[] for _ in range(7)]
    for i in range(depth):
        w = _prep_weights(params, i)
        xp1, qcp, ckv_p, kr_p, _ = _token_mixer(xp, w, sp, pos_p, None, tm)
        xs1, qcs, ckv_s, kr_s, vg_s = _token_mixer(xs, w, ss, pos_s, (cache_mla_ckv[i], cache_mla_krope[i]), tm)
        mk_p, mv_p = _mem_kv(mem_prompt.reshape(bp * n_mem, d), w, min(tm, bp * n_mem))
        xp = _tail(xp1, qcp, mk_p.reshape(bp, n_mem, -1), mv_p.reshape(bp, n_mem, -1), w, bp, sp, tm)
        xs = _tail(xs1, qcs, cache_mem_k[i].reshape(bs, n_mem, -1), cache_mem_v[i].reshape(bs, n_mem, -1),
                   w, bs, ss, tm)
        outs[0].append(ckv_p.reshape(bp, sp, -1))
        outs[1].append(kr_p.reshape(bp, sp, -1))
        outs[2].append(mk_p.reshape(bp, n_mem, MEM_HEADS, MEM_HEAD_DIM))
        outs[3].append(mv_p.reshape(bp, n_mem, MEM_HEADS, MEM_HEAD_DIM))
        outs[4].append(ckv_s.reshape(bs, ss, -1))
        outs[5].append(kr_s.reshape(bs, ss, -1))
        outs[6].append(vg_s.reshape(bs, ss, -1))
    return (xp.reshape(bp, sp, d), xs.reshape(bs, ss, d)) + tuple(jnp.stack(o) for o in outs)
```

```python
import functools

import jax
import jax.numpy as jnp
import numpy as np
from jax import lax
from jax.experimental import pallas as pl
from jax.experimental.pallas import tpu as pltpu

CHUNK = 64
EPS = 1e-6
MLA_HEADS = 16
QK_NOPE = 64
QK_ROPE = 32
V_HEAD = 64
ROPE_THETA = 10000.0
MLA_SCALE = (QK_NOPE + QK_ROPE) ** -0.5
GM_CHUNK = 128
GM_GROUPS = 8
MEM_HEADS = 4
MEM_HEAD_DIM = 128
MEM_SCALE = MEM_HEAD_DIM ** -0.5
PEER_HEADS = 8
N_KEYS = 128
PEER_TOPK = 16
PEER_HALF = 128

LANES = 128
SLOT = LANES
ROPE_LO = QK_NOPE
ROPE_HALF = QK_ROPE // 2
ROW_TILE = 512
ATTN_TILE = 512
LOG2_E = float(np.log2(np.e))
PEER_ROWS = 512
PEER_EXPERTS = 512
ROUTE_LANES = 512
MASK_PITCH = N_KEYS + 8
MASK_UNROLL = 32
VMEM_LIMIT = 48 * 1024 * 1024
PEER_VMEM_LIMIT = 60 * 1024 * 1024

F32 = jnp.float32
BF16 = jnp.bfloat16
_NT = (((1,), (1,)), ((), ()))


def _dot(a, b):
    return jnp.dot(a, b, preferred_element_type=F32)


def _dot_nt(a, b):
    return lax.dot_general(a, b, _NT, preferred_element_type=F32)


def _rms(xf, g):
    return xf * lax.rsqrt(jnp.mean(xf * xf, axis=-1, keepdims=True) + EPS) * g


def _gelu(x):
    return 0.5 * x * (1.0 + lax.erf(x * np.float32(np.sqrt(0.5))))


def _group_mean(sq, m_ref):
    hi = sq.astype(BF16)
    lo = (sq - hi.astype(F32)).astype(BF16)
    m = m_ref[...]
    return _dot(hi, m) + _dot(lo, m)


def _rope_slot(y, cos, sin_up, sin_dn):
    return (y * cos + pltpu.roll(y, ROPE_HALF, 1) * sin_up
            + pltpu.roll(y, SLOT - ROPE_HALF, 1) * sin_dn)


def _const_spec(shape):
    nd = len(shape)
    return pl.BlockSpec(shape, lambda *_: (0,) * nd, pipeline_mode=pl.Buffered(1))


def _row_spec(tm, width):
    return pl.BlockSpec((tm, width), lambda i: (i, 0))


def _params(sem, limit=VMEM_LIMIT):
    return pltpu.CompilerParams(dimension_semantics=sem, vmem_limit_bytes=limit)


def _in_proj_kernel(x_ref, gmix_ref, wcq_ref, wckv_ref, wkr_ref, wu_ref, wv_ref, wga_ref, wgb_ref,
                    gq_ref, gkv_ref, gkr_ref, ggm_ref, cos_ref, sup_ref, sdn_ref,
                    cq_out, ckv_out, kr_out, u_out, vg_out, siga_out, sigb_out):
    h = _rms(x_ref[...], gmix_ref[...]).astype(BF16)
    cq_out[...] = _rms(_dot(h, wcq_ref[...]), gq_ref[...]).astype(BF16)
    ckv_out[...] = _rms(_dot(h, wckv_ref[...]), gkv_ref[...])
    kr = _dot(h, wkr_ref[...])
    ms = jnp.sum(kr * kr, axis=-1, keepdims=True) * (1.0 / QK_ROPE)
    kr = kr * lax.rsqrt(ms + EPS) * gkr_ref[...]
    kr_out[...] = _rope_slot(kr, cos_ref[...], sup_ref[...], sdn_ref[...])
    u_out[...] = _gelu(_dot(h, wu_ref[...])).astype(BF16)
    vg_out[...] = _rms(_gelu(_dot(h, wv_ref[...])), ggm_ref[...])
    siga_out[...] = jax.nn.sigmoid(_dot(h, wga_ref[...])).astype(BF16)
    sigb_out[...] = jax.nn.sigmoid(_dot(h, wgb_ref[...])).astype(BF16)


def _in_proj(x2d, w, tabs, tm):
    t, d = x2d.shape
    cos, sup, sdn = tabs
    ntab = cos.shape[0] // tm
    tab_spec = pl.BlockSpec((tm, SLOT), lambda i: (i % ntab, 0))
    consts = [w['g_mix'], w['w_cq'], w['w_ckv'], w['w_kr'], w['w_u'], w['w_v'], w['w_ga'], w['w_gb'],
              w['g_q_lat'], w['g_kv_lat'], w['g_kr'], w['g_gm']]
    gw = w['w_u'].shape[1]
    widths = [w['w_cq'].shape[1], w['w_ckv'].shape[1], SLOT, gw, gw, d, d]
    dtypes = [BF16, F32, F32, BF16, F32, BF16, BF16]
    return pl.pallas_call(
        _in_proj_kernel,
        grid=(t // tm,),
        in_specs=[_row_spec(tm, d)] + [_const_spec(c.shape) for c in consts] + [tab_spec] * 3,
        out_specs=[_row_spec(tm, n) for n in widths],
        out_shape=[jax.ShapeDtypeStruct((t, n), dt) for n, dt in zip(widths, dtypes)],
        compiler_params=_params(("parallel",)),
        name="in_proj",
    )(x2d, *consts, cos, sup, sdn)


def _gmlp_kernel(u_ref, vg_ref, sigb_ref, mix_ref, bias_ref, wob_ref, out_ref):
    vgb = vg_ref[...].astype(BF16)
    parts = []
    for g in range(GM_GROUPS):
        sl = slice(g * LANES, (g + 1) * LANES)
        mixed = _dot(mix_ref[g], vgb[:, sl]) + bias_ref[:, sl]
        parts.append((u_ref[:, sl].astype(F32) * mixed).astype(BF16))
    ob = _dot(jnp.concatenate(parts, axis=1), wob_ref[...])
    out_ref[...] = sigb_ref[...].astype(F32) * ob


def _gmlp(u, vg, sigb, mix, bias, w_ob, tm):
    t, gw = u.shape
    d = w_ob.shape[1]
    return pl.pallas_call(
        _gmlp_kernel,
        grid=(t // tm,),
        in_specs=[_row_spec(tm, gw), _row_spec(tm, gw), _row_spec(tm, d),
                  _const_spec(mix.shape), _const_spec(bias.shape), _const_spec(w_ob.shape)],
        out_specs=_row_spec(tm, d),
        out_shape=jax.ShapeDtypeStruct((t, d), F32),
        compiler_params=_params(("parallel",)),
        name="gmlp",
    )(u, vg, sigb, mix, bias, w_ob)


def _q_proj_kernel(cq_ref, wuq_ref, gq_ref, mavg_ref, cos_ref, sup_ref, sdn_ref, q_out):
    q = _dot(cq_ref[...], wuq_ref[...])
    cos, sup, sdn = cos_ref[...], sup_ref[...], sdn_ref[...]
    for h in range(MLA_HEADS):
        sl = slice(h * SLOT, (h + 1) * SLOT)
        qs = q[:, sl]
        y = qs * lax.rsqrt(_group_mean(qs * qs, mavg_ref) + EPS) * gq_ref[...]
        q_out[:, sl] = _rope_slot(y, cos, sup, sdn).astype(BF16)


def _q_proj(cq, w, tabs, tm):
    t, ql = cq.shape
    cos, sup, sdn = tabs
    ntab = cos.shape[0] // tm
    tab_spec = pl.BlockSpec((tm, SLOT), lambda i: (i % ntab, 0))
    n = MLA_HEADS * SLOT
    return pl.pallas_call(
        _q_proj_kernel,
        grid=(t // tm,),
        in_specs=[_row_spec(tm, ql), _const_spec(w['w_uq'].shape), _const_spec(w['g_q'].shape),
                  _const_spec(w['m_avg'].shape)] + [tab_spec] * 3,
        out_specs=_row_spec(tm, n),
        out_shape=jax.ShapeDtypeStruct((t, n), BF16),
        compiler_params=_params(("parallel",)),
        name="q_proj",
    )(cq, w['w_uq'], w['g_q'], w['m_avg'], cos, sup, sdn)


def _kv_proj_kernel(ckv_ref, kr_ref, wuk_ref, wuv_ref, gkn_ref, mavg_ref, k_out, v_out):
    c = ckv_ref[...].astype(BF16)
    kn = _dot(c, wuk_ref[...])
    kr = kr_ref[...]
    for h in range(MLA_HEADS):
        sl = slice(h * SLOT, (h + 1) * SLOT)
        ks = kn[:, sl]
        y = ks * lax.rsqrt(_group_mean(ks * ks, mavg_ref) + EPS) * gkn_ref[...]
        k_out[:, sl] = (y + kr).astype(BF16)
    v_out[...] = _dot(c, wuv_ref[...]).astype(BF16)


def _kv_proj(ckv, kr, w, tm):
    t, kl = ckv.shape
    n = MLA_HEADS * SLOT
    return pl.pallas_call(
        _kv_proj_kernel,
        grid=(t // tm,),
        in_specs=[_row_spec(tm, kl), _row_spec(tm, SLOT), _const_spec(w['w_uk'].shape),
                  _const_spec(w['w_uv'].shape), _const_spec(w['g_kn'].shape),
                  _const_spec(w['m_avg'].shape)],
        out_specs=[_row_spec(tm, n), _row_spec(tm, n)],
        out_shape=[jax.ShapeDtypeStruct((t, n), BF16)] * 2,
        compiler_params=_params(("parallel",)),
        name="kv_proj",
    )(ckv, kr, w['w_uk'], w['w_uv'], w['g_kn'], w['m_avg'])


def _attn_prompt_kernel(q_ref, k_ref, v_ref, o_ref):
    tile = q_ref.shape[1]
    qi = pl.program_id(2)
    q = q_ref[0]
    first = lax.broadcasted_iota(jnp.int32, (1, LANES), 1) < V_HEAD
    row_chunk = lax.broadcasted_iota(jnp.int32, (tile, tile), 0) // CHUNK
    col_chunk = lax.broadcasted_iota(jnp.int32, (tile, tile), 1) // CHUNK
    visible = col_chunk <= row_chunk

    def step(kstart, carry, masked):
        ms, ls, acc = carry
        kstart = pl.multiple_of(kstart, tile)
        k = k_ref[0, pl.ds(kstart, tile), :]
        v = v_ref[0, pl.ds(kstart, tile), :]
        new_ms, new_ls, alphas, pvs = [], [], [], []
        for h in range(2):
            sl = slice(h * SLOT, (h + 1) * SLOT)
            s = _dot_nt(q[:, sl], k[:, sl]) * F32(MLA_SCALE * LOG2_E)
            if masked:
                s = jnp.where(visible, s, F32(-1e30))
            m_new = jnp.maximum(ms[h], jnp.max(s, axis=-1, keepdims=True))
            alpha = jnp.exp2(ms[h] - m_new)
            p = jnp.exp2(s - m_new)
            new_ls.append(alpha * ls[h] + jnp.sum(p, axis=-1, keepdims=True))
            new_ms.append(m_new)
            alphas.append(alpha)
            pvs.append(_dot(p.astype(BF16), v[:, sl]))
        acc = acc * jnp.where(first, alphas[0], alphas[1]) + pvs[0] + pvs[1]
        return tuple(new_ms), tuple(new_ls), acc

    neg = jnp.full((tile, 1), -1e30, F32)
    zero = jnp.zeros((tile, 1), F32)
    init = ((neg, neg), (zero, zero), jnp.zeros((tile, LANES), F32))
    carry = lax.fori_loop(0, qi, lambda i, c: step(i * tile, c, False), init)
    _, ls, acc = step(qi * tile, carry, True)
    o_ref[0] = (acc / jnp.where(first, ls[0], ls[1])).astype(BF16)


def _attn_prompt(q, k, v, tile):
    b, s, _ = q.shape
    pairs = MLA_HEADS // 2
    return pl.pallas_call(
        _attn_prompt_kernel,
        grid=(b, pairs, s // tile),
        in_specs=[pl.BlockSpec((1, tile, 2 * SLOT), lambda bi, p, i: (bi, i, p)),
                  pl.BlockSpec((1, s, 2 * SLOT), lambda bi, p, i: (bi, 0, p)),
                  pl.BlockSpec((1, s, 2 * SLOT), lambda bi, p, i: (bi, 0, p))],
        out_specs=pl.BlockSpec((1, tile, LANES), lambda bi, p, i: (bi, i, p)),
        out_shape=jax.ShapeDtypeStruct((b, s, pairs * LANES), BF16),
        compiler_params=_params(("parallel", "parallel", "arbitrary")),
        name="attn_prompt",
    )(q, k, v)


def _attn_sample_kernel(q_ref, k_ref, v_ref, o_ref):
    q, k, v = q_ref[0], k_ref[0], v_ref[0]
    outs = []
    for h in range(2):
        sl = slice(h * SLOT, (h + 1) * SLOT)
        s = _dot_nt(q[:, sl], k[:, sl]) * MLA_SCALE
        p = jnp.exp(s - jnp.max(s, axis=-1, keepdims=True))
        p = p / jnp.sum(p, axis=-1, keepdims=True)
        outs.append(_dot(p.astype(BF16), v[:, sl]))
    o_ref[0] = (outs[0] + outs[1]).astype(BF16)


def _attn_sample(q, k, v):
    b, sq, _ = q.shape
    sk = k.shape[1]
    pairs = MLA_HEADS // 2
    return pl.pallas_call(
        _attn_sample_kernel,
        grid=(b, pairs),
        in_specs=[pl.BlockSpec((1, sq, 2 * SLOT), lambda bi, p: (bi, 0, p)),
                  pl.BlockSpec((1, sk, 2 * SLOT), lambda bi, p: (bi, 0, p)),
                  pl.BlockSpec((1, sk, 2 * SLOT), lambda bi, p: (bi, 0, p))],
        out_specs=pl.BlockSpec((1, sq, LANES), lambda bi, p: (bi, 0, p)),
        out_shape=jax.ShapeDtypeStruct((b, sq, pairs * LANES), BF16),
        compiler_params=_params(("parallel", "parallel")),
        name="attn_sample",
    )(q, k, v)


def _merge_kernel(x_ref, o_ref, siga_ref, gbob_ref, woa_ref, wo_ref, gx_ref, wcq_ref, gcq_ref,
                  x1_out, qc_out):
    oa = _dot(o_ref[...], woa_ref[...])
    merged = siga_ref[...].astype(F32) * oa + gbob_ref[...]
    x1 = x_ref[...] + _dot(merged.astype(BF16), wo_ref[...])
    x1_out[...] = x1
    qc = _dot(_rms(x1, gx_ref[...]).astype(BF16), wcq_ref[...])
    for h in range(MEM_HEADS):
        sl = slice(h * MEM_HEAD_DIM, (h + 1) * MEM_HEAD_DIM)
        qc_out[:, sl] = _rms(qc[:, sl], gcq_ref[...]).astype(BF16)


def _merge(x2d, o, siga, gbob, w, tm):
    t, d = x2d.shape
    consts = [w['w_oa'], w['w_o'], w['g_xattn'], w['w_cq_mem'], w['g_cq']]
    nq = w['w_cq_mem'].shape[1]
    return pl.pallas_call(
        _merge_kernel,
        grid=(t // tm,),
        in_specs=[_row_spec(tm, d), _row_spec(tm, o.shape[1]), _row_spec(tm, d), _row_spec(tm, d)]
        + [_const_spec(c.shape) for c in consts],
        out_specs=[_row_spec(tm, d), _row_spec(tm, nq)],
        out_shape=[jax.ShapeDtypeStruct((t, d), F32), jax.ShapeDtypeStruct((t, nq), BF16)],
        compiler_params=_params(("parallel",)),
        name="merge",
    )(x2d, o, siga, gbob, *consts)


def _mem_kv_kernel(mem_ref, gmem_ref, wck_ref, gck_ref, wcv_ref, k_out, v_out):
    m = _rms(mem_ref[...], gmem_ref[...]).astype(BF16)
    k = _dot(m, wck_ref[...])
    for h in range(MEM_HEADS):
        sl = slice(h * MEM_HEAD_DIM, (h + 1) * MEM_HEAD_DIM)
        k_out[:, sl] = _rms(k[:, sl], gck_ref[...])
    v_out[...] = _dot(m, wcv_ref[...])


def _mem_kv(mem2d, w, tm):
    t, d = mem2d.shape
    consts = [w['g_mem'], w['w_ck'], w['g_ck'], w['w_cv']]
    n = w['w_ck'].shape[1]
    return pl.pallas_call(
        _mem_kv_kernel,
        grid=(t // tm,),
        in_specs=[_row_spec(tm, d)] + [_const_spec(c.shape) for c in consts],
        out_specs=[_row_spec(tm, n)] * 2,
        out_shape=[jax.ShapeDtypeStruct((t, n), F32)] * 2,
        compiler_params=_params(("parallel",)),
        name="mem_kv",
    )(mem2d, *consts)


def _cross_kernel(q_ref, k_ref, v_ref, o_ref):
    q = q_ref[0]
    k = k_ref[0].astype(BF16)
    v = v_ref[0].astype(BF16)
    for h in range(MEM_HEADS):
        sl = slice(h * MEM_HEAD_DIM, (h + 1) * MEM_HEAD_DIM)
        s = _dot_nt(q[:, sl], k[:, sl]) * MEM_SCALE
        p = jnp.exp(s - jnp.max(s, axis=-1, keepdims=True))
        p = p / jnp.sum(p, axis=-1, keepdims=True)
        o_ref[0, :, sl] = _dot(p.astype(BF16), v[:, sl]).astype(BF16)


def _cross(qc, mk, mv, tc):
    b, s, n = qc.shape
    nm = mk.shape[1]
    return pl.pallas_call(
        _cross_kernel,
        grid=(b, s // tc),
        in_specs=[pl.BlockSpec((1, tc, n), lambda bi, i: (bi, i, 0)),
                  pl.BlockSpec((1, nm, n), lambda bi, i: (bi, 0, 0)),
                  pl.BlockSpec((1, nm, n), lambda bi, i: (bi, 0, 0))],
        out_specs=pl.BlockSpec((1, tc, n), lambda bi, i: (bi, i, 0)),
        out_shape=jax.ShapeDtypeStruct((b, s, n), BF16),
        compiler_params=_params(("parallel", "parallel")),
        name="cross_attn",
    )(qc, mk, mv)


def _cross_out_kernel(x1_ref, oc_ref, wco_ref, gffn_ref, x2_out, h_out):
    x2 = x1_ref[...] + _dot(oc_ref[...], wco_ref[...])
    x2_out[...] = x2
    h_out[...] = _rms(x2, gffn_ref[...]).astype(BF16)


def _cross_out(x1, oc, w, tm):
    t, d = x1.shape
    return pl.pallas_call(
        _cross_out_kernel,
        grid=(t // tm,),
        in_specs=[_row_spec(tm, d), _row_spec(tm, oc.shape[1]), _const_spec(w['w_co'].shape),
                  _const_spec(w['g_ffn'].shape)],
        out_specs=[_row_spec(tm, d)] * 2,
        out_shape=[jax.ShapeDtypeStruct((t, d), F32), jax.ShapeDtypeStruct((t, d), BF16)],
        compiler_params=_params(("parallel",)),
        name="cross_out",
    )(x1, oc, w['w_co'], w['g_ffn'])


def _pair_blocks():
    k = PEER_TOPK
    blocks, flat = [], []
    n_row = sum(1 for a in range(k) if k // (a + 1) > 2)
    for a in range(n_row):
        nb = k if a == 0 else 8
        blocks.append(('row', a, nb))
        flat += [a * k + b if (a + 1) * (b + 1) <= k else -1 for b in range(nb)]
    for b in range(k // (n_row + 1)):
        na = k if b == 0 else 8
        blocks.append(('col', b, na))
        flat += [a * k + b if (a >= n_row and (a + 1) * (b + 1) <= k) else -1 for a in range(na)]
    return blocks, np.asarray(flat, np.float32)


def _pair_candidates(v0, v1, blocks, combine):
    parts = []
    for kind, fixed, n in blocks:
        if kind == 'row':
            parts.append(combine(v0[fixed:fixed + 1, :], v1[0:n, :]))
        else:
            parts.append(combine(v0[0:n, :], v1[fixed:fixed + 1, :]))
    return jnp.concatenate(parts, axis=0)


def _peer_route_kernel(h_ref, wpq_ref, keys_ref, flat_ref, i_out, j_out, g_out,
                       q_scr, sv_scr, si_scr, ts_scr, te_scr, it_scr, jt_scr, gt_scr):
    tm = h_ref.shape[0]
    width = sv_scr.shape[2]
    blocks, _ = _pair_blocks()
    q_scr[...] = _dot(h_ref[...], wpq_ref[...]).astype(BF16)
    rows_keys = lax.broadcasted_iota(jnp.int32, (N_KEYS, width), 0).astype(F32)
    neg_inf = F32(-jnp.inf)

    def group_body(c, _):
        t0 = pl.multiple_of(c * width, width)

        def slot_body(s, _):
            d0 = pl.multiple_of(s * PEER_HALF, PEER_HALF)
            a = _dot_nt(keys_ref[s], q_scr[pl.ds(t0, width), pl.ds(d0, PEER_HALF)])
            for kk in range(PEER_TOPK):
                m = jnp.max(a, axis=0, keepdims=True)
                idx = jnp.min(jnp.where(a == m, rows_keys, F32(N_KEYS)), axis=0, keepdims=True)
                a = jnp.where(rows_keys == idx, neg_inf, a)
                sv_scr[s, kk:kk + 1, :] = m
                si_scr[s, kk:kk + 1, :] = idx
            return 0

        lax.fori_loop(0, 2 * PEER_HEADS, slot_body, 0)

        def head_body(hd, _):
            sv0, sv1 = sv_scr[2 * hd], sv_scr[2 * hd + 1]
            si0, si1 = si_scr[2 * hd], si_scr[2 * hd + 1]
            flat = flat_ref[...]
            cand = _pair_candidates(sv0, sv1, blocks, lambda x, y: x + y)
            cand = jnp.where(flat >= 0.0, cand, neg_inf)
            cidx = _pair_candidates(si0, si1, blocks, lambda x, y: x * F32(N_KEYS) + y)
            for kk in range(PEER_TOPK):
                m = jnp.max(cand, axis=0, keepdims=True)
                pos = jnp.min(jnp.where(cand == m, flat, F32(PEER_TOPK * PEER_TOPK)), axis=0, keepdims=True)
                sel = flat == pos
                ts_scr[kk:kk + 1, :] = m
                te_scr[kk:kk + 1, :] = jnp.max(jnp.where(sel, cidx, F32(-1.0)), axis=0, keepdims=True)
                cand = jnp.where(sel, neg_inf, cand)
            top = ts_scr[...]
            ex = jnp.exp(top - top[0:1, :])
            gate = ex / jnp.sum(ex, axis=0, keepdims=True)
            eidx = te_scr[...]
            irow = jnp.floor(eidx * F32(1.0 / N_KEYS))
            r0 = pl.multiple_of(hd * PEER_TOPK, PEER_TOPK)
            it_scr[pl.ds(r0, PEER_TOPK), :] = irow
            jt_scr[pl.ds(r0, PEER_TOPK), :] = eidx - irow * F32(N_KEYS)
            gt_scr[pl.ds(r0, PEER_TOPK), :] = gate
            return 0

        lax.fori_loop(0, PEER_HEADS, head_body, 0)
        i_out[pl.ds(t0, width), :] = it_scr[...].T
        j_out[pl.ds(t0, width), :] = jt_scr[...].T
        g_out[pl.ds(t0, width), :] = gt_scr[...].T
        return 0

    lax.fori_loop(0, tm // width, group_body, 0)


def _peer_route(h, w, tm):
    t, d = h.shape
    nsel = PEER_HEADS * PEER_TOPK
    width = ROUTE_LANES
    _, flat = _pair_blocks()
    flat = jnp.asarray(np.repeat(flat[:, None], width, axis=1))
    return pl.pallas_call(
        _peer_route_kernel,
        grid=(t // tm,),
        in_specs=[_row_spec(tm, d), _const_spec(w['w_pq'].shape), _const_spec(w['sub_keys'].shape),
                  _const_spec(flat.shape)],
        out_specs=[_row_spec(tm, nsel)] * 3,
        out_shape=[jax.ShapeDtypeStruct((t, nsel), F32)] * 3,
        scratch_shapes=[pltpu.VMEM((tm, w['w_pq'].shape[1]), BF16),
                        pltpu.VMEM((2 * PEER_HEADS, PEER_TOPK, width), F32),
                        pltpu.VMEM((2 * PEER_HEADS, PEER_TOPK, width), F32),
                        pltpu.VMEM((PEER_TOPK, width), F32),
                        pltpu.VMEM((PEER_TOPK, width), F32),
                        pltpu.VMEM((nsel, width), F32),
                        pltpu.VMEM((nsel, width), F32),
                        pltpu.VMEM((nsel, width), F32)],
        compiler_params=_params(("parallel",)),
        name="peer_route",
    )(h, w['w_pq'], w['sub_keys'], flat)


def _peer_dense_kernel(h_ref, i_ref, j_ref, g_ref, x_ref, ut_ref, v_ref, out_ref, mask_scr):
    tm = h_ref.shape[0]
    te = ut_ref.shape[1]
    e = pl.program_id(1)

    @pl.when(e == 0)
    def _():
        out_ref[...] = x_ref[...]
        key_rows = lax.broadcasted_iota(jnp.int32, (N_KEYS, LANES), 0).astype(F32)

        def token_body(t, _):
            irow = i_ref[pl.ds(t, 1), :]
            jrow = j_ref[pl.ds(t, 1), :]
            grow = g_ref[pl.ds(t, 1), :]
            rw = jnp.where(key_rows == irow, grow, 0.0).astype(BF16)
            cw = jnp.where(key_rows == jrow, 1.0, 0.0).astype(BF16)
            r0 = pl.multiple_of(t * MASK_PITCH, 8)
            mask_scr[pl.ds(r0, N_KEYS), :] = _dot_nt(rw, cw)
            return 0

        lax.fori_loop(0, tm, token_body, 0, unroll=MASK_UNROLL)

    act = _gelu(_dot(h_ref[...], ut_ref[...]))
    row0 = e * (te // N_KEYS)
    mask = jnp.concatenate(
        [mask_scr[pl.ds(row0 + r, tm, stride=MASK_PITCH), :] for r in range(te // N_KEYS)], axis=1)
    out_ref[...] += _dot((act * mask).astype(BF16), v_ref[...])


def _peer_dense(h, isel, jsel, gate, x2, ut, v, tm, te):
    t, d = h.shape
    nexp = v.shape[0]
    nsel = isel.shape[1]
    row = lambda width: pl.BlockSpec((tm, width), lambda i, e: (i, 0), pipeline_mode=pl.Buffered(1))
    return pl.pallas_call(
        _peer_dense_kernel,
        grid=(t // tm, nexp // te),
        in_specs=[row(d), row(nsel), row(nsel), row(nsel), row(d),
                  pl.BlockSpec((d, te), lambda i, e: (0, e)),
                  pl.BlockSpec((te, d), lambda i, e: (e, 0))],
        out_specs=pl.BlockSpec((tm, d), lambda i, e: (i, 0)),
        out_shape=jax.ShapeDtypeStruct((t, d), F32),
        scratch_shapes=[pltpu.VMEM((tm * MASK_PITCH, LANES), F32)],
        compiler_params=_params(("parallel", "arbitrary"), PEER_VMEM_LIMIT),
        name="peer_dense",
    )(h, isel, jsel, gate, x2, ut, v)


def _slot_gain(parts):
    row = jnp.zeros((SLOT,), F32)
    for off, g in parts:
        row = lax.dynamic_update_slice(row, g.astype(F32), (off,))
    return row[None, :]


def _group_avg_matrix():
    m = np.zeros((SLOT, SLOT), np.float32)
    m[:QK_NOPE, :QK_NOPE] = 1.0 / QK_NOPE
    m[ROPE_LO:ROPE_LO + QK_ROPE, ROPE_LO:ROPE_LO + QK_ROPE] = 1.0 / QK_ROPE
    return jnp.asarray(m, BF16)


def _rope_tables(pos):
    inv = ROPE_THETA ** (-jnp.arange(ROPE_HALF, dtype=F32) / ROPE_HALF)
    ang = pos.astype(F32)[:, None] * inv[None, :]
    cos, sin = jnp.cos(ang), jnp.sin(ang)
    n = pos.shape[0]
    tail = SLOT - ROPE_LO - QK_ROPE
    cos_t = jnp.concatenate([jnp.ones((n, ROPE_LO), F32), cos, cos, jnp.zeros((n, tail), F32)], axis=1)
    sin_up = jnp.concatenate([jnp.zeros((n, ROPE_LO + ROPE_HALF), F32), sin, jnp.zeros((n, tail), F32)], axis=1)
    sin_dn = jnp.concatenate([jnp.zeros((n, ROPE_LO), F32), -sin, jnp.zeros((n, tail + ROPE_HALF), F32)], axis=1)
    return cos_t, sin_up, sin_dn


def _prep_weights(p, i):
    w_in = p['w_in'][i]
    q_lora = p['g_q_lat'].shape[1]
    kv_lora = p['g_kv_lat'].shape[1]
    gw = p['g_gm'].shape[1]
    d = w_in.shape[0]
    o = np.cumsum([0, q_lora, kv_lora, QK_ROPE, gw, gw, d, d])
    seg = lambda k: w_in[:, o[k]:o[k + 1]]
    row = lambda a: a[i][None, :].astype(F32)
    w = {}
    w['g_mix'] = row(p['g_mix'])
    w['w_cq'] = seg(0).astype(BF16)
    w['w_ckv'] = seg(1).astype(BF16)
    w['w_kr'] = jnp.pad(seg(2), ((0, 0), (ROPE_LO, SLOT - ROPE_LO - QK_ROPE))).astype(BF16)
    w['w_u'] = seg(3).astype(BF16)
    w['w_v'] = seg(4).astype(BF16)
    w['w_ga'] = seg(5).astype(BF16)
    w['w_gb'] = seg(6).astype(BF16)
    w['g_q_lat'] = row(p['g_q_lat'])
    w['g_kv_lat'] = row(p['g_kv_lat'])
    w['g_kr'] = _slot_gain([(ROPE_LO, p['g_kr'][i])])
    w['g_gm'] = row(p['g_gm'])
    head_dim = QK_NOPE + QK_ROPE
    w_uq = p['w_uq'][i].reshape(q_lora, MLA_HEADS, head_dim)
    w['w_uq'] = jnp.pad(w_uq, ((0, 0), (0, 0), (0, SLOT - head_dim))).reshape(q_lora, -1).astype(BF16)
    w['g_q'] = _slot_gain([(0, p['g_qn'][i]), (ROPE_LO, p['g_qr'][i])])
    w['m_avg'] = _group_avg_matrix()
    w_uk = p['w_uk'][i].reshape(kv_lora, MLA_HEADS, QK_NOPE)
    w['w_uk'] = jnp.pad(w_uk, ((0, 0), (0, 0), (0, SLOT - QK_NOPE))).reshape(kv_lora, -1).astype(BF16)
    w['g_kn'] = _slot_gain([(0, p['g_kn'][i])])
    w_uv = p['w_uv'][i].reshape(kv_lora, MLA_HEADS // 2, 2, V_HEAD)
    eye2 = jnp.eye(2, dtype=w_uv.dtype)
    w['w_uv'] = jnp.einsum('cpjd,jk->cpjkd', w_uv, eye2).reshape(kv_lora, -1).astype(BF16)
    w['w_oa'] = p['w_oa'][i].astype(BF16)
    w['w_ob'] = p['w_ob'][i].astype(BF16)
    w['w_o'] = p['w_o'][i].astype(BF16)
    w['g_xattn'] = row(p['g_xattn'])
    w['g_mem'] = row(p['g_mem'])
    w['w_cq_mem'] = p['w_cq'][i].astype(BF16)
    w['g_cq'] = row(p['g_cq'])
    w['w_ck'] = p['w_ck'][i].astype(BF16)
    w['g_ck'] = row(p['g_ck'])
    w['w_cv'] = p['w_cv'][i].astype(BF16)
    w['w_co'] = p['w_co'][i].astype(BF16)
    w['g_ffn'] = row(p['g_ffn'])
    w['w_pq'] = p['w_pq'][i].astype(BF16)
    w['sub_keys'] = p['sub_keys'][i].reshape(2 * PEER_HEADS, N_KEYS, PEER_HALF).astype(BF16)
    w['peer_ut'] = p['peer_u'][i].astype(BF16).T
    w['peer_v'] = p['peer_v'][i].astype(BF16)
    w['w_s'] = p['w_s'][i]
    w['b_s'] = p['b_s'][i]
    return w


def _spatial_operands(w, seq, tm):
    chunk = min(seq, GM_CHUNK)
    reps = tm // chunk
    w_mask = jnp.tril(w['w_s'][:, :chunk, :chunk])
    mix = jnp.einsum('ab,gts->gatbs', jnp.eye(reps, dtype=F32), w_mask).reshape(GM_GROUPS, tm, tm)
    bias = jnp.tile(w['b_s'][:, :chunk].T, (reps, 1))
    bias = jnp.repeat(bias, LANES, axis=1)
    return mix.astype(BF16), bias.astype(F32)


def _token_mixer(x2d, w, seq, pos, past, tm):
    t, d = x2d.shape
    nb = t // seq
    period = max(seq, tm)
    tabs = _rope_tables(jnp.tile(pos, period // seq))
    cq, ckv, kr, u, vg, siga, sigb = _in_proj(x2d, w, tabs, tm)
    mix, bias = _spatial_operands(w, seq, tm)
    gbob = _gmlp(u, vg, sigb, mix, bias, w['w_ob'], tm)
    q = _q_proj(cq, w, tabs, tm)
    if past is None:
        k, v = _kv_proj(ckv, kr, w, tm)
        o = _attn_prompt(q.reshape(nb, seq, -1), k.reshape(nb, seq, -1), v.reshape(nb, seq, -1), ATTN_TILE)
    else:
        past_ckv, past_kr = past
        ckv_all = jnp.concatenate([past_ckv, ckv.reshape(nb, seq, -1)], axis=1)
        past_kr = jnp.pad(past_kr, ((0, 0), (0, 0), (ROPE_LO, SLOT - ROPE_LO - QK_ROPE)))
        kr_all = jnp.concatenate([past_kr, kr.reshape(nb, seq, SLOT)], axis=1)
        sk = ckv_all.shape[1]
        k, v = _kv_proj(ckv_all.reshape(nb * sk, -1), kr_all.reshape(nb * sk, SLOT), w, tm)
        o = _attn_sample(q.reshape(nb, seq, -1), k.reshape(nb, sk, -1), v.reshape(nb, sk, -1))
    x1, qc = _merge(x2d, o.reshape(t, -1), siga, gbob, w, tm)
    return x1, qc, ckv, kr[:, ROPE_LO:ROPE_LO + QK_ROPE], vg


def _tail(x1, qc, mk, mv, w, nb, seq, tm):
    t, d = x1.shape
    oc = _cross(qc.reshape(nb, seq, -1), mk, mv, min(seq, tm))
    x2, h = _cross_out(x1, oc.reshape(t, -1), w, tm)
    isel, jsel, gate = _peer_route(h, w, tm)
    return _peer_dense(h, isel, jsel, gate, x2, w['peer_ut'], w['peer_v'], PEER_ROWS, PEER_EXPERTS)


def kernel(x_prompt, x_sample, cache_mla_ckv, cache_mla_krope, cache_mem_k, cache_mem_v, mem_prompt, g_mix, w_in, g_q_lat, w_uq, g_qn, g_qr, g_kv_lat, g_kr, w_uk, w_uv, g_kn, w_oa, g_gm, w_s, b_s, w_ob, w_o, g_xattn, g_mem, w_cq, g_cq, w_ck, g_ck, w_cv, w_co, g_ffn, w_pq, sub_keys, peer_u, peer_v):
    params = dict(g_mix=g_mix, w_in=w_in, g_q_lat=g_q_lat, w_uq=w_uq, g_qn=g_qn, g_qr=g_qr,
                  g_kv_lat=g_kv_lat, g_kr=g_kr, w_uk=w_uk, w_uv=w_uv, g_kn=g_kn, w_oa=w_oa, g_gm=g_gm,
                  w_s=w_s, b_s=b_s, w_ob=w_ob, w_o=w_o, g_xattn=g_xattn, g_mem=g_mem, w_cq=w_cq,
                  g_cq=g_cq, w_ck=w_ck, g_ck=g_ck, w_cv=w_cv, w_co=w_co, g_ffn=g_ffn, w_pq=w_pq,
                  sub_keys=sub_keys, peer_u=peer_u, peer_v=peer_v)
    bp, sp, d = x_prompt.shape
    bs, ss, _ = x_sample.shape
    depth = w_in.shape[0]
    past_len = cache_mla_ckv.shape[2]
    n_mem = mem_prompt.shape[1]
    tm = ROW_TILE
    assert (bp * sp) % tm == 0 and (bs * ss) % tm == 0 and sp % ATTN_TILE == 0
    pos_p = jnp.arange(sp)
    pos_s = past_len + jnp.arange(ss)
    xp = x_prompt.reshape(bp * sp, d)
    xs = x_sample.reshape(bs * ss, d)
    outs = [[] for _ in range(7)]
    for i in range(depth):
        w = _prep_weights(params, i)
        xp1, qcp, ckv_p, kr_p, _ = _token_mixer(xp, w, sp, pos_p, None, tm)
        xs1, qcs, ckv_s, kr_s, vg_s = _token_mixer(xs, w, ss, pos_s, (cache_mla_ckv[i], cache_mla_krope[i]), tm)
        mk_p, mv_p = _mem_kv(mem_prompt.reshape(bp * n_mem, d), w, min(tm, bp * n_mem))
        xp = _tail(xp1, qcp, mk_p.reshape(bp, n_mem, -1), mv_p.reshape(bp, n_mem, -1), w, bp, sp, tm)
        xs = _tail(xs1, qcs, cache_mem_k[i].reshape(bs, n_mem, -1), cache_mem_v[i].reshape(bs, n_mem, -1),
                   w, bs, ss, tm)
        outs[0].append(ckv_p.reshape(bp, sp, -1))
        outs[1].append(kr_p.reshape(bp, sp, -1))
        outs[2].append(mk_p.reshape(bp, n_mem, MEM_HEADS, MEM_HEAD_DIM))
        outs[3].append(mv_p.reshape(bp, n_mem, MEM_HEADS, MEM_HEAD_DIM))
        outs[4].append(ckv_s.reshape(bs, ss, -1))
        outs[5].append(kr_s.reshape(bs, ss, -1))
        outs[6].append(vg_s.reshape(bs, ss, -1))
    return (xp.reshape(bp, sp, d), xs.reshape(bs, ss, d)) + tuple(jnp.stack(o) for o in outs)
```

```python
import functools

import jax
import jax.numpy as jnp
import numpy as np
from jax import lax
from jax.experimental import pallas as pl
from jax.experimental.pallas import tpu as pltpu

CHUNK = 64
EPS = 1e-6
MLA_HEADS = 16
QK_NOPE = 64
QK_ROPE = 32
V_HEAD = 64
ROPE_THETA = 10000.0
MLA_SCALE = (QK_NOPE + QK_ROPE) ** -0.5
GM_CHUNK = 128
GM_GROUPS = 8
MEM_HEADS = 4
MEM_HEAD_DIM = 128
MEM_SCALE = MEM_HEAD_DIM ** -0.5
PEER_HEADS = 8
N_KEYS = 128
PEER_TOPK = 16
PEER_HALF = 128

LANES = 128
SLOT = LANES
ROPE_LO = QK_NOPE
ROPE_HALF = QK_ROPE // 2
ROW_TILE = 512
ATTN_TILE = 512
LOG2_E = float(np.log2(np.e))
PEER_ROWS = 512
PEER_EXPERTS = 1024
PEER_CHUNK = 512
ROUTE_LANES = 512
MASK_PITCH = N_KEYS + 8
MASK_UNROLL = 32
VMEM_LIMIT = 48 * 1024 * 1024
PEER_VMEM_LIMIT = 60 * 1024 * 1024

F32 = jnp.float32
BF16 = jnp.bfloat16
_NT = (((1,), (1,)), ((), ()))


def _dot(a, b):
    return jnp.dot(a, b, preferred_element_type=F32)


def _dot_nt(a, b):
    return lax.dot_general(a, b, _NT, preferred_element_type=F32)


def _rms(xf, g):
    return xf * lax.rsqrt(jnp.mean(xf * xf, axis=-1, keepdims=True) + EPS) * g


def _gelu(x):
    return 0.5 * x * (1.0 + lax.erf(x * np.float32(np.sqrt(0.5))))


def _group_mean(sq, m_ref):
    hi = sq.astype(BF16)
    lo = (sq - hi.astype(F32)).astype(BF16)
    m = m_ref[...]
    return _dot(hi, m) + _dot(lo, m)


def _rope_slot(y, cos, sin_up, sin_dn):
    return (y * cos + pltpu.roll(y, ROPE_HALF, 1) * sin_up
            + pltpu.roll(y, SLOT - ROPE_HALF, 1) * sin_dn)


def _const_spec(shape):
    nd = len(shape)
    return pl.BlockSpec(shape, lambda *_: (0,) * nd, pipeline_mode=pl.Buffered(1))


def _row_spec(tm, width):
    return pl.BlockSpec((tm, width), lambda i: (i, 0))


def _params(sem, limit=VMEM_LIMIT):
    return pltpu.CompilerParams(dimension_semantics=sem, vmem_limit_bytes=limit)


def _in_proj_kernel(x_ref, gmix_ref, wcq_ref, wckv_ref, wkr_ref, wu_ref, wv_ref, wga_ref, wgb_ref,
                    gq_ref, gkv_ref, gkr_ref, ggm_ref, cos_ref, sup_ref, sdn_ref,
                    cq_out, ckv_out, kr_out, u_out, vg_out, siga_out, sigb_out):
    h = _rms(x_ref[...], gmix_ref[...]).astype(BF16)
    cq_out[...] = _rms(_dot(h, wcq_ref[...]), gq_ref[...]).astype(BF16)
    ckv_out[...] = _rms(_dot(h, wckv_ref[...]), gkv_ref[...])
    kr = _dot(h, wkr_ref[...])
    ms = jnp.sum(kr * kr, axis=-1, keepdims=True) * (1.0 / QK_ROPE)
    kr = kr * lax.rsqrt(ms + EPS) * gkr_ref[...]
    kr_out[...] = _rope_slot(kr, cos_ref[...], sup_ref[...], sdn_ref[...])
    u_out[...] = _gelu(_dot(h, wu_ref[...])).astype(BF16)
    vg_out[...] = _rms(_gelu(_dot(h, wv_ref[...])), ggm_ref[...])
    siga_out[...] = jax.nn.sigmoid(_dot(h, wga_ref[...])).astype(BF16)
    sigb_out[...] = jax.nn.sigmoid(_dot(h, wgb_ref[...])).astype(BF16)


def _in_proj(x2d, w, tabs, tm):
    t, d = x2d.shape
    cos, sup, sdn = tabs
    ntab = cos.shape[0] // tm
    tab_spec = pl.BlockSpec((tm, SLOT), lambda i: (i % ntab, 0))
    consts = [w['g_mix'], w['w_cq'], w['w_ckv'], w['w_kr'], w['w_u'], w['w_v'], w['w_ga'], w['w_gb'],
              w['g_q_lat'], w['g_kv_lat'], w['g_kr'], w['g_gm']]
    gw = w['w_u'].shape[1]
    widths = [w['w_cq'].shape[1], w['w_ckv'].shape[1], SLOT, gw, gw, d, d]
    dtypes = [BF16, F32, F32, BF16, F32, BF16, BF16]
    return pl.pallas_call(
        _in_proj_kernel,
        grid=(t // tm,),
        in_specs=[_row_spec(tm, d)] + [_const_spec(c.shape) for c in consts] + [tab_spec] * 3,
        out_specs=[_row_spec(tm, n) for n in widths],
        out_shape=[jax.ShapeDtypeStruct((t, n), dt) for n, dt in zip(widths, dtypes)],
        compiler_params=_params(("parallel",)),
        name="in_proj",
    )(x2d, *consts, cos, sup, sdn)


def _gmlp_kernel(u_ref, vg_ref, sigb_ref, mix_ref, bias_ref, wob_ref, out_ref):
    vgb = vg_ref[...].astype(BF16)
    parts = []
    for g in range(GM_GROUPS):
        sl = slice(g * LANES, (g + 1) * LANES)
        mixed = _dot(mix_ref[g], vgb[:, sl]) + bias_ref[:, sl]
        parts.append((u_ref[:, sl].astype(F32) * mixed).astype(BF16))
    ob = _dot(jnp.concatenate(parts, axis=1), wob_ref[...])
    out_ref[...] = sigb_ref[...].astype(F32) * ob


def _gmlp(u, vg, sigb, mix, bias, w_ob, tm):
    t, gw = u.shape
    d = w_ob.shape[1]
    return pl.pallas_call(
        _gmlp_kernel,
        grid=(t // tm,),
        in_specs=[_row_spec(tm, gw), _row_spec(tm, gw), _row_spec(tm, d),
                  _const_spec(mix.shape), _const_spec(bias.shape), _const_spec(w_ob.shape)],
        out_specs=_row_spec(tm, d),
        out_shape=jax.ShapeDtypeStruct((t, d), F32),
        compiler_params=_params(("parallel",)),
        name="gmlp",
    )(u, vg, sigb, mix, bias, w_ob)


def _q_proj_kernel(cq_ref, wuq_ref, gq_ref, mavg_ref, cos_ref, sup_ref, sdn_ref, q_out):
    q = _dot(cq_ref[...], wuq_ref[...])
    cos, sup, sdn = cos_ref[...], sup_ref[...], sdn_ref[...]
    for h in range(MLA_HEADS):
        sl = slice(h * SLOT, (h + 1) * SLOT)
        qs = q[:, sl]
        y = qs * lax.rsqrt(_group_mean(qs * qs, mavg_ref) + EPS) * gq_ref[...]
        q_out[:, sl] = _rope_slot(y, cos, sup, sdn).astype(BF16)


def _q_proj(cq, w, tabs, tm):
    t, ql = cq.shape
    cos, sup, sdn = tabs
    ntab = cos.shape[0] // tm
    tab_spec = pl.BlockSpec((tm, SLOT), lambda i: (i % ntab, 0))
    n = MLA_HEADS * SLOT
    return pl.pallas_call(
        _q_proj_kernel,
        grid=(t // tm,),
        in_specs=[_row_spec(tm, ql), _const_spec(w['w_uq'].shape), _const_spec(w['g_q'].shape),
                  _const_spec(w['m_avg'].shape)] + [tab_spec] * 3,
        out_specs=_row_spec(tm, n),
        out_shape=jax.ShapeDtypeStruct((t, n), BF16),
        compiler_params=_params(("parallel",)),
        name="q_proj",
    )(cq, w['w_uq'], w['g_q'], w['m_avg'], cos, sup, sdn)


def _kv_proj_kernel(ckv_ref, kr_ref, wuk_ref, wuv_ref, gkn_ref, mavg_ref, k_out, v_out):
    c = ckv_ref[...].astype(BF16)
    kn = _dot(c, wuk_ref[...])
    kr = kr_ref[...]
    for h in range(MLA_HEADS):
        sl = slice(h * SLOT, (h + 1) * SLOT)
        ks = kn[:, sl]
        y = ks * lax.rsqrt(_group_mean(ks * ks, mavg_ref) + EPS) * gkn_ref[...]
        k_out[:, sl] = (y + kr).astype(BF16)
    v_out[...] = _dot(c, wuv_ref[...]).astype(BF16)


def _kv_proj(ckv, kr, w, tm):
    t, kl = ckv.shape
    n = MLA_HEADS * SLOT
    return pl.pallas_call(
        _kv_proj_kernel,
        grid=(t // tm,),
        in_specs=[_row_spec(tm, kl), _row_spec(tm, SLOT), _const_spec(w['w_uk'].shape),
                  _const_spec(w['w_uv'].shape), _const_spec(w['g_kn'].shape),
                  _const_spec(w['m_avg'].shape)],
        out_specs=[_row_spec(tm, n), _row_spec(tm, n)],
        out_shape=[jax.ShapeDtypeStruct((t, n), BF16)] * 2,
        compiler_params=_params(("parallel",)),
        name="kv_proj",
    )(ckv, kr, w['w_uk'], w['w_uv'], w['g_kn'], w['m_avg'])


def _attn_prompt_kernel(q_ref, k_ref, v_ref, o_ref):
    tile = q_ref.shape[1]
    qi = pl.program_id(2)
    q = q_ref[0]
    first = lax.broadcasted_iota(jnp.int32, (1, LANES), 1) < V_HEAD
    row_chunk = lax.broadcasted_iota(jnp.int32, (tile, tile), 0) // CHUNK
    col_chunk = lax.broadcasted_iota(jnp.int32, (tile, tile), 1) // CHUNK
    visible = col_chunk <= row_chunk

    def step(kstart, carry, masked):
        ms, ls, acc = carry
        kstart = pl.multiple_of(kstart, tile)
        k = k_ref[0, pl.ds(kstart, tile), :]
        v = v_ref[0, pl.ds(kstart, tile), :]
        new_ms, new_ls, alphas, pvs = [], [], [], []
        for h in range(2):
            sl = slice(h * SLOT, (h + 1) * SLOT)
            s = _dot_nt(q[:, sl], k[:, sl]) * F32(MLA_SCALE * LOG2_E)
            if masked:
                s = jnp.where(visible, s, F32(-1e30))
            m_new = jnp.maximum(ms[h], jnp.max(s, axis=-1, keepdims=True))
            alpha = jnp.exp2(ms[h] - m_new)
            p = jnp.exp2(s - m_new)
            new_ls.append(alpha * ls[h] + jnp.sum(p, axis=-1, keepdims=True))
            new_ms.append(m_new)
            alphas.append(alpha)
            pvs.append(_dot(p.astype(BF16), v[:, sl]))
        acc = acc * jnp.where(first, alphas[0], alphas[1]) + pvs[0] + pvs[1]
        return tuple(new_ms), tuple(new_ls), acc

    neg = jnp.full((tile, 1), -1e30, F32)
    zero = jnp.zeros((tile, 1), F32)
    init = ((neg, neg), (zero, zero), jnp.zeros((tile, LANES), F32))
    carry = lax.fori_loop(0, qi, lambda i, c: step(i * tile, c, False), init)
    _, ls, acc = step(qi * tile, carry, True)
    o_ref[0] = (acc / jnp.where(first, ls[0], ls[1])).astype(BF16)


def _attn_prompt(q, k, v, tile):
    b, s, _ = q.shape
    pairs = MLA_HEADS // 2
    return pl.pallas_call(
        _attn_prompt_kernel,
        grid=(b, pairs, s // tile),
        in_specs=[pl.BlockSpec((1, tile, 2 * SLOT), lambda bi, p, i: (bi, i, p)),
                  pl.BlockSpec((1, s, 2 * SLOT), lambda bi, p, i: (bi, 0, p)),
                  pl.BlockSpec((1, s, 2 * SLOT), lambda bi, p, i: (bi, 0, p))],
        out_specs=pl.BlockSpec((1, tile, LANES), lambda bi, p, i: (bi, i, p)),
        out_shape=jax.ShapeDtypeStruct((b, s, pairs * LANES), BF16),
        compiler_params=_params(("parallel", "parallel", "arbitrary")),
        name="attn_prompt",
    )(q, k, v)


def _attn_sample_kernel(q_ref, pckv_ref, pkr_ref, nckv_ref, nkr_ref, wuk_ref, wuv_ref, gkn_ref, place_ref,
                        o_ref):
    sq = q_ref.shape[1]
    rows = MLA_HEADS * sq
    gkn = gkn_ref[...]

    def keys_values(ckv, kr_slot):
        c = ckv.astype(BF16)
        kn = _dot(c, wuk_ref[...])
        parts = []
        for h in range(MLA_HEADS):
            ks = kn[:, h * SLOT:(h + 1) * SLOT]
            ms = jnp.sum(ks * ks, axis=-1, keepdims=True) * (1.0 / QK_NOPE)
            parts.append((ks * lax.rsqrt(ms + EPS) * gkn + kr_slot).astype(BF16))
        return jnp.concatenate(parts, axis=1), _dot(c, wuv_ref[...]).astype(BF16)

    past_kr = _dot(pkr_ref[0].astype(BF16), place_ref[...])
    k_p, v_p = keys_values(pckv_ref[0], past_kr)
    k_n, v_n = keys_values(nckv_ref[0], nkr_ref[0])
    k = jnp.concatenate([k_p, k_n], axis=0)
    v = jnp.concatenate([v_p, v_n], axis=0)

    qt = jnp.concatenate([q_ref[0].astype(F32)] * MLA_HEADS, axis=0)
    q_shape = (rows, MLA_HEADS * SLOT)
    own = (lax.broadcasted_iota(jnp.int32, q_shape, 0) // sq
           == lax.broadcasted_iota(jnp.int32, q_shape, 1) // SLOT)
    qbd = jnp.where(own, qt, 0.0).astype(BF16)
    s = _dot_nt(qbd, k) * MLA_SCALE
    p = jnp.exp(s - jnp.max(s, axis=-1, keepdims=True))
    p = p / jnp.sum(p, axis=-1, keepdims=True)
    o_all = _dot(p.astype(BF16), v)
    o_shape = o_all.shape
    own = (lax.broadcasted_iota(jnp.int32, o_shape, 0) // sq
           == lax.broadcasted_iota(jnp.int32, o_shape, 1) // V_HEAD)
    o_all = jnp.where(own, o_all, 0.0)
    o = o_all[0:sq]
    for h in range(1, MLA_HEADS):
        o = o + o_all[h * sq:(h + 1) * sq]
    o_ref[0] = o.astype(BF16)


def _attn_sample(q, past_ckv, past_kr, new_ckv, new_kr, w):
    b, sq, nq = q.shape
    npast, kl = past_ckv.shape[1:]
    nv = w['w_uv_plain'].shape[1]
    stream = lambda n, width: pl.BlockSpec((1, n, width), lambda bi: (bi, 0, 0))
    consts = [w['w_uk'], w['w_uv_plain'], w['g_kn'], w['rope_place']]
    return pl.pallas_call(
        _attn_sample_kernel,
        grid=(b,),
        in_specs=[stream(sq, nq), stream(npast, kl), stream(npast, QK_ROPE), stream(sq, kl), stream(sq, SLOT)]
        + [_const_spec(c.shape) for c in consts],
        out_specs=stream(sq, nv),
        out_shape=jax.ShapeDtypeStruct((b, sq, nv), BF16),
        compiler_params=_params(("parallel",)),
        name="attn_sample",
    )(q, past_ckv, past_kr, new_ckv, new_kr, *consts)


def _merge_kernel(x_ref, o_ref, siga_ref, gbob_ref, woa_ref, wo_ref, gx_ref, wcq_ref, gcq_ref,
                  x1_out, qc_out):
    oa = _dot(o_ref[...], woa_ref[...])
    merged = siga_ref[...].astype(F32) * oa + gbob_ref[...]
    x1 = x_ref[...] + _dot(merged.astype(BF16), wo_ref[...])
    x1_out[...] = x1
    qc = _dot(_rms(x1, gx_ref[...]).astype(BF16), wcq_ref[...])
    for h in range(MEM_HEADS):
        sl = slice(h * MEM_HEAD_DIM, (h + 1) * MEM_HEAD_DIM)
        qc_out[:, sl] = _rms(qc[:, sl], gcq_ref[...]).astype(BF16)


def _merge(x2d, o, siga, gbob, w, tm):
    t, d = x2d.shape
    consts = [w['w_oa'], w['w_o'], w['g_xattn'], w['w_cq_mem'], w['g_cq']]
    nq = w['w_cq_mem'].shape[1]
    return pl.pallas_call(
        _merge_kernel,
        grid=(t // tm,),
        in_specs=[_row_spec(tm, d), _row_spec(tm, o.shape[1]), _row_spec(tm, d), _row_spec(tm, d)]
        + [_const_spec(c.shape) for c in consts],
        out_specs=[_row_spec(tm, d), _row_spec(tm, nq)],
        out_shape=[jax.ShapeDtypeStruct((t, d), F32), jax.ShapeDtypeStruct((t, nq), BF16)],
        compiler_params=_params(("parallel",)),
        name="merge",
    )(x2d, o, siga, gbob, *consts)


def _mem_kv_kernel(mem_ref, gmem_ref, wck_ref, gck_ref, wcv_ref, k_out, v_out):
    m = _rms(mem_ref[...], gmem_ref[...]).astype(BF16)
    k = _dot(m, wck_ref[...])
    for h in range(MEM_HEADS):
        sl = slice(h * MEM_HEAD_DIM, (h + 1) * MEM_HEAD_DIM)
        k_out[:, sl] = _rms(k[:, sl], gck_ref[...])
    v_out[...] = _dot(m, wcv_ref[...])


def _mem_kv(mem2d, w, tm):
    t, d = mem2d.shape
    consts = [w['g_mem'], w['w_ck'], w['g_ck'], w['w_cv']]
    n = w['w_ck'].shape[1]
    return pl.pallas_call(
        _mem_kv_kernel,
        grid=(t // tm,),
        in_specs=[_row_spec(tm, d)] + [_const_spec(c.shape) for c in consts],
        out_specs=[_row_spec(tm, n)] * 2,
        out_shape=[jax.ShapeDtypeStruct((t, n), F32)] * 2,
        compiler_params=_params(("parallel",)),
        name="mem_kv",
    )(mem2d, *consts)


def _cross_kernel(q_ref, k_ref, v_ref, o_ref):
    q = q_ref[0]
    k = k_ref[0].astype(BF16)
    v = v_ref[0].astype(BF16)
    for h in range(MEM_HEADS):
        sl = slice(h * MEM_HEAD_DIM, (h + 1) * MEM_HEAD_DIM)
        s = _dot_nt(q[:, sl], k[:, sl]) * MEM_SCALE
        p = jnp.exp(s - jnp.max(s, axis=-1, keepdims=True))
        p = p / jnp.sum(p, axis=-1, keepdims=True)
        o_ref[0, :, sl] = _dot(p.astype(BF16), v[:, sl]).astype(BF16)


def _cross(qc, mk, mv, tc):
    b, s, n = qc.shape
    nm = mk.shape[1]
    return pl.pallas_call(
        _cross_kernel,
        grid=(b, s // tc),
        in_specs=[pl.BlockSpec((1, tc, n), lambda bi, i: (bi, i, 0)),
                  pl.BlockSpec((1, nm, n), lambda bi, i: (bi, 0, 0)),
                  pl.BlockSpec((1, nm, n), lambda bi, i: (bi, 0, 0))],
        out_specs=pl.BlockSpec((1, tc, n), lambda bi, i: (bi, i, 0)),
        out_shape=jax.ShapeDtypeStruct((b, s, n), BF16),
        compiler_params=_params(("parallel", "parallel")),
        name="cross_attn",
    )(qc, mk, mv)


def _cross_out_kernel(x1_ref, oc_ref, wco_ref, gffn_ref, x2_out, h_out):
    x2 = x1_ref[...] + _dot(oc_ref[...], wco_ref[...])
    x2_out[...] = x2
    h_out[...] = _rms(x2, gffn_ref[...]).astype(BF16)


def _cross_out(x1, oc, w, tm):
    t, d = x1.shape
    return pl.pallas_call(
        _cross_out_kernel,
        grid=(t // tm,),
        in_specs=[_row_spec(tm, d), _row_spec(tm, oc.shape[1]), _const_spec(w['w_co'].shape),
                  _const_spec(w['g_ffn'].shape)],
        out_specs=[_row_spec(tm, d)] * 2,
        out_shape=[jax.ShapeDtypeStruct((t, d), F32), jax.ShapeDtypeStruct((t, d), BF16)],
        compiler_params=_params(("parallel",)),
        name="cross_out",
    )(x1, oc, w['w_co'], w['g_ffn'])


def _pair_blocks():
    k = PEER_TOPK
    blocks, flat = [], []
    n_row = sum(1 for a in range(k) if k // (a + 1) > 2)
    for a in range(n_row):
        nb = k if a == 0 else 8
        blocks.append(('row', a, nb))
        flat += [a * k + b if (a + 1) * (b + 1) <= k else -1 for b in range(nb)]
    for b in range(k // (n_row + 1)):
        na = k if b == 0 else 8
        blocks.append(('col', b, na))
        flat += [a * k + b if (a >= n_row and (a + 1) * (b + 1) <= k) else -1 for a in range(na)]
    return blocks, np.asarray(flat, np.float32)


def _pair_candidates(v0, v1, blocks, combine):
    parts = []
    for kind, fixed, n in blocks:
        if kind == 'row':
            parts.append(combine(v0[fixed:fixed + 1, :], v1[0:n, :]))
        else:
            parts.append(combine(v0[0:n, :], v1[fixed:fixed + 1, :]))
    return jnp.concatenate(parts, axis=0)


def _peer_route_kernel(h_ref, wpq_ref, keys_ref, flat_ref, i_out, j_out, g_out,
                       q_scr, sv_scr, si_scr, ts_scr, te_scr, it_scr, jt_scr, gt_scr):
    tm = h_ref.shape[0]
    width = sv_scr.shape[2]
    blocks, _ = _pair_blocks()
    q_scr[...] = _dot(h_ref[...], wpq_ref[...]).astype(BF16)
    rows_keys = lax.broadcasted_iota(jnp.int32, (N_KEYS, width), 0).astype(F32)
    neg_inf = F32(-jnp.inf)

    def group_body(c, _):
        t0 = pl.multiple_of(c * width, width)

        def slot_body(s, _):
            d0 = pl.multiple_of(s * PEER_HALF, PEER_HALF)
            a = _dot_nt(keys_ref[s], q_scr[pl.ds(t0, width), pl.ds(d0, PEER_HALF)])
            for kk in range(PEER_TOPK):
                m = jnp.max(a, axis=0, keepdims=True)
                idx = jnp.min(jnp.where(a == m, rows_keys, F32(N_KEYS)), axis=0, keepdims=True)
                a = jnp.where(rows_keys == idx, neg_inf, a)
                sv_scr[s, kk:kk + 1, :] = m
                si_scr[s, kk:kk + 1, :] = idx
            return 0

        lax.fori_loop(0, 2 * PEER_HEADS, slot_body, 0)

        def head_body(hd, _):
            sv0, sv1 = sv_scr[2 * hd], sv_scr[2 * hd + 1]
            si0, si1 = si_scr[2 * hd], si_scr[2 * hd + 1]
            flat = flat_ref[...]
            cand = _pair_candidates(sv0, sv1, blocks, lambda x, y: x + y)
            cand = jnp.where(flat >= 0.0, cand, neg_inf)
            cidx = _pair_candidates(si0, si1, blocks, lambda x, y: x * F32(N_KEYS) + y)
            for kk in range(PEER_TOPK):
                m = jnp.max(cand, axis=0, keepdims=True)
                pos = jnp.min(jnp.where(cand == m, flat, F32(PEER_TOPK * PEER_TOPK)), axis=0, keepdims=True)
                sel = flat == pos
                ts_scr[kk:kk + 1, :] = m
                te_scr[kk:kk + 1, :] = jnp.max(jnp.where(sel, cidx, F32(-1.0)), axis=0, keepdims=True)
                cand = jnp.where(sel, neg_inf, cand)
            top = ts_scr[...]
            ex = jnp.exp(top - top[0:1, :])
            gate = ex / jnp.sum(ex, axis=0, keepdims=True)
            eidx = te_scr[...]
            irow = jnp.floor(eidx * F32(1.0 / N_KEYS))
            r0 = pl.multiple_of(hd * PEER_TOPK, PEER_TOPK)
            it_scr[pl.ds(r0, PEER_TOPK), :] = irow
            jt_scr[pl.ds(r0, PEER_TOPK), :] = eidx - irow * F32(N_KEYS)
            gt_scr[pl.ds(r0, PEER_TOPK), :] = gate
            return 0

        lax.fori_loop(0, PEER_HEADS, head_body, 0)
        i_out[pl.ds(t0, width), :] = it_scr[...].T
        j_out[pl.ds(t0, width), :] = jt_scr[...].T
        g_out[pl.ds(t0, width), :] = gt_scr[...].T
        return 0

    lax.fori_loop(0, tm // width, group_body, 0)


def _peer_route(h, w, tm):
    t, d = h.shape
    nsel = PEER_HEADS * PEER_TOPK
    width = ROUTE_LANES
    _, flat = _pair_blocks()
    flat = jnp.asarray(np.repeat(flat[:, None], width, axis=1))
    return pl.pallas_call(
        _peer_route_kernel,
        grid=(t // tm,),
        in_specs=[_row_spec(tm, d), _const_spec(w['w_pq'].shape), _const_spec(w['sub_keys'].shape),
                  _const_spec(flat.shape)],
        out_specs=[_row_spec(tm, nsel)] * 3,
        out_shape=[jax.ShapeDtypeStruct((t, nsel), F32)] * 3,
        scratch_shapes=[pltpu.VMEM((tm, w['w_pq'].shape[1]), BF16),
                        pltpu.VMEM((2 * PEER_HEADS, PEER_TOPK, width), F32),
                        pltpu.VMEM((2 * PEER_HEADS, PEER_TOPK, width), F32),
                        pltpu.VMEM((PEER_TOPK, width), F32),
                        pltpu.VMEM((PEER_TOPK, width), F32),
                        pltpu.VMEM((nsel, width), F32),
                        pltpu.VMEM((nsel, width), F32),
                        pltpu.VMEM((nsel, width), F32)],
        compiler_params=_params(("parallel",)),
        name="peer_route",
    )(h, w['w_pq'], w['sub_keys'], flat)


def _peer_dense_kernel(h_ref, i_ref, j_ref, g_ref, x_ref, u_ref, v_ref, out_ref, mask_scr):
    tm = h_ref.shape[0]
    te = u_ref.shape[0]
    e = pl.program_id(1)

    @pl.when(e == 0)
    def _():
        out_ref[...] = x_ref[...]
        key_rows = lax.broadcasted_iota(jnp.int32, (N_KEYS, LANES), 0).astype(F32)

        def token_body(t, _):
            irow = i_ref[pl.ds(t, 1), :]
            jrow = j_ref[pl.ds(t, 1), :]
            grow = g_ref[pl.ds(t, 1), :]
            rw = jnp.where(key_rows == irow, grow, 0.0).astype(BF16)
            cw = jnp.where(key_rows == jrow, 1.0, 0.0).astype(BF16)
            r0 = pl.multiple_of(t * MASK_PITCH, 8)
            mask_scr[pl.ds(r0, N_KEYS), :] = _dot_nt(rw, cw)
            return 0

        lax.fori_loop(0, tm, token_body, 0, unroll=MASK_UNROLL)

    h = h_ref[...]
    row0 = e * (te // N_KEYS)
    total = None
    for c in range(te // PEER_CHUNK):
        rows = slice(c * PEER_CHUNK, (c + 1) * PEER_CHUNK)
        act = _gelu(_dot_nt(h, u_ref[rows, :]))
        mask = jnp.concatenate(
            [mask_scr[pl.ds(row0 + c * (PEER_CHUNK // N_KEYS) + r, tm, stride=MASK_PITCH), :]
             for r in range(PEER_CHUNK // N_KEYS)], axis=1)
        part = _dot((act * mask).astype(BF16), v_ref[rows, :])
        total = part if total is None else total + part
    out_ref[...] += total


def _peer_dense(h, isel, jsel, gate, x2, u, v, tm, te):
    t, d = h.shape
    nexp = v.shape[0]
    nsel = isel.shape[1]
    row = lambda width: pl.BlockSpec((tm, width), lambda i, e: (i, 0), pipeline_mode=pl.Buffered(1))
    return pl.pallas_call(
        _peer_dense_kernel,
        grid=(t // tm, nexp // te),
        in_specs=[row(d), row(nsel), row(nsel), row(nsel), row(d),
                  pl.BlockSpec((te, d), lambda i, e: (e, 0)),
                  pl.BlockSpec((te, d), lambda i, e: (e, 0))],
        out_specs=pl.BlockSpec((tm, d), lambda i, e: (i, 0)),
        out_shape=jax.ShapeDtypeStruct((t, d), F32),
        scratch_shapes=[pltpu.VMEM((tm * MASK_PITCH, LANES), F32)],
        compiler_params=_params(("parallel", "arbitrary"), PEER_VMEM_LIMIT),
        name="peer_dense",
    )(h, isel, jsel, gate, x2, u, v)


def _slot_gain(parts):
    row = jnp.zeros((SLOT,), F32)
    for off, g in parts:
        row = lax.dynamic_update_slice(row, g.astype(F32), (off,))
    return row[None, :]


def _group_avg_matrix():
    m = np.zeros((SLOT, SLOT), np.float32)
    m[:QK_NOPE, :QK_NOPE] = 1.0 / QK_NOPE
    m[ROPE_LO:ROPE_LO + QK_ROPE, ROPE_LO:ROPE_LO + QK_ROPE] = 1.0 / QK_ROPE
    return jnp.asarray(m, BF16)


def _rope_tables(pos):
    inv = ROPE_THETA ** (-jnp.arange(ROPE_HALF, dtype=F32) / ROPE_HALF)
    ang = pos.astype(F32)[:, None] * inv[None, :]
    cos, sin = jnp.cos(ang), jnp.sin(ang)
    n = pos.shape[0]
    tail = SLOT - ROPE_LO - QK_ROPE
    cos_t = jnp.concatenate([jnp.ones((n, ROPE_LO), F32), cos, cos, jnp.zeros((n, tail), F32)], axis=1)
    sin_up = jnp.concatenate([jnp.zeros((n, ROPE_LO + ROPE_HALF), F32), sin, jnp.zeros((n, tail), F32)], axis=1)
    sin_dn = jnp.concatenate([jnp.zeros((n, ROPE_LO), F32), -sin, jnp.zeros((n, tail + ROPE_HALF), F32)], axis=1)
    return cos_t, sin_up, sin_dn


def _prep_weights(p, i):
    w_in = p['w_in'][i]
    q_lora = p['g_q_lat'].shape[1]
    kv_lora = p['g_kv_lat'].shape[1]
    gw = p['g_gm'].shape[1]
    d = w_in.shape[0]
    o = np.cumsum([0, q_lora, kv_lora, QK_ROPE, gw, gw, d, d])
    seg = lambda k: w_in[:, o[k]:o[k + 1]]
    row = lambda a: a[i][None, :].astype(F32)
    w = {}
    w['g_mix'] = row(p['g_mix'])
    w['w_cq'] = seg(0).astype(BF16)
    w['w_ckv'] = seg(1).astype(BF16)
    w['w_kr'] = jnp.pad(seg(2), ((0, 0), (ROPE_LO, SLOT - ROPE_LO - QK_ROPE))).astype(BF16)
    w['w_u'] = seg(3).astype(BF16)
    w['w_v'] = seg(4).astype(BF16)
    w['w_ga'] = seg(5).astype(BF16)
    w['w_gb'] = seg(6).astype(BF16)
    w['g_q_lat'] = row(p['g_q_lat'])
    w['g_kv_lat'] = row(p['g_kv_lat'])
    w['g_kr'] = _slot_gain([(ROPE_LO, p['g_kr'][i])])
    w['g_gm'] = row(p['g_gm'])
    head_dim = QK_NOPE + QK_ROPE
    w_uq = p['w_uq'][i].reshape(q_lora, MLA_HEADS, head_dim)
    w['w_uq'] = jnp.pad(w_uq, ((0, 0), (0, 0), (0, SLOT - head_dim))).reshape(q_lora, -1).astype(BF16)
    w['g_q'] = _slot_gain([(0, p['g_qn'][i]), (ROPE_LO, p['g_qr'][i])])
    w['m_avg'] = _group_avg_matrix()
    w_uk = p['w_uk'][i].reshape(kv_lora, MLA_HEADS, QK_NOPE)
    w['w_uk'] = jnp.pad(w_uk, ((0, 0), (0, 0), (0, SLOT - QK_NOPE))).reshape(kv_lora, -1).astype(BF16)
    w['g_kn'] = _slot_gain([(0, p['g_kn'][i])])
    w_uv = p['w_uv'][i].reshape(kv_lora, MLA_HEADS // 2, 2, V_HEAD)
    eye2 = jnp.eye(2, dtype=w_uv.dtype)
    w['w_uv'] = jnp.einsum('cpjd,jk->cpjkd', w_uv, eye2).reshape(kv_lora, -1).astype(BF16)
    w['w_uv_plain'] = p['w_uv'][i].astype(BF16)
    place = np.zeros((QK_ROPE, SLOT), np.float32)
    place[np.arange(QK_ROPE), ROPE_LO + np.arange(QK_ROPE)] = 1.0
    w['rope_place'] = jnp.asarray(place, BF16)
    w['w_oa'] = p['w_oa'][i].astype(BF16)
    w['w_ob'] = p['w_ob'][i].astype(BF16)
    w['w_o'] = p['w_o'][i].astype(BF16)
    w['g_xattn'] = row(p['g_xattn'])
    w['g_mem'] = row(p['g_mem'])
    w['w_cq_mem'] = p['w_cq'][i].astype(BF16)
    w['g_cq'] = row(p['g_cq'])
    w['w_ck'] = p['w_ck'][i].astype(BF16)
    w['g_ck'] = row(p['g_ck'])
    w['w_cv'] = p['w_cv'][i].astype(BF16)
    w['w_co'] = p['w_co'][i].astype(BF16)
    w['g_ffn'] = row(p['g_ffn'])
    w['w_pq'] = p['w_pq'][i].astype(BF16)
    w['sub_keys'] = p['sub_keys'][i].reshape(2 * PEER_HEADS, N_KEYS, PEER_HALF).astype(BF16)
    w['peer_u'] = p['peer_u'][i].astype(BF16)
    w['peer_v'] = p['peer_v'][i].astype(BF16)
    w['w_s'] = p['w_s'][i]
    w['b_s'] = p['b_s'][i]
    return w


def _spatial_operands(w, seq, tm):
    chunk = min(seq, GM_CHUNK)
    reps = tm // chunk
    w_mask = jnp.tril(w['w_s'][:, :chunk, :chunk])
    mix = jnp.einsum('ab,gts->gatbs', jnp.eye(reps, dtype=F32), w_mask).reshape(GM_GROUPS, tm, tm)
    bias = jnp.tile(w['b_s'][:, :chunk].T, (reps, 1))
    bias = jnp.repeat(bias, LANES, axis=1)
    return mix.astype(BF16), bias.astype(F32)


def _token_mixer(x2d, w, seq, pos, past, tm):
    t, d = x2d.shape
    nb = t // seq
    period = max(seq, tm)
    tabs = _rope_tables(jnp.tile(pos, period // seq))
    cq, ckv, kr, u, vg, siga, sigb = _in_proj(x2d, w, tabs, tm)
    mix, bias = _spatial_operands(w, seq, tm)
    gbob = _gmlp(u, vg, sigb, mix, bias, w['w_ob'], tm)
    q = _q_proj(cq, w, tabs, tm)
    if past is None:
        k, v = _kv_proj(ckv, kr, w, tm)
        o = _attn_prompt(q.reshape(nb, seq, -1), k.reshape(nb, seq, -1), v.reshape(nb, seq, -1), ATTN_TILE)
    else:
        past_ckv, past_kr = past
        o = _attn_sample(q.reshape(nb, seq, -1), past_ckv, past_kr, ckv.reshape(nb, seq, -1),
                         kr.reshape(nb, seq, SLOT), w)
    x1, qc = _merge(x2d, o.reshape(t, -1), siga, gbob, w, tm)
    return x1, qc, ckv, kr[:, ROPE_LO:ROPE_LO + QK_ROPE], vg


def _tail(x1, qc, mk, mv, w, nb, seq, tm):
    t, d = x1.shape
    oc = _cross(qc.reshape(nb, seq, -1), mk, mv, min(seq, tm))
    x2, h = _cross_out(x1, oc.reshape(t, -1), w, tm)
    isel, jsel, gate = _peer_route(h, w, tm)
    return _peer_dense(h, isel, jsel, gate, x2, w['peer_u'], w['peer_v'], PEER_ROWS, PEER_EXPERTS)


def kernel(x_prompt, x_sample, cache_mla_ckv, cache_mla_krope, cache_mem_k, cache_mem_v, mem_prompt, g_mix, w_in, g_q_lat, w_uq, g_qn, g_qr, g_kv_lat, g_kr, w_uk, w_uv, g_kn, w_oa, g_gm, w_s, b_s, w_ob, w_o, g_xattn, g_mem, w_cq, g_cq, w_ck, g_ck, w_cv, w_co, g_ffn, w_pq, sub_keys, peer_u, peer_v):
    params = dict(g_mix=g_mix, w_in=w_in, g_q_lat=g_q_lat, w_uq=w_uq, g_qn=g_qn, g_qr=g_qr,
                  g_kv_lat=g_kv_lat, g_kr=g_kr, w_uk=w_uk, w_uv=w_uv, g_kn=g_kn, w_oa=w_oa, g_gm=g_gm,
                  w_s=w_s, b_s=b_s, w_ob=w_ob, w_o=w_o, g_xattn=g_xattn, g_mem=g_mem, w_cq=w_cq,
                  g_cq=g_cq, w_ck=w_ck, g_ck=g_ck, w_cv=w_cv, w_co=w_co, g_ffn=g_ffn, w_pq=w_pq,
                  sub_keys=sub_keys, peer_u=peer_u, peer_v=peer_v)
    bp, sp, d = x_prompt.shape
    bs, ss, _ = x_sample.shape
    depth = w_in.shape[0]
    past_len = cache_mla_ckv.shape[2]
    n_mem = mem_prompt.shape[1]
    tm = ROW_TILE
    assert (bp * sp) % tm == 0 and (bs * ss) % tm == 0 and sp % ATTN_TILE == 0
    pos_p = jnp.arange(sp)
    pos_s = past_len + jnp.arange(ss)
    xp = x_prompt.reshape(bp * sp, d)
    xs = x_sample.reshape(bs * ss, d)
    outs = [[] for _ in range(7)]
    for i in range(depth):
        w = _prep_weights(params, i)
        xp1, qcp, ckv_p, kr_p, _ = _token_mixer(xp, w, sp, pos_p, None, tm)
        xs1, qcs, ckv_s, kr_s, vg_s = _token_mixer(xs, w, ss, pos_s, (cache_mla_ckv[i], cache_mla_krope[i]), tm)
        mk_p, mv_p = _mem_kv(mem_prompt.reshape(bp * n_mem, d), w, min(tm, bp * n_mem))
        xp = _tail(xp1, qcp, mk_p.reshape(bp, n_mem, -1), mv_p.reshape(bp, n_mem, -1), w, bp, sp, tm)
        xs = _tail(xs1, qcs, cache_mem_k[i].reshape(bs, n_mem, -1), cache_mem_v[i].reshape(bs, n_mem, -1),
                   w, bs, ss, tm)
        outs[0].append(ckv_p.reshape(bp, sp, -1))
        outs[1].append(kr_p.reshape(bp, sp, -1))
        outs[2].append(mk_p.reshape(bp, n_mem, MEM_HEADS, MEM_HEAD_DIM))
        outs[3].append(mv_p.reshape(bp, n_mem, MEM_HEADS, MEM_HEAD_DIM))
        outs[4].append(ckv_s.reshape(bs, ss, -1))
        outs[5].append(kr_s.reshape(bs, ss, -1))
        outs[6].append(vg_s.reshape(bs, ss, -1))
    return (xp.reshape(bp, sp, d), xs.reshape(bs, ss, d)) + tuple(jnp.stack(o) for o in outs)
```

```python
import functools

import jax
import jax.numpy as jnp
import numpy as np
from jax import lax
from jax.experimental import pallas as pl
from jax.experimental.pallas import tpu as pltpu

CHUNK = 64
EPS = 1e-6
MLA_HEADS = 16
QK_NOPE = 64
QK_ROPE = 32
V_HEAD = 64
ROPE_THETA = 10000.0
MLA_SCALE = (QK_NOPE + QK_ROPE) ** -0.5
GM_CHUNK = 128
GM_GROUPS = 8
MEM_HEADS = 4
MEM_HEAD_DIM = 128
MEM_SCALE = MEM_HEAD_DIM ** -0.5
PEER_HEADS = 8
N_KEYS = 128
PEER_TOPK = 16
PEER_HALF = 128

LANES = 128
SLOT = LANES
ROPE_LO = QK_NOPE
ROPE_HALF = QK_ROPE // 2
ROW_TILE = 512
ATTN_TILE = 512
LOG2_E = float(np.log2(np.e))
STATIC_SHIFT_LIMIT = 48.0
BOUND_MARGIN = 1.02
PEER_ROWS = 512
PEER_EXPERTS = 1024
PEER_CHUNK = 512
ROUTE_LANES = 512
MASK_PITCH = N_KEYS + 8
MASK_UNROLL = 32
VMEM_LIMIT = 48 * 1024 * 1024
PEER_VMEM_LIMIT = 60 * 1024 * 1024

F32 = jnp.float32
BF16 = jnp.bfloat16
_NT = (((1,), (1,)), ((), ()))


def _dot(a, b):
    return jnp.dot(a, b, preferred_element_type=F32)


def _dot_nt(a, b):
    return lax.dot_general(a, b, _NT, preferred_element_type=F32)


def _rms(xf, g):
    return xf * lax.rsqrt(jnp.mean(xf * xf, axis=-1, keepdims=True) + EPS) * g


def _gelu(x):
    return 0.5 * x * (1.0 + lax.erf(x * np.float32(np.sqrt(0.5))))


def _group_mean(sq, m_ref):
    hi = sq.astype(BF16)
    lo = (sq - hi.astype(F32)).astype(BF16)
    m = m_ref[...]
    return _dot(hi, m) + _dot(lo, m)


def _rope_slot(y, cos, sin_up, sin_dn):
    return (y * cos + pltpu.roll(y, ROPE_HALF, 1) * sin_up
            + pltpu.roll(y, SLOT - ROPE_HALF, 1) * sin_dn)


def _const_spec(shape):
    nd = len(shape)
    return pl.BlockSpec(shape, lambda *_: (0,) * nd, pipeline_mode=pl.Buffered(1))


def _row_spec(tm, width):
    return pl.BlockSpec((tm, width), lambda i: (i, 0))


def _params(sem, limit=VMEM_LIMIT):
    return pltpu.CompilerParams(dimension_semantics=sem, vmem_limit_bytes=limit)


def _in_proj_kernel(x_ref, gmix_ref, wcq_ref, wckv_ref, wkr_ref, wu_ref, wv_ref, wga_ref, wgb_ref,
                    gq_ref, gkv_ref, gkr_ref, ggm_ref, cos_ref, sup_ref, sdn_ref,
                    cq_out, ckv_out, kr_out, u_out, vg_out, siga_out, sigb_out):
    h = _rms(x_ref[...], gmix_ref[...]).astype(BF16)
    cq_out[...] = _rms(_dot(h, wcq_ref[...]), gq_ref[...]).astype(BF16)
    ckv_out[...] = _rms(_dot(h, wckv_ref[...]), gkv_ref[...])
    kr = _dot(h, wkr_ref[...])
    ms = jnp.sum(kr * kr, axis=-1, keepdims=True) * (1.0 / QK_ROPE)
    kr = kr * lax.rsqrt(ms + EPS) * gkr_ref[...]
    kr_out[...] = _rope_slot(kr, cos_ref[...], sup_ref[...], sdn_ref[...])
    u_out[...] = _gelu(_dot(h, wu_ref[...])).astype(BF16)
    vg_out[...] = _rms(_gelu(_dot(h, wv_ref[...])), ggm_ref[...])
    siga_out[...] = jax.nn.sigmoid(_dot(h, wga_ref[...])).astype(BF16)
    sigb_out[...] = jax.nn.sigmoid(_dot(h, wgb_ref[...])).astype(BF16)


def _in_proj(x2d, w, tabs, tm):
    t, d = x2d.shape
    cos, sup, sdn = tabs
    ntab = cos.shape[0] // tm
    tab_spec = pl.BlockSpec((tm, SLOT), lambda i: (i % ntab, 0))
    consts = [w['g_mix'], w['w_cq'], w['w_ckv'], w['w_kr'], w['w_u'], w['w_v'], w['w_ga'], w['w_gb'],
              w['g_q_lat'], w['g_kv_lat'], w['g_kr'], w['g_gm']]
    gw = w['w_u'].shape[1]
    widths = [w['w_cq'].shape[1], w['w_ckv'].shape[1], SLOT, gw, gw, d, d]
    dtypes = [BF16, F32, F32, BF16, F32, BF16, BF16]
    return pl.pallas_call(
        _in_proj_kernel,
        grid=(t // tm,),
        in_specs=[_row_spec(tm, d)] + [_const_spec(c.shape) for c in consts] + [tab_spec] * 3,
        out_specs=[_row_spec(tm, n) for n in widths],
        out_shape=[jax.ShapeDtypeStruct((t, n), dt) for n, dt in zip(widths, dtypes)],
        compiler_params=_params(("parallel",)),
        name="in_proj",
    )(x2d, *consts, cos, sup, sdn)


def _gmlp_kernel(u_ref, vg_ref, sigb_ref, mix_ref, bias_ref, wob_ref, out_ref):
    vgb = vg_ref[...].astype(BF16)
    parts = []
    for g in range(GM_GROUPS):
        sl = slice(g * LANES, (g + 1) * LANES)
        mixed = _dot(mix_ref[g], vgb[:, sl]) + bias_ref[:, sl]
        parts.append((u_ref[:, sl].astype(F32) * mixed).astype(BF16))
    ob = _dot(jnp.concatenate(parts, axis=1), wob_ref[...])
    out_ref[...] = sigb_ref[...].astype(F32) * ob


def _gmlp(u, vg, sigb, mix, bias, w_ob, tm):
    t, gw = u.shape
    d = w_ob.shape[1]
    return pl.pallas_call(
        _gmlp_kernel,
        grid=(t // tm,),
        in_specs=[_row_spec(tm, gw), _row_spec(tm, gw), _row_spec(tm, d),
                  _const_spec(mix.shape), _const_spec(bias.shape), _const_spec(w_ob.shape)],
        out_specs=_row_spec(tm, d),
        out_shape=jax.ShapeDtypeStruct((t, d), F32),
        compiler_params=_params(("parallel",)),
        name="gmlp",
    )(u, vg, sigb, mix, bias, w_ob)


def _q_proj_kernel(cq_ref, wuq_ref, gq_ref, mavg_ref, cos_ref, sup_ref, sdn_ref, q_out):
    q = _dot(cq_ref[...], wuq_ref[...])
    cos, sup, sdn = cos_ref[...], sup_ref[...], sdn_ref[...]
    for h in range(MLA_HEADS):
        sl = slice(h * SLOT, (h + 1) * SLOT)
        qs = q[:, sl]
        y = qs * lax.rsqrt(_group_mean(qs * qs, mavg_ref) + EPS) * gq_ref[...]
        q_out[:, sl] = _rope_slot(y, cos, sup, sdn).astype(BF16)


def _q_proj(cq, w, tabs, tm):
    t, ql = cq.shape
    cos, sup, sdn = tabs
    ntab = cos.shape[0] // tm
    tab_spec = pl.BlockSpec((tm, SLOT), lambda i: (i % ntab, 0))
    n = MLA_HEADS * SLOT
    return pl.pallas_call(
        _q_proj_kernel,
        grid=(t // tm,),
        in_specs=[_row_spec(tm, ql), _const_spec(w['w_uq'].shape), _const_spec(w['g_q'].shape),
                  _const_spec(w['m_avg'].shape)] + [tab_spec] * 3,
        out_specs=_row_spec(tm, n),
        out_shape=jax.ShapeDtypeStruct((t, n), BF16),
        compiler_params=_params(("parallel",)),
        name="q_proj",
    )(cq, w['w_uq'], w['g_q'], w['m_avg'], cos, sup, sdn)


def _kv_proj_kernel(ckv_ref, kr_ref, wuk_ref, wuv_ref, gkn_ref, mavg_ref, k_out, v_out):
    c = ckv_ref[...].astype(BF16)
    kn = _dot(c, wuk_ref[...])
    kr = kr_ref[...]
    for h in range(MLA_HEADS):
        sl = slice(h * SLOT, (h + 1) * SLOT)
        ks = kn[:, sl]
        y = ks * lax.rsqrt(_group_mean(ks * ks, mavg_ref) + EPS) * gkn_ref[...]
        k_out[:, sl] = (y + kr).astype(BF16)
    v_out[...] = _dot(c, wuv_ref[...]).astype(BF16)


def _kv_proj(ckv, kr, w, tm):
    t, kl = ckv.shape
    n = MLA_HEADS * SLOT
    return pl.pallas_call(
        _kv_proj_kernel,
        grid=(t // tm,),
        in_specs=[_row_spec(tm, kl), _row_spec(tm, SLOT), _const_spec(w['w_uk'].shape),
                  _const_spec(w['w_uv'].shape), _const_spec(w['g_kn'].shape),
                  _const_spec(w['m_avg'].shape)],
        out_specs=[_row_spec(tm, n), _row_spec(tm, n)],
        out_shape=[jax.ShapeDtypeStruct((t, n), BF16)] * 2,
        compiler_params=_params(("parallel",)),
        name="kv_proj",
    )(ckv, kr, w['w_uk'], w['w_uv'], w['g_kn'], w['m_avg'])


def _attn_prompt_kernel(bound_ref, q_ref, k_ref, v_ref, o_ref):
    tile = q_ref.shape[1]
    qi = pl.program_id(2)
    q = q_ref[0]
    bound = bound_ref[0]
    first = lax.broadcasted_iota(jnp.int32, (1, LANES), 1) < V_HEAD
    row_chunk = lax.broadcasted_iota(jnp.int32, (tile, tile), 0) // CHUNK
    col_chunk = lax.broadcasted_iota(jnp.int32, (tile, tile), 1) // CHUNK
    visible = col_chunk <= row_chunk

    def static_step(kstart, carry, masked):
        sums, acc = carry
        kstart = pl.multiple_of(kstart, tile)
        k = k_ref[0, pl.ds(kstart, tile), :]
        v = v_ref[0, pl.ds(kstart, tile), :]
        new_sums = []
        for h in range(2):
            sl = slice(h * SLOT, (h + 1) * SLOT)
            s = _dot_nt(q[:, sl], k[:, sl]) * F32(MLA_SCALE * LOG2_E) - bound
            if masked:
                s = jnp.where(visible, s, F32(-1e30))
            p = jnp.exp2(s)
            part = sums[h]
            for c in range(tile // LANES):
                part = part + p[:, c * LANES:(c + 1) * LANES]
            new_sums.append(part)
            acc = acc + _dot(p.astype(BF16), v[:, sl])
        return tuple(new_sums), acc

    @pl.when(bound < STATIC_SHIFT_LIMIT)
    def _():
        zeros = jnp.zeros((tile, LANES), F32)

        def pair(i, c):
            c = static_step(2 * i * tile, c, False)
            return static_step((2 * i + 1) * tile, c, False)

        def odd_tail(c):
            return static_step(qi * tile, static_step((qi - 1) * tile, c, False), True)

        carry = lax.fori_loop(0, qi // 2, pair, ((zeros, zeros), zeros))
        sums, acc = lax.cond(qi % 2 == 1, odd_tail, lambda c: static_step(qi * tile, c, True), carry)
        l0 = jnp.sum(sums[0], axis=-1, keepdims=True)
        l1 = jnp.sum(sums[1], axis=-1, keepdims=True)
        o_ref[0] = (acc / jnp.where(first, l0, l1)).astype(BF16)

    def step(kstart, carry, masked):
        ms, ls, acc = carry
        kstart = pl.multiple_of(kstart, tile)
        k = k_ref[0, pl.ds(kstart, tile), :]
        v = v_ref[0, pl.ds(kstart, tile), :]
        new_ms, new_ls, alphas, pvs = [], [], [], []
        for h in range(2):
            sl = slice(h * SLOT, (h + 1) * SLOT)
            s = _dot_nt(q[:, sl], k[:, sl]) * F32(MLA_SCALE * LOG2_E)
            if masked:
                s = jnp.where(visible, s, F32(-1e30))
            m_new = jnp.maximum(ms[h], jnp.max(s, axis=-1, keepdims=True))
            alpha = jnp.exp2(ms[h] - m_new)
            p = jnp.exp2(s - m_new)
            new_ls.append(alpha * ls[h] + jnp.sum(p, axis=-1, keepdims=True))
            new_ms.append(m_new)
            alphas.append(alpha)
            pvs.append(_dot(p.astype(BF16), v[:, sl]))
        acc = acc * jnp.where(first, alphas[0], alphas[1]) + pvs[0] + pvs[1]
        return tuple(new_ms), tuple(new_ls), acc

    @pl.when(bound >= STATIC_SHIFT_LIMIT)
    def _():
        neg = jnp.full((tile, 1), -1e30, F32)
        zero = jnp.zeros((tile, 1), F32)
        init = ((neg, neg), (zero, zero), jnp.zeros((tile, LANES), F32))
        carry = lax.fori_loop(0, qi, lambda i, c: step(i * tile, c, False), init)
        _, ls, acc = step(qi * tile, carry, True)
        o_ref[0] = (acc / jnp.where(first, ls[0], ls[1])).astype(BF16)


def _score_bound(w):
    gmax = lambda g, lo, n: jnp.max(jnp.abs(g[0, lo:lo + n]))
    nope = QK_NOPE * gmax(w['g_q'], 0, QK_NOPE) * gmax(w['g_kn'], 0, QK_NOPE)
    rope = QK_ROPE * gmax(w['g_q'], ROPE_LO, QK_ROPE) * gmax(w['g_kr'], ROPE_LO, QK_ROPE)
    return (BOUND_MARGIN * MLA_SCALE * LOG2_E * (nope + rope)).reshape(1).astype(F32)


def _attn_prompt(q, k, v, bound, tile):
    b, s, _ = q.shape
    pairs = MLA_HEADS // 2
    return pl.pallas_call(
        _attn_prompt_kernel,
        grid=(b, pairs, s // tile),
        in_specs=[pl.BlockSpec(memory_space=pltpu.SMEM),
                  pl.BlockSpec((1, tile, 2 * SLOT), lambda bi, p, i: (bi, i, p)),
                  pl.BlockSpec((1, s, 2 * SLOT), lambda bi, p, i: (bi, 0, p)),
                  pl.BlockSpec((1, s, 2 * SLOT), lambda bi, p, i: (bi, 0, p))],
        out_specs=pl.BlockSpec((1, tile, LANES), lambda bi, p, i: (bi, i, p)),
        out_shape=jax.ShapeDtypeStruct((b, s, pairs * LANES), BF16),
        compiler_params=_params(("parallel", "parallel", "arbitrary")),
        name="attn_prompt",
    )(bound, q, k, v)


def _attn_sample_kernel(q_ref, pckv_ref, pkr_ref, nckv_ref, nkr_ref, wuk_ref, wuv_ref, gkn_ref, place_ref,
                        o_ref):
    sq = q_ref.shape[1]
    rows = MLA_HEADS * sq
    gkn = gkn_ref[...]

    def keys_values(ckv, kr_slot):
        c = ckv.astype(BF16)
        kn = _dot(c, wuk_ref[...])
        parts = []
        for h in range(MLA_HEADS):
            ks = kn[:, h * SLOT:(h + 1) * SLOT]
            ms = jnp.sum(ks * ks, axis=-1, keepdims=True) * (1.0 / QK_NOPE)
            parts.append((ks * lax.rsqrt(ms + EPS) * gkn + kr_slot).astype(BF16))
        return jnp.concatenate(parts, axis=1), _dot(c, wuv_ref[...]).astype(BF16)

    past_kr = _dot(pkr_ref[0].astype(BF16), place_ref[...])
    k_p, v_p = keys_values(pckv_ref[0], past_kr)
    k_n, v_n = keys_values(nckv_ref[0], nkr_ref[0])
    k = jnp.concatenate([k_p, k_n], axis=0)
    v = jnp.concatenate([v_p, v_n], axis=0)

    qt = jnp.concatenate([q_ref[0].astype(F32)] * MLA_HEADS, axis=0)
    q_shape = (rows, MLA_HEADS * SLOT)
    own = (lax.broadcasted_iota(jnp.int32, q_shape, 0) // sq
           == lax.broadcasted_iota(jnp.int32, q_shape, 1) // SLOT)
    qbd = jnp.where(own, qt, 0.0).astype(BF16)
    s = _dot_nt(qbd, k) * MLA_SCALE
    p = jnp.exp(s - jnp.max(s, axis=-1, keepdims=True))
    p = p / jnp.sum(p, axis=-1, keepdims=True)
    o_all = _dot(p.astype(BF16), v)
    o_shape = o_all.shape
    own = (lax.broadcasted_iota(jnp.int32, o_shape, 0) // sq
           == lax.broadcasted_iota(jnp.int32, o_shape, 1) // V_HEAD)
    o_all = jnp.where(own, o_all, 0.0)
    o = o_all[0:sq]
    for h in range(1, MLA_HEADS):
        o = o + o_all[h * sq:(h + 1) * sq]
    o_ref[0] = o.astype(BF16)


def _attn_sample(q, past_ckv, past_kr, new_ckv, new_kr, w):
    b, sq, nq = q.shape
    npast, kl = past_ckv.shape[1:]
    nv = w['w_uv_plain'].shape[1]
    stream = lambda n, width: pl.BlockSpec((1, n, width), lambda bi: (bi, 0, 0))
    consts = [w['w_uk'], w['w_uv_plain'], w['g_kn'], w['rope_place']]
    return pl.pallas_call(
        _attn_sample_kernel,
        grid=(b,),
        in_specs=[stream(sq, nq), stream(npast, kl), stream(npast, QK_ROPE), stream(sq, kl), stream(sq, SLOT)]
        + [_const_spec(c.shape) for c in consts],
        out_specs=stream(sq, nv),
        out_shape=jax.ShapeDtypeStruct((b, sq, nv), BF16),
        compiler_params=_params(("parallel",)),
        name="attn_sample",
    )(q, past_ckv, past_kr, new_ckv, new_kr, *consts)


def _merge_kernel(x_ref, o_ref, siga_ref, gbob_ref, woa_ref, wo_ref, gx_ref, wcq_ref, gcq_ref,
                  x1_out, qc_out):
    oa = _dot(o_ref[...], woa_ref[...])
    merged = siga_ref[...].astype(F32) * oa + gbob_ref[...]
    x1 = x_ref[...] + _dot(merged.astype(BF16), wo_ref[...])
    x1_out[...] = x1
    qc = _dot(_rms(x1, gx_ref[...]).astype(BF16), wcq_ref[...])
    for h in range(MEM_HEADS):
        sl = slice(h * MEM_HEAD_DIM, (h + 1) * MEM_HEAD_DIM)
        qc_out[:, sl] = _rms(qc[:, sl], gcq_ref[...]).astype(BF16)


def _merge(x2d, o, siga, gbob, w, tm):
    t, d = x2d.shape
    consts = [w['w_oa'], w['w_o'], w['g_xattn'], w['w_cq_mem'], w['g_cq']]
    nq = w['w_cq_mem'].shape[1]
    return pl.pallas_call(
        _merge_kernel,
        grid=(t // tm,),
        in_specs=[_row_spec(tm, d), _row_spec(tm, o.shape[1]), _row_spec(tm, d), _row_spec(tm, d)]
        + [_const_spec(c.shape) for c in consts],
        out_specs=[_row_spec(tm, d), _row_spec(tm, nq)],
        out_shape=[jax.ShapeDtypeStruct((t, d), F32), jax.ShapeDtypeStruct((t, nq), BF16)],
        compiler_params=_params(("parallel",)),
        name="merge",
    )(x2d, o, siga, gbob, *consts)


def _mem_kv_kernel(mem_ref, gmem_ref, wck_ref, gck_ref, wcv_ref, k_out, v_out):
    m = _rms(mem_ref[...], gmem_ref[...]).astype(BF16)
    k = _dot(m, wck_ref[...])
    for h in range(MEM_HEADS):
        sl = slice(h * MEM_HEAD_DIM, (h + 1) * MEM_HEAD_DIM)
        k_out[:, sl] = _rms(k[:, sl], gck_ref[...])
    v_out[...] = _dot(m, wcv_ref[...])


def _mem_kv(mem2d, w, tm):
    t, d = mem2d.shape
    consts = [w['g_mem'], w['w_ck'], w['g_ck'], w['w_cv']]
    n = w['w_ck'].shape[1]
    return pl.pallas_call(
        _mem_kv_kernel,
        grid=(t // tm,),
        in_specs=[_row_spec(tm, d)] + [_const_spec(c.shape) for c in consts],
        out_specs=[_row_spec(tm, n)] * 2,
        out_shape=[jax.ShapeDtypeStruct((t, n), F32)] * 2,
        compiler_params=_params(("parallel",)),
        name="mem_kv",
    )(mem2d, *consts)


def _cross_kernel(q_ref, k_ref, v_ref, o_ref):
    q = q_ref[0]
    k = k_ref[0].astype(BF16)
    v = v_ref[0].astype(BF16)
    for h in range(MEM_HEADS):
        sl = slice(h * MEM_HEAD_DIM, (h + 1) * MEM_HEAD_DIM)
        s = _dot_nt(q[:, sl], k[:, sl]) * MEM_SCALE
        p = jnp.exp(s - jnp.max(s, axis=-1, keepdims=True))
        p = p / jnp.sum(p, axis=-1, keepdims=True)
        o_ref[0, :, sl] = _dot(p.astype(BF16), v[:, sl]).astype(BF16)


def _cross(qc, mk, mv, tc):
    b, s, n = qc.shape
    nm = mk.shape[1]
    return pl.pallas_call(
        _cross_kernel,
        grid=(b, s // tc),
        in_specs=[pl.BlockSpec((1, tc, n), lambda bi, i: (bi, i, 0)),
                  pl.BlockSpec((1, nm, n), lambda bi, i: (bi, 0, 0)),
                  pl.BlockSpec((1, nm, n), lambda bi, i: (bi, 0, 0))],
        out_specs=pl.BlockSpec((1, tc, n), lambda bi, i: (bi, i, 0)),
        out_shape=jax.ShapeDtypeStruct((b, s, n), BF16),
        compiler_params=_params(("parallel", "parallel")),
        name="cross_attn",
    )(qc, mk, mv)


def _cross_out_kernel(x1_ref, oc_ref, wco_ref, gffn_ref, x2_out, h_out):
    x2 = x1_ref[...] + _dot(oc_ref[...], wco_ref[...])
    x2_out[...] = x2
    h_out[...] = _rms(x2, gffn_ref[...]).astype(BF16)


def _cross_out(x1, oc, w, tm):
    t, d = x1.shape
    return pl.pallas_call(
        _cross_out_kernel,
        grid=(t // tm,),
        in_specs=[_row_spec(tm, d), _row_spec(tm, oc.shape[1]), _const_spec(w['w_co'].shape),
                  _const_spec(w['g_ffn'].shape)],
        out_specs=[_row_spec(tm, d)] * 2,
        out_shape=[jax.ShapeDtypeStruct((t, d), F32), jax.ShapeDtypeStruct((t, d), BF16)],
        compiler_params=_params(("parallel",)),
        name="cross_out",
    )(x1, oc, w['w_co'], w['g_ffn'])


def _pair_blocks():
    k = PEER_TOPK
    blocks, flat = [], []
    n_row = sum(1 for a in range(k) if k // (a + 1) > 2)
    for a in range(n_row):
        nb = k if a == 0 else 8
        blocks.append(('row', a, nb))
        flat += [a * k + b if (a + 1) * (b + 1) <= k else -1 for b in range(nb)]
    for b in range(k // (n_row + 1)):
        na = k if b == 0 else 8
        blocks.append(('col', b, na))
        flat += [a * k + b if (a >= n_row and (a + 1) * (b + 1) <= k) else -1 for a in range(na)]
    return blocks, np.asarray(flat, np.float32)


def _pair_candidates(v0, v1, blocks, combine):
    parts = []
    for kind, fixed, n in blocks:
        if kind == 'row':
            parts.append(combine(v0[fixed:fixed + 1, :], v1[0:n, :]))
        else:
            parts.append(combine(v0[0:n, :], v1[fixed:fixed + 1, :]))
    return jnp.concatenate(parts, axis=0)


def _peer_route_kernel(h_ref, wpq_ref, keys_ref, flat_ref, i_out, j_out, g_out,
                       q_scr, sv_scr, si_scr, ts_scr, te_scr, it_scr, jt_scr, gt_scr):
    tm = h_ref.shape[0]
    width = sv_scr.shape[2]
    blocks, _ = _pair_blocks()
    q_scr[...] = _dot(h_ref[...], wpq_ref[...]).astype(BF16)
    rows_keys = lax.broadcasted_iota(jnp.int32, (N_KEYS, width), 0).astype(F32)
    neg_inf = F32(-jnp.inf)

    def group_body(c, _):
        t0 = pl.multiple_of(c * width, width)

        def slot_body(s, _):
            d0 = pl.multiple_of(s * PEER_HALF, PEER_HALF)
            a = _dot_nt(keys_ref[s], q_scr[pl.ds(t0, width), pl.ds(d0, PEER_HALF)])
            for kk in range(PEER_TOPK):
                m = jnp.max(a, axis=0, keepdims=True)
                idx = jnp.min(jnp.where(a == m, rows_keys, F32(N_KEYS)), axis=0, keepdims=True)
                a = jnp.where(rows_keys == idx, neg_inf, a)
                sv_scr[s, kk:kk + 1, :] = m
                si_scr[s, kk:kk + 1, :] = idx
            return 0

        lax.fori_loop(0, 2 * PEER_HEADS, slot_body, 0)

        def head_body(hd, _):
            sv0, sv1 = sv_scr[2 * hd], sv_scr[2 * hd + 1]
            si0, si1 = si_scr[2 * hd], si_scr[2 * hd + 1]
            flat = flat_ref[...]
            cand = _pair_candidates(sv0, sv1, blocks, lambda x, y: x + y)
            cand = jnp.where(flat >= 0.0, cand, neg_inf)
            cidx = _pair_candidates(si0, si1, blocks, lambda x, y: x * F32(N_KEYS) + y)
            for kk in range(PEER_TOPK):
                m = jnp.max(cand, axis=0, keepdims=True)
                pos = jnp.min(jnp.where(cand == m, flat, F32(PEER_TOPK * PEER_TOPK)), axis=0, keepdims=True)
                sel = flat == pos
                ts_scr[kk:kk + 1, :] = m
                te_scr[kk:kk + 1, :] = jnp.max(jnp.where(sel, cidx, F32(-1.0)), axis=0, keepdims=True)
                cand = jnp.where(sel, neg_inf, cand)
            top = ts_scr[...]
            ex = jnp.exp(top - top[0:1, :])
            gate = ex / jnp.sum(ex, axis=0, keepdims=True)
            eidx = te_scr[...]
            irow = jnp.floor(eidx * F32(1.0 / N_KEYS))
            r0 = pl.multiple_of(hd * PEER_TOPK, PEER_TOPK)
            it_scr[pl.ds(r0, PEER_TOPK), :] = irow
            jt_scr[pl.ds(r0, PEER_TOPK), :] = eidx - irow * F32(N_KEYS)
            gt_scr[pl.ds(r0, PEER_TOPK), :] = gate
            return 0

        lax.fori_loop(0, PEER_HEADS, head_body, 0)
        i_out[pl.ds(t0, width), :] = it_scr[...].T
        j_out[pl.ds(t0, width), :] = jt_scr[...].T
        g_out[pl.ds(t0, width), :] = gt_scr[...].T
        return 0

    lax.fori_loop(0, tm // width, group_body, 0)


def _peer_route(h, w, tm):
    t, d = h.shape
    nsel = PEER_HEADS * PEER_TOPK
    width = ROUTE_LANES
    _, flat = _pair_blocks()
    flat = jnp.asarray(np.repeat(flat[:, None], width, axis=1))
    return pl.pallas_call(
        _peer_route_kernel,
        grid=(t // tm,),
        in_specs=[_row_spec(tm, d), _const_spec(w['w_pq'].shape), _const_spec(w['sub_keys'].shape),
                  _const_spec(flat.shape)],
        out_specs=[_row_spec(tm, nsel)] * 3,
        out_shape=[jax.ShapeDtypeStruct((t, nsel), F32)] * 3,
        scratch_shapes=[pltpu.VMEM((tm, w['w_pq'].shape[1]), BF16),
                        pltpu.VMEM((2 * PEER_HEADS, PEER_TOPK, width), F32),
                        pltpu.VMEM((2 * PEER_HEADS, PEER_TOPK, width), F32),
                        pltpu.VMEM((PEER_TOPK, width), F32),
                        pltpu.VMEM((PEER_TOPK, width), F32),
                        pltpu.VMEM((nsel, width), F32),
                        pltpu.VMEM((nsel, width), F32),
                        pltpu.VMEM((nsel, width), F32)],
        compiler_params=_params(("parallel",)),
        name="peer_route",
    )(h, w['w_pq'], w['sub_keys'], flat)


def _peer_dense_kernel(h_ref, i_ref, j_ref, g_ref, x_ref, u_ref, v_ref, out_ref, mask_scr):
    tm = h_ref.shape[0]
    te = u_ref.shape[0]
    e = pl.program_id(1)

    @pl.when(e == 0)
    def _():
        out_ref[...] = x_ref[...]
        key_rows = lax.broadcasted_iota(jnp.int32, (N_KEYS, LANES), 0).astype(F32)

        def token_body(t, _):
            irow = i_ref[pl.ds(t, 1), :]
            jrow = j_ref[pl.ds(t, 1), :]
            grow = g_ref[pl.ds(t, 1), :]
            rw = jnp.where(key_rows == irow, grow, 0.0).astype(BF16)
            cw = jnp.where(key_rows == jrow, 1.0, 0.0).astype(BF16)
            r0 = pl.multiple_of(t * MASK_PITCH, 8)
            mask_scr[pl.ds(r0, N_KEYS), :] = _dot_nt(rw, cw)
            return 0

        lax.fori_loop(0, tm, token_body, 0, unroll=MASK_UNROLL)

    h = h_ref[...]
    row0 = e * (te // N_KEYS)
    total = None
    for c in range(te // PEER_CHUNK):
        rows = slice(c * PEER_CHUNK, (c + 1) * PEER_CHUNK)
        act = _gelu(_dot_nt(h, u_ref[rows, :]))
        mask = jnp.concatenate(
            [mask_scr[pl.ds(row0 + c * (PEER_CHUNK // N_KEYS) + r, tm, stride=MASK_PITCH), :]
             for r in range(PEER_CHUNK // N_KEYS)], axis=1)
        part = _dot((act * mask).astype(BF16), v_ref[rows, :])
        total = part if total is None else total + part
    out_ref[...] += total


def _peer_dense(h, isel, jsel, gate, x2, u, v, tm, te):
    t, d = h.shape
    nexp = v.shape[0]
    nsel = isel.shape[1]
    row = lambda width: pl.BlockSpec((tm, width), lambda i, e: (i, 0), pipeline_mode=pl.Buffered(1))
    return pl.pallas_call(
        _peer_dense_kernel,
        grid=(t // tm, nexp // te),
        in_specs=[row(d), row(nsel), row(nsel), row(nsel), row(d),
                  pl.BlockSpec((te, d), lambda i, e: (e, 0)),
                  pl.BlockSpec((te, d), lambda i, e: (e, 0))],
        out_specs=pl.BlockSpec((tm, d), lambda i, e: (i, 0)),
        out_shape=jax.ShapeDtypeStruct((t, d), F32),
        scratch_shapes=[pltpu.VMEM((tm * MASK_PITCH, LANES), F32)],
        compiler_params=_params(("parallel", "arbitrary"), PEER_VMEM_LIMIT),
        name="peer_dense",
    )(h, isel, jsel, gate, x2, u, v)


def _slot_gain(parts):
    row = jnp.zeros((SLOT,), F32)
    for off, g in parts:
        row = lax.dynamic_update_slice(row, g.astype(F32), (off,))
    return row[None, :]


def _group_avg_matrix():
    m = np.zeros((SLOT, SLOT), np.float32)
    m[:QK_NOPE, :QK_NOPE] = 1.0 / QK_NOPE
    m[ROPE_LO:ROPE_LO + QK_ROPE, ROPE_LO:ROPE_LO + QK_ROPE] = 1.0 / QK_ROPE
    return jnp.asarray(m, BF16)


def _rope_tables(pos):
    inv = ROPE_THETA ** (-jnp.arange(ROPE_HALF, dtype=F32) / ROPE_HALF)
    ang = pos.astype(F32)[:, None] * inv[None, :]
    cos, sin = jnp.cos(ang), jnp.sin(ang)
    n = pos.shape[0]
    tail = SLOT - ROPE_LO - QK_ROPE
    cos_t = jnp.concatenate([jnp.ones((n, ROPE_LO), F32), cos, cos, jnp.zeros((n, tail), F32)], axis=1)
    sin_up = jnp.concatenate([jnp.zeros((n, ROPE_LO + ROPE_HALF), F32), sin, jnp.zeros((n, tail), F32)], axis=1)
    sin_dn = jnp.concatenate([jnp.zeros((n, ROPE_LO), F32), -sin, jnp.zeros((n, tail + ROPE_HALF), F32)], axis=1)
    return cos_t, sin_up, sin_dn


def _prep_weights(p, i):
    w_in = p['w_in'][i]
    q_lora = p['g_q_lat'].shape[1]
    kv_lora = p['g_kv_lat'].shape[1]
    gw = p['g_gm'].shape[1]
    d = w_in.shape[0]
    o = np.cumsum([0, q_lora, kv_lora, QK_ROPE, gw, gw, d, d])
    seg = lambda k: w_in[:, o[k]:o[k + 1]]
    row = lambda a: a[i][None, :].astype(F32)
    w = {}
    w['g_mix'] = row(p['g_mix'])
    w['w_cq'] = seg(0).astype(BF16)
    w['w_ckv'] = seg(1).astype(BF16)
    w['w_kr'] = jnp.pad(seg(2), ((0, 0), (ROPE_LO, SLOT - ROPE_LO - QK_ROPE))).astype(BF16)
    w['w_u'] = seg(3).astype(BF16)
    w['w_v'] = seg(4).astype(BF16)
    w['w_ga'] = seg(5).astype(BF16)
    w['w_gb'] = seg(6).astype(BF16)
    w['g_q_lat'] = row(p['g_q_lat'])
    w['g_kv_lat'] = row(p['g_kv_lat'])
    w['g_kr'] = _slot_gain([(ROPE_LO, p['g_kr'][i])])
    w['g_gm'] = row(p['g_gm'])
    head_dim = QK_NOPE + QK_ROPE
    w_uq = p['w_uq'][i].reshape(q_lora, MLA_HEADS, head_dim)
    w['w_uq'] = jnp.pad(w_uq, ((0, 0), (0, 0), (0, SLOT - head_dim))).reshape(q_lora, -1).astype(BF16)
    w['g_q'] = _slot_gain([(0, p['g_qn'][i]), (ROPE_LO, p['g_qr'][i])])
    w['m_avg'] = _group_avg_matrix()
    w_uk = p['w_uk'][i].reshape(kv_lora, MLA_HEADS, QK_NOPE)
    w['w_uk'] = jnp.pad(w_uk, ((0, 0), (0, 0), (0, SLOT - QK_NOPE))).reshape(kv_lora, -1).astype(BF16)
    w['g_kn'] = _slot_gain([(0, p['g_kn'][i])])
    w_uv = p['w_uv'][i].reshape(kv_lora, MLA_HEADS // 2, 2, V_HEAD)
    eye2 = jnp.eye(2, dtype=w_uv.dtype)
    w['w_uv'] = jnp.einsum('cpjd,jk->cpjkd', w_uv, eye2).reshape(kv_lora, -1).astype(BF16)
    w['w_uv_plain'] = p['w_uv'][i].astype(BF16)
    place = np.zeros((QK_ROPE, SLOT), np.float32)
    place[np.arange(QK_ROPE), ROPE_LO + np.arange(QK_ROPE)] = 1.0
    w['rope_place'] = jnp.asarray(place, BF16)
    w['w_oa'] = p['w_oa'][i].astype(BF16)
    w['w_ob'] = p['w_ob'][i].astype(BF16)
    w['w_o'] = p['w_o'][i].astype(BF16)
    w['g_xattn'] = row(p['g_xattn'])
    w['g_mem'] = row(p['g_mem'])
    w['w_cq_mem'] = p['w_cq'][i].astype(BF16)
    w['g_cq'] = row(p['g_cq'])
    w['w_ck'] = p['w_ck'][i].astype(BF16)
    w['g_ck'] = row(p['g_ck'])
    w['w_cv'] = p['w_cv'][i].astype(BF16)
    w['w_co'] = p['w_co'][i].astype(BF16)
    w['g_ffn'] = row(p['g_ffn'])
    w['w_pq'] = p['w_pq'][i].astype(BF16)
    w['sub_keys'] = p['sub_keys'][i].reshape(2 * PEER_HEADS, N_KEYS, PEER_HALF).astype(BF16)
    w['peer_u'] = p['peer_u'][i].astype(BF16)
    w['peer_v'] = p['peer_v'][i].astype(BF16)
    w['w_s'] = p['w_s'][i]
    w['b_s'] = p['b_s'][i]
    return w


def _spatial_operands(w, seq, tm):
    chunk = min(seq, GM_CHUNK)
    reps = tm // chunk
    w_mask = jnp.tril(w['w_s'][:, :chunk, :chunk])
    w_mask = w_mask.astype(BF16)
    mix = jnp.concatenate(
        [jnp.pad(w_mask, ((0, 0), (0, 0), (a * chunk, (reps - 1 - a) * chunk))) for a in range(reps)], axis=1)
    bias = jnp.tile(w['b_s'][:, :chunk].T, (reps, 1))
    bias = jnp.repeat(bias, LANES, axis=1)
    return mix.astype(BF16), bias.astype(F32)


def _token_mixer(x2d, w, seq, pos, past, tm):
    t, d = x2d.shape
    nb = t // seq
    period = max(seq, tm)
    tabs = _rope_tables(jnp.tile(pos, period // seq))
    cq, ckv, kr, u, vg, siga, sigb = _in_proj(x2d, w, tabs, tm)
    mix, bias = _spatial_operands(w, seq, tm)
    gbob = _gmlp(u, vg, sigb, mix, bias, w['w_ob'], tm)
    q = _q_proj(cq, w, tabs, tm)
    if past is None:
        k, v = _kv_proj(ckv, kr, w, tm)
        o = _attn_prompt(q.reshape(nb, seq, -1), k.reshape(nb, seq, -1), v.reshape(nb, seq, -1),
                         _score_bound(w), ATTN_TILE)
    else:
        past_ckv, past_kr = past
        o = _attn_sample(q.reshape(nb, seq, -1), past_ckv, past_kr, ckv.reshape(nb, seq, -1),
                         kr.reshape(nb, seq, SLOT), w)
    x1, qc = _merge(x2d, o.reshape(t, -1), siga, gbob, w, tm)
    return x1, qc, ckv, kr[:, ROPE_LO:ROPE_LO + QK_ROPE], vg


def _tail(x1, qc, mk, mv, w, nb, seq, tm):
    t, d = x1.shape
    oc = _cross(qc.reshape(nb, seq, -1), mk, mv, min(seq, tm))
    x2, h = _cross_out(x1, oc.reshape(t, -1), w, tm)
    isel, jsel, gate = _peer_route(h, w, tm)
    return _peer_dense(h, isel, jsel, gate, x2, w['peer_u'], w['peer_v'], PEER_ROWS, PEER_EXPERTS)


def kernel(x_prompt, x_sample, cache_mla_ckv, cache_mla_krope, cache_mem_k, cache_mem_v, mem_prompt, g_mix, w_in, g_q_lat, w_uq, g_qn, g_qr, g_kv_lat, g_kr, w_uk, w_uv, g_kn, w_oa, g_gm, w_s, b_s, w_ob, w_o, g_xattn, g_mem, w_cq, g_cq, w_ck, g_ck, w_cv, w_co, g_ffn, w_pq, sub_keys, peer_u, peer_v):
    params = dict(g_mix=g_mix, w_in=w_in, g_q_lat=g_q_lat, w_uq=w_uq, g_qn=g_qn, g_qr=g_qr,
                  g_kv_lat=g_kv_lat, g_kr=g_kr, w_uk=w_uk, w_uv=w_uv, g_kn=g_kn, w_oa=w_oa, g_gm=g_gm,
                  w_s=w_s, b_s=b_s, w_ob=w_ob, w_o=w_o, g_xattn=g_xattn, g_mem=g_mem, w_cq=w_cq,
                  g_cq=g_cq, w_ck=w_ck, g_ck=g_ck, w_cv=w_cv, w_co=w_co, g_ffn=g_ffn, w_pq=w_pq,
                  sub_keys=sub_keys, peer_u=peer_u, peer_v=peer_v)
    bp, sp, d = x_prompt.shape
    bs, ss, _ = x_sample.shape
    depth = w_in.shape[0]
    past_len = cache_mla_ckv.shape[2]
    n_mem = mem_prompt.shape[1]
    tm = ROW_TILE
    assert (bp * sp) % tm == 0 and (bs * ss) % tm == 0 and sp % ATTN_TILE == 0
    pos_p = jnp.arange(sp)
    pos_s = past_len + jnp.arange(ss)
    xp = x_prompt.reshape(bp * sp, d)
    xs = x_sample.reshape(bs * ss, d)
    outs = [[] for _ in range(7)]
    for i in range(depth):
        w = _prep_weights(params, i)
        xp1, qcp, ckv_p, kr_p, _ = _token_mixer(xp, w, sp, pos_p, None, tm)
        xs1, qcs, ckv_s, kr_s, vg_s = _token_mixer(xs, w, ss, pos_s, (cache_mla_ckv[i], cache_mla_krope[i]), tm)
        mk_p, mv_p = _mem_kv(mem_prompt.reshape(bp * n_mem, d), w, min(tm, bp * n_mem))
        xp = _tail(xp1, qcp, mk_p.reshape(bp, n_mem, -1), mv_p.reshape(bp, n_mem, -1), w, bp, sp, tm)
        xs = _tail(xs1, qcs, cache_mem_k[i].reshape(bs, n_mem, -1), cache_mem_v[i].reshape(bs, n_mem, -1),
                   w, bs, ss, tm)
        outs[0].append(ckv_p.reshape(bp, sp, -1))
        outs[1].append(kr_p.reshape(bp, sp, -1))
        outs[2].append(mk_p.reshape(bp, n_mem, MEM_HEADS, MEM_HEAD_DIM))
        outs[3].append(mv_p.reshape(bp, n_mem, MEM_HEADS, MEM_HEAD_DIM))
        outs[4].append(ckv_s.reshape(bs, ss, -1))
        outs[5].append(kr_s.reshape(bs, ss, -1))
        outs[6].append(vg_s.reshape(bs, ss, -1))
    return (xp.reshape(bp, sp, d), xs.reshape(bs, ss, d)) + tuple(jnp.stack(o) for o in outs)
```

```python
import functools

import jax
import jax.numpy as jnp
import numpy as np
from jax import lax
from jax.experimental import pallas as pl
from jax.experimental.pallas import tpu as pltpu

CHUNK = 64
EPS = 1e-6
MLA_HEADS = 16
QK_NOPE = 64
QK_ROPE = 32
V_HEAD = 64
ROPE_THETA = 10000.0
MLA_SCALE = (QK_NOPE + QK_ROPE) ** -0.5
GM_CHUNK = 128
GM_GROUPS = 8
MEM_HEADS = 4
MEM_HEAD_DIM = 128
MEM_SCALE = MEM_HEAD_DIM ** -0.5
PEER_HEADS = 8
N_KEYS = 128
PEER_TOPK = 16
PEER_HALF = 128

LANES = 128
SLOT = LANES
ROPE_LO = QK_NOPE
ROPE_HALF = QK_ROPE // 2
ROW_TILE = 512
ATTN_TILE = 512
LOG2_E = float(np.log2(np.e))
STATIC_SHIFT_LIMIT = 48.0
BOUND_MARGIN = 1.02
PEER_ROWS = 512
PEER_EXPERTS = 1024
PEER_CHUNK = 512
MASK_BUILDS = 1
MASK_KEYS = N_KEYS // MASK_BUILDS
MASK_PITCH = MASK_KEYS + 8
MASK_UNROLL = 32
VMEM_LIMIT = 48 * 1024 * 1024
PEER_VMEM_LIMIT = 60 * 1024 * 1024

F32 = jnp.float32
BF16 = jnp.bfloat16
_NT = (((1,), (1,)), ((), ()))


def _dot(a, b):
    return jnp.dot(a, b, preferred_element_type=F32)


def _dot_nt(a, b):
    return lax.dot_general(a, b, _NT, preferred_element_type=F32)


def _rms(xf, g):
    return xf * lax.rsqrt(jnp.mean(xf * xf, axis=-1, keepdims=True) + EPS) * g


def _gelu(x):
    return 0.5 * x * (1.0 + lax.erf(x * np.float32(np.sqrt(0.5))))


def _group_mean(sq, m_ref):
    hi = sq.astype(BF16)
    lo = (sq - hi.astype(F32)).astype(BF16)
    m = m_ref[...]
    return _dot(hi, m) + _dot(lo, m)


def _rope_slot(y, cos, sin_up, sin_dn):
    return (y * cos + pltpu.roll(y, ROPE_HALF, 1) * sin_up
            + pltpu.roll(y, SLOT - ROPE_HALF, 1) * sin_dn)


def _const_spec(shape):
    nd = len(shape)
    return pl.BlockSpec(shape, lambda *_: (0,) * nd, pipeline_mode=pl.Buffered(1))


def _row_spec(tm, width):
    return pl.BlockSpec((tm, width), lambda i: (i, 0))


def _params(sem, limit=VMEM_LIMIT):
    return pltpu.CompilerParams(dimension_semantics=sem, vmem_limit_bytes=limit)


def _in_proj_kernel(x_ref, gmix_ref, wcq_ref, wckv_ref, wkr_ref, wu_ref, wv_ref, wga_ref, wgb_ref,
                    gq_ref, gkv_ref, gkr_ref, ggm_ref, cos_ref, sup_ref, sdn_ref,
                    cq_out, ckv_out, kr_out, u_out, vg_out, siga_out, sigb_out):
    h = _rms(x_ref[...], gmix_ref[...]).astype(BF16)
    cq_out[...] = _rms(_dot(h, wcq_ref[...]), gq_ref[...]).astype(BF16)
    ckv_out[...] = _rms(_dot(h, wckv_ref[...]), gkv_ref[...])
    kr = _dot(h, wkr_ref[...])
    ms = jnp.sum(kr * kr, axis=-1, keepdims=True) * (1.0 / QK_ROPE)
    kr = kr * lax.rsqrt(ms + EPS) * gkr_ref[...]
    kr_out[...] = _rope_slot(kr, cos_ref[...], sup_ref[...], sdn_ref[...])
    u_out[...] = _gelu(_dot(h, wu_ref[...])).astype(BF16)
    vg_out[...] = _rms(_gelu(_dot(h, wv_ref[...])), ggm_ref[...])
    siga_out[...] = jax.nn.sigmoid(_dot(h, wga_ref[...])).astype(BF16)
    sigb_out[...] = jax.nn.sigmoid(_dot(h, wgb_ref[...])).astype(BF16)


def _in_proj(x2d, w, tabs, tm):
    t, d = x2d.shape
    cos, sup, sdn = tabs
    ntab = cos.shape[0] // tm
    tab_spec = pl.BlockSpec((tm, SLOT), lambda i: (i % ntab, 0))
    consts = [w['g_mix'], w['w_cq'], w['w_ckv'], w['w_kr'], w['w_u'], w['w_v'], w['w_ga'], w['w_gb'],
              w['g_q_lat'], w['g_kv_lat'], w['g_kr'], w['g_gm']]
    gw = w['w_u'].shape[1]
    widths = [w['w_cq'].shape[1], w['w_ckv'].shape[1], SLOT, gw, gw, d, d]
    dtypes = [BF16, F32, F32, BF16, F32, BF16, BF16]
    return pl.pallas_call(
        _in_proj_kernel,
        grid=(t // tm,),
        in_specs=[_row_spec(tm, d)] + [_const_spec(c.shape) for c in consts] + [tab_spec] * 3,
        out_specs=[_row_spec(tm, n) for n in widths],
        out_shape=[jax.ShapeDtypeStruct((t, n), dt) for n, dt in zip(widths, dtypes)],
        compiler_params=_params(("parallel",)),
        name="in_proj",
    )(x2d, *consts, cos, sup, sdn)


def _gmlp_kernel(u_ref, vg_ref, sigb_ref, mix_ref, bias_ref, wob_ref, out_ref):
    vgb = vg_ref[...].astype(BF16)
    parts = []
    for g in range(GM_GROUPS):
        sl = slice(g * LANES, (g + 1) * LANES)
        mixed = _dot(mix_ref[g], vgb[:, sl]) + bias_ref[:, sl]
        parts.append((u_ref[:, sl].astype(F32) * mixed).astype(BF16))
    ob = _dot(jnp.concatenate(parts, axis=1), wob_ref[...])
    out_ref[...] = sigb_ref[...].astype(F32) * ob


def _gmlp(u, vg, sigb, mix, bias, w_ob, tm):
    t, gw = u.shape
    d = w_ob.shape[1]
    return pl.pallas_call(
        _gmlp_kernel,
        grid=(t // tm,),
        in_specs=[_row_spec(tm, gw), _row_spec(tm, gw), _row_spec(tm, d),
                  _const_spec(mix.shape), _const_spec(bias.shape), _const_spec(w_ob.shape)],
        out_specs=_row_spec(tm, d),
        out_shape=jax.ShapeDtypeStruct((t, d), F32),
        compiler_params=_params(("parallel",)),
        name="gmlp",
    )(u, vg, sigb, mix, bias, w_ob)


def _q_proj_kernel(cq_ref, wuq_ref, gq_ref, mavg_ref, cos_ref, sup_ref, sdn_ref, q_out):
    q = _dot(cq_ref[...], wuq_ref[...])
    cos, sup, sdn = cos_ref[...], sup_ref[...], sdn_ref[...]
    for h in range(MLA_HEADS):
        sl = slice(h * SLOT, (h + 1) * SLOT)
        qs = q[:, sl]
        y = qs * lax.rsqrt(_group_mean(qs * qs, mavg_ref) + EPS) * gq_ref[...]
        q_out[:, sl] = _rope_slot(y, cos, sup, sdn).astype(BF16)


def _q_proj(cq, w, tabs, tm):
    t, ql = cq.shape
    cos, sup, sdn = tabs
    ntab = cos.shape[0] // tm
    tab_spec = pl.BlockSpec((tm, SLOT), lambda i: (i % ntab, 0))
    n = MLA_HEADS * SLOT
    return pl.pallas_call(
        _q_proj_kernel,
        grid=(t // tm,),
        in_specs=[_row_spec(tm, ql), _const_spec(w['w_uq'].shape), _const_spec(w['g_q'].shape),
                  _const_spec(w['m_avg'].shape)] + [tab_spec] * 3,
        out_specs=_row_spec(tm, n),
        out_shape=jax.ShapeDtypeStruct((t, n), BF16),
        compiler_params=_params(("parallel",)),
        name="q_proj",
    )(cq, w['w_uq'], w['g_q'], w['m_avg'], cos, sup, sdn)


def _kv_proj_kernel(ckv_ref, kr_ref, wuk_ref, wuv_ref, gkn_ref, mavg_ref, k_out, v_out):
    c = ckv_ref[...].astype(BF16)
    kn = _dot(c, wuk_ref[...])
    kr = kr_ref[...]
    for h in range(MLA_HEADS):
        sl = slice(h * SLOT, (h + 1) * SLOT)
        ks = kn[:, sl]
        y = ks * lax.rsqrt(_group_mean(ks * ks, mavg_ref) + EPS) * gkn_ref[...]
        k_out[:, sl] = (y + kr).astype(BF16)
    v_out[...] = _dot(c, wuv_ref[...]).astype(BF16)


def _kv_proj(ckv, kr, w, tm):
    t, kl = ckv.shape
    n = MLA_HEADS * SLOT
    return pl.pallas_call(
        _kv_proj_kernel,
        grid=(t // tm,),
        in_specs=[_row_spec(tm, kl), _row_spec(tm, SLOT), _const_spec(w['w_uk'].shape),
                  _const_spec(w['w_uv'].shape), _const_spec(w['g_kn'].shape),
                  _const_spec(w['m_avg'].shape)],
        out_specs=[_row_spec(tm, n), _row_spec(tm, n)],
        out_shape=[jax.ShapeDtypeStruct((t, n), BF16)] * 2,
        compiler_params=_params(("parallel",)),
        name="kv_proj",
    )(ckv, kr, w['w_uk'], w['w_uv'], w['g_kn'], w['m_avg'])


def _attn_prompt_kernel(bound_ref, q_ref, k_ref, v_ref, o_ref):
    tile = q_ref.shape[1]
    qi = pl.program_id(2)
    q = q_ref[0]
    bound = bound_ref[0]
    first = lax.broadcasted_iota(jnp.int32, (1, LANES), 1) < V_HEAD
    row_chunk = lax.broadcasted_iota(jnp.int32, (tile, tile), 0) // CHUNK
    col_chunk = lax.broadcasted_iota(jnp.int32, (tile, tile), 1) // CHUNK
    visible = col_chunk <= row_chunk

    def static_step(kstart, carry, masked):
        sums, acc = carry
        kstart = pl.multiple_of(kstart, tile)
        k = k_ref[0, pl.ds(kstart, tile), :]
        v = v_ref[0, pl.ds(kstart, tile), :]
        new_sums = []
        for h in range(2):
            sl = slice(h * SLOT, (h + 1) * SLOT)
            s = _dot_nt(q[:, sl], k[:, sl]) * F32(MLA_SCALE * LOG2_E) - bound
            if masked:
                s = jnp.where(visible, s, F32(-1e30))
            p = jnp.exp2(s)
            part = sums[h]
            for c in range(tile // LANES):
                part = part + p[:, c * LANES:(c + 1) * LANES]
            new_sums.append(part)
            acc = acc + _dot(p.astype(BF16), v[:, sl])
        return tuple(new_sums), acc

    @pl.when(bound < STATIC_SHIFT_LIMIT)
    def _():
        zeros = jnp.zeros((tile, LANES), F32)

        def pair(i, c):
            c = static_step(2 * i * tile, c, False)
            return static_step((2 * i + 1) * tile, c, False)

        def odd_tail(c):
            return static_step(qi * tile, static_step((qi - 1) * tile, c, False), True)

        carry = lax.fori_loop(0, qi // 2, pair, ((zeros, zeros), zeros))
        sums, acc = lax.cond(qi % 2 == 1, odd_tail, lambda c: static_step(qi * tile, c, True), carry)
        l0 = jnp.sum(sums[0], axis=-1, keepdims=True)
        l1 = jnp.sum(sums[1], axis=-1, keepdims=True)
        o_ref[0] = (acc / jnp.where(first, l0, l1)).astype(BF16)

    def step(kstart, carry, masked):
        ms, ls, acc = carry
        kstart = pl.multiple_of(kstart, tile)
        k = k_ref[0, pl.ds(kstart, tile), :]
        v = v_ref[0, pl.ds(kstart, tile), :]
        new_ms, new_ls, alphas, pvs = [], [], [], []
        for h in range(2):
            sl = slice(h * SLOT, (h + 1) * SLOT)
            s = _dot_nt(q[:, sl], k[:, sl]) * F32(MLA_SCALE * LOG2_E)
            if masked:
                s = jnp.where(visible, s, F32(-1e30))
            m_new = jnp.maximum(ms[h], jnp.max(s, axis=-1, keepdims=True))
            alpha = jnp.exp2(ms[h] - m_new)
            p = jnp.exp2(s - m_new)
            new_ls.append(alpha * ls[h] + jnp.sum(p, axis=-1, keepdims=True))
            new_ms.append(m_new)
            alphas.append(alpha)
            pvs.append(_dot(p.astype(BF16), v[:, sl]))
        acc = acc * jnp.where(first, alphas[0], alphas[1]) + pvs[0] + pvs[1]
        return tuple(new_ms), tuple(new_ls), acc

    @pl.when(bound >= STATIC_SHIFT_LIMIT)
    def _():
        neg = jnp.full((tile, 1), -1e30, F32)
        zero = jnp.zeros((tile, 1), F32)
        init = ((neg, neg), (zero, zero), jnp.zeros((tile, LANES), F32))
        carry = lax.fori_loop(0, qi, lambda i, c: step(i * tile, c, False), init)
        _, ls, acc = step(qi * tile, carry, True)
        o_ref[0] = (acc / jnp.where(first, ls[0], ls[1])).astype(BF16)


def _score_bound(w):
    gmax = lambda g, lo, n: jnp.max(jnp.abs(g[0, lo:lo + n]))
    nope = QK_NOPE * gmax(w['g_q'], 0, QK_NOPE) * gmax(w['g_kn'], 0, QK_NOPE)
    rope = QK_ROPE * gmax(w['g_q'], ROPE_LO, QK_ROPE) * gmax(w['g_kr'], ROPE_LO, QK_ROPE)
    return (BOUND_MARGIN * MLA_SCALE * LOG2_E * (nope + rope)).reshape(1).astype(F32)


def _attn_prompt(q, k, v, bound, tile):
    b, s, _ = q.shape
    pairs = MLA_HEADS // 2
    return pl.pallas_call(
        _attn_prompt_kernel,
        grid=(b, pairs, s // tile),
        in_specs=[pl.BlockSpec(memory_space=pltpu.SMEM),
                  pl.BlockSpec((1, tile, 2 * SLOT), lambda bi, p, i: (bi, i, p)),
                  pl.BlockSpec((1, s, 2 * SLOT), lambda bi, p, i: (bi, 0, p)),
                  pl.BlockSpec((1, s, 2 * SLOT), lambda bi, p, i: (bi, 0, p))],
        out_specs=pl.BlockSpec((1, tile, LANES), lambda bi, p, i: (bi, i, p)),
        out_shape=jax.ShapeDtypeStruct((b, s, pairs * LANES), BF16),
        compiler_params=_params(("parallel", "parallel", "arbitrary")),
        name="attn_prompt",
    )(bound, q, k, v)


def _attn_sample_kernel(q_ref, pckv_ref, pkr_ref, nckv_ref, nkr_ref, wuk_ref, wuv_ref, gkn_ref, place_ref,
                        o_ref):
    sq = q_ref.shape[1]
    rows = MLA_HEADS * sq
    gkn = gkn_ref[...]

    def keys_values(ckv, kr_slot):
        c = ckv.astype(BF16)
        kn = _dot(c, wuk_ref[...])
        parts = []
        for h in range(MLA_HEADS):
            ks = kn[:, h * SLOT:(h + 1) * SLOT]
            ms = jnp.sum(ks * ks, axis=-1, keepdims=True) * (1.0 / QK_NOPE)
            parts.append((ks * lax.rsqrt(ms + EPS) * gkn + kr_slot).astype(BF16))
        return jnp.concatenate(parts, axis=1), _dot(c, wuv_ref[...]).astype(BF16)

    past_kr = _dot(pkr_ref[0].astype(BF16), place_ref[...])
    k_p, v_p = keys_values(pckv_ref[0], past_kr)
    k_n, v_n = keys_values(nckv_ref[0], nkr_ref[0])
    k = jnp.concatenate([k_p, k_n], axis=0)
    v = jnp.concatenate([v_p, v_n], axis=0)

    qt = jnp.concatenate([q_ref[0].astype(F32)] * MLA_HEADS, axis=0)
    q_shape = (rows, MLA_HEADS * SLOT)
    own = (lax.broadcasted_iota(jnp.int32, q_shape, 0) // sq
           == lax.broadcasted_iota(jnp.int32, q_shape, 1) // SLOT)
    qbd = jnp.where(own, qt, 0.0).astype(BF16)
    s = _dot_nt(qbd, k) * MLA_SCALE
    p = jnp.exp(s - jnp.max(s, axis=-1, keepdims=True))
    p = p / jnp.sum(p, axis=-1, keepdims=True)
    o_all = _dot(p.astype(BF16), v)
    o_shape = o_all.shape
    own = (lax.broadcasted_iota(jnp.int32, o_shape, 0) // sq
           == lax.broadcasted_iota(jnp.int32, o_shape, 1) // V_HEAD)
    o_all = jnp.where(own, o_all, 0.0)
    o = o_all[0:sq]
    for h in range(1, MLA_HEADS):
        o = o + o_all[h * sq:(h + 1) * sq]
    o_ref[0] = o.astype(BF16)


def _attn_sample(q, past_ckv, past_kr, new_ckv, new_kr, w):
    b, sq, nq = q.shape
    npast, kl = past_ckv.shape[1:]
    nv = w['w_uv_plain'].shape[1]
    stream = lambda n, width: pl.BlockSpec((1, n, width), lambda bi: (bi, 0, 0))
    consts = [w['w_uk'], w['w_uv_plain'], w['g_kn'], w['rope_place']]
    return pl.pallas_call(
        _attn_sample_kernel,
        grid=(b,),
        in_specs=[stream(sq, nq), stream(npast, kl), stream(npast, QK_ROPE), stream(sq, kl), stream(sq, SLOT)]
        + [_const_spec(c.shape) for c in consts],
        out_specs=stream(sq, nv),
        out_shape=jax.ShapeDtypeStruct((b, sq, nv), BF16),
        compiler_params=_params(("parallel",)),
        name="attn_sample",
    )(q, past_ckv, past_kr, new_ckv, new_kr, *consts)


def _merge_kernel(x_ref, o_ref, siga_ref, gbob_ref, woa_ref, wo_ref, gx_ref, wcq_ref, gcq_ref,
                  x1_out, qc_out):
    oa = _dot(o_ref[...], woa_ref[...])
    merged = siga_ref[...].astype(F32) * oa + gbob_ref[...]
    x1 = x_ref[...] + _dot(merged.astype(BF16), wo_ref[...])
    x1_out[...] = x1
    qc = _dot(_rms(x1, gx_ref[...]).astype(BF16), wcq_ref[...])
    for h in range(MEM_HEADS):
        sl = slice(h * MEM_HEAD_DIM, (h + 1) * MEM_HEAD_DIM)
        qc_out[:, sl] = _rms(qc[:, sl], gcq_ref[...]).astype(BF16)


def _merge(x2d, o, siga, gbob, w, tm):
    t, d = x2d.shape
    consts = [w['w_oa'], w['w_o'], w['g_xattn'], w['w_cq_mem'], w['g_cq']]
    nq = w['w_cq_mem'].shape[1]
    return pl.pallas_call(
        _merge_kernel,
        grid=(t // tm,),
        in_specs=[_row_spec(tm, d), _row_spec(tm, o.shape[1]), _row_spec(tm, d), _row_spec(tm, d)]
        + [_const_spec(c.shape) for c in consts],
        out_specs=[_row_spec(tm, d), _row_spec(tm, nq)],
        out_shape=[jax.ShapeDtypeStruct((t, d), F32), jax.ShapeDtypeStruct((t, nq), BF16)],
        compiler_params=_params(("parallel",)),
        name="merge",
    )(x2d, o, siga, gbob, *consts)


def _mem_kv_kernel(mem_ref, gmem_ref, wck_ref, gck_ref, wcv_ref, k_out, v_out):
    m = _rms(mem_ref[...], gmem_ref[...]).astype(BF16)
    k = _dot(m, wck_ref[...])
    for h in range(MEM_HEADS):
        sl = slice(h * MEM_HEAD_DIM, (h + 1) * MEM_HEAD_DIM)
        k_out[:, sl] = _rms(k[:, sl], gck_ref[...])
    v_out[...] = _dot(m, wcv_ref[...])


def _mem_kv(mem2d, w, tm):
    t, d = mem2d.shape
    consts = [w['g_mem'], w['w_ck'], w['g_ck'], w['w_cv']]
    n = w['w_ck'].shape[1]
    return pl.pallas_call(
        _mem_kv_kernel,
        grid=(t // tm,),
        in_specs=[_row_spec(tm, d)] + [_const_spec(c.shape) for c in consts],
        out_specs=[_row_spec(tm, n)] * 2,
        out_shape=[jax.ShapeDtypeStruct((t, n), F32)] * 2,
        compiler_params=_params(("parallel",)),
        name="mem_kv",
    )(mem2d, *consts)


def _cross_kernel(q_ref, k_ref, v_ref, o_ref):
    q = q_ref[0]
    k = k_ref[0].astype(BF16)
    v = v_ref[0].astype(BF16)
    for h in range(MEM_HEADS):
        sl = slice(h * MEM_HEAD_DIM, (h + 1) * MEM_HEAD_DIM)
        s = _dot_nt(q[:, sl], k[:, sl]) * MEM_SCALE
        p = jnp.exp(s - jnp.max(s, axis=-1, keepdims=True))
        p = p / jnp.sum(p, axis=-1, keepdims=True)
        o_ref[0, :, sl] = _dot(p.astype(BF16), v[:, sl]).astype(BF16)


def _cross(qc, mk, mv, tc):
    b, s, n = qc.shape
    nm = mk.shape[1]
    return pl.pallas_call(
        _cross_kernel,
        grid=(b, s // tc),
        in_specs=[pl.BlockSpec((1, tc, n), lambda bi, i: (bi, i, 0)),
                  pl.BlockSpec((1, nm, n), lambda bi, i: (bi, 0, 0)),
                  pl.BlockSpec((1, nm, n), lambda bi, i: (bi, 0, 0))],
        out_specs=pl.BlockSpec((1, tc, n), lambda bi, i: (bi, i, 0)),
        out_shape=jax.ShapeDtypeStruct((b, s, n), BF16),
        compiler_params=_params(("parallel", "parallel")),
        name="cross_attn",
    )(qc, mk, mv)


def _cross_out_kernel(x1_ref, oc_ref, wco_ref, gffn_ref, wpq_ref, x2_out, h_out, q_out):
    x2 = x1_ref[...] + _dot(oc_ref[...], wco_ref[...])
    x2_out[...] = x2
    h = _rms(x2, gffn_ref[...]).astype(BF16)
    h_out[...] = h
    q_out[...] = _dot(h, wpq_ref[...]).astype(BF16)


def _cross_out(x1, oc, w, tm):
    t, d = x1.shape
    nq = w['w_pq'].shape[1]
    return pl.pallas_call(
        _cross_out_kernel,
        grid=(t // tm,),
        in_specs=[_row_spec(tm, d), _row_spec(tm, oc.shape[1]), _const_spec(w['w_co'].shape),
                  _const_spec(w['g_ffn'].shape), _const_spec(w['w_pq'].shape)],
        out_specs=[_row_spec(tm, d), _row_spec(tm, d), _row_spec(tm, nq)],
        out_shape=[jax.ShapeDtypeStruct((t, d), F32), jax.ShapeDtypeStruct((t, d), BF16),
                   jax.ShapeDtypeStruct((t, nq), BF16)],
        compiler_params=_params(("parallel",)),
        name="cross_out",
    )(x1, oc, w['w_co'], w['g_ffn'], w['w_pq'])


def _pair_blocks():
    k = PEER_TOPK
    blocks, flat = [], []
    n_row = sum(1 for a in range(k) if k // (a + 1) > 2)
    for a in range(n_row):
        nb = k if a == 0 else 8
        blocks.append(('row', a, nb))
        flat += [a * k + b if (a + 1) * (b + 1) <= k else -1 for b in range(nb)]
    for b in range(k // (n_row + 1)):
        na = k if b == 0 else 8
        blocks.append(('col', b, na))
        flat += [a * k + b if (a >= n_row and (a + 1) * (b + 1) <= k) else -1 for a in range(na)]
    return blocks, np.asarray(flat, np.float32)


def _pair_candidates(v0, v1, blocks, combine):
    parts = []
    for kind, fixed, n in blocks:
        if kind == 'row':
            parts.append(combine(v0[fixed:fixed + 1, :], v1[0:n, :]))
        else:
            parts.append(combine(v0[0:n, :], v1[fixed:fixed + 1, :]))
    return jnp.concatenate(parts, axis=0)


def _peer_kernel(h_ref, x_ref, q_ref, keys_ref, flat_ref, u_ref, v_ref, out_ref,
                 mask_scr, sv_scr, si_scr, ts_scr, te_scr, it_scr, jt_scr, gt_scr,
                 isel_scr, jsel_scr, gsel_scr):
    tm = h_ref.shape[0]
    te = v_ref.shape[0]
    r = pl.program_id(0)
    e = pl.program_id(1)
    rows_per_step = te // N_KEYS
    steps_per_build = MASK_KEYS // rows_per_step
    blocks, _ = _pair_blocks()
    neg_inf = F32(-jnp.inf)

    def build_mask(part):
        key_rows = (lax.broadcasted_iota(jnp.int32, (MASK_KEYS, LANES), 0) + part * MASK_KEYS).astype(F32)
        col_rows = lax.broadcasted_iota(jnp.int32, (N_KEYS, LANES), 0).astype(F32)

        def token_body(t, _):
            irow = isel_scr[pl.ds(t, 1), :]
            jrow = jsel_scr[pl.ds(t, 1), :]
            grow = gsel_scr[pl.ds(t, 1), :]
            rw = jnp.where(key_rows == irow, grow, 0.0).astype(BF16)
            cw = jnp.where(col_rows == jrow, 1.0, 0.0).astype(BF16)
            r0 = pl.multiple_of(t * MASK_PITCH, 8)
            mask_scr[pl.ds(r0, MASK_KEYS), :] = _dot_nt(rw, cw)
            return 0

        lax.fori_loop(0, tm, token_body, 0, unroll=MASK_UNROLL)

    def rank_pairs(hd, _):
        sv0, sv1 = sv_scr[2 * hd], sv_scr[2 * hd + 1]
        si0, si1 = si_scr[2 * hd], si_scr[2 * hd + 1]
        flat = flat_ref[...]
        cand = _pair_candidates(sv0, sv1, blocks, lambda x, y: x + y)
        cand = jnp.where(flat >= 0.0, cand, neg_inf)
        cidx = _pair_candidates(si0, si1, blocks, lambda x, y: x * F32(N_KEYS) + y)
        for kk in range(PEER_TOPK):
            m = jnp.max(cand, axis=0, keepdims=True)
            pos = jnp.min(jnp.where(cand == m, flat, F32(PEER_TOPK * PEER_TOPK)), axis=0, keepdims=True)
            sel = flat == pos
            ts_scr[kk:kk + 1, :] = m
            te_scr[kk:kk + 1, :] = jnp.max(jnp.where(sel, cidx, F32(-1.0)), axis=0, keepdims=True)
            cand = jnp.where(sel, neg_inf, cand)
        top = ts_scr[...]
        ex = jnp.exp(top - top[0:1, :])
        gate = ex / jnp.sum(ex, axis=0, keepdims=True)
        eidx = te_scr[...]
        irow = jnp.floor(eidx * F32(1.0 / N_KEYS))
        r0 = pl.multiple_of(hd * PEER_TOPK, PEER_TOPK)
        it_scr[pl.ds(r0, PEER_TOPK), :] = irow
        jt_scr[pl.ds(r0, PEER_TOPK), :] = eidx - irow * F32(N_KEYS)
        gt_scr[pl.ds(r0, PEER_TOPK), :] = gate
        return 0

    @pl.when(e == 0)
    def _():
        out_ref[...] = x_ref[...]

        @pl.when(r > 0)
        def _():
            lax.fori_loop(0, PEER_HEADS, rank_pairs, 0)
            isel_scr[...] = it_scr[...].T
            jsel_scr[...] = jt_scr[...].T
            gsel_scr[...] = gt_scr[...].T
            build_mask(0)

    for part in range(1, MASK_BUILDS):
        @pl.when(jnp.logical_and(e == part * steps_per_build, r > 0))
        def _():
            build_mask(part)

    def slot_scores():
        d0 = pl.multiple_of(e * PEER_HALF, PEER_HALF)
        return _dot_nt(keys_ref[e], q_ref[:, pl.ds(d0, PEER_HALF)])

    def rank_slot(a):
        rows_keys = lax.broadcasted_iota(jnp.int32, (N_KEYS, tm), 0).astype(F32)
        for kk in range(PEER_TOPK):
            m = jnp.max(a, axis=0, keepdims=True)
            idx = jnp.min(jnp.where(a == m, rows_keys, F32(N_KEYS)), axis=0, keepdims=True)
            a = jnp.where(rows_keys == idx, neg_inf, a)
            sv_scr[e, kk:kk + 1, :] = m
            si_scr[e, kk:kk + 1, :] = idx

    def expert_chunk():
        h = h_ref[...]
        row0 = lax.rem(e, steps_per_build) * rows_per_step
        total = None
        for c in range(te // PEER_CHUNK):
            rows = slice(c * PEER_CHUNK, (c + 1) * PEER_CHUNK)
            act = _gelu(_dot_nt(h, u_ref[rows, :]))
            mask = jnp.concatenate(
                [mask_scr[pl.ds(row0 + c * (PEER_CHUNK // N_KEYS) + rr, tm, stride=MASK_PITCH), :]
                 for rr in range(PEER_CHUNK // N_KEYS)], axis=1)
            term = _dot((act * mask).astype(BF16), v_ref[rows, :])
            total = term if total is None else total + term
        out_ref[...] += total

    last = pl.num_programs(0) - 1

    @pl.when(r == 0)
    def _():
        rank_slot(slot_scores())

    @pl.when(jnp.logical_and(r > 0, r < last))
    def _():
        a = slot_scores()
        expert_chunk()
        rank_slot(a)

    @pl.when(r == last)
    def _():
        expert_chunk()


def _peer(h, x2, q_route, w, tm, te):
    t, d = h.shape
    n = t // tm
    nexp = w['peer_v'].shape[0]
    assert nexp // te == 2 * PEER_HEADS, "one half-key slot is ranked per expert step"
    nsel = PEER_HEADS * PEER_TOPK
    _, flat = _pair_blocks()
    flat = jnp.asarray(np.repeat(flat[:, None], tm, axis=1))
    dense_tile = lambda width: pl.BlockSpec((tm, width), lambda r, e: (jnp.maximum(r - 1, 0), 0),
                                            pipeline_mode=pl.Buffered(1))
    expert_tile = pl.BlockSpec((te, d), lambda r, e: (e, 0))
    return pl.pallas_call(
        _peer_kernel,
        grid=(n + 1, nexp // te),
        in_specs=[dense_tile(d), dense_tile(d),
                  pl.BlockSpec((tm, q_route.shape[1]), lambda r, e: (jnp.minimum(r, n - 1), 0),
                               pipeline_mode=pl.Buffered(1)),
                  _const_spec(w['sub_keys'].shape), _const_spec(flat.shape), expert_tile, expert_tile],
        out_specs=pl.BlockSpec((tm, d), lambda r, e: (jnp.maximum(r - 1, 0), 0)),
        out_shape=jax.ShapeDtypeStruct((t, d), F32),
        scratch_shapes=[pltpu.VMEM((tm * MASK_PITCH, LANES), F32),
                        pltpu.VMEM((2 * PEER_HEADS, PEER_TOPK, tm), F32),
                        pltpu.VMEM((2 * PEER_HEADS, PEER_TOPK, tm), F32),
                        pltpu.VMEM((PEER_TOPK, tm), F32),
                        pltpu.VMEM((PEER_TOPK, tm), F32),
                        pltpu.VMEM((nsel, tm), F32),
                        pltpu.VMEM((nsel, tm), F32),
                        pltpu.VMEM((nsel, tm), F32),
                        pltpu.VMEM((tm, nsel), F32),
                        pltpu.VMEM((tm, nsel), F32),
                        pltpu.VMEM((tm, nsel), F32)],
        compiler_params=_params(("arbitrary", "arbitrary"), PEER_VMEM_LIMIT),
        name="peer",
    )(h, x2, q_route, w['sub_keys'], flat, w['peer_u'], w['peer_v'])


def _slot_gain(parts):
    row = jnp.zeros((SLOT,), F32)
    for off, g in parts:
        row = lax.dynamic_update_slice(row, g.astype(F32), (off,))
    return row[None, :]


def _group_avg_matrix():
    m = np.zeros((SLOT, SLOT), np.float32)
    m[:QK_NOPE, :QK_NOPE] = 1.0 / QK_NOPE
    m[ROPE_LO:ROPE_LO + QK_ROPE, ROPE_LO:ROPE_LO + QK_ROPE] = 1.0 / QK_ROPE
    return jnp.asarray(m, BF16)


def _rope_tables(pos):
    inv = ROPE_THETA ** (-jnp.arange(ROPE_HALF, dtype=F32) / ROPE_HALF)
    ang = pos.astype(F32)[:, None] * inv[None, :]
    cos, sin = jnp.cos(ang), jnp.sin(ang)
    n = pos.shape[0]
    tail = SLOT - ROPE_LO - QK_ROPE
    cos_t = jnp.concatenate([jnp.ones((n, ROPE_LO), F32), cos, cos, jnp.zeros((n, tail), F32)], axis=1)
    sin_up = jnp.concatenate([jnp.zeros((n, ROPE_LO + ROPE_HALF), F32), sin, jnp.zeros((n, tail), F32)], axis=1)
    sin_dn = jnp.concatenate([jnp.zeros((n, ROPE_LO), F32), -sin, jnp.zeros((n, tail + ROPE_HALF), F32)], axis=1)
    return cos_t, sin_up, sin_dn


def _prep_weights(p, i):
    w_in = p['w_in'][i]
    q_lora = p['g_q_lat'].shape[1]
    kv_lora = p['g_kv_lat'].shape[1]
    gw = p['g_gm'].shape[1]
    d = w_in.shape[0]
    o = np.cumsum([0, q_lora, kv_lora, QK_ROPE, gw, gw, d, d])
    seg = lambda k: w_in[:, o[k]:o[k + 1]]
    row = lambda a: a[i][None, :].astype(F32)
    w = {}
    w['g_mix'] = row(p['g_mix'])
    w['w_cq'] = seg(0).astype(BF16)
    w['w_ckv'] = seg(1).astype(BF16)
    w['w_kr'] = jnp.pad(seg(2), ((0, 0), (ROPE_LO, SLOT - ROPE_LO - QK_ROPE))).astype(BF16)
    w['w_u'] = seg(3).astype(BF16)
    w['w_v'] = seg(4).astype(BF16)
    w['w_ga'] = seg(5).astype(BF16)
    w['w_gb'] = seg(6).astype(BF16)
    w['g_q_lat'] = row(p['g_q_lat'])
    w['g_kv_lat'] = row(p['g_kv_lat'])
    w['g_kr'] = _slot_gain([(ROPE_LO, p['g_kr'][i])])
    w['g_gm'] = row(p['g_gm'])
    head_dim = QK_NOPE + QK_ROPE
    w_uq = p['w_uq'][i].reshape(q_lora, MLA_HEADS, head_dim)
    w['w_uq'] = jnp.pad(w_uq, ((0, 0), (0, 0), (0, SLOT - head_dim))).reshape(q_lora, -1).astype(BF16)
    w['g_q'] = _slot_gain([(0, p['g_qn'][i]), (ROPE_LO, p['g_qr'][i])])
    w['m_avg'] = _group_avg_matrix()
    w_uk = p['w_uk'][i].reshape(kv_lora, MLA_HEADS, QK_NOPE)
    w['w_uk'] = jnp.pad(w_uk, ((0, 0), (0, 0), (0, SLOT - QK_NOPE))).reshape(kv_lora, -1).astype(BF16)
    w['g_kn'] = _slot_gain([(0, p['g_kn'][i])])
    w_uv = p['w_uv'][i].reshape(kv_lora, MLA_HEADS // 2, 2, V_HEAD)
    eye2 = jnp.eye(2, dtype=w_uv.dtype)
    w['w_uv'] = jnp.einsum('cpjd,jk->cpjkd', w_uv, eye2).reshape(kv_lora, -1).astype(BF16)
    w['w_uv_plain'] = p['w_uv'][i].astype(BF16)
    place = np.zeros((QK_ROPE, SLOT), np.float32)
    place[np.arange(QK_ROPE), ROPE_LO + np.arange(QK_ROPE)] = 1.0
    w['rope_place'] = jnp.asarray(place, BF16)
    w['w_oa'] = p['w_oa'][i].astype(BF16)
    w['w_ob'] = p['w_ob'][i].astype(BF16)
    w['w_o'] = p['w_o'][i].astype(BF16)
    w['g_xattn'] = row(p['g_xattn'])
    w['g_mem'] = row(p['g_mem'])
    w['w_cq_mem'] = p['w_cq'][i].astype(BF16)
    w['g_cq'] = row(p['g_cq'])
    w['w_ck'] = p['w_ck'][i].astype(BF16)
    w['g_ck'] = row(p['g_ck'])
    w['w_cv'] = p['w_cv'][i].astype(BF16)
    w['w_co'] = p['w_co'][i].astype(BF16)
    w['g_ffn'] = row(p['g_ffn'])
    w['w_pq'] = p['w_pq'][i].astype(BF16)
    w['sub_keys'] = p['sub_keys'][i].reshape(2 * PEER_HEADS, N_KEYS, PEER_HALF).astype(BF16)
    w['peer_u'] = p['peer_u'][i].astype(BF16)
    w['peer_v'] = p['peer_v'][i].astype(BF16)
    w['w_s'] = p['w_s'][i]
    w['b_s'] = p['b_s'][i]
    return w


def _spatial_operands(w, seq, tm):
    chunk = min(seq, GM_CHUNK)
    reps = tm // chunk
    w_mask = jnp.tril(w['w_s'][:, :chunk, :chunk])
    w_mask = w_mask.astype(BF16)
    mix = jnp.concatenate(
        [jnp.pad(w_mask, ((0, 0), (0, 0), (a * chunk, (reps - 1 - a) * chunk))) for a in range(reps)], axis=1)
    bias = jnp.tile(w['b_s'][:, :chunk].T, (reps, 1))
    bias = jnp.repeat(bias, LANES, axis=1)
    return mix.astype(BF16), bias.astype(F32)


def _token_mixer(x2d, w, seq, pos, past, tm):
    t, d = x2d.shape
    nb = t // seq
    period = max(seq, tm)
    tabs = _rope_tables(jnp.tile(pos, period // seq))
    cq, ckv, kr, u, vg, siga, sigb = _in_proj(x2d, w, tabs, tm)
    mix, bias = _spatial_operands(w, seq, tm)
    gbob = _gmlp(u, vg, sigb, mix, bias, w['w_ob'], tm)
    q = _q_proj(cq, w, tabs, tm)
    if past is None:
        k, v = _kv_proj(ckv, kr, w, tm)
        o = _attn_prompt(q.reshape(nb, seq, -1), k.reshape(nb, seq, -1), v.reshape(nb, seq, -1),
                         _score_bound(w), ATTN_TILE)
    else:
        past_ckv, past_kr = past
        o = _attn_sample(q.reshape(nb, seq, -1), past_ckv, past_kr, ckv.reshape(nb, seq, -1),
                         kr.reshape(nb, seq, SLOT), w)
    x1, qc = _merge(x2d, o.reshape(t, -1), siga, gbob, w, tm)
    return x1, qc, ckv, kr[:, ROPE_LO:ROPE_LO + QK_ROPE], vg


def _tail(x1, qc, mk, mv, w, nb, seq, tm):
    t, d = x1.shape
    oc = _cross(qc.reshape(nb, seq, -1), mk, mv, min(seq, tm))
    x2, h, q_route = _cross_out(x1, oc.reshape(t, -1), w, tm)
    return _peer(h, x2, q_route, w, PEER_ROWS, PEER_EXPERTS)


def kernel(x_prompt, x_sample, cache_mla_ckv, cache_mla_krope, cache_mem_k, cache_mem_v, mem_prompt, g_mix, w_in, g_q_lat, w_uq, g_qn, g_qr, g_kv_lat, g_kr, w_uk, w_uv, g_kn, w_oa, g_gm, w_s, b_s, w_ob, w_o, g_xattn, g_mem, w_cq, g_cq, w_ck, g_ck, w_cv, w_co, g_ffn, w_pq, sub_keys, peer_u, peer_v):
    params = dict(g_mix=g_mix, w_in=w_in, g_q_lat=g_q_lat, w_uq=w_uq, g_qn=g_qn, g_qr=g_qr,
                  g_kv_lat=g_kv_lat, g_kr=g_kr, w_uk=w_uk, w_uv=w_uv, g_kn=g_kn, w_oa=w_oa, g_gm=g_gm,
                  w_s=w_s, b_s=b_s, w_ob=w_ob, w_o=w_o, g_xattn=g_xattn, g_mem=g_mem, w_cq=w_cq,
                  g_cq=g_cq, w_ck=w_ck, g_ck=g_ck, w_cv=w_cv, w_co=w_co, g_ffn=g_ffn, w_pq=w_pq,
                  sub_keys=sub_keys, peer_u=peer_u, peer_v=peer_v)
    bp, sp, d = x_prompt.shape
    bs, ss, _ = x_sample.shape
    depth = w_in.shape[0]
    past_len = cache_mla_ckv.shape[2]
    n_mem = mem_prompt.shape[1]
    tm = ROW_TILE
    assert (bp * sp) % tm == 0 and (bs * ss) % tm == 0 and sp % ATTN_TILE == 0
    pos_p = jnp.arange(sp)
    pos_s = past_len + jnp.arange(ss)
    xp = x_prompt.reshape(bp * sp, d)
    xs = x_sample.reshape(bs * ss, d)
    outs = [[] for _ in range(7)]
    for i in range(depth):
        w = _prep_weights(params, i)
        xp1, qcp, ckv_p, kr_p, _ = _token_mixer(xp, w, sp, pos_p, None, tm)
        xs1, qcs, ckv_s, kr_s, vg_s = _token_mixer(xs, w, ss, pos_s, (cache_mla_ckv[i], cache_mla_krope[i]), tm)
        mk_p, mv_p = _mem_kv(mem_prompt.reshape(bp * n_mem, d), w, min(tm, bp * n_mem))
        xp = _tail(xp1, qcp, mk_p.reshape(bp, n_mem, -1), mv_p.reshape(bp, n_mem, -1), w, bp, sp, tm)
        xs = _tail(xs1, qcs, cache_mem_k[i].reshape(bs, n_mem, -1), cache_mem_v[i].reshape(bs, n_mem, -1),
                   w, bs, ss, tm)
        outs[0].append(ckv_p.reshape(bp, sp, -1))
        outs[1].append(kr_p.reshape(bp, sp, -1))
        outs[2].append(mk_p.reshape(bp, n_mem, MEM_HEADS, MEM_HEAD_DIM))
        outs[3].append(mv_p.reshape(bp, n_mem, MEM_HEADS, MEM_HEAD_DIM))
        outs[4].append(ckv_s.reshape(bs, ss, -1))
        outs[5].append(kr_s.reshape(bs, ss, -1))
        outs[6].append(vg_s.reshape(bs, ss, -1))
    return (xp.reshape(bp, sp, d), xs.reshape(bs, ss, d)) + tuple(jnp.stack(o) for o in outs)
```

```python
import functools

import jax
import jax.numpy as jnp
import numpy as np
from jax import lax
from jax.experimental import pallas as pl
from jax.experimental.pallas import tpu as pltpu

CHUNK = 64
EPS = 1e-6
MLA_HEADS = 16
QK_NOPE = 64
QK_ROPE = 32
V_HEAD = 64
ROPE_THETA = 10000.0
MLA_SCALE = (QK_NOPE + QK_ROPE) ** -0.5
GM_CHUNK = 128
GM_GROUPS = 8
MEM_HEADS = 4
MEM_HEAD_DIM = 128
MEM_SCALE = MEM_HEAD_DIM ** -0.5
PEER_HEADS = 8
N_KEYS = 128
PEER_TOPK = 16
PEER_HALF = 128

LANES = 128
SLOT = LANES
ROPE_LO = QK_NOPE
ROPE_HALF = QK_ROPE // 2
ROW_TILE = 512
ATTN_TILE = 512
LOG2_E = float(np.log2(np.e))
STATIC_SHIFT_LIMIT = 48.0
BOUND_MARGIN = 1.02
PEER_ROWS = 512
PEER_EXPERTS = 1024
PEER_CHUNK = 512
MASK_BUILDS = 1
MASK_KEYS = N_KEYS // MASK_BUILDS
MASK_PITCH = MASK_KEYS + 8
MASK_UNROLL = 32
VMEM_LIMIT = 48 * 1024 * 1024
PEER_VMEM_LIMIT = 60 * 1024 * 1024

F32 = jnp.float32
BF16 = jnp.bfloat16
_NT = (((1,), (1,)), ((), ()))


def _dot(a, b):
    return jnp.dot(a, b, preferred_element_type=F32)


def _dot_nt(a, b):
    return lax.dot_general(a, b, _NT, preferred_element_type=F32)


def _rms(xf, g):
    return xf * lax.rsqrt(jnp.mean(xf * xf, axis=-1, keepdims=True) + EPS) * g


def _gelu(x):
    return 0.5 * x * (1.0 + lax.erf(x * np.float32(np.sqrt(0.5))))


def _group_mean(sq, m_ref):
    hi = sq.astype(BF16)
    lo = (sq - hi.astype(F32)).astype(BF16)
    return _dot(jnp.concatenate([hi, lo], axis=1), m_ref[...])


def _rope_slot(y, cos, sin_up, sin_dn):
    return (y * cos + pltpu.roll(y, ROPE_HALF, 1) * sin_up
            + pltpu.roll(y, SLOT - ROPE_HALF, 1) * sin_dn)


def _const_spec(shape):
    nd = len(shape)
    return pl.BlockSpec(shape, lambda *_: (0,) * nd, pipeline_mode=pl.Buffered(1))


def _row_spec(tm, width):
    return pl.BlockSpec((tm, width), lambda i: (i, 0))


def _params(sem, limit=VMEM_LIMIT):
    return pltpu.CompilerParams(dimension_semantics=sem, vmem_limit_bytes=limit)


def _in_proj_kernel(x_ref, gmix_ref, wcq_ref, wckv_ref, wkr_ref, wu_ref, wv_ref, wga_ref, wgb_ref,
                    gq_ref, gkv_ref, gkr_ref, ggm_ref, cos_ref, sup_ref, sdn_ref,
                    cq_out, ckv_out, kr_out, u_out, vg_out, siga_out, sigb_out):
    h = _rms(x_ref[...], gmix_ref[...]).astype(BF16)
    cq_out[...] = _rms(_dot(h, wcq_ref[...]), gq_ref[...]).astype(BF16)
    ckv_out[...] = _rms(_dot(h, wckv_ref[...]), gkv_ref[...])
    kr = _dot(h, wkr_ref[...])
    ms = jnp.sum(kr * kr, axis=-1, keepdims=True) * (1.0 / QK_ROPE)
    kr = kr * lax.rsqrt(ms + EPS) * gkr_ref[...]
    kr_out[...] = _rope_slot(kr, cos_ref[...], sup_ref[...], sdn_ref[...])
    u_out[...] = _gelu(_dot(h, wu_ref[...])).astype(BF16)
    vg_out[...] = _rms(_gelu(_dot(h, wv_ref[...])), ggm_ref[...])
    siga_out[...] = jax.nn.sigmoid(_dot(h, wga_ref[...])).astype(BF16)
    sigb_out[...] = jax.nn.sigmoid(_dot(h, wgb_ref[...])).astype(BF16)


def _in_proj(x2d, w, tabs, tm):
    t, d = x2d.shape
    cos, sup, sdn = tabs
    ntab = cos.shape[0] // tm
    tab_spec = pl.BlockSpec((tm, SLOT), lambda i: (i % ntab, 0))
    consts = [w['g_mix'], w['w_cq'], w['w_ckv'], w['w_kr'], w['w_u'], w['w_v'], w['w_ga'], w['w_gb'],
              w['g_q_lat'], w['g_kv_lat'], w['g_kr'], w['g_gm']]
    gw = w['w_u'].shape[1]
    widths = [w['w_cq'].shape[1], w['w_ckv'].shape[1], SLOT, gw, gw, d, d]
    dtypes = [BF16, F32, F32, BF16, F32, BF16, BF16]
    return pl.pallas_call(
        _in_proj_kernel,
        grid=(t // tm,),
        in_specs=[_row_spec(tm, d)] + [_const_spec(c.shape) for c in consts] + [tab_spec] * 3,
        out_specs=[_row_spec(tm, n) for n in widths],
        out_shape=[jax.ShapeDtypeStruct((t, n), dt) for n, dt in zip(widths, dtypes)],
        compiler_params=_params(("parallel",)),
        name="in_proj",
    )(x2d, *consts, cos, sup, sdn)


def _gmlp_kernel(u_ref, vg_ref, sigb_ref, mix_ref, bias_ref, wob_ref, out_ref):
    vgb = vg_ref[...].astype(BF16)
    parts = []
    for g in range(GM_GROUPS):
        sl = slice(g * LANES, (g + 1) * LANES)
        mixed = _dot(mix_ref[g], vgb[:, sl]) + bias_ref[:, sl]
        parts.append((u_ref[:, sl].astype(F32) * mixed).astype(BF16))
    ob = _dot(jnp.concatenate(parts, axis=1), wob_ref[...])
    out_ref[...] = sigb_ref[...].astype(F32) * ob


def _gmlp(u, vg, sigb, mix, bias, w_ob, tm):
    t, gw = u.shape
    d = w_ob.shape[1]
    return pl.pallas_call(
        _gmlp_kernel,
        grid=(t // tm,),
        in_specs=[_row_spec(tm, gw), _row_spec(tm, gw), _row_spec(tm, d),
                  _const_spec(mix.shape), _const_spec(bias.shape), _const_spec(w_ob.shape)],
        out_specs=_row_spec(tm, d),
        out_shape=jax.ShapeDtypeStruct((t, d), F32),
        compiler_params=_params(("parallel",)),
        name="gmlp",
    )(u, vg, sigb, mix, bias, w_ob)


def _q_proj_kernel(cq_ref, wuq_ref, gq_ref, mavg_ref, cos_ref, sup_ref, sdn_ref, q_out):
    q = _dot(cq_ref[...], wuq_ref[...])
    cos, sup, sdn = cos_ref[...], sup_ref[...], sdn_ref[...]
    for h in range(MLA_HEADS):
        sl = slice(h * SLOT, (h + 1) * SLOT)
        qs = q[:, sl]
        y = qs * lax.rsqrt(_group_mean(qs * qs, mavg_ref) + EPS) * gq_ref[...]
        q_out[:, sl] = _rope_slot(y, cos, sup, sdn).astype(BF16)


def _q_proj(cq, w, tabs, tm):
    t, ql = cq.shape
    cos, sup, sdn = tabs
    ntab = cos.shape[0] // tm
    tab_spec = pl.BlockSpec((tm, SLOT), lambda i: (i % ntab, 0))
    n = MLA_HEADS * SLOT
    return pl.pallas_call(
        _q_proj_kernel,
        grid=(t // tm,),
        in_specs=[_row_spec(tm, ql), _const_spec(w['w_uq'].shape), _const_spec(w['g_q'].shape),
                  _const_spec(w['m_avg'].shape)] + [tab_spec] * 3,
        out_specs=_row_spec(tm, n),
        out_shape=jax.ShapeDtypeStruct((t, n), BF16),
        compiler_params=_params(("parallel",)),
        name="q_proj",
    )(cq, w['w_uq'], w['g_q'], w['m_avg'], cos, sup, sdn)


def _kv_proj_kernel(ckv_ref, kr_ref, wuk_ref, wuv_ref, gkn_ref, mavg_ref, k_out, v_out):
    c = ckv_ref[...].astype(BF16)
    kn = _dot(c, wuk_ref[...])
    kr = kr_ref[...]
    for h in range(MLA_HEADS):
        sl = slice(h * SLOT, (h + 1) * SLOT)
        ks = kn[:, sl]
        y = ks * lax.rsqrt(_group_mean(ks * ks, mavg_ref) + EPS) * gkn_ref[...]
        k_out[:, sl] = (y + kr).astype(BF16)
    v_out[...] = _dot(c, wuv_ref[...]).astype(BF16)


def _kv_proj(ckv, kr, w, tm):
    t, kl = ckv.shape
    n = MLA_HEADS * SLOT
    return pl.pallas_call(
        _kv_proj_kernel,
        grid=(t // tm,),
        in_specs=[_row_spec(tm, kl), _row_spec(tm, SLOT), _const_spec(w['w_uk'].shape),
                  _const_spec(w['w_uv'].shape), _const_spec(w['g_kn'].shape),
                  _const_spec(w['m_avg'].shape)],
        out_specs=[_row_spec(tm, n), _row_spec(tm, n)],
        out_shape=[jax.ShapeDtypeStruct((t, n), BF16)] * 2,
        compiler_params=_params(("parallel",)),
        name="kv_proj",
    )(ckv, kr, w['w_uk'], w['w_uv'], w['g_kn'], w['m_avg'])


def _attn_prompt_kernel(bound_ref, q_ref, k_ref, v_ref, o_ref):
    tile = q_ref.shape[1]
    qi = pl.program_id(2)
    q = q_ref[0]
    bound = bound_ref[0]
    first = lax.broadcasted_iota(jnp.int32, (1, LANES), 1) < V_HEAD
    row_chunk = lax.broadcasted_iota(jnp.int32, (tile, tile), 0) // CHUNK
    col_chunk = lax.broadcasted_iota(jnp.int32, (tile, tile), 1) // CHUNK
    visible = col_chunk <= row_chunk

    def static_step(kstart, carry, masked):
        sums, acc = carry
        kstart = pl.multiple_of(kstart, tile)
        k = k_ref[0, pl.ds(kstart, tile), :]
        v = v_ref[0, pl.ds(kstart, tile), :]
        new_sums = []
        for h in range(2):
            sl = slice(h * SLOT, (h + 1) * SLOT)
            s = _dot_nt(q[:, sl], k[:, sl]) * F32(MLA_SCALE * LOG2_E) - bound
            if masked:
                s = jnp.where(visible, s, F32(-1e30))
            p = jnp.exp2(s)
            part = sums[h]
            for c in range(tile // LANES):
                part = part + p[:, c * LANES:(c + 1) * LANES]
            new_sums.append(part)
            acc = acc + _dot(p.astype(BF16), v[:, sl])
        return tuple(new_sums), acc

    @pl.when(bound < STATIC_SHIFT_LIMIT)
    def _():
        zeros = jnp.zeros((tile, LANES), F32)

        def pair(i, c):
            c = static_step(2 * i * tile, c, False)
            return static_step((2 * i + 1) * tile, c, False)

        def odd_tail(c):
            return static_step(qi * tile, static_step((qi - 1) * tile, c, False), True)

        carry = lax.fori_loop(0, qi // 2, pair, ((zeros, zeros), zeros))
        sums, acc = lax.cond(qi % 2 == 1, odd_tail, lambda c: static_step(qi * tile, c, True), carry)
        l0 = jnp.sum(sums[0], axis=-1, keepdims=True)
        l1 = jnp.sum(sums[1], axis=-1, keepdims=True)
        o_ref[0] = (acc / jnp.where(first, l0, l1)).astype(BF16)

    def step(kstart, carry, masked):
        ms, ls, acc = carry
        kstart = pl.multiple_of(kstart, tile)
        k = k_ref[0, pl.ds(kstart, tile), :]
        v = v_ref[0, pl.ds(kstart, tile), :]
        new_ms, new_ls, alphas, pvs = [], [], [], []
        for h in range(2):
            sl = slice(h * SLOT, (h + 1) * SLOT)
            s = _dot_nt(q[:, sl], k[:, sl]) * F32(MLA_SCALE * LOG2_E)
            if masked:
                s = jnp.where(visible, s, F32(-1e30))
            m_new = jnp.maximum(ms[h], jnp.max(s, axis=-1, keepdims=True))
            alpha = jnp.exp2(ms[h] - m_new)
            p = jnp.exp2(s - m_new)
            new_ls.append(alpha * ls[h] + jnp.sum(p, axis=-1, keepdims=True))
            new_ms.append(m_new)
            alphas.append(alpha)
            pvs.append(_dot(p.astype(BF16), v[:, sl]))
        acc = acc * jnp.where(first, alphas[0], alphas[1]) + pvs[0] + pvs[1]
        return tuple(new_ms), tuple(new_ls), acc

    @pl.when(bound >= STATIC_SHIFT_LIMIT)
    def _():
        neg = jnp.full((tile, 1), -1e30, F32)
        zero = jnp.zeros((tile, 1), F32)
        init = ((neg, neg), (zero, zero), jnp.zeros((tile, LANES), F32))
        carry = lax.fori_loop(0, qi, lambda i, c: step(i * tile, c, False), init)
        _, ls, acc = step(qi * tile, carry, True)
        o_ref[0] = (acc / jnp.where(first, ls[0], ls[1])).astype(BF16)


def _score_bound(w):
    gmax = lambda g, lo, n: jnp.max(jnp.abs(g[0, lo:lo + n]))
    nope = QK_NOPE * gmax(w['g_q'], 0, QK_NOPE) * gmax(w['g_kn'], 0, QK_NOPE)
    rope = QK_ROPE * gmax(w['g_q'], ROPE_LO, QK_ROPE) * gmax(w['g_kr'], ROPE_LO, QK_ROPE)
    return (BOUND_MARGIN * MLA_SCALE * LOG2_E * (nope + rope)).reshape(1).astype(F32)


def _attn_prompt(q, k, v, bound, tile):
    b, s, _ = q.shape
    pairs = MLA_HEADS // 2
    return pl.pallas_call(
        _attn_prompt_kernel,
        grid=(b, pairs, s // tile),
        in_specs=[pl.BlockSpec(memory_space=pltpu.SMEM),
                  pl.BlockSpec((1, tile, 2 * SLOT), lambda bi, p, i: (bi, i, p)),
                  pl.BlockSpec((1, s, 2 * SLOT), lambda bi, p, i: (bi, 0, p)),
                  pl.BlockSpec((1, s, 2 * SLOT), lambda bi, p, i: (bi, 0, p))],
        out_specs=pl.BlockSpec((1, tile, LANES), lambda bi, p, i: (bi, i, p)),
        out_shape=jax.ShapeDtypeStruct((b, s, pairs * LANES), BF16),
        compiler_params=_params(("parallel", "parallel", "arbitrary")),
        name="attn_prompt",
    )(bound, q, k, v)


def _attn_sample_kernel(q_ref, pckv_ref, pkr_ref, nckv_ref, nkr_ref, wuk_ref, wuv_ref, gkn_ref, place_ref,
                        o_ref):
    sq = q_ref.shape[1]
    rows = MLA_HEADS * sq
    gkn = gkn_ref[...]

    def keys_values(ckv, kr_slot):
        c = ckv.astype(BF16)
        kn = _dot(c, wuk_ref[...])
        parts = []
        for h in range(MLA_HEADS):
            ks = kn[:, h * SLOT:(h + 1) * SLOT]
            ms = jnp.sum(ks * ks, axis=-1, keepdims=True) * (1.0 / QK_NOPE)
            parts.append((ks * lax.rsqrt(ms + EPS) * gkn + kr_slot).astype(BF16))
        return jnp.concatenate(parts, axis=1), _dot(c, wuv_ref[...]).astype(BF16)

    past_kr = _dot(pkr_ref[0].astype(BF16), place_ref[...])
    k_p, v_p = keys_values(pckv_ref[0], past_kr)
    k_n, v_n = keys_values(nckv_ref[0], nkr_ref[0])
    k = jnp.concatenate([k_p, k_n], axis=0)
    v = jnp.concatenate([v_p, v_n], axis=0)

    qt = jnp.concatenate([q_ref[0].astype(F32)] * MLA_HEADS, axis=0)
    q_shape = (rows, MLA_HEADS * SLOT)
    own = (lax.broadcasted_iota(jnp.int32, q_shape, 0) // sq
           == lax.broadcasted_iota(jnp.int32, q_shape, 1) // SLOT)
    qbd = jnp.where(own, qt, 0.0).astype(BF16)
    s = _dot_nt(qbd, k) * MLA_SCALE
    p = jnp.exp(s - jnp.max(s, axis=-1, keepdims=True))
    p = p / jnp.sum(p, axis=-1, keepdims=True)
    o_all = _dot(p.astype(BF16), v)
    o_shape = o_all.shape
    own = (lax.broadcasted_iota(jnp.int32, o_shape, 0) // sq
           == lax.broadcasted_iota(jnp.int32, o_shape, 1) // V_HEAD)
    o_all = jnp.where(own, o_all, 0.0)
    o = o_all[0:sq]
    for h in range(1, MLA_HEADS):
        o = o + o_all[h * sq:(h + 1) * sq]
    o_ref[0] = o.astype(BF16)


def _attn_sample(q, past_ckv, past_kr, new_ckv, new_kr, w):
    b, sq, nq = q.shape
    npast, kl = past_ckv.shape[1:]
    nv = w['w_uv_plain'].shape[1]
    stream = lambda n, width: pl.BlockSpec((1, n, width), lambda bi: (bi, 0, 0))
    consts = [w['w_uk'], w['w_uv_plain'], w['g_kn'], w['rope_place']]
    return pl.pallas_call(
        _attn_sample_kernel,
        grid=(b,),
        in_specs=[stream(sq, nq), stream(npast, kl), stream(npast, QK_ROPE), stream(sq, kl), stream(sq, SLOT)]
        + [_const_spec(c.shape) for c in consts],
        out_specs=stream(sq, nv),
        out_shape=jax.ShapeDtypeStruct((b, sq, nv), BF16),
        compiler_params=_params(("parallel",)),
        name="attn_sample",
    )(q, past_ckv, past_kr, new_ckv, new_kr, *consts)


def _merge_kernel(x_ref, o_ref, siga_ref, gbob_ref, woa_ref, wo_ref, gx_ref, wcq_ref, gcq_ref,
                  x1_out, qc_out):
    oa = _dot(o_ref[...], woa_ref[...])
    merged = siga_ref[...].astype(F32) * oa + gbob_ref[...]
    x1 = x_ref[...] + _dot(merged.astype(BF16), wo_ref[...])
    x1_out[...] = x1
    qc = _dot(_rms(x1, gx_ref[...]).astype(BF16), wcq_ref[...])
    for h in range(MEM_HEADS):
        sl = slice(h * MEM_HEAD_DIM, (h + 1) * MEM_HEAD_DIM)
        qc_out[:, sl] = _rms(qc[:, sl], gcq_ref[...]).astype(BF16)


def _merge(x2d, o, siga, gbob, w, tm):
    t, d = x2d.shape
    consts = [w['w_oa'], w['w_o'], w['g_xattn'], w['w_cq_mem'], w['g_cq']]
    nq = w['w_cq_mem'].shape[1]
    return pl.pallas_call(
        _merge_kernel,
        grid=(t // tm,),
        in_specs=[_row_spec(tm, d), _row_spec(tm, o.shape[1]), _row_spec(tm, d), _row_spec(tm, d)]
        + [_const_spec(c.shape) for c in consts],
        out_specs=[_row_spec(tm, d), _row_spec(tm, nq)],
        out_shape=[jax.ShapeDtypeStruct((t, d), F32), jax.ShapeDtypeStruct((t, nq), BF16)],
        compiler_params=_params(("parallel",)),
        name="merge",
    )(x2d, o, siga, gbob, *consts)


def _mem_kv_kernel(mem_ref, gmem_ref, wck_ref, gck_ref, wcv_ref, k_out, v_out):
    m = _rms(mem_ref[...], gmem_ref[...]).astype(BF16)
    k = _dot(m, wck_ref[...])
    for h in range(MEM_HEADS):
        sl = slice(h * MEM_HEAD_DIM, (h + 1) * MEM_HEAD_DIM)
        k_out[:, sl] = _rms(k[:, sl], gck_ref[...])
    v_out[...] = _dot(m, wcv_ref[...])


def _mem_kv(mem2d, w, tm):
    t, d = mem2d.shape
    consts = [w['g_mem'], w['w_ck'], w['g_ck'], w['w_cv']]
    n = w['w_ck'].shape[1]
    return pl.pallas_call(
        _mem_kv_kernel,
        grid=(t // tm,),
        in_specs=[_row_spec(tm, d)] + [_const_spec(c.shape) for c in consts],
        out_specs=[_row_spec(tm, n)] * 2,
        out_shape=[jax.ShapeDtypeStruct((t, n), F32)] * 2,
        compiler_params=_params(("parallel",)),
        name="mem_kv",
    )(mem2d, *consts)


def _cross_kernel(q_ref, k_ref, v_ref, o_ref):
    q = q_ref[0]
    k = k_ref[0].astype(BF16)
    v = v_ref[0].astype(BF16)
    for h in range(MEM_HEADS):
        sl = slice(h * MEM_HEAD_DIM, (h + 1) * MEM_HEAD_DIM)
        s = _dot_nt(q[:, sl], k[:, sl]) * MEM_SCALE
        p = jnp.exp(s - jnp.max(s, axis=-1, keepdims=True))
        p = p / jnp.sum(p, axis=-1, keepdims=True)
        o_ref[0, :, sl] = _dot(p.astype(BF16), v[:, sl]).astype(BF16)


def _cross(qc, mk, mv, tc):
    b, s, n = qc.shape
    nm = mk.shape[1]
    return pl.pallas_call(
        _cross_kernel,
        grid=(b, s // tc),
        in_specs=[pl.BlockSpec((1, tc, n), lambda bi, i: (bi, i, 0)),
                  pl.BlockSpec((1, nm, n), lambda bi, i: (bi, 0, 0)),
                  pl.BlockSpec((1, nm, n), lambda bi, i: (bi, 0, 0))],
        out_specs=pl.BlockSpec((1, tc, n), lambda bi, i: (bi, i, 0)),
        out_shape=jax.ShapeDtypeStruct((b, s, n), BF16),
        compiler_params=_params(("parallel", "parallel")),
        name="cross_attn",
    )(qc, mk, mv)


def _cross_out_kernel(x1_ref, oc_ref, wco_ref, gffn_ref, wpq_ref, x2_out, h_out, q_out):
    x2 = x1_ref[...] + _dot(oc_ref[...], wco_ref[...])
    x2_out[...] = x2
    h = _rms(x2, gffn_ref[...]).astype(BF16)
    h_out[...] = h
    q_out[...] = _dot(h, wpq_ref[...]).astype(BF16)


def _cross_out(x1, oc, w, tm):
    t, d = x1.shape
    nq = w['w_pq'].shape[1]
    return pl.pallas_call(
        _cross_out_kernel,
        grid=(t // tm,),
        in_specs=[_row_spec(tm, d), _row_spec(tm, oc.shape[1]), _const_spec(w['w_co'].shape),
                  _const_spec(w['g_ffn'].shape), _const_spec(w['w_pq'].shape)],
        out_specs=[_row_spec(tm, d), _row_spec(tm, d), _row_spec(tm, nq)],
        out_shape=[jax.ShapeDtypeStruct((t, d), F32), jax.ShapeDtypeStruct((t, d), BF16),
                   jax.ShapeDtypeStruct((t, nq), BF16)],
        compiler_params=_params(("parallel",)),
        name="cross_out",
    )(x1, oc, w['w_co'], w['g_ffn'], w['w_pq'])


def _pair_blocks():
    k = PEER_TOPK
    blocks, flat = [], []
    n_row = sum(1 for a in range(k) if k // (a + 1) > 2)
    for a in range(n_row):
        nb = k if a == 0 else 8
        blocks.append(('row', a, nb))
        flat += [a * k + b if (a + 1) * (b + 1) <= k else -1 for b in range(nb)]
    for b in range(k // (n_row + 1)):
        na = k if b == 0 else 8
        blocks.append(('col', b, na))
        flat += [a * k + b if (a >= n_row and (a + 1) * (b + 1) <= k) else -1 for a in range(na)]
    return blocks, np.asarray(flat, np.float32)


def _pair_candidates(v0, v1, blocks, combine):
    parts = []
    for kind, fixed, n in blocks:
        if kind == 'row':
            parts.append(combine(v0[fixed:fixed + 1, :], v1[0:n, :]))
        else:
            parts.append(combine(v0[0:n, :], v1[fixed:fixed + 1, :]))
    return jnp.concatenate(parts, axis=0)


def _peer_kernel(h_ref, x_ref, q_ref, keys_ref, flat_ref, u_ref, v_ref, out_ref,
                 mask_scr, sv_scr, si_scr, ts_scr, te_scr, it_scr, jt_scr, gt_scr,
                 isel_scr, jsel_scr, gsel_scr):
    tm = h_ref.shape[0]
    te = v_ref.shape[0]
    r = pl.program_id(0)
    e = pl.program_id(1)
    rows_per_step = te // N_KEYS
    steps_per_build = MASK_KEYS // rows_per_step
    blocks, _ = _pair_blocks()
    neg_inf = F32(-jnp.inf)

    def build_mask(part):
        key_rows = (lax.broadcasted_iota(jnp.int32, (MASK_KEYS, LANES), 0) + part * MASK_KEYS).astype(F32)
        col_rows = lax.broadcasted_iota(jnp.int32, (N_KEYS, LANES), 0).astype(F32)

        def token_body(t, _):
            irow = isel_scr[pl.ds(t, 1), :]
            jrow = jsel_scr[pl.ds(t, 1), :]
            grow = gsel_scr[pl.ds(t, 1), :]
            rw = jnp.where(key_rows == irow, 0.5 * grow, 0.0).astype(BF16)
            cw = jnp.where(col_rows == jrow, 1.0, 0.0).astype(BF16)
            r0 = pl.multiple_of(t * MASK_PITCH, 8)
            mask_scr[pl.ds(r0, MASK_KEYS), :] = _dot_nt(rw, cw)
            return 0

        lax.fori_loop(0, tm, token_body, 0, unroll=MASK_UNROLL)

    def rank_pairs(hd, _):
        sv0, sv1 = sv_scr[2 * hd], sv_scr[2 * hd + 1]
        si0, si1 = si_scr[2 * hd], si_scr[2 * hd + 1]
        flat = flat_ref[...]
        cand = _pair_candidates(sv0, sv1, blocks, lambda x, y: x + y)
        cand = jnp.where(flat >= 0.0, cand, neg_inf)
        cidx = _pair_candidates(si0, si1, blocks, lambda x, y: x * F32(N_KEYS) + y)
        for kk in range(PEER_TOPK):
            m = jnp.max(cand, axis=0, keepdims=True)
            pos = jnp.min(jnp.where(cand == m, flat, F32(PEER_TOPK * PEER_TOPK)), axis=0, keepdims=True)
            sel = flat == pos
            ts_scr[kk:kk + 1, :] = m
            te_scr[kk:kk + 1, :] = jnp.max(jnp.where(sel, cidx, F32(-1.0)), axis=0, keepdims=True)
            cand = jnp.where(sel, neg_inf, cand)
        top = ts_scr[...]
        ex = jnp.exp(top - top[0:1, :])
        gate = ex / jnp.sum(ex, axis=0, keepdims=True)
        eidx = te_scr[...]
        irow = jnp.floor(eidx * F32(1.0 / N_KEYS))
        r0 = pl.multiple_of(hd * PEER_TOPK, PEER_TOPK)
        it_scr[pl.ds(r0, PEER_TOPK), :] = irow
        jt_scr[pl.ds(r0, PEER_TOPK), :] = eidx - irow * F32(N_KEYS)
        gt_scr[pl.ds(r0, PEER_TOPK), :] = gate
        return 0

    @pl.when(e == 0)
    def _():
        out_ref[...] = x_ref[...]

        @pl.when(r > 0)
        def _():
            lax.fori_loop(0, PEER_HEADS, rank_pairs, 0)
            isel_scr[...] = it_scr[...].T
            jsel_scr[...] = jt_scr[...].T
            gsel_scr[...] = gt_scr[...].T
            build_mask(0)

    for part in range(1, MASK_BUILDS):
        @pl.when(jnp.logical_and(e == part * steps_per_build, r > 0))
        def _():
            build_mask(part)

    def slot_scores():
        d0 = pl.multiple_of(e * PEER_HALF, PEER_HALF)
        return _dot_nt(q_ref[:, pl.ds(d0, PEER_HALF)], keys_ref[e])

    def rank_slot(scores):
        lane = lax.broadcasted_iota(jnp.int32, (tm, N_KEYS), 1)
        picked = jnp.zeros((tm, N_KEYS), jnp.int32)
        a = scores
        for kk in range(PEER_TOPK):
            idx = jnp.argmax(a, axis=-1, keepdims=True).astype(jnp.int32)
            a = jnp.where(lane == idx, neg_inf, a)
            picked = jnp.where(lane == kk, idx, picked)
        vals = jnp.take_along_axis(scores, picked, axis=1)
        sv_scr[e] = vals.T[0:PEER_TOPK, :]
        si_scr[e] = (N_KEYS - 1 - picked).astype(F32).T[0:PEER_TOPK, :]

    def expert_chunk():
        h = h_ref[...]
        row0 = lax.rem(e, steps_per_build) * rows_per_step
        weighted = []
        for c in range(te // PEER_CHUNK):
            rows = slice(c * PEER_CHUNK, (c + 1) * PEER_CHUNK)
            pre = _dot_nt(h, u_ref[rows, :])
            half_mask = jnp.concatenate(
                [mask_scr[pl.ds(row0 + c * (PEER_CHUNK // N_KEYS) + rr, tm, stride=MASK_PITCH), :]
                 for rr in range(PEER_CHUNK // N_KEYS)], axis=1)
            weighted.append((pre * (1.0 + lax.erf(pre * np.float32(np.sqrt(0.5)))) * half_mask).astype(BF16))
        out_ref[...] += _dot(jnp.concatenate(weighted, axis=1), v_ref[...])

    last = pl.num_programs(0) - 1

    @pl.when(r == 0)
    def _():
        rank_slot(slot_scores())

    @pl.when(jnp.logical_and(r > 0, r < last))
    def _():
        a = slot_scores()
        expert_chunk()
        rank_slot(a)

    @pl.when(r == last)
    def _():
        expert_chunk()


def _peer(h, x2, q_route, w, tm, te):
    t, d = h.shape
    n = t // tm
    nexp = w['peer_v'].shape[0]
    assert nexp // te == 2 * PEER_HEADS, "one half-key slot is ranked per expert step"
    nsel = PEER_HEADS * PEER_TOPK
    _, flat = _pair_blocks()
    flat = jnp.asarray(np.repeat(flat[:, None], tm, axis=1))
    dense_tile = lambda width: pl.BlockSpec((tm, width), lambda r, e: (jnp.maximum(r - 1, 0), 0),
                                            pipeline_mode=pl.Buffered(1))
    expert_tile = pl.BlockSpec((te, d), lambda r, e: (e, 0))
    return pl.pallas_call(
        _peer_kernel,
        grid=(n + 1, nexp // te),
        in_specs=[dense_tile(d), dense_tile(d),
                  pl.BlockSpec((tm, q_route.shape[1]), lambda r, e: (jnp.minimum(r, n - 1), 0),
                               pipeline_mode=pl.Buffered(1)),
                  _const_spec(w['sub_keys'].shape), _const_spec(flat.shape), expert_tile, expert_tile],
        out_specs=pl.BlockSpec((tm, d), lambda r, e: (jnp.maximum(r - 1, 0), 0)),
        out_shape=jax.ShapeDtypeStruct((t, d), F32),
        scratch_shapes=[pltpu.VMEM((tm * MASK_PITCH, LANES), F32),
                        pltpu.VMEM((2 * PEER_HEADS, PEER_TOPK, tm), F32),
                        pltpu.VMEM((2 * PEER_HEADS, PEER_TOPK, tm), F32),
                        pltpu.VMEM((PEER_TOPK, tm), F32),
                        pltpu.VMEM((PEER_TOPK, tm), F32),
                        pltpu.VMEM((nsel, tm), F32),
                        pltpu.VMEM((nsel, tm), F32),
                        pltpu.VMEM((nsel, tm), F32),
                        pltpu.VMEM((tm, nsel), F32),
                        pltpu.VMEM((tm, nsel), F32),
                        pltpu.VMEM((tm, nsel), F32)],
        compiler_params=_params(("arbitrary", "arbitrary"), PEER_VMEM_LIMIT),
        name="peer",
    )(h, x2, q_route, w['sub_keys'], flat, w['peer_u'], w['peer_v'])


def _slot_gain(parts):
    row = jnp.zeros((SLOT,), F32)
    for off, g in parts:
        row = lax.dynamic_update_slice(row, g.astype(F32), (off,))
    return row[None, :]


def _group_avg_matrix():
    m = np.zeros((SLOT, SLOT), np.float32)
    m[:QK_NOPE, :QK_NOPE] = 1.0 / QK_NOPE
    m[ROPE_LO:ROPE_LO + QK_ROPE, ROPE_LO:ROPE_LO + QK_ROPE] = 1.0 / QK_ROPE
    return jnp.asarray(np.concatenate([m, m], axis=0), BF16)


def _rope_tables(pos):
    inv = ROPE_THETA ** (-jnp.arange(ROPE_HALF, dtype=F32) / ROPE_HALF)
    ang = pos.astype(F32)[:, None] * inv[None, :]
    cos, sin = jnp.cos(ang), jnp.sin(ang)
    n = pos.shape[0]
    tail = SLOT - ROPE_LO - QK_ROPE
    cos_t = jnp.concatenate([jnp.ones((n, ROPE_LO), F32), cos, cos, jnp.zeros((n, tail), F32)], axis=1)
    sin_up = jnp.concatenate([jnp.zeros((n, ROPE_LO + ROPE_HALF), F32), sin, jnp.zeros((n, tail), F32)], axis=1)
    sin_dn = jnp.concatenate([jnp.zeros((n, ROPE_LO), F32), -sin, jnp.zeros((n, tail + ROPE_HALF), F32)], axis=1)
    return cos_t, sin_up, sin_dn


def _prep_weights(p, i):
    w_in = p['w_in'][i]
    q_lora = p['g_q_lat'].shape[1]
    kv_lora = p['g_kv_lat'].shape[1]
    gw = p['g_gm'].shape[1]
    d = w_in.shape[0]
    o = np.cumsum([0, q_lora, kv_lora, QK_ROPE, gw, gw, d, d])
    seg = lambda k: w_in[:, o[k]:o[k + 1]]
    row = lambda a: a[i][None, :].astype(F32)
    w = {}
    w['g_mix'] = row(p['g_mix'])
    w['w_cq'] = seg(0).astype(BF16)
    w['w_ckv'] = seg(1).astype(BF16)
    w['w_kr'] = jnp.pad(seg(2), ((0, 0), (ROPE_LO, SLOT - ROPE_LO - QK_ROPE))).astype(BF16)
    w['w_u'] = seg(3).astype(BF16)
    w['w_v'] = seg(4).astype(BF16)
    w['w_ga'] = seg(5).astype(BF16)
    w['w_gb'] = seg(6).astype(BF16)
    w['g_q_lat'] = row(p['g_q_lat'])
    w['g_kv_lat'] = row(p['g_kv_lat'])
    w['g_kr'] = _slot_gain([(ROPE_LO, p['g_kr'][i])])
    w['g_gm'] = row(p['g_gm'])
    head_dim = QK_NOPE + QK_ROPE
    w_uq = p['w_uq'][i].reshape(q_lora, MLA_HEADS, head_dim)
    w['w_uq'] = jnp.pad(w_uq, ((0, 0), (0, 0), (0, SLOT - head_dim))).reshape(q_lora, -1).astype(BF16)
    w['g_q'] = _slot_gain([(0, p['g_qn'][i]), (ROPE_LO, p['g_qr'][i])])
    w['m_avg'] = _group_avg_matrix()
    w_uk = p['w_uk'][i].reshape(kv_lora, MLA_HEADS, QK_NOPE)
    w['w_uk'] = jnp.pad(w_uk, ((0, 0), (0, 0), (0, SLOT - QK_NOPE))).reshape(kv_lora, -1).astype(BF16)
    w['g_kn'] = _slot_gain([(0, p['g_kn'][i])])
    w_uv = p['w_uv'][i].reshape(kv_lora, MLA_HEADS // 2, 2, V_HEAD)
    eye2 = jnp.eye(2, dtype=w_uv.dtype)
    w['w_uv'] = jnp.einsum('cpjd,jk->cpjkd', w_uv, eye2).reshape(kv_lora, -1).astype(BF16)
    w['w_uv_plain'] = p['w_uv'][i].astype(BF16)
    place = np.zeros((QK_ROPE, SLOT), np.float32)
    place[np.arange(QK_ROPE), ROPE_LO + np.arange(QK_ROPE)] = 1.0
    w['rope_place'] = jnp.asarray(place, BF16)
    w['w_oa'] = p['w_oa'][i].astype(BF16)
    w['w_ob'] = p['w_ob'][i].astype(BF16)
    w['w_o'] = p['w_o'][i].astype(BF16)
    w['g_xattn'] = row(p['g_xattn'])
    w['g_mem'] = row(p['g_mem'])
    w['w_cq_mem'] = p['w_cq'][i].astype(BF16)
    w['g_cq'] = row(p['g_cq'])
    w['w_ck'] = p['w_ck'][i].astype(BF16)
    w['g_ck'] = row(p['g_ck'])
    w['w_cv'] = p['w_cv'][i].astype(BF16)
    w['w_co'] = p['w_co'][i].astype(BF16)
    w['g_ffn'] = row(p['g_ffn'])
    w['w_pq'] = p['w_pq'][i].astype(BF16)
    w['sub_keys'] = p['sub_keys'][i].reshape(2 * PEER_HEADS, N_KEYS, PEER_HALF)[:, ::-1, :].astype(BF16)
    w['peer_u'] = p['peer_u'][i].astype(BF16)
    w['peer_v'] = p['peer_v'][i].astype(BF16)
    w['w_s'] = p['w_s'][i]
    w['b_s'] = p['b_s'][i]
    return w


def _spatial_operands(w, seq, tm):
    chunk = min(seq, GM_CHUNK)
    reps = tm // chunk
    w_mask = jnp.tril(w['w_s'][:, :chunk, :chunk])
    w_mask = w_mask.astype(BF16)
    mix = jnp.concatenate(
        [jnp.pad(w_mask, ((0, 0), (0, 0), (a * chunk, (reps - 1 - a) * chunk))) for a in range(reps)], axis=1)
    bias = jnp.tile(w['b_s'][:, :chunk].T, (reps, 1))
    bias = jnp.repeat(bias, LANES, axis=1)
    return mix.astype(BF16), bias.astype(F32)


def _token_mixer(x2d, w, seq, pos, past, tm):
    t, d = x2d.shape
    nb = t // seq
    period = max(seq, tm)
    tabs = _rope_tables(jnp.tile(pos, period // seq))
    cq, ckv, kr, u, vg, siga, sigb = _in_proj(x2d, w, tabs, tm)
    mix, bias = _spatial_operands(w, seq, tm)
    gbob = _gmlp(u, vg, sigb, mix, bias, w['w_ob'], tm)
    q = _q_proj(cq, w, tabs, tm)
    if past is None:
        k, v = _kv_proj(ckv, kr, w, tm)
        o = _attn_prompt(q.reshape(nb, seq, -1), k.reshape(nb, seq, -1), v.reshape(nb, seq, -1),
                         _score_bound(w), ATTN_TILE)
    else:
        past_ckv, past_kr = past
        o = _attn_sample(q.reshape(nb, seq, -1), past_ckv, past_kr, ckv.reshape(nb, seq, -1),
                         kr.reshape(nb, seq, SLOT), w)
    x1, qc = _merge(x2d, o.reshape(t, -1), siga, gbob, w, tm)
    return x1, qc, ckv, kr[:, ROPE_LO:ROPE_LO + QK_ROPE], vg


def _tail(x1, qc, mk, mv, w, nb, seq, tm):
    t, d = x1.shape
    oc = _cross(qc.reshape(nb, seq, -1), mk, mv, min(seq, tm))
    x2, h, q_route = _cross_out(x1, oc.reshape(t, -1), w, tm)
    return _peer(h, x2, q_route, w, PEER_ROWS, PEER_EXPERTS)


def kernel(x_prompt, x_sample, cache_mla_ckv, cache_mla_krope, cache_mem_k, cache_mem_v, mem_prompt, g_mix, w_in, g_q_lat, w_uq, g_qn, g_qr, g_kv_lat, g_kr, w_uk, w_uv, g_kn, w_oa, g_gm, w_s, b_s, w_ob, w_o, g_xattn, g_mem, w_cq, g_cq, w_ck, g_ck, w_cv, w_co, g_ffn, w_pq, sub_keys, peer_u, peer_v):
    params = dict(g_mix=g_mix, w_in=w_in, g_q_lat=g_q_lat, w_uq=w_uq, g_qn=g_qn, g_qr=g_qr,
                  g_kv_lat=g_kv_lat, g_kr=g_kr, w_uk=w_uk, w_uv=w_uv, g_kn=g_kn, w_oa=w_oa, g_gm=g_gm,
                  w_s=w_s, b_s=b_s, w_ob=w_ob, w_o=w_o, g_xattn=g_xattn, g_mem=g_mem, w_cq=w_cq,
                  g_cq=g_cq, w_ck=w_ck, g_ck=g_ck, w_cv=w_cv, w_co=w_co, g_ffn=g_ffn, w_pq=w_pq,
                  sub_keys=sub_keys, peer_u=peer_u, peer_v=peer_v)
    bp, sp, d = x_prompt.shape
    bs, ss, _ = x_sample.shape
    depth = w_in.shape[0]
    past_len = cache_mla_ckv.shape[2]
    n_mem = mem_prompt.shape[1]
    tm = ROW_TILE
    assert (bp * sp) % tm == 0 and (bs * ss) % tm == 0 and sp % ATTN_TILE == 0
    pos_p = jnp.arange(sp)
    pos_s = past_len + jnp.arange(ss)
    xp = x_prompt.reshape(bp * sp, d)
    xs = x_sample.reshape(bs * ss, d)
    outs = [[] for _ in range(7)]
    for i in range(depth):
        w = _prep_weights(params, i)
        xp1, qcp, ckv_p, kr_p, _ = _token_mixer(xp, w, sp, pos_p, None, tm)
        xs1, qcs, ckv_s, kr_s, vg_s = _token_mixer(xs, w, ss, pos_s, (cache_mla_ckv[i], cache_mla_krope[i]), tm)
        mk_p, mv_p = _mem_kv(mem_prompt.reshape(bp * n_mem, d), w, min(tm, bp * n_mem))
        xp = _tail(xp1, qcp, mk_p.reshape(bp, n_mem, -1), mv_p.reshape(bp, n_mem, -1), w, bp, sp, tm)
        xs = _tail(xs1, qcs, cache_mem_k[i].reshape(bs, n_mem, -1), cache_mem_v[i].reshape(bs, n_mem, -1),
                   w, bs, ss, tm)
        outs[0].append(ckv_p.reshape(bp, sp, -1))
        outs[1].append(kr_p.reshape(bp, sp, -1))
        outs[2].append(mk_p.reshape(bp, n_mem, MEM_HEADS, MEM_HEAD_DIM))
        outs[3].append(mv_p.reshape(bp, n_mem, MEM_HEADS, MEM_HEAD_DIM))
        outs[4].append(ckv_s.reshape(bs, ss, -1))
        outs[5].append(kr_s.reshape(bs, ss, -1))
        outs[6].append(vg_s.reshape(bs, ss, -1))
    return (xp.reshape(bp, sp, d), xs.reshape(bs, ss, d)) + tuple(jnp.stack(o) for o in outs)
```

```python
import functools

import jax
import jax.numpy as jnp
import numpy as np
from jax import lax
from jax.experimental import pallas as pl
from jax.experimental.pallas import tpu as pltpu

CHUNK = 64
EPS = 1e-6
MLA_HEADS = 16
QK_NOPE = 64
QK_ROPE = 32
V_HEAD = 64
ROPE_THETA = 10000.0
MLA_SCALE = (QK_NOPE + QK_ROPE) ** -0.5
GM_CHUNK = 128
GM_GROUPS = 8
MEM_HEADS = 4
MEM_HEAD_DIM = 128
MEM_SCALE = MEM_HEAD_DIM ** -0.5
PEER_HEADS = 8
N_KEYS = 128
PEER_TOPK = 16
PEER_HALF = 128

LANES = 128
SLOT = LANES
ROPE_LO = QK_NOPE
ROPE_HALF = QK_ROPE // 2
ROW_TILE = 512
ATTN_TILE = 512
LOG2_E = float(np.log2(np.e))
STATIC_SHIFT_LIMIT = 48.0
BOUND_MARGIN = 1.02
PEER_ROWS = 512
PEER_EXPERTS = 1024
PEER_CHUNK = 512
MASK_PITCH = N_KEYS + 8
MASK_UNROLL = 32
VMEM_LIMIT = 48 * 1024 * 1024
PEER_VMEM_LIMIT = 60 * 1024 * 1024

F32 = jnp.float32
BF16 = jnp.bfloat16
_NT = (((1,), (1,)), ((), ()))


def _dot(a, b):
    return jnp.dot(a, b, preferred_element_type=F32)


def _dot_nt(a, b):
    return lax.dot_general(a, b, _NT, preferred_element_type=F32)


def _rms(xf, g):
    return xf * lax.rsqrt(jnp.mean(xf * xf, axis=-1, keepdims=True) + EPS) * g


def _gelu(x):
    return 0.5 * x * (1.0 + lax.erf(x * np.float32(np.sqrt(0.5))))


def _group_mean(sq, m_ref):
    hi = sq.astype(BF16)
    lo = (sq - hi.astype(F32)).astype(BF16)
    return _dot(jnp.concatenate([hi, lo], axis=1), m_ref[...])


def _rope_slot(y, cos, sin_up, sin_dn):
    return (y * cos + pltpu.roll(y, ROPE_HALF, 1) * sin_up
            + pltpu.roll(y, SLOT - ROPE_HALF, 1) * sin_dn)


def _const_spec(shape):
    nd = len(shape)
    return pl.BlockSpec(shape, lambda *_: (0,) * nd, pipeline_mode=pl.Buffered(1))


def _row_spec(tm, width):
    return pl.BlockSpec((tm, width), lambda i: (i, 0))


def _params(sem, limit=VMEM_LIMIT):
    return pltpu.CompilerParams(dimension_semantics=sem, vmem_limit_bytes=limit)


def _in_proj_kernel(x_ref, gmix_ref, wcq_ref, wckv_ref, wkr_ref, wu_ref, wv_ref, wga_ref, wgb_ref,
                    gq_ref, gkv_ref, gkr_ref, ggm_ref, cos_ref, sup_ref, sdn_ref,
                    cq_out, ckv_out, kr_out, u_out, vg_out, siga_out, sigb_out):
    h = _rms(x_ref[...], gmix_ref[...]).astype(BF16)
    cq_out[...] = _rms(_dot(h, wcq_ref[...]), gq_ref[...]).astype(BF16)
    ckv_out[...] = _rms(_dot(h, wckv_ref[...]), gkv_ref[...])
    kr = _dot(h, wkr_ref[...])
    ms = jnp.sum(kr * kr, axis=-1, keepdims=True) * (1.0 / QK_ROPE)
    kr = kr * lax.rsqrt(ms + EPS) * gkr_ref[...]
    kr_out[...] = _rope_slot(kr, cos_ref[...], sup_ref[...], sdn_ref[...])
    u_out[...] = _gelu(_dot(h, wu_ref[...])).astype(BF16)
    vg_out[...] = _rms(_gelu(_dot(h, wv_ref[...])), ggm_ref[...])
    siga_out[...] = jax.nn.sigmoid(_dot(h, wga_ref[...])).astype(BF16)
    sigb_out[...] = jax.nn.sigmoid(_dot(h, wgb_ref[...])).astype(BF16)


def _in_proj(x2d, w, tabs, tm):
    t, d = x2d.shape
    cos, sup, sdn = tabs
    ntab = cos.shape[0] // tm
    tab_spec = pl.BlockSpec((tm, SLOT), lambda i: (i % ntab, 0))
    consts = [w['g_mix'], w['w_cq'], w['w_ckv'], w['w_kr'], w['w_u'], w['w_v'], w['w_ga'], w['w_gb'],
              w['g_q_lat'], w['g_kv_lat'], w['g_kr'], w['g_gm']]
    gw = w['w_u'].shape[1]
    widths = [w['w_cq'].shape[1], w['w_ckv'].shape[1], SLOT, gw, gw, d, d]
    dtypes = [BF16, F32, F32, BF16, F32, BF16, BF16]
    return pl.pallas_call(
        _in_proj_kernel,
        grid=(t // tm,),
        in_specs=[_row_spec(tm, d)] + [_const_spec(c.shape) for c in consts] + [tab_spec] * 3,
        out_specs=[_row_spec(tm, n) for n in widths],
        out_shape=[jax.ShapeDtypeStruct((t, n), dt) for n, dt in zip(widths, dtypes)],
        compiler_params=_params(("parallel",)),
        name="in_proj",
    )(x2d, *consts, cos, sup, sdn)


def _gmlp_kernel(u_ref, vg_ref, sigb_ref, mix_ref, bias_ref, wob_ref, out_ref):
    vgb = vg_ref[...].astype(BF16)
    parts = []
    for g in range(GM_GROUPS):
        sl = slice(g * LANES, (g + 1) * LANES)
        mixed = _dot(mix_ref[g], vgb[:, sl]) + bias_ref[:, sl]
        parts.append((u_ref[:, sl].astype(F32) * mixed).astype(BF16))
    ob = _dot(jnp.concatenate(parts, axis=1), wob_ref[...])
    out_ref[...] = sigb_ref[...].astype(F32) * ob


def _gmlp(u, vg, sigb, mix, bias, w_ob, tm):
    t, gw = u.shape
    d = w_ob.shape[1]
    return pl.pallas_call(
        _gmlp_kernel,
        grid=(t // tm,),
        in_specs=[_row_spec(tm, gw), _row_spec(tm, gw), _row_spec(tm, d),
                  _const_spec(mix.shape), _const_spec(bias.shape), _const_spec(w_ob.shape)],
        out_specs=_row_spec(tm, d),
        out_shape=jax.ShapeDtypeStruct((t, d), F32),
        compiler_params=_params(("parallel",)),
        name="gmlp",
    )(u, vg, sigb, mix, bias, w_ob)


def _q_proj_kernel(cq_ref, wuq_ref, gq_ref, mavg_ref, cos_ref, sup_ref, sdn_ref, q_out):
    q = _dot(cq_ref[...], wuq_ref[...])
    cos, sup, sdn = cos_ref[...], sup_ref[...], sdn_ref[...]
    for h in range(MLA_HEADS):
        sl = slice(h * SLOT, (h + 1) * SLOT)
        qs = q[:, sl]
        y = qs * lax.rsqrt(_group_mean(qs * qs, mavg_ref) + EPS) * gq_ref[...]
        q_out[:, sl] = _rope_slot(y, cos, sup, sdn).astype(BF16)


def _q_proj(cq, w, tabs, tm):
    t, ql = cq.shape
    cos, sup, sdn = tabs
    ntab = cos.shape[0] // tm
    tab_spec = pl.BlockSpec((tm, SLOT), lambda i: (i % ntab, 0))
    n = MLA_HEADS * SLOT
    return pl.pallas_call(
        _q_proj_kernel,
        grid=(t // tm,),
        in_specs=[_row_spec(tm, ql), _const_spec(w['w_uq'].shape), _const_spec(w['g_q'].shape),
                  _const_spec(w['m_avg'].shape)] + [tab_spec] * 3,
        out_specs=_row_spec(tm, n),
        out_shape=jax.ShapeDtypeStruct((t, n), BF16),
        compiler_params=_params(("parallel",)),
        name="q_proj",
    )(cq, w['w_uq'], w['g_q'], w['m_avg'], cos, sup, sdn)


def _kv_proj_kernel(ckv_ref, kr_ref, wuk_ref, wuv_ref, gkn_ref, mavg_ref, k_out, v_out):
    c = ckv_ref[...].astype(BF16)
    kn = _dot(c, wuk_ref[...])
    kr = kr_ref[...]
    for h in range(MLA_HEADS):
        sl = slice(h * SLOT, (h + 1) * SLOT)
        ks = kn[:, sl]
        y = ks * lax.rsqrt(_group_mean(ks * ks, mavg_ref) + EPS) * gkn_ref[...]
        k_out[:, sl] = (y + kr).astype(BF16)
    v_out[...] = _dot(c, wuv_ref[...]).astype(BF16)


def _kv_proj(ckv, kr, w, tm):
    t, kl = ckv.shape
    n = MLA_HEADS * SLOT
    return pl.pallas_call(
        _kv_proj_kernel,
        grid=(t // tm,),
        in_specs=[_row_spec(tm, kl), _row_spec(tm, SLOT), _const_spec(w['w_uk'].shape),
                  _const_spec(w['w_uv'].shape), _const_spec(w['g_kn'].shape),
                  _const_spec(w['m_avg'].shape)],
        out_specs=[_row_spec(tm, n), _row_spec(tm, n)],
        out_shape=[jax.ShapeDtypeStruct((t, n), BF16)] * 2,
        compiler_params=_params(("parallel",)),
        name="kv_proj",
    )(ckv, kr, w['w_uk'], w['w_uv'], w['g_kn'], w['m_avg'])


def _attn_prompt_kernel(bound_ref, q_ref, k_ref, v_ref, o_ref):
    tile = q_ref.shape[1]
    qi = pl.program_id(2)
    q = q_ref[0]
    bound = bound_ref[0]
    first = lax.broadcasted_iota(jnp.int32, (1, LANES), 1) < V_HEAD
    row_chunk = lax.broadcasted_iota(jnp.int32, (tile, tile), 0) // CHUNK
    col_chunk = lax.broadcasted_iota(jnp.int32, (tile, tile), 1) // CHUNK
    visible = col_chunk <= row_chunk

    def static_step(kstart, carry, masked):
        sums, acc = carry
        kstart = pl.multiple_of(kstart, tile)
        k = k_ref[0, pl.ds(kstart, tile), :]
        v = v_ref[0, pl.ds(kstart, tile), :]
        new_sums = []
        for h in range(2):
            sl = slice(h * SLOT, (h + 1) * SLOT)
            s = _dot_nt(q[:, sl], k[:, sl]) * F32(MLA_SCALE * LOG2_E) - bound
            if masked:
                s = jnp.where(visible, s, F32(-1e30))
            p = jnp.exp2(s)
            part = sums[h]
            for c in range(tile // LANES):
                part = part + p[:, c * LANES:(c + 1) * LANES]
            new_sums.append(part)
            acc = acc + _dot(p.astype(BF16), v[:, sl])
        return tuple(new_sums), acc

    @pl.when(bound < STATIC_SHIFT_LIMIT)
    def _():
        zeros = jnp.zeros((tile, LANES), F32)

        def pair(i, c):
            c = static_step(2 * i * tile, c, False)
            return static_step((2 * i + 1) * tile, c, False)

        def odd_tail(c):
            return static_step(qi * tile, static_step((qi - 1) * tile, c, False), True)

        carry = lax.fori_loop(0, qi // 2, pair, ((zeros, zeros), zeros))
        sums, acc = lax.cond(qi % 2 == 1, odd_tail, lambda c: static_step(qi * tile, c, True), carry)
        l0 = jnp.sum(sums[0], axis=-1, keepdims=True)
        l1 = jnp.sum(sums[1], axis=-1, keepdims=True)
        o_ref[0] = (acc / jnp.where(first, l0, l1)).astype(BF16)

    def step(kstart, carry, masked):
        ms, ls, acc = carry
        kstart = pl.multiple_of(kstart, tile)
        k = k_ref[0, pl.ds(kstart, tile), :]
        v = v_ref[0, pl.ds(kstart, tile), :]
        new_ms, new_ls, alphas, pvs = [], [], [], []
        for h in range(2):
            sl = slice(h * SLOT, (h + 1) * SLOT)
            s = _dot_nt(q[:, sl], k[:, sl]) * F32(MLA_SCALE * LOG2_E)
            if masked:
                s = jnp.where(visible, s, F32(-1e30))
            m_new = jnp.maximum(ms[h], jnp.max(s, axis=-1, keepdims=True))
            alpha = jnp.exp2(ms[h] - m_new)
            p = jnp.exp2(s - m_new)
            new_ls.append(alpha * ls[h] + jnp.sum(p, axis=-1, keepdims=True))
            new_ms.append(m_new)
            alphas.append(alpha)
            pvs.append(_dot(p.astype(BF16), v[:, sl]))
        acc = acc * jnp.where(first, alphas[0], alphas[1]) + pvs[0] + pvs[1]
        return tuple(new_ms), tuple(new_ls), acc

    @pl.when(bound >= STATIC_SHIFT_LIMIT)
    def _():
        neg = jnp.full((tile, 1), -1e30, F32)
        zero = jnp.zeros((tile, 1), F32)
        init = ((neg, neg), (zero, zero), jnp.zeros((tile, LANES), F32))
        carry = lax.fori_loop(0, qi, lambda i, c: step(i * tile, c, False), init)
        _, ls, acc = step(qi * tile, carry, True)
        o_ref[0] = (acc / jnp.where(first, ls[0], ls[1])).astype(BF16)


def _score_bound(w):
    gmax = lambda g, lo, n: jnp.max(jnp.abs(g[0, lo:lo + n]))
    nope = QK_NOPE * gmax(w['g_q'], 0, QK_NOPE) * gmax(w['g_kn'], 0, QK_NOPE)
    rope = QK_ROPE * gmax(w['g_q'], ROPE_LO, QK_ROPE) * gmax(w['g_kr'], ROPE_LO, QK_ROPE)
    return (BOUND_MARGIN * MLA_SCALE * LOG2_E * (nope + rope)).reshape(1).astype(F32)


def _attn_prompt(q, k, v, bound, tile):
    b, s, _ = q.shape
    pairs = MLA_HEADS // 2
    return pl.pallas_call(
        _attn_prompt_kernel,
        grid=(b, pairs, s // tile),
        in_specs=[pl.BlockSpec(memory_space=pltpu.SMEM),
                  pl.BlockSpec((1, tile, 2 * SLOT), lambda bi, p, i: (bi, i, p)),
                  pl.BlockSpec((1, s, 2 * SLOT), lambda bi, p, i: (bi, 0, p)),
                  pl.BlockSpec((1, s, 2 * SLOT), lambda bi, p, i: (bi, 0, p))],
        out_specs=pl.BlockSpec((1, tile, LANES), lambda bi, p, i: (bi, i, p)),
        out_shape=jax.ShapeDtypeStruct((b, s, pairs * LANES), BF16),
        compiler_params=_params(("parallel", "parallel", "arbitrary")),
        name="attn_prompt",
    )(bound, q, k, v)


def _attn_sample_kernel(q_ref, pckv_ref, pkr_ref, nckv_ref, nkr_ref, wuk_ref, wuv_ref, gkn_ref, place_ref,
                        o_ref):
    sq = q_ref.shape[1]
    rows = MLA_HEADS * sq
    gkn = gkn_ref[...]

    def keys_values(ckv, kr_slot):
        c = ckv.astype(BF16)
        kn = _dot(c, wuk_ref[...])
        parts = []
        for h in range(MLA_HEADS):
            ks = kn[:, h * SLOT:(h + 1) * SLOT]
            ms = jnp.sum(ks * ks, axis=-1, keepdims=True) * (1.0 / QK_NOPE)
            parts.append((ks * lax.rsqrt(ms + EPS) * gkn + kr_slot).astype(BF16))
        return jnp.concatenate(parts, axis=1), _dot(c, wuv_ref[...]).astype(BF16)

    past_kr = _dot(pkr_ref[0].astype(BF16), place_ref[...])
    k_p, v_p = keys_values(pckv_ref[0], past_kr)
    k_n, v_n = keys_values(nckv_ref[0], nkr_ref[0])
    k = jnp.concatenate([k_p, k_n], axis=0)
    v = jnp.concatenate([v_p, v_n], axis=0)

    qt = jnp.concatenate([q_ref[0].astype(F32)] * MLA_HEADS, axis=0)
    q_shape = (rows, MLA_HEADS * SLOT)
    own = (lax.broadcasted_iota(jnp.int32, q_shape, 0) // sq
           == lax.broadcasted_iota(jnp.int32, q_shape, 1) // SLOT)
    qbd = jnp.where(own, qt, 0.0).astype(BF16)
    s = _dot_nt(qbd, k) * MLA_SCALE
    p = jnp.exp(s - jnp.max(s, axis=-1, keepdims=True))
    p = p / jnp.sum(p, axis=-1, keepdims=True)
    o_all = _dot(p.astype(BF16), v)
    o_shape = o_all.shape
    own = (lax.broadcasted_iota(jnp.int32, o_shape, 0) // sq
           == lax.broadcasted_iota(jnp.int32, o_shape, 1) // V_HEAD)
    o_all = jnp.where(own, o_all, 0.0)
    o = o_all[0:sq]
    for h in range(1, MLA_HEADS):
        o = o + o_all[h * sq:(h + 1) * sq]
    o_ref[0] = o.astype(BF16)


def _attn_sample(q, past_ckv, past_kr, new_ckv, new_kr, w):
    b, sq, nq = q.shape
    npast, kl = past_ckv.shape[1:]
    nv = w['w_uv_plain'].shape[1]
    stream = lambda n, width: pl.BlockSpec((1, n, width), lambda bi: (bi, 0, 0))
    consts = [w['w_uk'], w['w_uv_plain'], w['g_kn'], w['rope_place']]
    return pl.pallas_call(
        _attn_sample_kernel,
        grid=(b,),
        in_specs=[stream(sq, nq), stream(npast, kl), stream(npast, QK_ROPE), stream(sq, kl), stream(sq, SLOT)]
        + [_const_spec(c.shape) for c in consts],
        out_specs=stream(sq, nv),
        out_shape=jax.ShapeDtypeStruct((b, sq, nv), BF16),
        compiler_params=_params(("parallel",)),
        name="attn_sample",
    )(q, past_ckv, past_kr, new_ckv, new_kr, *consts)


def _merge_kernel(x_ref, o_ref, siga_ref, gbob_ref, woa_ref, wo_ref, gx_ref, wcq_ref, gcq_ref,
                  x1_out, qc_out):
    oa = _dot(o_ref[...], woa_ref[...])
    merged = siga_ref[...].astype(F32) * oa + gbob_ref[...]
    x1 = x_ref[...] + _dot(merged.astype(BF16), wo_ref[...])
    x1_out[...] = x1
    qc = _dot(_rms(x1, gx_ref[...]).astype(BF16), wcq_ref[...])
    for h in range(MEM_HEADS):
        sl = slice(h * MEM_HEAD_DIM, (h + 1) * MEM_HEAD_DIM)
        qc_out[:, sl] = _rms(qc[:, sl], gcq_ref[...]).astype(BF16)


def _merge(x2d, o, siga, gbob, w, tm):
    t, d = x2d.shape
    consts = [w['w_oa'], w['w_o'], w['g_xattn'], w['w_cq_mem'], w['g_cq']]
    nq = w['w_cq_mem'].shape[1]
    return pl.pallas_call(
        _merge_kernel,
        grid=(t // tm,),
        in_specs=[_row_spec(tm, d), _row_spec(tm, o.shape[1]), _row_spec(tm, d), _row_spec(tm, d)]
        + [_const_spec(c.shape) for c in consts],
        out_specs=[_row_spec(tm, d), _row_spec(tm, nq)],
        out_shape=[jax.ShapeDtypeStruct((t, d), F32), jax.ShapeDtypeStruct((t, nq), BF16)],
        compiler_params=_params(("parallel",)),
        name="merge",
    )(x2d, o, siga, gbob, *consts)


def _mem_kv_kernel(mem_ref, gmem_ref, wck_ref, gck_ref, wcv_ref, k_out, v_out):
    m = _rms(mem_ref[...], gmem_ref[...]).astype(BF16)
    k = _dot(m, wck_ref[...])
    for h in range(MEM_HEADS):
        sl = slice(h * MEM_HEAD_DIM, (h + 1) * MEM_HEAD_DIM)
        k_out[:, sl] = _rms(k[:, sl], gck_ref[...])
    v_out[...] = _dot(m, wcv_ref[...])


def _mem_kv(mem2d, w, tm):
    t, d = mem2d.shape
    consts = [w['g_mem'], w['w_ck'], w['g_ck'], w['w_cv']]
    n = w['w_ck'].shape[1]
    return pl.pallas_call(
        _mem_kv_kernel,
        grid=(t // tm,),
        in_specs=[_row_spec(tm, d)] + [_const_spec(c.shape) for c in consts],
        out_specs=[_row_spec(tm, n)] * 2,
        out_shape=[jax.ShapeDtypeStruct((t, n), F32)] * 2,
        compiler_params=_params(("parallel",)),
        name="mem_kv",
    )(mem2d, *consts)


def _cross_kernel(q_ref, k_ref, v_ref, o_ref):
    q = q_ref[0]
    k = k_ref[0].astype(BF16)
    v = v_ref[0].astype(BF16)
    for h in range(MEM_HEADS):
        sl = slice(h * MEM_HEAD_DIM, (h + 1) * MEM_HEAD_DIM)
        s = _dot_nt(q[:, sl], k[:, sl]) * MEM_SCALE
        p = jnp.exp(s - jnp.max(s, axis=-1, keepdims=True))
        p = p / jnp.sum(p, axis=-1, keepdims=True)
        o_ref[0, :, sl] = _dot(p.astype(BF16), v[:, sl]).astype(BF16)


def _cross(qc, mk, mv, tc):
    b, s, n = qc.shape
    nm = mk.shape[1]
    return pl.pallas_call(
        _cross_kernel,
        grid=(b, s // tc),
        in_specs=[pl.BlockSpec((1, tc, n), lambda bi, i: (bi, i, 0)),
                  pl.BlockSpec((1, nm, n), lambda bi, i: (bi, 0, 0)),
                  pl.BlockSpec((1, nm, n), lambda bi, i: (bi, 0, 0))],
        out_specs=pl.BlockSpec((1, tc, n), lambda bi, i: (bi, i, 0)),
        out_shape=jax.ShapeDtypeStruct((b, s, n), BF16),
        compiler_params=_params(("parallel", "parallel")),
        name="cross_attn",
    )(qc, mk, mv)


def _cross_out_kernel(x1_ref, oc_ref, wco_ref, gffn_ref, wpq_ref, x2_out, h_out, q_out):
    x2 = x1_ref[...] + _dot(oc_ref[...], wco_ref[...])
    x2_out[...] = x2
    h = _rms(x2, gffn_ref[...]).astype(BF16)
    h_out[...] = h
    q_out[...] = _dot(h, wpq_ref[...]).astype(BF16)


def _cross_out(x1, oc, w, tm):
    t, d = x1.shape
    nq = w['w_pq'].shape[1]
    return pl.pallas_call(
        _cross_out_kernel,
        grid=(t // tm,),
        in_specs=[_row_spec(tm, d), _row_spec(tm, oc.shape[1]), _const_spec(w['w_co'].shape),
                  _const_spec(w['g_ffn'].shape), _const_spec(w['w_pq'].shape)],
        out_specs=[_row_spec(tm, d), _row_spec(tm, d), _row_spec(tm, nq)],
        out_shape=[jax.ShapeDtypeStruct((t, d), F32), jax.ShapeDtypeStruct((t, d), BF16),
                   jax.ShapeDtypeStruct((t, nq), BF16)],
        compiler_params=_params(("parallel",)),
        name="cross_out",
    )(x1, oc, w['w_co'], w['g_ffn'], w['w_pq'])


def _pair_blocks():
    k = PEER_TOPK
    blocks, flat = [], []
    n_row = sum(1 for a in range(k) if k // (a + 1) > 2)
    for a in range(n_row):
        nb = k if a == 0 else 8
        blocks.append(('row', a, nb))
        flat += [a * k + b if (a + 1) * (b + 1) <= k else -1 for b in range(nb)]
    for b in range(k // (n_row + 1)):
        na = k if b == 0 else 8
        blocks.append(('col', b, na))
        flat += [a * k + b if (a >= n_row and (a + 1) * (b + 1) <= k) else -1 for a in range(na)]
    return blocks, np.asarray(flat, np.float32)


def _pair_candidates(v0, v1, blocks, combine):
    parts = []
    for kind, fixed, n in blocks:
        if kind == 'row':
            parts.append(combine(v0[fixed:fixed + 1, :], v1[0:n, :]))
        else:
            parts.append(combine(v0[0:n, :], v1[fixed:fixed + 1, :]))
    return jnp.concatenate(parts, axis=0)


def _peer_kernel(h_ref, x_ref, q_ref, keys_ref, flat_ref, u_ref, v_ref, out_ref,
                 mask_scr, sv_scr, si_scr, ts_scr, te_scr, it_scr, jt_scr, gt_scr,
                 isel_scr, jsel_scr, gsel_scr):
    tm = h_ref.shape[0]
    te = v_ref.shape[0]
    r = pl.program_id(0)
    e = pl.program_id(1)
    n = pl.num_programs(0) - 2
    rows_per_step = te // N_KEYS
    tokens_per_head = tm // PEER_HEADS
    blocks, _ = _pair_blocks()
    neg_inf = F32(-jnp.inf)
    key_rows = lax.broadcasted_iota(jnp.int32, (N_KEYS, LANES), 0).astype(F32)
    swept = lax.rem(r, 2)
    ranked = 1 - swept

    def mask_token(t):
        irow = isel_scr[swept, pl.ds(t, 1), :]
        jrow = jsel_scr[swept, pl.ds(t, 1), :]
        grow = gsel_scr[swept, pl.ds(t, 1), :]
        rw = jnp.where(key_rows == irow, 0.5 * grow, 0.0).astype(BF16)
        cw = jnp.where(key_rows == jrow, 1.0, 0.0).astype(BF16)
        r0 = pl.multiple_of(t * MASK_PITCH, 8)
        mask_scr[pl.ds(r0, N_KEYS), :] = _dot_nt(rw, cw)

    def rank_pairs(hd):
        sv0, sv1 = sv_scr[2 * hd], sv_scr[2 * hd + 1]
        si0, si1 = si_scr[2 * hd], si_scr[2 * hd + 1]
        flat = flat_ref[...]
        cand = _pair_candidates(sv0, sv1, blocks, lambda x, y: x + y)
        cand = jnp.where(flat >= 0.0, cand, neg_inf)
        cidx = _pair_candidates(si0, si1, blocks, lambda x, y: x * F32(N_KEYS) + y)
        for kk in range(PEER_TOPK):
            m = jnp.max(cand, axis=0, keepdims=True)
            pos = jnp.min(jnp.where(cand == m, flat, F32(PEER_TOPK * PEER_TOPK)), axis=0, keepdims=True)
            sel = flat == pos
            ts_scr[kk:kk + 1, :] = m
            te_scr[kk:kk + 1, :] = jnp.max(jnp.where(sel, cidx, F32(-1.0)), axis=0, keepdims=True)
            cand = jnp.where(sel, neg_inf, cand)
        top = ts_scr[...]
        ex = jnp.exp(top - top[0:1, :])
        gate = ex / jnp.sum(ex, axis=0, keepdims=True)
        eidx = te_scr[...]
        irow = jnp.floor(eidx * F32(1.0 / N_KEYS))
        r0 = pl.multiple_of(hd * PEER_TOPK, PEER_TOPK)
        it_scr[pl.ds(r0, PEER_TOPK), :] = irow
        jt_scr[pl.ds(r0, PEER_TOPK), :] = eidx - irow * F32(N_KEYS)
        gt_scr[pl.ds(r0, PEER_TOPK), :] = gate

    def publish_ranked():
        isel_scr[ranked] = it_scr[...].T
        jsel_scr[ranked] = jt_scr[...].T
        gsel_scr[ranked] = gt_scr[...].T

    @pl.when(e == 0)
    def _():
        out_ref[...] = x_ref[...]

        @pl.when(r == 1)
        def _():
            def head(hd, _):
                rank_pairs(hd)
                return 0

            lax.fori_loop(0, PEER_HEADS, head, 0)
            publish_ranked()

        @pl.when(jnp.logical_and(r >= 2, r <= n))
        def _():
            def trip(i, _):
                for t in range(tokens_per_head):
                    mask_token(i * tokens_per_head + t)
                rank_pairs(i)
                return 0

            lax.fori_loop(0, PEER_HEADS, trip, 0)
            publish_ranked()

        @pl.when(r == n + 1)
        def _():
            def trip(i, _):
                for t in range(MASK_UNROLL):
                    mask_token(i * MASK_UNROLL + t)
                return 0

            lax.fori_loop(0, tm // MASK_UNROLL, trip, 0)

    def slot_scores():
        d0 = pl.multiple_of(e * PEER_HALF, PEER_HALF)
        return _dot_nt(q_ref[:, pl.ds(d0, PEER_HALF)], keys_ref[e])

    def rank_slot(scores):
        lane = lax.broadcasted_iota(jnp.int32, (tm, N_KEYS), 1)
        picked = jnp.zeros((tm, N_KEYS), jnp.int32)
        a = scores
        for kk in range(PEER_TOPK):
            idx = jnp.argmax(a, axis=-1, keepdims=True).astype(jnp.int32)
            a = jnp.where(lane == idx, neg_inf, a)
            picked = jnp.where(lane == kk, idx, picked)
        vals = jnp.take_along_axis(scores, picked, axis=1)
        sv_scr[e] = vals.T[0:PEER_TOPK, :]
        si_scr[e] = (N_KEYS - 1 - picked).astype(F32).T[0:PEER_TOPK, :]

    def expert_chunk():
        h = h_ref[...]
        row0 = e * rows_per_step
        weighted = []
        for c in range(te // PEER_CHUNK):
            rows = slice(c * PEER_CHUNK, (c + 1) * PEER_CHUNK)
            pre = _dot_nt(h, u_ref[rows, :])
            half_mask = jnp.concatenate(
                [mask_scr[pl.ds(row0 + c * (PEER_CHUNK // N_KEYS) + rr, tm, stride=MASK_PITCH), :]
                 for rr in range(PEER_CHUNK // N_KEYS)], axis=1)
            weighted.append((pre * (1.0 + lax.erf(pre * np.float32(np.sqrt(0.5)))) * half_mask).astype(BF16))
        out_ref[...] += _dot(jnp.concatenate(weighted, axis=1), v_ref[...])

    ranks = r < n
    sweeps = r >= 2

    @pl.when(jnp.logical_and(ranks, jnp.logical_not(sweeps)))
    def _():
        rank_slot(slot_scores())

    @pl.when(jnp.logical_and(ranks, sweeps))
    def _():
        a = slot_scores()
        expert_chunk()
        rank_slot(a)

    @pl.when(jnp.logical_and(jnp.logical_not(ranks), sweeps))
    def _():
        expert_chunk()


def _peer(h, x2, q_route, w, tm, te):
    t, d = h.shape
    n = t // tm
    nexp = w['peer_v'].shape[0]
    assert nexp // te == 2 * PEER_HEADS, "one half-key slot is ranked per expert step"
    assert tm % PEER_HEADS == 0 and tm % MASK_UNROLL == 0
    nsel = PEER_HEADS * PEER_TOPK
    _, flat = _pair_blocks()
    flat = jnp.asarray(np.repeat(flat[:, None], tm, axis=1))
    swept_tile = lambda r, e: (jnp.clip(r - 2, 0, n - 1), 0)
    dense_tile = lambda width: pl.BlockSpec((tm, width), swept_tile, pipeline_mode=pl.Buffered(1))
    expert_tile = pl.BlockSpec((te, d), lambda r, e: (e, 0))
    return pl.pallas_call(
        _peer_kernel,
        grid=(n + 2, nexp // te),
        in_specs=[dense_tile(d), dense_tile(d),
                  pl.BlockSpec((tm, q_route.shape[1]), lambda r, e: (jnp.minimum(r, n - 1), 0),
                               pipeline_mode=pl.Buffered(1)),
                  _const_spec(w['sub_keys'].shape), _const_spec(flat.shape), expert_tile, expert_tile],
        out_specs=pl.BlockSpec((tm, d), swept_tile),
        out_shape=jax.ShapeDtypeStruct((t, d), F32),
        scratch_shapes=[pltpu.VMEM((tm * MASK_PITCH, LANES), F32),
                        pltpu.VMEM((2 * PEER_HEADS, PEER_TOPK, tm), F32),
                        pltpu.VMEM((2 * PEER_HEADS, PEER_TOPK, tm), F32),
                        pltpu.VMEM((PEER_TOPK, tm), F32),
                        pltpu.VMEM((PEER_TOPK, tm), F32),
                        pltpu.VMEM((nsel, tm), F32),
                        pltpu.VMEM((nsel, tm), F32),
                        pltpu.VMEM((nsel, tm), F32),
                        pltpu.VMEM((2, tm, nsel), F32),
                        pltpu.VMEM((2, tm, nsel), F32),
                        pltpu.VMEM((2, tm, nsel), F32)],
        compiler_params=_params(("arbitrary", "arbitrary"), PEER_VMEM_LIMIT),
        name="peer",
    )(h, x2, q_route, w['sub_keys'], flat, w['peer_u'], w['peer_v'])


def _slot_gain(parts):
    row = jnp.zeros((SLOT,), F32)
    for off, g in parts:
        row = lax.dynamic_update_slice(row, g.astype(F32), (off,))
    return row[None, :]


def _group_avg_matrix():
    m = np.zeros((SLOT, SLOT), np.float32)
    m[:QK_NOPE, :QK_NOPE] = 1.0 / QK_NOPE
    m[ROPE_LO:ROPE_LO + QK_ROPE, ROPE_LO:ROPE_LO + QK_ROPE] = 1.0 / QK_ROPE
    return jnp.asarray(np.concatenate([m, m], axis=0), BF16)


def _rope_tables(pos):
    inv = ROPE_THETA ** (-jnp.arange(ROPE_HALF, dtype=F32) / ROPE_HALF)
    ang = pos.astype(F32)[:, None] * inv[None, :]
    cos, sin = jnp.cos(ang), jnp.sin(ang)
    n = pos.shape[0]
    tail = SLOT - ROPE_LO - QK_ROPE
    cos_t = jnp.concatenate([jnp.ones((n, ROPE_LO), F32), cos, cos, jnp.zeros((n, tail), F32)], axis=1)
    sin_up = jnp.concatenate([jnp.zeros((n, ROPE_LO + ROPE_HALF), F32), sin, jnp.zeros((n, tail), F32)], axis=1)
    sin_dn = jnp.concatenate([jnp.zeros((n, ROPE_LO), F32), -sin, jnp.zeros((n, tail + ROPE_HALF), F32)], axis=1)
    return cos_t, sin_up, sin_dn


def _prep_weights(p, i):
    w_in = p['w_in'][i]
    q_lora = p['g_q_lat'].shape[1]
    kv_lora = p['g_kv_lat'].shape[1]
    gw = p['g_gm'].shape[1]
    d = w_in.shape[0]
    o = np.cumsum([0, q_lora, kv_lora, QK_ROPE, gw, gw, d, d])
    seg = lambda k: w_in[:, o[k]:o[k + 1]]
    row = lambda a: a[i][None, :].astype(F32)
    w = {}
    w['g_mix'] = row(p['g_mix'])
    w['w_cq'] = seg(0).astype(BF16)
    w['w_ckv'] = seg(1).astype(BF16)
    w['w_kr'] = jnp.pad(seg(2), ((0, 0), (ROPE_LO, SLOT - ROPE_LO - QK_ROPE))).astype(BF16)
    w['w_u'] = seg(3).astype(BF16)
    w['w_v'] = seg(4).astype(BF16)
    w['w_ga'] = seg(5).astype(BF16)
    w['w_gb'] = seg(6).astype(BF16)
    w['g_q_lat'] = row(p['g_q_lat'])
    w['g_kv_lat'] = row(p['g_kv_lat'])
    w['g_kr'] = _slot_gain([(ROPE_LO, p['g_kr'][i])])
    w['g_gm'] = row(p['g_gm'])
    head_dim = QK_NOPE + QK_ROPE
    w_uq = p['w_uq'][i].reshape(q_lora, MLA_HEADS, head_dim)
    w['w_uq'] = jnp.pad(w_uq, ((0, 0), (0, 0), (0, SLOT - head_dim))).reshape(q_lora, -1).astype(BF16)
    w['g_q'] = _slot_gain([(0, p['g_qn'][i]), (ROPE_LO, p['g_qr'][i])])
    w['m_avg'] = _group_avg_matrix()
    w_uk = p['w_uk'][i].reshape(kv_lora, MLA_HEADS, QK_NOPE)
    w['w_uk'] = jnp.pad(w_uk, ((0, 0), (0, 0), (0, SLOT - QK_NOPE))).reshape(kv_lora, -1).astype(BF16)
    w['g_kn'] = _slot_gain([(0, p['g_kn'][i])])
    w_uv = p['w_uv'][i].reshape(kv_lora, MLA_HEADS // 2, 2, V_HEAD)
    eye2 = jnp.eye(2, dtype=w_uv.dtype)
    w['w_uv'] = jnp.einsum('cpjd,jk->cpjkd', w_uv, eye2).reshape(kv_lora, -1).astype(BF16)
    w['w_uv_plain'] = p['w_uv'][i].astype(BF16)
    place = np.zeros((QK_ROPE, SLOT), np.float32)
    place[np.arange(QK_ROPE), ROPE_LO + np.arange(QK_ROPE)] = 1.0
    w['rope_place'] = jnp.asarray(place, BF16)
    w['w_oa'] = p['w_oa'][i].astype(BF16)
    w['w_ob'] = p['w_ob'][i].astype(BF16)
    w['w_o'] = p['w_o'][i].astype(BF16)
    w['g_xattn'] = row(p['g_xattn'])
    w['g_mem'] = row(p['g_mem'])
    w['w_cq_mem'] = p['w_cq'][i].astype(BF16)
    w['g_cq'] = row(p['g_cq'])
    w['w_ck'] = p['w_ck'][i].astype(BF16)
    w['g_ck'] = row(p['g_ck'])
    w['w_cv'] = p['w_cv'][i].astype(BF16)
    w['w_co'] = p['w_co'][i].astype(BF16)
    w['g_ffn'] = row(p['g_ffn'])
    w['w_pq'] = p['w_pq'][i].astype(BF16)
    w['sub_keys'] = p['sub_keys'][i].reshape(2 * PEER_HEADS, N_KEYS, PEER_HALF)[:, ::-1, :].astype(BF16)
    w['peer_u'] = p['peer_u'][i].astype(BF16)
    w['peer_v'] = p['peer_v'][i].astype(BF16)
    w['w_s'] = p['w_s'][i]
    w['b_s'] = p['b_s'][i]
    return w


def _spatial_operands(w, seq, tm):
    chunk = min(seq, GM_CHUNK)
    reps = tm // chunk
    w_mask = jnp.tril(w['w_s'][:, :chunk, :chunk])
    w_mask = w_mask.astype(BF16)
    mix = jnp.concatenate(
        [jnp.pad(w_mask, ((0, 0), (0, 0), (a * chunk, (reps - 1 - a) * chunk))) for a in range(reps)], axis=1)
    bias = jnp.tile(w['b_s'][:, :chunk].T, (reps, 1))
    bias = jnp.repeat(bias, LANES, axis=1)
    return mix.astype(BF16), bias.astype(F32)


def _token_mixer(x2d, w, seq, pos, past, tm):
    t, d = x2d.shape
    nb = t // seq
    period = max(seq, tm)
    tabs = _rope_tables(jnp.tile(pos, period // seq))
    cq, ckv, kr, u, vg, siga, sigb = _in_proj(x2d, w, tabs, tm)
    mix, bias = _spatial_operands(w, seq, tm)
    gbob = _gmlp(u, vg, sigb, mix, bias, w['w_ob'], tm)
    q = _q_proj(cq, w, tabs, tm)
    if past is None:
        k, v = _kv_proj(ckv, kr, w, tm)
        o = _attn_prompt(q.reshape(nb, seq, -1), k.reshape(nb, seq, -1), v.reshape(nb, seq, -1),
                         _score_bound(w), ATTN_TILE)
    else:
        past_ckv, past_kr = past
        o = _attn_sample(q.reshape(nb, seq, -1), past_ckv, past_kr, ckv.reshape(nb, seq, -1),
                         kr.reshape(nb, seq, SLOT), w)
    x1, qc = _merge(x2d, o.reshape(t, -1), siga, gbob, w, tm)
    return x1, qc, ckv, kr[:, ROPE_LO:ROPE_LO + QK_ROPE], vg


def _tail(x1, qc, mk, mv, w, nb, seq, tm):
    t, d = x1.shape
    oc = _cross(qc.reshape(nb, seq, -1), mk, mv, min(seq, tm))
    x2, h, q_route = _cross_out(x1, oc.reshape(t, -1), w, tm)
    return _peer(h, x2, q_route, w, PEER_ROWS, PEER_EXPERTS)


def kernel(x_prompt, x_sample, cache_mla_ckv, cache_mla_krope, cache_mem_k, cache_mem_v, mem_prompt, g_mix, w_in, g_q_lat, w_uq, g_qn, g_qr, g_kv_lat, g_kr, w_uk, w_uv, g_kn, w_oa, g_gm, w_s, b_s, w_ob, w_o, g_xattn, g_mem, w_cq, g_cq, w_ck, g_ck, w_cv, w_co, g_ffn, w_pq, sub_keys, peer_u, peer_v):
    params = dict(g_mix=g_mix, w_in=w_in, g_q_lat=g_q_lat, w_uq=w_uq, g_qn=g_qn, g_qr=g_qr,
                  g_kv_lat=g_kv_lat, g_kr=g_kr, w_uk=w_uk, w_uv=w_uv, g_kn=g_kn, w_oa=w_oa, g_gm=g_gm,
                  w_s=w_s, b_s=b_s, w_ob=w_ob, w_o=w_o, g_xattn=g_xattn, g_mem=g_mem, w_cq=w_cq,
                  g_cq=g_cq, w_ck=w_ck, g_ck=g_ck, w_cv=w_cv, w_co=w_co, g_ffn=g_ffn, w_pq=w_pq,
                  sub_keys=sub_keys, peer_u=peer_u, peer_v=peer_v)
    bp, sp, d = x_prompt.shape
    bs, ss, _ = x_sample.shape
    depth = w_in.shape[0]
    past_len = cache_mla_ckv.shape[2]
    n_mem = mem_prompt.shape[1]
    tm = ROW_TILE
    assert (bp * sp) % tm == 0 and (bs * ss) % tm == 0 and sp % ATTN_TILE == 0
    pos_p = jnp.arange(sp)
    pos_s = past_len + jnp.arange(ss)
    xp = x_prompt.reshape(bp * sp, d)
    xs = x_sample.reshape(bs * ss, d)
    outs = [[] for _ in range(7)]
    for i in range(depth):
        w = _prep_weights(params, i)
        xp1, qcp, ckv_p, kr_p, _ = _token_mixer(xp, w, sp, pos_p, None, tm)
        xs1, qcs, ckv_s, kr_s, vg_s = _token_mixer(xs, w, ss, pos_s, (cache_mla_ckv[i], cache_mla_krope[i]), tm)
        mk_p, mv_p = _mem_kv(mem_prompt.reshape(bp * n_mem, d), w, min(tm, bp * n_mem))
        xp = _tail(xp1, qcp, mk_p.reshape(bp, n_mem, -1), mv_p.reshape(bp, n_mem, -1), w, bp, sp, tm)
        xs = _tail(xs1, qcs, cache_mem_k[i].reshape(bs, n_mem, -1), cache_mem_v[i].reshape(bs, n_mem, -1),
                   w, bs, ss, tm)
        outs[0].append(ckv_p.reshape(bp, sp, -1))
        outs[1].append(kr_p.reshape(bp, sp, -1))
        outs[2].append(mk_p.reshape(bp, n_mem, MEM_HEADS, MEM_HEAD_DIM))
        outs[3].append(mv_p.reshape(bp, n_mem, MEM_HEADS, MEM_HEAD_DIM))
        outs[4].append(ckv_s.reshape(bs, ss, -1))
        outs[5].append(kr_s.reshape(bs, ss, -1))
        outs[6].append(vg_s.reshape(bs, ss, -1))
    return (xp.reshape(bp, sp, d), xs.reshape(bs, ss, d)) + tuple(jnp.stack(o) for o in outs)
```

```python
import functools

import jax
import jax.numpy as jnp
import numpy as np
from jax import lax
from jax.experimental import pallas as pl
from jax.experimental.pallas import tpu as pltpu

CHUNK = 64
EPS = 1e-6
MLA_HEADS = 16
QK_NOPE = 64
QK_ROPE = 32
V_HEAD = 64
ROPE_THETA = 10000.0
MLA_SCALE = (QK_NOPE + QK_ROPE) ** -0.5
GM_CHUNK = 128
GM_GROUPS = 8
MEM_HEADS = 4
MEM_HEAD_DIM = 128
MEM_SCALE = MEM_HEAD_DIM ** -0.5
PEER_HEADS = 8
N_KEYS = 128
PEER_TOPK = 16
PEER_HALF = 128

LANES = 128
SLOT = LANES
ROPE_LO = QK_NOPE
ROPE_HALF = QK_ROPE // 2
ROW_TILE = 512
ATTN_TILE = 512
ATTN_UNROLL = 4
LOG2_E = float(np.log2(np.e))
STATIC_SHIFT_LIMIT = 48.0
BOUND_MARGIN = 1.02
PEER_ROWS = 512
PEER_EXPERTS = 1024
PEER_CHUNK = 512
MASK_PITCH = N_KEYS + 8
MASK_UNROLL = 32
VMEM_LIMIT = 48 * 1024 * 1024
PEER_VMEM_LIMIT = 60 * 1024 * 1024

F32 = jnp.float32
BF16 = jnp.bfloat16
_NT = (((1,), (1,)), ((), ()))


def _dot(a, b):
    return jnp.dot(a, b, preferred_element_type=F32)


def _dot_nt(a, b):
    return lax.dot_general(a, b, _NT, preferred_element_type=F32)


def _rms(xf, g):
    return xf * lax.rsqrt(jnp.mean(xf * xf, axis=-1, keepdims=True) + EPS) * g


def _gelu(x):
    return 0.5 * x * (1.0 + lax.erf(x * np.float32(np.sqrt(0.5))))


def _group_mean(sq, m_ref):
    hi = sq.astype(BF16)
    lo = (sq - hi.astype(F32)).astype(BF16)
    return _dot(jnp.concatenate([hi, lo], axis=1), m_ref[...])


def _rope_slot(y, cos, sin_up, sin_dn):
    return (y * cos + pltpu.roll(y, ROPE_HALF, 1) * sin_up
            + pltpu.roll(y, SLOT - ROPE_HALF, 1) * sin_dn)


def _const_spec(shape):
    nd = len(shape)
    return pl.BlockSpec(shape, lambda *_: (0,) * nd, pipeline_mode=pl.Buffered(1))


def _row_spec(tm, width):
    return pl.BlockSpec((tm, width), lambda i: (i, 0))


def _params(sem, limit=VMEM_LIMIT):
    return pltpu.CompilerParams(dimension_semantics=sem, vmem_limit_bytes=limit)


def _in_proj_kernel(x_ref, gmix_ref, wcq_ref, wckv_ref, wkr_ref, wu_ref, wv_ref, wga_ref, wgb_ref,
                    gq_ref, gkv_ref, gkr_ref, ggm_ref, cos_ref, sup_ref, sdn_ref,
                    cq_out, ckv_out, kr_out, u_out, vg_out, siga_out, sigb_out):
    h = _rms(x_ref[...], gmix_ref[...]).astype(BF16)
    cq_out[...] = _rms(_dot(h, wcq_ref[...]), gq_ref[...]).astype(BF16)
    ckv_out[...] = _rms(_dot(h, wckv_ref[...]), gkv_ref[...])
    kr = _dot(h, wkr_ref[...])
    ms = jnp.sum(kr * kr, axis=-1, keepdims=True) * (1.0 / QK_ROPE)
    kr = kr * lax.rsqrt(ms + EPS) * gkr_ref[...]
    kr_out[...] = _rope_slot(kr, cos_ref[...], sup_ref[...], sdn_ref[...])
    u_out[...] = _gelu(_dot(h, wu_ref[...])).astype(BF16)
    vg_out[...] = _rms(_gelu(_dot(h, wv_ref[...])), ggm_ref[...])
    siga_out[...] = jax.nn.sigmoid(_dot(h, wga_ref[...])).astype(BF16)
    sigb_out[...] = jax.nn.sigmoid(_dot(h, wgb_ref[...])).astype(BF16)


def _in_proj(x2d, w, tabs, tm):
    t, d = x2d.shape
    cos, sup, sdn = tabs
    ntab = cos.shape[0] // tm
    tab_spec = pl.BlockSpec((tm, SLOT), lambda i: (i % ntab, 0))
    consts = [w['g_mix'], w['w_cq'], w['w_ckv'], w['w_kr'], w['w_u'], w['w_v'], w['w_ga'], w['w_gb'],
              w['g_q_lat'], w['g_kv_lat'], w['g_kr'], w['g_gm']]
    gw = w['w_u'].shape[1]
    widths = [w['w_cq'].shape[1], w['w_ckv'].shape[1], SLOT, gw, gw, d, d]
    dtypes = [BF16, F32, F32, BF16, F32, BF16, BF16]
    return pl.pallas_call(
        _in_proj_kernel,
        grid=(t // tm,),
        in_specs=[_row_spec(tm, d)] + [_const_spec(c.shape) for c in consts] + [tab_spec] * 3,
        out_specs=[_row_spec(tm, n) for n in widths],
        out_shape=[jax.ShapeDtypeStruct((t, n), dt) for n, dt in zip(widths, dtypes)],
        compiler_params=_params(("parallel",)),
        name="in_proj",
    )(x2d, *consts, cos, sup, sdn)


def _gmlp_kernel(u_ref, vg_ref, sigb_ref, mix_ref, bias_ref, wob_ref, out_ref):
    vgb = vg_ref[...].astype(BF16)
    parts = []
    for g in range(GM_GROUPS):
        sl = slice(g * LANES, (g + 1) * LANES)
        mixed = _dot(mix_ref[g], vgb[:, sl]) + bias_ref[:, sl]
        parts.append((u_ref[:, sl].astype(F32) * mixed).astype(BF16))
    ob = _dot(jnp.concatenate(parts, axis=1), wob_ref[...])
    out_ref[...] = sigb_ref[...].astype(F32) * ob


def _gmlp(u, vg, sigb, mix, bias, w_ob, tm):
    t, gw = u.shape
    d = w_ob.shape[1]
    return pl.pallas_call(
        _gmlp_kernel,
        grid=(t // tm,),
        in_specs=[_row_spec(tm, gw), _row_spec(tm, gw), _row_spec(tm, d),
                  _const_spec(mix.shape), _const_spec(bias.shape), _const_spec(w_ob.shape)],
        out_specs=_row_spec(tm, d),
        out_shape=jax.ShapeDtypeStruct((t, d), F32),
        compiler_params=_params(("parallel",)),
        name="gmlp",
    )(u, vg, sigb, mix, bias, w_ob)


def _q_proj_kernel(cq_ref, wuq_ref, gq_ref, mavg_ref, cos_ref, sup_ref, sdn_ref, q_out):
    q = _dot(cq_ref[...], wuq_ref[...])
    cos, sup, sdn = cos_ref[...], sup_ref[...], sdn_ref[...]
    for h in range(MLA_HEADS):
        sl = slice(h * SLOT, (h + 1) * SLOT)
        qs = q[:, sl]
        y = qs * lax.rsqrt(_group_mean(qs * qs, mavg_ref) + EPS) * gq_ref[...]
        q_out[:, sl] = _rope_slot(y, cos, sup, sdn).astype(BF16)


def _q_proj(cq, w, tabs, tm):
    t, ql = cq.shape
    cos, sup, sdn = tabs
    ntab = cos.shape[0] // tm
    tab_spec = pl.BlockSpec((tm, SLOT), lambda i: (i % ntab, 0))
    n = MLA_HEADS * SLOT
    return pl.pallas_call(
        _q_proj_kernel,
        grid=(t // tm,),
        in_specs=[_row_spec(tm, ql), _const_spec(w['w_uq'].shape), _const_spec(w['g_q'].shape),
                  _const_spec(w['m_avg'].shape)] + [tab_spec] * 3,
        out_specs=_row_spec(tm, n),
        out_shape=jax.ShapeDtypeStruct((t, n), BF16),
        compiler_params=_params(("parallel",)),
        name="q_proj",
    )(cq, w['w_uq'], w['g_q'], w['m_avg'], cos, sup, sdn)


def _kv_proj_kernel(ckv_ref, kr_ref, wuk_ref, wuv_ref, gkn_ref, mavg_ref, k_out, v_out):
    c = ckv_ref[...].astype(BF16)
    kn = _dot(c, wuk_ref[...])
    kr = kr_ref[...]
    for h in range(MLA_HEADS):
        sl = slice(h * SLOT, (h + 1) * SLOT)
        ks = kn[:, sl]
        y = ks * lax.rsqrt(_group_mean(ks * ks, mavg_ref) + EPS) * gkn_ref[...]
        k_out[:, sl] = (y + kr).astype(BF16)
    v_out[...] = _dot(c, wuv_ref[...]).astype(BF16)


def _kv_proj(ckv, kr, w, tm):
    t, kl = ckv.shape
    n = MLA_HEADS * SLOT
    return pl.pallas_call(
        _kv_proj_kernel,
        grid=(t // tm,),
        in_specs=[_row_spec(tm, kl), _row_spec(tm, SLOT), _const_spec(w['w_uk'].shape),
                  _const_spec(w['w_uv'].shape), _const_spec(w['g_kn'].shape),
                  _const_spec(w['m_avg'].shape)],
        out_specs=[_row_spec(tm, n), _row_spec(tm, n)],
        out_shape=[jax.ShapeDtypeStruct((t, n), BF16)] * 2,
        compiler_params=_params(("parallel",)),
        name="kv_proj",
    )(ckv, kr, w['w_uk'], w['w_uv'], w['g_kn'], w['m_avg'])


def _attn_prompt_kernel(bound_ref, q_ref, k_ref, v_ref, o_ref):
    tile = q_ref.shape[1]
    qi = pl.program_id(2)
    q = q_ref[0]
    bound = bound_ref[0]
    first = lax.broadcasted_iota(jnp.int32, (1, LANES), 1) < V_HEAD
    row_chunk = lax.broadcasted_iota(jnp.int32, (tile, tile), 0) // CHUNK
    col_chunk = lax.broadcasted_iota(jnp.int32, (tile, tile), 1) // CHUNK
    visible = col_chunk <= row_chunk

    def static_rows(rows, row0, kstart, width, masked):
        k = k_ref[0, pl.ds(kstart, width), :]
        v = v_ref[0, pl.ds(kstart, width), :]
        nrows = rows.stop - rows.start
        if masked:
            seen = (lax.broadcasted_iota(jnp.int32, (nrows, width), 1) // CHUNK
                    <= (lax.broadcasted_iota(jnp.int32, (nrows, width), 0) + row0) // CHUNK)
        parts, weighted = [], None
        for h in range(2):
            sl = slice(h * SLOT, (h + 1) * SLOT)
            s = _dot_nt(q[rows, sl], k[:, sl]) * F32(MLA_SCALE * LOG2_E) - bound
            if masked:
                s = jnp.where(seen, s, F32(-1e30))
            p = jnp.exp2(s)
            part = p[:, 0:LANES]
            for c in range(1, width // LANES):
                part = part + p[:, c * LANES:(c + 1) * LANES]
            parts.append(part)
            term = _dot(p.astype(BF16), v[:, sl])
            weighted = term if weighted is None else weighted + term
        return parts, weighted

    def full_step(kstart, carry):
        sums, acc = carry
        parts, weighted = static_rows(slice(0, tile), 0, pl.multiple_of(kstart, tile), tile, False)
        return (sums[0] + parts[0], sums[1] + parts[1]), acc + weighted

    def diag_step(kstart, carry):
        sums, acc = carry
        half = tile // 2
        kstart = pl.multiple_of(kstart, tile)
        top, top_w = static_rows(slice(0, half), 0, kstart, half, True)
        bot, bot_w = static_rows(slice(half, tile), half, kstart, tile, True)
        parts = [jnp.concatenate([a, b], axis=0) for a, b in zip(top, bot)]
        return ((sums[0] + parts[0], sums[1] + parts[1]),
                acc + jnp.concatenate([top_w, bot_w], axis=0))

    @pl.when(bound < STATIC_SHIFT_LIMIT)
    def _():
        zeros = jnp.zeros((tile, LANES), F32)
        group = ATTN_UNROLL

        def trip(i, c):
            for j in range(group):
                c = full_step((group * i + j) * tile, c)
            return c

        def tail(nfull):
            def run(c):
                first_tile = qi - nfull
                for j in range(nfull):
                    c = full_step((first_tile + j) * tile, c)
                return diag_step(qi * tile, c)
            return run

        carry = lax.fori_loop(0, qi // group, trip, ((zeros, zeros), zeros))
        sums, acc = lax.switch(qi % group, [tail(nfull) for nfull in range(group)], carry)
        l0 = jnp.sum(sums[0], axis=-1, keepdims=True)
        l1 = jnp.sum(sums[1], axis=-1, keepdims=True)
        o_ref[0] = (acc / jnp.where(first, l0, l1)).astype(BF16)

    def step(kstart, carry, masked):
        ms, ls, acc = carry
        kstart = pl.multiple_of(kstart, tile)
        k = k_ref[0, pl.ds(kstart, tile), :]
        v = v_ref[0, pl.ds(kstart, tile), :]
        new_ms, new_ls, alphas, pvs = [], [], [], []
        for h in range(2):
            sl = slice(h * SLOT, (h + 1) * SLOT)
            s = _dot_nt(q[:, sl], k[:, sl]) * F32(MLA_SCALE * LOG2_E)
            if masked:
                s = jnp.where(visible, s, F32(-1e30))
            m_new = jnp.maximum(ms[h], jnp.max(s, axis=-1, keepdims=True))
            alpha = jnp.exp2(ms[h] - m_new)
            p = jnp.exp2(s - m_new)
            new_ls.append(alpha * ls[h] + jnp.sum(p, axis=-1, keepdims=True))
            new_ms.append(m_new)
            alphas.append(alpha)
            pvs.append(_dot(p.astype(BF16), v[:, sl]))
        acc = acc * jnp.where(first, alphas[0], alphas[1]) + pvs[0] + pvs[1]
        return tuple(new_ms), tuple(new_ls), acc

    @pl.when(bound >= STATIC_SHIFT_LIMIT)
    def _():
        neg = jnp.full((tile, 1), -1e30, F32)
        zero = jnp.zeros((tile, 1), F32)
        init = ((neg, neg), (zero, zero), jnp.zeros((tile, LANES), F32))
        carry = lax.fori_loop(0, qi, lambda i, c: step(i * tile, c, False), init)
        _, ls, acc = step(qi * tile, carry, True)
        o_ref[0] = (acc / jnp.where(first, ls[0], ls[1])).astype(BF16)


def _score_bound(w):
    gmax = lambda g, lo, n: jnp.max(jnp.abs(g[0, lo:lo + n]))
    nope = QK_NOPE * gmax(w['g_q'], 0, QK_NOPE) * gmax(w['g_kn'], 0, QK_NOPE)
    rope = QK_ROPE * gmax(w['g_q'], ROPE_LO, QK_ROPE) * gmax(w['g_kr'], ROPE_LO, QK_ROPE)
    return (BOUND_MARGIN * MLA_SCALE * LOG2_E * (nope + rope)).reshape(1).astype(F32)


def _attn_prompt(q, k, v, bound, tile):
    b, s, _ = q.shape
    pairs = MLA_HEADS // 2
    return pl.pallas_call(
        _attn_prompt_kernel,
        grid=(b, pairs, s // tile),
        in_specs=[pl.BlockSpec(memory_space=pltpu.SMEM),
                  pl.BlockSpec((1, tile, 2 * SLOT), lambda bi, p, i: (bi, i, p)),
                  pl.BlockSpec((1, s, 2 * SLOT), lambda bi, p, i: (bi, 0, p)),
                  pl.BlockSpec((1, s, 2 * SLOT), lambda bi, p, i: (bi, 0, p))],
        out_specs=pl.BlockSpec((1, tile, LANES), lambda bi, p, i: (bi, i, p)),
        out_shape=jax.ShapeDtypeStruct((b, s, pairs * LANES), BF16),
        compiler_params=_params(("parallel", "parallel", "arbitrary")),
        name="attn_prompt",
    )(bound, q, k, v)


def _attn_sample_kernel(q_ref, pckv_ref, pkr_ref, nckv_ref, nkr_ref, wuk_ref, wuv_ref, gkn_ref, place_ref,
                        o_ref):
    sq = q_ref.shape[1]
    rows = MLA_HEADS * sq
    gkn = gkn_ref[...]

    def keys_values(ckv, kr_slot):
        c = ckv.astype(BF16)
        kn = _dot(c, wuk_ref[...])
        parts = []
        for h in range(MLA_HEADS):
            ks = kn[:, h * SLOT:(h + 1) * SLOT]
            ms = jnp.sum(ks * ks, axis=-1, keepdims=True) * (1.0 / QK_NOPE)
            parts.append((ks * lax.rsqrt(ms + EPS) * gkn + kr_slot).astype(BF16))
        return jnp.concatenate(parts, axis=1), _dot(c, wuv_ref[...]).astype(BF16)

    past_kr = _dot(pkr_ref[0].astype(BF16), place_ref[...])
    k_p, v_p = keys_values(pckv_ref[0], past_kr)
    k_n, v_n = keys_values(nckv_ref[0], nkr_ref[0])
    k = jnp.concatenate([k_p, k_n], axis=0)
    v = jnp.concatenate([v_p, v_n], axis=0)

    qt = jnp.concatenate([q_ref[0].astype(F32)] * MLA_HEADS, axis=0)
    q_shape = (rows, MLA_HEADS * SLOT)
    own = (lax.broadcasted_iota(jnp.int32, q_shape, 0) // sq
           == lax.broadcasted_iota(jnp.int32, q_shape, 1) // SLOT)
    qbd = jnp.where(own, qt, 0.0).astype(BF16)
    s = _dot_nt(qbd, k) * MLA_SCALE
    p = jnp.exp(s - jnp.max(s, axis=-1, keepdims=True))
    p = p / jnp.sum(p, axis=-1, keepdims=True)
    o_all = _dot(p.astype(BF16), v)
    o_shape = o_all.shape
    own = (lax.broadcasted_iota(jnp.int32, o_shape, 0) // sq
           == lax.broadcasted_iota(jnp.int32, o_shape, 1) // V_HEAD)
    o_all = jnp.where(own, o_all, 0.0)
    o = o_all[0:sq]
    for h in range(1, MLA_HEADS):
        o = o + o_all[h * sq:(h + 1) * sq]
    o_ref[0] = o.astype(BF16)


def _attn_sample(q, past_ckv, past_kr, new_ckv, new_kr, w):
    b, sq, nq = q.shape
    npast, kl = past_ckv.shape[1:]
    nv = w['w_uv_plain'].shape[1]
    stream = lambda n, width: pl.BlockSpec((1, n, width), lambda bi: (bi, 0, 0))
    consts = [w['w_uk'], w['w_uv_plain'], w['g_kn'], w['rope_place']]
    return pl.pallas_call(
        _attn_sample_kernel,
        grid=(b,),
        in_specs=[stream(sq, nq), stream(npast, kl), stream(npast, QK_ROPE), stream(sq, kl), stream(sq, SLOT)]
        + [_const_spec(c.shape) for c in consts],
        out_specs=stream(sq, nv),
        out_shape=jax.ShapeDtypeStruct((b, sq, nv), BF16),
        compiler_params=_params(("parallel",)),
        name="attn_sample",
    )(q, past_ckv, past_kr, new_ckv, new_kr, *consts)


def _merge_kernel(x_ref, o_ref, siga_ref, gbob_ref, woa_ref, wo_ref, gx_ref, wcq_ref, gcq_ref,
                  x1_out, qc_out):
    oa = _dot(o_ref[...], woa_ref[...])
    merged = siga_ref[...].astype(F32) * oa + gbob_ref[...]
    x1 = x_ref[...] + _dot(merged.astype(BF16), wo_ref[...])
    x1_out[...] = x1
    qc = _dot(_rms(x1, gx_ref[...]).astype(BF16), wcq_ref[...])
    for h in range(MEM_HEADS):
        sl = slice(h * MEM_HEAD_DIM, (h + 1) * MEM_HEAD_DIM)
        qc_out[:, sl] = _rms(qc[:, sl], gcq_ref[...]).astype(BF16)


def _merge(x2d, o, siga, gbob, w, tm):
    t, d = x2d.shape
    consts = [w['w_oa'], w['w_o'], w['g_xattn'], w['w_cq_mem'], w['g_cq']]
    nq = w['w_cq_mem'].shape[1]
    return pl.pallas_call(
        _merge_kernel,
        grid=(t // tm,),
        in_specs=[_row_spec(tm, d), _row_spec(tm, o.shape[1]), _row_spec(tm, d), _row_spec(tm, d)]
        + [_const_spec(c.shape) for c in consts],
        out_specs=[_row_spec(tm, d), _row_spec(tm, nq)],
        out_shape=[jax.ShapeDtypeStruct((t, d), F32), jax.ShapeDtypeStruct((t, nq), BF16)],
        compiler_params=_params(("parallel",)),
        name="merge",
    )(x2d, o, siga, gbob, *consts)


def _mem_kv_kernel(mem_ref, gmem_ref, wck_ref, gck_ref, wcv_ref, k_out, v_out):
    m = _rms(mem_ref[...], gmem_ref[...]).astype(BF16)
    k = _dot(m, wck_ref[...])
    for h in range(MEM_HEADS):
        sl = slice(h * MEM_HEAD_DIM, (h + 1) * MEM_HEAD_DIM)
        k_out[:, sl] = _rms(k[:, sl], gck_ref[...])
    v_out[...] = _dot(m, wcv_ref[...])


def _mem_kv(mem2d, w, tm):
    t, d = mem2d.shape
    consts = [w['g_mem'], w['w_ck'], w['g_ck'], w['w_cv']]
    n = w['w_ck'].shape[1]
    return pl.pallas_call(
        _mem_kv_kernel,
        grid=(t // tm,),
        in_specs=[_row_spec(tm, d)] + [_const_spec(c.shape) for c in consts],
        out_specs=[_row_spec(tm, n)] * 2,
        out_shape=[jax.ShapeDtypeStruct((t, n), F32)] * 2,
        compiler_params=_params(("parallel",)),
        name="mem_kv",
    )(mem2d, *consts)


def _cross_kernel(q_ref, k_ref, v_ref, o_ref):
    q = q_ref[0]
    k = k_ref[0].astype(BF16)
    v = v_ref[0].astype(BF16)
    for h in range(MEM_HEADS):
        sl = slice(h * MEM_HEAD_DIM, (h + 1) * MEM_HEAD_DIM)
        s = _dot_nt(q[:, sl], k[:, sl]) * MEM_SCALE
        p = jnp.exp(s - jnp.max(s, axis=-1, keepdims=True))
        p = p / jnp.sum(p, axis=-1, keepdims=True)
        o_ref[0, :, sl] = _dot(p.astype(BF16), v[:, sl]).astype(BF16)


def _cross(qc, mk, mv, tc):
    b, s, n = qc.shape
    nm = mk.shape[1]
    return pl.pallas_call(
        _cross_kernel,
        grid=(b, s // tc),
        in_specs=[pl.BlockSpec((1, tc, n), lambda bi, i: (bi, i, 0)),
                  pl.BlockSpec((1, nm, n), lambda bi, i: (bi, 0, 0)),
                  pl.BlockSpec((1, nm, n), lambda bi, i: (bi, 0, 0))],
        out_specs=pl.BlockSpec((1, tc, n), lambda bi, i: (bi, i, 0)),
        out_shape=jax.ShapeDtypeStruct((b, s, n), BF16),
        compiler_params=_params(("parallel", "parallel")),
        name="cross_attn",
    )(qc, mk, mv)


def _cross_out_kernel(x1_ref, oc_ref, wco_ref, gffn_ref, wpq_ref, x2_out, h_out, q_out):
    x2 = x1_ref[...] + _dot(oc_ref[...], wco_ref[...])
    x2_out[...] = x2
    h = _rms(x2, gffn_ref[...]).astype(BF16)
    h_out[...] = h
    q_out[...] = _dot(h, wpq_ref[...]).astype(BF16)


def _cross_out(x1, oc, w, tm):
    t, d = x1.shape
    nq = w['w_pq'].shape[1]
    return pl.pallas_call(
        _cross_out_kernel,
        grid=(t // tm,),
        in_specs=[_row_spec(tm, d), _row_spec(tm, oc.shape[1]), _const_spec(w['w_co'].shape),
                  _const_spec(w['g_ffn'].shape), _const_spec(w['w_pq'].shape)],
        out_specs=[_row_spec(tm, d), _row_spec(tm, d), _row_spec(tm, nq)],
        out_shape=[jax.ShapeDtypeStruct((t, d), F32), jax.ShapeDtypeStruct((t, d), BF16),
                   jax.ShapeDtypeStruct((t, nq), BF16)],
        compiler_params=_params(("parallel",)),
        name="cross_out",
    )(x1, oc, w['w_co'], w['g_ffn'], w['w_pq'])


def _pair_blocks():
    k = PEER_TOPK
    blocks, flat = [], []
    n_row = sum(1 for a in range(k) if k // (a + 1) > 2)
    for a in range(n_row):
        nb = k if a == 0 else 8
        blocks.append(('row', a, nb))
        flat += [a * k + b if (a + 1) * (b + 1) <= k else -1 for b in range(nb)]
    for b in range(k // (n_row + 1)):
        na = k if b == 0 else 8
        blocks.append(('col', b, na))
        flat += [a * k + b if (a >= n_row and (a + 1) * (b + 1) <= k) else -1 for a in range(na)]
    return blocks, np.asarray(flat, np.float32)


def _pair_candidates(v0, v1, blocks, combine):
    parts = []
    for kind, fixed, n in blocks:
        if kind == 'row':
            parts.append(combine(v0[fixed:fixed + 1, :], v1[0:n, :]))
        else:
            parts.append(combine(v0[0:n, :], v1[fixed:fixed + 1, :]))
    return jnp.concatenate(parts, axis=0)


def _peer_kernel(h_ref, x_ref, q_ref, keys_ref, flat_ref, u_ref, v_ref, out_ref,
                 mask_scr, sv_scr, si_scr, ts_scr, te_scr, it_scr, jt_scr, gt_scr,
                 isel_scr, jsel_scr, gsel_scr):
    tm = h_ref.shape[0]
    te = v_ref.shape[0]
    r = pl.program_id(0)
    e = pl.program_id(1)
    n = pl.num_programs(0) - 2
    rows_per_step = te // N_KEYS
    tokens_per_head = tm // PEER_HEADS
    blocks, _ = _pair_blocks()
    neg_inf = F32(-jnp.inf)
    key_rows = lax.broadcasted_iota(jnp.int32, (N_KEYS, LANES), 0).astype(F32)
    swept = lax.rem(r, 2)
    ranked = 1 - swept

    def mask_token(t):
        irow = isel_scr[swept, pl.ds(t, 1), :]
        jrow = jsel_scr[swept, pl.ds(t, 1), :]
        grow = gsel_scr[swept, pl.ds(t, 1), :]
        rw = jnp.where(key_rows == irow, 0.5 * grow, 0.0).astype(BF16)
        cw = jnp.where(key_rows == jrow, 1.0, 0.0).astype(BF16)
        r0 = pl.multiple_of(t * MASK_PITCH, 8)
        mask_scr[pl.ds(r0, N_KEYS), :] = _dot_nt(rw, cw)

    def rank_pairs(hd):
        sv0, sv1 = sv_scr[2 * hd], sv_scr[2 * hd + 1]
        si0, si1 = si_scr[2 * hd], si_scr[2 * hd + 1]
        flat = flat_ref[...]
        cand = _pair_candidates(sv0, sv1, blocks, lambda x, y: x + y)
        cand = jnp.where(flat >= 0.0, cand, neg_inf)
        cidx = _pair_candidates(si0, si1, blocks, lambda x, y: x * F32(N_KEYS) + y)
        for kk in range(PEER_TOPK):
            m = jnp.max(cand, axis=0, keepdims=True)
            pos = jnp.min(jnp.where(cand == m, flat, F32(PEER_TOPK * PEER_TOPK)), axis=0, keepdims=True)
            sel = flat == pos
            ts_scr[kk:kk + 1, :] = m
            te_scr[kk:kk + 1, :] = jnp.max(jnp.where(sel, cidx, F32(-1.0)), axis=0, keepdims=True)
            cand = jnp.where(sel, neg_inf, cand)
        top = ts_scr[...]
        ex = jnp.exp(top - top[0:1, :])
        gate = ex / jnp.sum(ex, axis=0, keepdims=True)
        eidx = te_scr[...]
        irow = jnp.floor(eidx * F32(1.0 / N_KEYS))
        r0 = pl.multiple_of(hd * PEER_TOPK, PEER_TOPK)
        it_scr[pl.ds(r0, PEER_TOPK), :] = irow
        jt_scr[pl.ds(r0, PEER_TOPK), :] = eidx - irow * F32(N_KEYS)
        gt_scr[pl.ds(r0, PEER_TOPK), :] = gate

    def publish_ranked():
        isel_scr[ranked] = it_scr[...].T
        jsel_scr[ranked] = jt_scr[...].T
        gsel_scr[ranked] = gt_scr[...].T

    @pl.when(e == 0)
    def _():
        out_ref[...] = x_ref[...]

        @pl.when(r == 1)
        def _():
            def head(hd, _):
                rank_pairs(hd)
                return 0

            lax.fori_loop(0, PEER_HEADS, head, 0)
            publish_ranked()

        @pl.when(jnp.logical_and(r >= 2, r <= n))
        def _():
            def trip(i, _):
                for t in range(tokens_per_head):
                    mask_token(i * tokens_per_head + t)
                rank_pairs(i)
                return 0

            lax.fori_loop(0, PEER_HEADS, trip, 0)
            publish_ranked()

        @pl.when(r == n + 1)
        def _():
            def trip(i, _):
                for t in range(MASK_UNROLL):
                    mask_token(i * MASK_UNROLL + t)
                return 0

            lax.fori_loop(0, tm // MASK_UNROLL, trip, 0)

    def slot_scores():
        d0 = pl.multiple_of(e * PEER_HALF, PEER_HALF)
        return _dot_nt(q_ref[:, pl.ds(d0, PEER_HALF)], keys_ref[e])

    def rank_slot(scores):
        lane = lax.broadcasted_iota(jnp.int32, (tm, N_KEYS), 1)
        picked = jnp.zeros((tm, N_KEYS), jnp.int32)
        a = scores
        for kk in range(PEER_TOPK):
            idx = jnp.argmax(a, axis=-1, keepdims=True).astype(jnp.int32)
            a = jnp.where(lane == idx, neg_inf, a)
            picked = jnp.where(lane == kk, idx, picked)
        vals = jnp.take_along_axis(scores, picked, axis=1)
        sv_scr[e] = vals.T[0:PEER_TOPK, :]
        si_scr[e] = (N_KEYS - 1 - picked).astype(F32).T[0:PEER_TOPK, :]

    def expert_chunk():
        h = h_ref[...]
        row0 = e * rows_per_step
        weighted = []
        for c in range(te // PEER_CHUNK):
            rows = slice(c * PEER_CHUNK, (c + 1) * PEER_CHUNK)
            pre = _dot_nt(h, u_ref[rows, :])
            half_mask = jnp.concatenate(
                [mask_scr[pl.ds(row0 + c * (PEER_CHUNK // N_KEYS) + rr, tm, stride=MASK_PITCH), :]
                 for rr in range(PEER_CHUNK // N_KEYS)], axis=1)
            weighted.append((pre * (1.0 + lax.erf(pre * np.float32(np.sqrt(0.5)))) * half_mask).astype(BF16))
        out_ref[...] += _dot(jnp.concatenate(weighted, axis=1), v_ref[...])

    ranks = r < n
    sweeps = r >= 2

    @pl.when(jnp.logical_and(ranks, jnp.logical_not(sweeps)))
    def _():
        rank_slot(slot_scores())

    @pl.when(jnp.logical_and(ranks, sweeps))
    def _():
        a = slot_scores()
        expert_chunk()
        rank_slot(a)

    @pl.when(jnp.logical_and(jnp.logical_not(ranks), sweeps))
    def _():
        expert_chunk()


def _peer(h, x2, q_route, w, tm, te):
    t, d = h.shape
    n = t // tm
    nexp = w['peer_v'].shape[0]
    assert nexp // te == 2 * PEER_HEADS, "one half-key slot is ranked per expert step"
    assert tm % PEER_HEADS == 0 and tm % MASK_UNROLL == 0
    nsel = PEER_HEADS * PEER_TOPK
    _, flat = _pair_blocks()
    flat = jnp.asarray(np.repeat(flat[:, None], tm, axis=1))
    swept_tile = lambda r, e: (jnp.clip(r - 2, 0, n - 1), 0)
    dense_tile = lambda width: pl.BlockSpec((tm, width), swept_tile, pipeline_mode=pl.Buffered(1))
    expert_tile = pl.BlockSpec((te, d), lambda r, e: (jnp.where(r >= 2, e, 0), 0))
    return pl.pallas_call(
        _peer_kernel,
        grid=(n + 2, nexp // te),
        in_specs=[dense_tile(d), dense_tile(d),
                  pl.BlockSpec((tm, q_route.shape[1]), lambda r, e: (jnp.minimum(r, n - 1), 0),
                               pipeline_mode=pl.Buffered(1)),
                  _const_spec(w['sub_keys'].shape), _const_spec(flat.shape), expert_tile, expert_tile],
        out_specs=pl.BlockSpec((tm, d), swept_tile),
        out_shape=jax.ShapeDtypeStruct((t, d), F32),
        scratch_shapes=[pltpu.VMEM((tm * MASK_PITCH, LANES), F32),
                        pltpu.VMEM((2 * PEER_HEADS, PEER_TOPK, tm), F32),
                        pltpu.VMEM((2 * PEER_HEADS, PEER_TOPK, tm), F32),
                        pltpu.VMEM((PEER_TOPK, tm), F32),
                        pltpu.VMEM((PEER_TOPK, tm), F32),
                        pltpu.VMEM((nsel, tm), F32),
                        pltpu.VMEM((nsel, tm), F32),
                        pltpu.VMEM((nsel, tm), F32),
                        pltpu.VMEM((2, tm, nsel), F32),
                        pltpu.VMEM((2, tm, nsel), F32),
                        pltpu.VMEM((2, tm, nsel), F32)],
        compiler_params=_params(("arbitrary", "arbitrary"), PEER_VMEM_LIMIT),
        name="peer",
    )(h, x2, q_route, w['sub_keys'], flat, w['peer_u'], w['peer_v'])


def _slot_gain(parts):
    row = jnp.zeros((SLOT,), F32)
    for off, g in parts:
        row = lax.dynamic_update_slice(row, g.astype(F32), (off,))
    return row[None, :]


def _group_avg_matrix():
    m = np.zeros((SLOT, SLOT), np.float32)
    m[:QK_NOPE, :QK_NOPE] = 1.0 / QK_NOPE
    m[ROPE_LO:ROPE_LO + QK_ROPE, ROPE_LO:ROPE_LO + QK_ROPE] = 1.0 / QK_ROPE
    return jnp.asarray(np.concatenate([m, m], axis=0), BF16)


def _rope_tables(pos):
    inv = ROPE_THETA ** (-jnp.arange(ROPE_HALF, dtype=F32) / ROPE_HALF)
    ang = pos.astype(F32)[:, None] * inv[None, :]
    cos, sin = jnp.cos(ang), jnp.sin(ang)
    n = pos.shape[0]
    tail = SLOT - ROPE_LO - QK_ROPE
    cos_t = jnp.concatenate([jnp.ones((n, ROPE_LO), F32), cos, cos, jnp.zeros((n, tail), F32)], axis=1)
    sin_up = jnp.concatenate([jnp.zeros((n, ROPE_LO + ROPE_HALF), F32), sin, jnp.zeros((n, tail), F32)], axis=1)
    sin_dn = jnp.concatenate([jnp.zeros((n, ROPE_LO), F32), -sin, jnp.zeros((n, tail + ROPE_HALF), F32)], axis=1)
    return cos_t, sin_up, sin_dn


def _prep_weights(p, i):
    w_in = p['w_in'][i]
    q_lora = p['g_q_lat'].shape[1]
    kv_lora = p['g_kv_lat'].shape[1]
    gw = p['g_gm'].shape[1]
    d = w_in.shape[0]
    o = np.cumsum([0, q_lora, kv_lora, QK_ROPE, gw, gw, d, d])
    seg = lambda k: w_in[:, o[k]:o[k + 1]]
    row = lambda a: a[i][None, :].astype(F32)
    w = {}
    w['g_mix'] = row(p['g_mix'])
    w['w_cq'] = seg(0).astype(BF16)
    w['w_ckv'] = seg(1).astype(BF16)
    w['w_kr'] = jnp.pad(seg(2), ((0, 0), (ROPE_LO, SLOT - ROPE_LO - QK_ROPE))).astype(BF16)
    w['w_u'] = seg(3).astype(BF16)
    w['w_v'] = seg(4).astype(BF16)
    w['w_ga'] = seg(5).astype(BF16)
    w['w_gb'] = seg(6).astype(BF16)
    w['g_q_lat'] = row(p['g_q_lat'])
    w['g_kv_lat'] = row(p['g_kv_lat'])
    w['g_kr'] = _slot_gain([(ROPE_LO, p['g_kr'][i])])
    w['g_gm'] = row(p['g_gm'])
    head_dim = QK_NOPE + QK_ROPE
    w_uq = p['w_uq'][i].reshape(q_lora, MLA_HEADS, head_dim)
    w['w_uq'] = jnp.pad(w_uq, ((0, 0), (0, 0), (0, SLOT - head_dim))).reshape(q_lora, -1).astype(BF16)
    w['g_q'] = _slot_gain([(0, p['g_qn'][i]), (ROPE_LO, p['g_qr'][i])])
    w['m_avg'] = _group_avg_matrix()
    w_uk = p['w_uk'][i].reshape(kv_lora, MLA_HEADS, QK_NOPE)
    w['w_uk'] = jnp.pad(w_uk, ((0, 0), (0, 0), (0, SLOT - QK_NOPE))).reshape(kv_lora, -1).astype(BF16)
    w['g_kn'] = _slot_gain([(0, p['g_kn'][i])])
    w_uv = p['w_uv'][i].reshape(kv_lora, MLA_HEADS // 2, 2, V_HEAD)
    eye2 = jnp.eye(2, dtype=w_uv.dtype)
    w['w_uv'] = jnp.einsum('cpjd,jk->cpjkd', w_uv, eye2).reshape(kv_lora, -1).astype(BF16)
    w['w_uv_plain'] = p['w_uv'][i].astype(BF16)
    place = np.zeros((QK_ROPE, SLOT), np.float32)
    place[np.arange(QK_ROPE), ROPE_LO + np.arange(QK_ROPE)] = 1.0
    w['rope_place'] = jnp.asarray(place, BF16)
    w['w_oa'] = p['w_oa'][i].astype(BF16)
    w['w_ob'] = p['w_ob'][i].astype(BF16)
    w['w_o'] = p['w_o'][i].astype(BF16)
    w['g_xattn'] = row(p['g_xattn'])
    w['g_mem'] = row(p['g_mem'])
    w['w_cq_mem'] = p['w_cq'][i].astype(BF16)
    w['g_cq'] = row(p['g_cq'])
    w['w_ck'] = p['w_ck'][i].astype(BF16)
    w['g_ck'] = row(p['g_ck'])
    w['w_cv'] = p['w_cv'][i].astype(BF16)
    w['w_co'] = p['w_co'][i].astype(BF16)
    w['g_ffn'] = row(p['g_ffn'])
    w['w_pq'] = p['w_pq'][i].astype(BF16)
    w['sub_keys'] = p['sub_keys'][i].reshape(2 * PEER_HEADS, N_KEYS, PEER_HALF)[:, ::-1, :].astype(BF16)
    w['peer_u'] = p['peer_u'][i].astype(BF16)
    w['peer_v'] = p['peer_v'][i].astype(BF16)
    w['w_s'] = p['w_s'][i]
    w['b_s'] = p['b_s'][i]
    return w


def _spatial_operands(w, seq, tm):
    chunk = min(seq, GM_CHUNK)
    reps = tm // chunk
    w_mask = jnp.tril(w['w_s'][:, :chunk, :chunk])
    w_mask = w_mask.astype(BF16)
    mix = jnp.concatenate(
        [jnp.pad(w_mask, ((0, 0), (0, 0), (a * chunk, (reps - 1 - a) * chunk))) for a in range(reps)], axis=1)
    bias = jnp.tile(w['b_s'][:, :chunk].T, (reps, 1))
    bias = jnp.repeat(bias, LANES, axis=1)
    return mix.astype(BF16), bias.astype(F32)


def _token_mixer(x2d, w, seq, pos, past, tm):
    t, d = x2d.shape
    nb = t // seq
    period = max(seq, tm)
    tabs = _rope_tables(jnp.tile(pos, period // seq))
    cq, ckv, kr, u, vg, siga, sigb = _in_proj(x2d, w, tabs, tm)
    mix, bias = _spatial_operands(w, seq, tm)
    gbob = _gmlp(u, vg, sigb, mix, bias, w['w_ob'], tm)
    q = _q_proj(cq, w, tabs, tm)
    if past is None:
        k, v = _kv_proj(ckv, kr, w, tm)
        o = _attn_prompt(q.reshape(nb, seq, -1), k.reshape(nb, seq, -1), v.reshape(nb, seq, -1),
                         _score_bound(w), ATTN_TILE)
    else:
        past_ckv, past_kr = past
        o = _attn_sample(q.reshape(nb, seq, -1), past_ckv, past_kr, ckv.reshape(nb, seq, -1),
                         kr.reshape(nb, seq, SLOT), w)
    x1, qc = _merge(x2d, o.reshape(t, -1), siga, gbob, w, tm)
    return x1, qc, ckv, kr[:, ROPE_LO:ROPE_LO + QK_ROPE], vg


def _tail(x1, qc, mk, mv, w, nb, seq, tm):
    t, d = x1.shape
    oc = _cross(qc.reshape(nb, seq, -1), mk, mv, min(seq, tm))
    x2, h, q_route = _cross_out(x1, oc.reshape(t, -1), w, tm)
    return _peer(h, x2, q_route, w, PEER_ROWS, PEER_EXPERTS)


def kernel(x_prompt, x_sample, cache_mla_ckv, cache_mla_krope, cache_mem_k, cache_mem_v, mem_prompt, g_mix, w_in, g_q_lat, w_uq, g_qn, g_qr, g_kv_lat, g_kr, w_uk, w_uv, g_kn, w_oa, g_gm, w_s, b_s, w_ob, w_o, g_xattn, g_mem, w_cq, g_cq, w_ck, g_ck, w_cv, w_co, g_ffn, w_pq, sub_keys, peer_u, peer_v):
    params = dict(g_mix=g_mix, w_in=w_in, g_q_lat=g_q_lat, w_uq=w_uq, g_qn=g_qn, g_qr=g_qr,
                  g_kv_lat=g_kv_lat, g_kr=g_kr, w_uk=w_uk, w_uv=w_uv, g_kn=g_kn, w_oa=w_oa, g_gm=g_gm,
                  w_s=w_s, b_s=b_s, w_ob=w_ob, w_o=w_o, g_xattn=g_xattn, g_mem=g_mem, w_cq=w_cq,
                  g_cq=g_cq, w_ck=w_ck, g_ck=g_ck, w_cv=w_cv, w_co=w_co, g_ffn=g_ffn, w_pq=w_pq,
                  sub_keys=sub_keys, peer_u=peer_u, peer_v=peer_v)
    bp, sp, d = x_prompt.shape
    bs, ss, _ = x_sample.shape
    depth = w_in.shape[0]
    past_len = cache_mla_ckv.shape[2]
    n_mem = mem_prompt.shape[1]
    tm = ROW_TILE
    assert (bp * sp) % tm == 0 and (bs * ss) % tm == 0 and sp % ATTN_TILE == 0
    pos_p = jnp.arange(sp)
    pos_s = past_len + jnp.arange(ss)
    xp = x_prompt.reshape(bp * sp, d)
    xs = x_sample.reshape(bs * ss, d)
    outs = [[] for _ in range(7)]
    for i in range(depth):
        w = _prep_weights(params, i)
        xp1, qcp, ckv_p, kr_p, _ = _token_mixer(xp, w, sp, pos_p, None, tm)
        xs1, qcs, ckv_s, kr_s, vg_s = _token_mixer(xs, w, ss, pos_s, (cache_mla_ckv[i], cache_mla_krope[i]), tm)
        mk_p, mv_p = _mem_kv(mem_prompt.reshape(bp * n_mem, d), w, min(tm, bp * n_mem))
        xp = _tail(xp1, qcp, mk_p.reshape(bp, n_mem, -1), mv_p.reshape(bp, n_mem, -1), w, bp, sp, tm)
        xs = _tail(xs1, qcs, cache_mem_k[i].reshape(bs, n_mem, -1), cache_mem_v[i].reshape(bs, n_mem, -1),
                   w, bs, ss, tm)
        outs[0].append(ckv_p.reshape(bp, sp, -1))
        outs[1].append(kr_p.reshape(bp, sp, -1))
        outs[2].append(mk_p.reshape(bp, n_mem, MEM_HEADS, MEM_HEAD_DIM))
        outs[3].append(mv_p.reshape(bp, n_mem, MEM_HEADS, MEM_HEAD_DIM))
        outs[4].append(ckv_s.reshape(bs, ss, -1))
        outs[5].append(kr_s.reshape(bs, ss, -1))
        outs[6].append(vg_s.reshape(bs, ss, -1))
    return (xp.reshape(bp, sp, d), xs.reshape(bs, ss, d)) + tuple(jnp.stack(o) for o in outs)
```

```python
import functools

import jax
import jax.numpy as jnp
import numpy as np
from jax import lax
from jax.experimental import pallas as pl
from jax.experimental.pallas import tpu as pltpu

CHUNK = 64
EPS = 1e-6
MLA_HEADS = 16
QK_NOPE = 64
QK_ROPE = 32
V_HEAD = 64
ROPE_THETA = 10000.0
MLA_SCALE = (QK_NOPE + QK_ROPE) ** -0.5
GM_CHUNK = 128
GM_GROUPS = 8
MEM_HEADS = 4
MEM_HEAD_DIM = 128
MEM_SCALE = MEM_HEAD_DIM ** -0.5
PEER_HEADS = 8
N_KEYS = 128
PEER_TOPK = 16
PEER_HALF = 128

LANES = 128
SLOT = LANES
ROPE_LO = QK_NOPE
ROPE_HALF = QK_ROPE // 2
ROW_TILE = 512
ATTN_TILE = 512
LOG2_E = float(np.log2(np.e))
STATIC_SHIFT_LIMIT = 48.0
BOUND_MARGIN = 1.02
PEER_ROWS = 512
PEER_EXPERTS = 1024
PEER_CHUNK = 512
MASK_PITCH = N_KEYS + 8
MASK_UNROLL = 32
VMEM_LIMIT = 48 * 1024 * 1024
PEER_VMEM_LIMIT = 60 * 1024 * 1024

F32 = jnp.float32
BF16 = jnp.bfloat16
_NT = (((1,), (1,)), ((), ()))


def _dot(a, b):
    return jnp.dot(a, b, preferred_element_type=F32)


def _dot_nt(a, b):
    return lax.dot_general(a, b, _NT, preferred_element_type=F32)


def _rms(xf, g):
    return xf * lax.rsqrt(jnp.mean(xf * xf, axis=-1, keepdims=True) + EPS) * g


def _gelu(x):
    return 0.5 * x * (1.0 + lax.erf(x * np.float32(np.sqrt(0.5))))


def _group_mean(sq, m_ref):
    hi = sq.astype(BF16)
    lo = (sq - hi.astype(F32)).astype(BF16)
    return _dot(jnp.concatenate([hi, lo], axis=1), m_ref[...])


def _rope_slot(y, cos, sin_up, sin_dn):
    return (y * cos + pltpu.roll(y, ROPE_HALF, 1) * sin_up
            + pltpu.roll(y, SLOT - ROPE_HALF, 1) * sin_dn)


def _const_spec(shape):
    nd = len(shape)
    return pl.BlockSpec(shape, lambda *_: (0,) * nd, pipeline_mode=pl.Buffered(1))


def _row_spec(tm, width):
    return pl.BlockSpec((tm, width), lambda i: (i, 0))


def _params(sem, limit=VMEM_LIMIT):
    return pltpu.CompilerParams(dimension_semantics=sem, vmem_limit_bytes=limit)


def _in_proj_kernel(x_ref, gmix_ref, wcq_ref, wckv_ref, wkr_ref, wu_ref, wv_ref, wga_ref, wgb_ref,
                    gq_ref, gkv_ref, gkr_ref, ggm_ref, cos_ref, sup_ref, sdn_ref,
                    cq_out, ckv_out, kr_out, u_out, vg_out, siga_out, sigb_out):
    h = _rms(x_ref[...], gmix_ref[...]).astype(BF16)
    cq_out[...] = _rms(_dot(h, wcq_ref[...]), gq_ref[...]).astype(BF16)
    ckv_out[...] = _rms(_dot(h, wckv_ref[...]), gkv_ref[...])
    kr = _dot(h, wkr_ref[...])
    ms = jnp.sum(kr * kr, axis=-1, keepdims=True) * (1.0 / QK_ROPE)
    kr = kr * lax.rsqrt(ms + EPS) * gkr_ref[...]
    kr_out[...] = _rope_slot(kr, cos_ref[...], sup_ref[...], sdn_ref[...])
    u_out[...] = _gelu(_dot(h, wu_ref[...])).astype(BF16)
    vg_out[...] = _rms(_gelu(_dot(h, wv_ref[...])), ggm_ref[...])
    siga_out[...] = jax.nn.sigmoid(_dot(h, wga_ref[...])).astype(BF16)
    sigb_out[...] = jax.nn.sigmoid(_dot(h, wgb_ref[...])).astype(BF16)


def _in_proj(x2d, w, tabs, tm):
    t, d = x2d.shape
    cos, sup, sdn = tabs
    ntab = cos.shape[0] // tm
    tab_spec = pl.BlockSpec((tm, SLOT), lambda i: (i % ntab, 0))
    consts = [w['g_mix'], w['w_cq'], w['w_ckv'], w['w_kr'], w['w_u'], w['w_v'], w['w_ga'], w['w_gb'],
              w['g_q_lat'], w['g_kv_lat'], w['g_kr'], w['g_gm']]
    gw = w['w_u'].shape[1]
    widths = [w['w_cq'].shape[1], w['w_ckv'].shape[1], SLOT, gw, gw, d, d]
    dtypes = [BF16, F32, F32, BF16, F32, BF16, BF16]
    return pl.pallas_call(
        _in_proj_kernel,
        grid=(t // tm,),
        in_specs=[_row_spec(tm, d)] + [_const_spec(c.shape) for c in consts] + [tab_spec] * 3,
        out_specs=[_row_spec(tm, n) for n in widths],
        out_shape=[jax.ShapeDtypeStruct((t, n), dt) for n, dt in zip(widths, dtypes)],
        compiler_params=_params(("parallel",)),
        name="in_proj",
    )(x2d, *consts, cos, sup, sdn)


def _gmlp_kernel(u_ref, vg_ref, sigb_ref, mix_ref, bias_ref, wob_ref, out_ref):
    vgb = vg_ref[...].astype(BF16)
    parts = []
    for g in range(GM_GROUPS):
        sl = slice(g * LANES, (g + 1) * LANES)
        mixed = _dot(mix_ref[g], vgb[:, sl]) + bias_ref[:, sl]
        parts.append((u_ref[:, sl].astype(F32) * mixed).astype(BF16))
    ob = _dot(jnp.concatenate(parts, axis=1), wob_ref[...])
    out_ref[...] = sigb_ref[...].astype(F32) * ob


def _gmlp(u, vg, sigb, mix, bias, w_ob, tm):
    t, gw = u.shape
    d = w_ob.shape[1]
    return pl.pallas_call(
        _gmlp_kernel,
        grid=(t // tm,),
        in_specs=[_row_spec(tm, gw), _row_spec(tm, gw), _row_spec(tm, d),
                  _const_spec(mix.shape), _const_spec(bias.shape), _const_spec(w_ob.shape)],
        out_specs=_row_spec(tm, d),
        out_shape=jax.ShapeDtypeStruct((t, d), F32),
        compiler_params=_params(("parallel",)),
        name="gmlp",
    )(u, vg, sigb, mix, bias, w_ob)


def _q_proj_kernel(cq_ref, wuq_ref, gq_ref, mavg_ref, cos_ref, sup_ref, sdn_ref, q_out):
    q = _dot(cq_ref[...], wuq_ref[...])
    cos, sup, sdn = cos_ref[...], sup_ref[...], sdn_ref[...]
    for h in range(MLA_HEADS):
        sl = slice(h * SLOT, (h + 1) * SLOT)
        qs = q[:, sl]
        y = qs * lax.rsqrt(_group_mean(qs * qs, mavg_ref) + EPS) * gq_ref[...]
        q_out[:, sl] = _rope_slot(y, cos, sup, sdn).astype(BF16)


def _q_proj(cq, w, tabs, tm):
    t, ql = cq.shape
    cos, sup, sdn = tabs
    ntab = cos.shape[0] // tm
    tab_spec = pl.BlockSpec((tm, SLOT), lambda i: (i % ntab, 0))
    n = MLA_HEADS * SLOT
    return pl.pallas_call(
        _q_proj_kernel,
        grid=(t // tm,),
        in_specs=[_row_spec(tm, ql), _const_spec(w['w_uq'].shape), _const_spec(w['g_q'].shape),
                  _const_spec(w['m_avg'].shape)] + [tab_spec] * 3,
        out_specs=_row_spec(tm, n),
        out_shape=jax.ShapeDtypeStruct((t, n), BF16),
        compiler_params=_params(("parallel",)),
        name="q_proj",
    )(cq, w['w_uq'], w['g_q'], w['m_avg'], cos, sup, sdn)


def _kv_proj_kernel(ckv_ref, kr_ref, wuk_ref, wuv_ref, gkn_ref, mavg_ref, k_out, v_out):
    c = ckv_ref[...].astype(BF16)
    kn = _dot(c, wuk_ref[...])
    kr = kr_ref[...]
    for h in range(MLA_HEADS):
        sl = slice(h * SLOT, (h + 1) * SLOT)
        ks = kn[:, sl]
        y = ks * lax.rsqrt(_group_mean(ks * ks, mavg_ref) + EPS) * gkn_ref[...]
        k_out[:, sl] = (y + kr).astype(BF16)
    v_out[...] = _dot(c, wuv_ref[...]).astype(BF16)


def _kv_proj(ckv, kr, w, tm):
    t, kl = ckv.shape
    n = MLA_HEADS * SLOT
    return pl.pallas_call(
        _kv_proj_kernel,
        grid=(t // tm,),
        in_specs=[_row_spec(tm, kl), _row_spec(tm, SLOT), _const_spec(w['w_uk'].shape),
                  _const_spec(w['w_uv'].shape), _const_spec(w['g_kn'].shape),
                  _const_spec(w['m_avg'].shape)],
        out_specs=[_row_spec(tm, n), _row_spec(tm, n)],
        out_shape=[jax.ShapeDtypeStruct((t, n), BF16)] * 2,
        compiler_params=_params(("parallel",)),
        name="kv_proj",
    )(ckv, kr, w['w_uk'], w['w_uv'], w['g_kn'], w['m_avg'])


def _attn_prompt_kernel(bound_ref, q_ref, k_ref, v_ref, o_ref):
    tile = q_ref.shape[1]
    qi = pl.program_id(2)
    q = q_ref[0]
    bound = bound_ref[0]
    first = lax.broadcasted_iota(jnp.int32, (1, LANES), 1) < V_HEAD
    row_chunk = lax.broadcasted_iota(jnp.int32, (tile, tile), 0) // CHUNK
    col_chunk = lax.broadcasted_iota(jnp.int32, (tile, tile), 1) // CHUNK
    visible = col_chunk <= row_chunk

    def static_step(kstart, carry, masked):
        sums, acc = carry
        kstart = pl.multiple_of(kstart, tile)
        k = k_ref[0, pl.ds(kstart, tile), :]
        v = v_ref[0, pl.ds(kstart, tile), :]
        new_sums = []
        for h in range(2):
            sl = slice(h * SLOT, (h + 1) * SLOT)
            s = _dot_nt(q[:, sl], k[:, sl]) * F32(MLA_SCALE * LOG2_E) - bound
            if masked:
                s = jnp.where(visible, s, F32(-1e30))
            p = jnp.exp2(s)
            part = sums[h]
            for c in range(tile // LANES):
                part = part + p[:, c * LANES:(c + 1) * LANES]
            new_sums.append(part)
            acc = acc + _dot(p.astype(BF16), v[:, sl])
        return tuple(new_sums), acc

    @pl.when(bound < STATIC_SHIFT_LIMIT)
    def _():
        zeros = jnp.zeros((tile, LANES), F32)

        def pair(i, c):
            c = static_step(2 * i * tile, c, False)
            return static_step((2 * i + 1) * tile, c, False)

        def odd_tail(c):
            return static_step(qi * tile, static_step((qi - 1) * tile, c, False), True)

        carry = lax.fori_loop(0, qi // 2, pair, ((zeros, zeros), zeros))
        sums, acc = lax.cond(qi % 2 == 1, odd_tail, lambda c: static_step(qi * tile, c, True), carry)
        l0 = jnp.sum(sums[0], axis=-1, keepdims=True)
        l1 = jnp.sum(sums[1], axis=-1, keepdims=True)
        o_ref[0] = (acc / jnp.where(first, l0, l1)).astype(BF16)

    def step(kstart, carry, masked):
        ms, ls, acc = carry
        kstart = pl.multiple_of(kstart, tile)
        k = k_ref[0, pl.ds(kstart, tile), :]
        v = v_ref[0, pl.ds(kstart, tile), :]
        new_ms, new_ls, alphas, pvs = [], [], [], []
        for h in range(2):
            sl = slice(h * SLOT, (h + 1) * SLOT)
            s = _dot_nt(q[:, sl], k[:, sl]) * F32(MLA_SCALE * LOG2_E)
            if masked:
                s = jnp.where(visible, s, F32(-1e30))
            m_new = jnp.maximum(ms[h], jnp.max(s, axis=-1, keepdims=True))
            alpha = jnp.exp2(ms[h] - m_new)
            p = jnp.exp2(s - m_new)
            new_ls.append(alpha * ls[h] + jnp.sum(p, axis=-1, keepdims=True))
            new_ms.append(m_new)
            alphas.append(alpha)
            pvs.append(_dot(p.astype(BF16), v[:, sl]))
        acc = acc * jnp.where(first, alphas[0], alphas[1]) + pvs[0] + pvs[1]
        return tuple(new_ms), tuple(new_ls), acc

    @pl.when(bound >= STATIC_SHIFT_LIMIT)
    def _():
        neg = jnp.full((tile, 1), -1e30, F32)
        zero = jnp.zeros((tile, 1), F32)
        init = ((neg, neg), (zero, zero), jnp.zeros((tile, LANES), F32))
        carry = lax.fori_loop(0, qi, lambda i, c: step(i * tile, c, False), init)
        _, ls, acc = step(qi * tile, carry, True)
        o_ref[0] = (acc / jnp.where(first, ls[0], ls[1])).astype(BF16)


def _score_bound(w):
    gmax = lambda g, lo, n: jnp.max(jnp.abs(g[0, lo:lo + n]))
    nope = QK_NOPE * gmax(w['g_q'], 0, QK_NOPE) * gmax(w['g_kn'], 0, QK_NOPE)
    rope = QK_ROPE * gmax(w['g_q'], ROPE_LO, QK_ROPE) * gmax(w['g_kr'], ROPE_LO, QK_ROPE)
    return (BOUND_MARGIN * MLA_SCALE * LOG2_E * (nope + rope)).reshape(1).astype(F32)


def _attn_prompt(q, k, v, bound, tile):
    b, s, _ = q.shape
    pairs = MLA_HEADS // 2
    return pl.pallas_call(
        _attn_prompt_kernel,
        grid=(b, pairs, s // tile),
        in_specs=[pl.BlockSpec(memory_space=pltpu.SMEM),
                  pl.BlockSpec((1, tile, 2 * SLOT), lambda bi, p, i: (bi, i, p)),
                  pl.BlockSpec((1, s, 2 * SLOT), lambda bi, p, i: (bi, 0, p)),
                  pl.BlockSpec((1, s, 2 * SLOT), lambda bi, p, i: (bi, 0, p))],
        out_specs=pl.BlockSpec((1, tile, LANES), lambda bi, p, i: (bi, i, p)),
        out_shape=jax.ShapeDtypeStruct((b, s, pairs * LANES), BF16),
        compiler_params=_params(("parallel", "parallel", "arbitrary")),
        name="attn_prompt",
    )(bound, q, k, v)


def _attn_sample_kernel(q_ref, pckv_ref, pkr_ref, nckv_ref, nkr_ref, wuk_ref, wuv_ref, gkn_ref, place_ref,
                        o_ref):
    sq = q_ref.shape[1]
    rows = MLA_HEADS * sq
    gkn = gkn_ref[...]

    def keys_values(ckv, kr_slot):
        c = ckv.astype(BF16)
        kn = _dot(c, wuk_ref[...])
        parts = []
        for h in range(MLA_HEADS):
            ks = kn[:, h * SLOT:(h + 1) * SLOT]
            ms = jnp.sum(ks * ks, axis=-1, keepdims=True) * (1.0 / QK_NOPE)
            parts.append((ks * lax.rsqrt(ms + EPS) * gkn + kr_slot).astype(BF16))
        return jnp.concatenate(parts, axis=1), _dot(c, wuv_ref[...]).astype(BF16)

    past_kr = _dot(pkr_ref[0].astype(BF16), place_ref[...])
    k_p, v_p = keys_values(pckv_ref[0], past_kr)
    k_n, v_n = keys_values(nckv_ref[0], nkr_ref[0])
    k = jnp.concatenate([k_p, k_n], axis=0)
    v = jnp.concatenate([v_p, v_n], axis=0)

    qt = jnp.concatenate([q_ref[0].astype(F32)] * MLA_HEADS, axis=0)
    q_shape = (rows, MLA_HEADS * SLOT)
    own = (lax.broadcasted_iota(jnp.int32, q_shape, 0) // sq
           == lax.broadcasted_iota(jnp.int32, q_shape, 1) // SLOT)
    qbd = jnp.where(own, qt, 0.0).astype(BF16)
    s = _dot_nt(qbd, k) * MLA_SCALE
    p = jnp.exp(s - jnp.max(s, axis=-1, keepdims=True))
    p = p / jnp.sum(p, axis=-1, keepdims=True)
    o_all = _dot(p.astype(BF16), v)
    o_shape = o_all.shape
    own = (lax.broadcasted_iota(jnp.int32, o_shape, 0) // sq
           == lax.broadcasted_iota(jnp.int32, o_shape, 1) // V_HEAD)
    o_all = jnp.where(own, o_all, 0.0)
    o = o_all[0:sq]
    for h in range(1, MLA_HEADS):
        o = o + o_all[h * sq:(h + 1) * sq]
    o_ref[0] = o.astype(BF16)


def _attn_sample(q, past_ckv, past_kr, new_ckv, new_kr, w):
    b, sq, nq = q.shape
    npast, kl = past_ckv.shape[1:]
    nv = w['w_uv_plain'].shape[1]
    stream = lambda n, width: pl.BlockSpec((1, n, width), lambda bi: (bi, 0, 0))
    consts = [w['w_uk'], w['w_uv_plain'], w['g_kn'], w['rope_place']]
    return pl.pallas_call(
        _attn_sample_kernel,
        grid=(b,),
        in_specs=[stream(sq, nq), stream(npast, kl), stream(npast, QK_ROPE), stream(sq, kl), stream(sq, SLOT)]
        + [_const_spec(c.shape) for c in consts],
        out_specs=stream(sq, nv),
        out_shape=jax.ShapeDtypeStruct((b, sq, nv), BF16),
        compiler_params=_params(("parallel",)),
        name="attn_sample",
    )(q, past_ckv, past_kr, new_ckv, new_kr, *consts)


def _merge_kernel(x_ref, o_ref, siga_ref, gbob_ref, woa_ref, wo_ref, gx_ref, wcq_ref, gcq_ref,
                  x1_out, qc_out):
    oa = _dot(o_ref[...], woa_ref[...])
    merged = siga_ref[...].astype(F32) * oa + gbob_ref[...]
    x1 = x_ref[...] + _dot(merged.astype(BF16), wo_ref[...])
    x1_out[...] = x1
    qc = _dot(_rms(x1, gx_ref[...]).astype(BF16), wcq_ref[...])
    for h in range(MEM_HEADS):
        sl = slice(h * MEM_HEAD_DIM, (h + 1) * MEM_HEAD_DIM)
        qc_out[:, sl] = _rms(qc[:, sl], gcq_ref[...]).astype(BF16)


def _merge(x2d, o, siga, gbob, w, tm):
    t, d = x2d.shape
    consts = [w['w_oa'], w['w_o'], w['g_xattn'], w['w_cq_mem'], w['g_cq']]
    nq = w['w_cq_mem'].shape[1]
    return pl.pallas_call(
        _merge_kernel,
        grid=(t // tm,),
        in_specs=[_row_spec(tm, d), _row_spec(tm, o.shape[1]), _row_spec(tm, d), _row_spec(tm, d)]
        + [_const_spec(c.shape) for c in consts],
        out_specs=[_row_spec(tm, d), _row_spec(tm, nq)],
        out_shape=[jax.ShapeDtypeStruct((t, d), F32), jax.ShapeDtypeStruct((t, nq), BF16)],
        compiler_params=_params(("parallel",)),
        name="merge",
    )(x2d, o, siga, gbob, *consts)


def _mem_kv_kernel(mem_ref, gmem_ref, wck_ref, gck_ref, wcv_ref, k_out, v_out):
    m = _rms(mem_ref[...], gmem_ref[...]).astype(BF16)
    k = _dot(m, wck_ref[...])
    for h in range(MEM_HEADS):
        sl = slice(h * MEM_HEAD_DIM, (h + 1) * MEM_HEAD_DIM)
        k_out[:, sl] = _rms(k[:, sl], gck_ref[...])
    v_out[...] = _dot(m, wcv_ref[...])


def _mem_kv(mem2d, w, tm):
    t, d = mem2d.shape
    consts = [w['g_mem'], w['w_ck'], w['g_ck'], w['w_cv']]
    n = w['w_ck'].shape[1]
    return pl.pallas_call(
        _mem_kv_kernel,
        grid=(t // tm,),
        in_specs=[_row_spec(tm, d)] + [_const_spec(c.shape) for c in consts],
        out_specs=[_row_spec(tm, n)] * 2,
        out_shape=[jax.ShapeDtypeStruct((t, n), F32)] * 2,
        compiler_params=_params(("parallel",)),
        name="mem_kv",
    )(mem2d, *consts)


def _cross_kernel(q_ref, k_ref, v_ref, o_ref):
    q = q_ref[0]
    k = k_ref[0].astype(BF16)
    v = v_ref[0].astype(BF16)
    for h in range(MEM_HEADS):
        sl = slice(h * MEM_HEAD_DIM, (h + 1) * MEM_HEAD_DIM)
        s = _dot_nt(q[:, sl], k[:, sl]) * MEM_SCALE
        p = jnp.exp(s - jnp.max(s, axis=-1, keepdims=True))
        p = p / jnp.sum(p, axis=-1, keepdims=True)
        o_ref[0, :, sl] = _dot(p.astype(BF16), v[:, sl]).astype(BF16)


def _cross(qc, mk, mv, tc):
    b, s, n = qc.shape
    nm = mk.shape[1]
    return pl.pallas_call(
        _cross_kernel,
        grid=(b, s // tc),
        in_specs=[pl.BlockSpec((1, tc, n), lambda bi, i: (bi, i, 0)),
                  pl.BlockSpec((1, nm, n), lambda bi, i: (bi, 0, 0)),
                  pl.BlockSpec((1, nm, n), lambda bi, i: (bi, 0, 0))],
        out_specs=pl.BlockSpec((1, tc, n), lambda bi, i: (bi, i, 0)),
        out_shape=jax.ShapeDtypeStruct((b, s, n), BF16),
        compiler_params=_params(("parallel", "parallel")),
        name="cross_attn",
    )(qc, mk, mv)


def _cross_out_kernel(x1_ref, oc_ref, wco_ref, gffn_ref, wpq_ref, x2_out, h_out, q_out):
    x2 = x1_ref[...] + _dot(oc_ref[...], wco_ref[...])
    x2_out[...] = x2
    h = _rms(x2, gffn_ref[...]).astype(BF16)
    h_out[...] = h
    q_out[...] = _dot(h, wpq_ref[...]).astype(BF16)


def _cross_out(x1, oc, w, tm):
    t, d = x1.shape
    nq = w['w_pq'].shape[1]
    return pl.pallas_call(
        _cross_out_kernel,
        grid=(t // tm,),
        in_specs=[_row_spec(tm, d), _row_spec(tm, oc.shape[1]), _const_spec(w['w_co'].shape),
                  _const_spec(w['g_ffn'].shape), _const_spec(w['w_pq'].shape)],
        out_specs=[_row_spec(tm, d), _row_spec(tm, d), _row_spec(tm, nq)],
        out_shape=[jax.ShapeDtypeStruct((t, d), F32), jax.ShapeDtypeStruct((t, d), BF16),
                   jax.ShapeDtypeStruct((t, nq), BF16)],
        compiler_params=_params(("parallel",)),
        name="cross_out",
    )(x1, oc, w['w_co'], w['g_ffn'], w['w_pq'])


def _pair_blocks():
    k = PEER_TOPK
    blocks, flat = [], []
    n_row = sum(1 for a in range(k) if k // (a + 1) > 2)
    for a in range(n_row):
        nb = k if a == 0 else 8
        blocks.append(('row', a, nb))
        flat += [a * k + b if (a + 1) * (b + 1) <= k else -1 for b in range(nb)]
    for b in range(k // (n_row + 1)):
        na = k if b == 0 else 8
        blocks.append(('col', b, na))
        flat += [a * k + b if (a >= n_row and (a + 1) * (b + 1) <= k) else -1 for a in range(na)]
    return blocks, np.asarray(flat, np.float32)


def _pair_candidates(v0, v1, blocks, combine):
    parts = []
    for kind, fixed, n in blocks:
        if kind == 'row':
            parts.append(combine(v0[fixed:fixed + 1, :], v1[0:n, :]))
        else:
            parts.append(combine(v0[0:n, :], v1[fixed:fixed + 1, :]))
    return jnp.concatenate(parts, axis=0)


def _peer_kernel(h_ref, x_ref, q_ref, keys_ref, flat_ref, u_ref, v_ref, out_ref,
                 mask_scr, sv_scr, si_scr, ts_scr, te_scr, it_scr, jt_scr, gt_scr,
                 isel_scr, jsel_scr, gsel_scr):
    tm = h_ref.shape[0]
    te = v_ref.shape[0]
    r = pl.program_id(0)
    e = pl.program_id(1)
    n = pl.num_programs(0) - 2
    rows_per_step = te // N_KEYS
    tokens_per_head = tm // PEER_HEADS
    blocks, _ = _pair_blocks()
    neg_inf = F32(-jnp.inf)
    key_rows = lax.broadcasted_iota(jnp.int32, (N_KEYS, LANES), 0).astype(F32)
    swept = lax.rem(r, 2)
    ranked = 1 - swept

    def mask_token(t):
        irow = isel_scr[swept, pl.ds(t, 1), :]
        jrow = jsel_scr[swept, pl.ds(t, 1), :]
        grow = gsel_scr[swept, pl.ds(t, 1), :]
        rw = jnp.where(key_rows == irow, 0.5 * grow, 0.0).astype(BF16)
        cw = jnp.where(key_rows == jrow, 1.0, 0.0).astype(BF16)
        r0 = pl.multiple_of(t * MASK_PITCH, 8)
        mask_scr[pl.ds(r0, N_KEYS), :] = _dot_nt(rw, cw)

    def rank_pairs(hd):
        sv0, sv1 = sv_scr[2 * hd], sv_scr[2 * hd + 1]
        si0, si1 = si_scr[2 * hd], si_scr[2 * hd + 1]
        flat = flat_ref[...]
        cand = _pair_candidates(sv0, sv1, blocks, lambda x, y: x + y)
        cand = jnp.where(flat >= 0.0, cand, neg_inf)
        cidx = _pair_candidates(si0, si1, blocks, lambda x, y: x * F32(N_KEYS) + y)
        for kk in range(PEER_TOPK):
            m = jnp.max(cand, axis=0, keepdims=True)
            pos = jnp.min(jnp.where(cand == m, flat, F32(PEER_TOPK * PEER_TOPK)), axis=0, keepdims=True)
            sel = flat == pos
            ts_scr[kk:kk + 1, :] = m
            te_scr[kk:kk + 1, :] = jnp.max(jnp.where(sel, cidx, F32(-1.0)), axis=0, keepdims=True)
            cand = jnp.where(sel, neg_inf, cand)
        top = ts_scr[...]
        ex = jnp.exp(top - top[0:1, :])
        gate = ex / jnp.sum(ex, axis=0, keepdims=True)
        eidx = te_scr[...]
        irow = jnp.floor(eidx * F32(1.0 / N_KEYS))
        r0 = pl.multiple_of(hd * PEER_TOPK, PEER_TOPK)
        it_scr[pl.ds(r0, PEER_TOPK), :] = irow
        jt_scr[pl.ds(r0, PEER_TOPK), :] = eidx - irow * F32(N_KEYS)
        gt_scr[pl.ds(r0, PEER_TOPK), :] = gate

    def publish_ranked():
        isel_scr[ranked] = it_scr[...].T
        jsel_scr[ranked] = jt_scr[...].T
        gsel_scr[ranked] = gt_scr[...].T

    @pl.when(e == 0)
    def _():
        out_ref[...] = x_ref[...]

        @pl.when(r == 1)
        def _():
            def head(hd, _):
                rank_pairs(hd)
                return 0

            lax.fori_loop(0, PEER_HEADS, head, 0)
            publish_ranked()

        @pl.when(jnp.logical_and(r >= 2, r <= n))
        def _():
            def trip(i, _):
                for t in range(tokens_per_head):
                    mask_token(i * tokens_per_head + t)
                rank_pairs(i)
                return 0

            lax.fori_loop(0, PEER_HEADS, trip, 0)
            publish_ranked()

        @pl.when(r == n + 1)
        def _():
            def trip(i, _):
                for t in range(MASK_UNROLL):
                    mask_token(i * MASK_UNROLL + t)
                return 0

            lax.fori_loop(0, tm // MASK_UNROLL, trip, 0)

    def slot_scores():
        d0 = pl.multiple_of(e * PEER_HALF, PEER_HALF)
        return _dot_nt(q_ref[:, pl.ds(d0, PEER_HALF)], keys_ref[e])

    def rank_slot(scores):
        lane = lax.broadcasted_iota(jnp.int32, (tm, N_KEYS), 1)
        picked = jnp.zeros((tm, N_KEYS), jnp.int32)
        a = scores
        for kk in range(PEER_TOPK):
            idx = jnp.argmax(a, axis=-1, keepdims=True).astype(jnp.int32)
            a = jnp.where(lane == idx, neg_inf, a)
            picked = jnp.where(lane == kk, idx, picked)
        vals = jnp.take_along_axis(scores, picked, axis=1)
        sv_scr[e] = vals.T[0:PEER_TOPK, :]
        si_scr[e] = (N_KEYS - 1 - picked).astype(F32).T[0:PEER_TOPK, :]

    def expert_chunk():
        h = h_ref[...]
        row0 = e * rows_per_step
        weighted = []
        for c in range(te // PEER_CHUNK):
            rows = slice(c * PEER_CHUNK, (c + 1) * PEER_CHUNK)
            pre = _dot_nt(h, u_ref[rows, :])
            half_mask = jnp.concatenate(
                [mask_scr[pl.ds(row0 + c * (PEER_CHUNK // N_KEYS) + rr, tm, stride=MASK_PITCH), :]
                 for rr in range(PEER_CHUNK // N_KEYS)], axis=1)
            weighted.append((pre * (1.0 + lax.erf(pre * np.float32(np.sqrt(0.5)))) * half_mask).astype(BF16))
        out_ref[...] += _dot(jnp.concatenate(weighted, axis=1), v_ref[...])

    ranks = r < n
    sweeps = r >= 2

    @pl.when(jnp.logical_and(ranks, jnp.logical_not(sweeps)))
    def _():
        rank_slot(slot_scores())

    @pl.when(jnp.logical_and(ranks, sweeps))
    def _():
        a = slot_scores()
        expert_chunk()
        rank_slot(a)

    @pl.when(jnp.logical_and(jnp.logical_not(ranks), sweeps))
    def _():
        expert_chunk()


def _peer(h, x2, q_route, w, tm, te):
    t, d = h.shape
    n = t // tm
    nexp = w['peer_v'].shape[0]
    assert nexp // te == 2 * PEER_HEADS, "one half-key slot is ranked per expert step"
    assert tm % PEER_HEADS == 0 and tm % MASK_UNROLL == 0
    nsel = PEER_HEADS * PEER_TOPK
    _, flat = _pair_blocks()
    flat = jnp.asarray(np.repeat(flat[:, None], tm, axis=1))
    swept_tile = lambda r, e: (jnp.clip(r - 2, 0, n - 1), 0)
    dense_tile = lambda width: pl.BlockSpec((tm, width), swept_tile, pipeline_mode=pl.Buffered(1))
    expert_tile = pl.BlockSpec((te, d), lambda r, e: (jnp.where(r >= 2, e, 0), 0))
    return pl.pallas_call(
        _peer_kernel,
        grid=(n + 2, nexp // te),
        in_specs=[dense_tile(d), dense_tile(d),
                  pl.BlockSpec((tm, q_route.shape[1]), lambda r, e: (jnp.minimum(r, n - 1), 0),
                               pipeline_mode=pl.Buffered(1)),
                  _const_spec(w['sub_keys'].shape), _const_spec(flat.shape), expert_tile, expert_tile],
        out_specs=pl.BlockSpec((tm, d), swept_tile),
        out_shape=jax.ShapeDtypeStruct((t, d), F32),
        scratch_shapes=[pltpu.VMEM((tm * MASK_PITCH, LANES), F32),
                        pltpu.VMEM((2 * PEER_HEADS, PEER_TOPK, tm), F32),
                        pltpu.VMEM((2 * PEER_HEADS, PEER_TOPK, tm), F32),
                        pltpu.VMEM((PEER_TOPK, tm), F32),
                        pltpu.VMEM((PEER_TOPK, tm), F32),
                        pltpu.VMEM((nsel, tm), F32),
                        pltpu.VMEM((nsel, tm), F32),
                        pltpu.VMEM((nsel, tm), F32),
                        pltpu.VMEM((2, tm, nsel), F32),
                        pltpu.VMEM((2, tm, nsel), F32),
                        pltpu.VMEM((2, tm, nsel), F32)],
        compiler_params=_params(("arbitrary", "arbitrary"), PEER_VMEM_LIMIT),
        name="peer",
    )(h, x2, q_route, w['sub_keys'], flat, w['peer_u'], w['peer_v'])


def _slot_gain(parts):
    row = jnp.zeros((SLOT,), F32)
    for off, g in parts:
        row = lax.dynamic_update_slice(row, g.astype(F32), (off,))
    return row[None, :]


def _group_avg_matrix():
    m = np.zeros((SLOT, SLOT), np.float32)
    m[:QK_NOPE, :QK_NOPE] = 1.0 / QK_NOPE
    m[ROPE_LO:ROPE_LO + QK_ROPE, ROPE_LO:ROPE_LO + QK_ROPE] = 1.0 / QK_ROPE
    return jnp.asarray(np.concatenate([m, m], axis=0), BF16)


def _rope_tables(pos):
    inv = ROPE_THETA ** (-jnp.arange(ROPE_HALF, dtype=F32) / ROPE_HALF)
    ang = pos.astype(F32)[:, None] * inv[None, :]
    cos, sin = jnp.cos(ang), jnp.sin(ang)
    n = pos.shape[0]
    tail = SLOT - ROPE_LO - QK_ROPE
    cos_t = jnp.concatenate([jnp.ones((n, ROPE_LO), F32), cos, cos, jnp.zeros((n, tail), F32)], axis=1)
    sin_up = jnp.concatenate([jnp.zeros((n, ROPE_LO + ROPE_HALF), F32), sin, jnp.zeros((n, tail), F32)], axis=1)
    sin_dn = jnp.concatenate([jnp.zeros((n, ROPE_LO), F32), -sin, jnp.zeros((n, tail + ROPE_HALF), F32)], axis=1)
    return cos_t, sin_up, sin_dn


def _prep_weights(p, i):
    w_in = p['w_in'][i]
    q_lora = p['g_q_lat'].shape[1]
    kv_lora = p['g_kv_lat'].shape[1]
    gw = p['g_gm'].shape[1]
    d = w_in.shape[0]
    o = np.cumsum([0, q_lora, kv_lora, QK_ROPE, gw, gw, d, d])
    seg = lambda k: w_in[:, o[k]:o[k + 1]]
    row = lambda a: a[i][None, :].astype(F32)
    w = {}
    w['g_mix'] = row(p['g_mix'])
    w['w_cq'] = seg(0).astype(BF16)
    w['w_ckv'] = seg(1).astype(BF16)
    w['w_kr'] = jnp.pad(seg(2), ((0, 0), (ROPE_LO, SLOT - ROPE_LO - QK_ROPE))).astype(BF16)
    w['w_u'] = seg(3).astype(BF16)
    w['w_v'] = seg(4).astype(BF16)
    w['w_ga'] = seg(5).astype(BF16)
    w['w_gb'] = seg(6).astype(BF16)
    w['g_q_lat'] = row(p['g_q_lat'])
    w['g_kv_lat'] = row(p['g_kv_lat'])
    w['g_kr'] = _slot_gain([(ROPE_LO, p['g_kr'][i])])
    w['g_gm'] = row(p['g_gm'])
    head_dim = QK_NOPE + QK_ROPE
    w_uq = p['w_uq'][i].reshape(q_lora, MLA_HEADS, head_dim)
    w['w_uq'] = jnp.pad(w_uq, ((0, 0), (0, 0), (0, SLOT - head_dim))).reshape(q_lora, -1).astype(BF16)
    w['g_q'] = _slot_gain([(0, p['g_qn'][i]), (ROPE_LO, p['g_qr'][i])])
    w['m_avg'] = _group_avg_matrix()
    w_uk = p['w_uk'][i].reshape(kv_lora, MLA_HEADS, QK_NOPE)
    w['w_uk'] = jnp.pad(w_uk, ((0, 0), (0, 0), (0, SLOT - QK_NOPE))).reshape(kv_lora, -1).astype(BF16)
    w['g_kn'] = _slot_gain([(0, p['g_kn'][i])])
    w_uv = p['w_uv'][i].reshape(kv_lora, MLA_HEADS // 2, 2, V_HEAD)
    eye2 = jnp.eye(2, dtype=w_uv.dtype)
    w['w_uv'] = jnp.einsum('cpjd,jk->cpjkd', w_uv, eye2).reshape(kv_lora, -1).astype(BF16)
    w['w_uv_plain'] = p['w_uv'][i].astype(BF16)
    place = np.zeros((QK_ROPE, SLOT), np.float32)
    place[np.arange(QK_ROPE), ROPE_LO + np.arange(QK_ROPE)] = 1.0
    w['rope_place'] = jnp.asarray(place, BF16)
    w['w_oa'] = p['w_oa'][i].astype(BF16)
    w['w_ob'] = p['w_ob'][i].astype(BF16)
    w['w_o'] = p['w_o'][i].astype(BF16)
    w['g_xattn'] = row(p['g_xattn'])
    w['g_mem'] = row(p['g_mem'])
    w['w_cq_mem'] = p['w_cq'][i].astype(BF16)
    w['g_cq'] = row(p['g_cq'])
    w['w_ck'] = p['w_ck'][i].astype(BF16)
    w['g_ck'] = row(p['g_ck'])
    w['w_cv'] = p['w_cv'][i].astype(BF16)
    w['w_co'] = p['w_co'][i].astype(BF16)
    w['g_ffn'] = row(p['g_ffn'])
    w['w_pq'] = p['w_pq'][i].astype(BF16)
    w['sub_keys'] = p['sub_keys'][i].reshape(2 * PEER_HEADS, N_KEYS, PEER_HALF)[:, ::-1, :].astype(BF16)
    w['peer_u'] = p['peer_u'][i].astype(BF16)
    w['peer_v'] = p['peer_v'][i].astype(BF16)
    w['w_s'] = p['w_s'][i]
    w['b_s'] = p['b_s'][i]
    return w


def _spatial_operands(w, seq, tm):
    chunk = min(seq, GM_CHUNK)
    reps = tm // chunk
    w_mask = jnp.tril(w['w_s'][:, :chunk, :chunk])
    w_mask = w_mask.astype(BF16)
    mix = jnp.concatenate(
        [jnp.pad(w_mask, ((0, 0), (0, 0), (a * chunk, (reps - 1 - a) * chunk))) for a in range(reps)], axis=1)
    bias = jnp.tile(w['b_s'][:, :chunk].T, (reps, 1))
    bias = jnp.repeat(bias, LANES, axis=1)
    return mix.astype(BF16), bias.astype(F32)


def _token_mixer(x2d, w, seq, pos, past, tm):
    t, d = x2d.shape
    nb = t // seq
    period = max(seq, tm)
    tabs = _rope_tables(jnp.tile(pos, period // seq))
    cq, ckv, kr, u, vg, siga, sigb = _in_proj(x2d, w, tabs, tm)
    mix, bias = _spatial_operands(w, seq, tm)
    gbob = _gmlp(u, vg, sigb, mix, bias, w['w_ob'], tm)
    q = _q_proj(cq, w, tabs, tm)
    if past is None:
        k, v = _kv_proj(ckv, kr, w, tm)
        o = _attn_prompt(q.reshape(nb, seq, -1), k.reshape(nb, seq, -1), v.reshape(nb, seq, -1),
                         _score_bound(w), ATTN_TILE)
    else:
        past_ckv, past_kr = past
        o = _attn_sample(q.reshape(nb, seq, -1), past_ckv, past_kr, ckv.reshape(nb, seq, -1),
                         kr.reshape(nb, seq, SLOT), w)
    x1, qc = _merge(x2d, o.reshape(t, -1), siga, gbob, w, tm)
    return x1, qc, ckv, kr[:, ROPE_LO:ROPE_LO + QK_ROPE], vg


def _tail(x1, qc, mk, mv, w, nb, seq, tm):
    t, d = x1.shape
    oc = _cross(qc.reshape(nb, seq, -1), mk, mv, min(seq, tm))
    x2, h, q_route = _cross_out(x1, oc.reshape(t, -1), w, tm)
    return _peer(h, x2, q_route, w, PEER_ROWS, PEER_EXPERTS)


def kernel(x_prompt, x_sample, cache_mla_ckv, cache_mla_krope, cache_mem_k, cache_mem_v, mem_prompt, g_mix, w_in, g_q_lat, w_uq, g_qn, g_qr, g_kv_lat, g_kr, w_uk, w_uv, g_kn, w_oa, g_gm, w_s, b_s, w_ob, w_o, g_xattn, g_mem, w_cq, g_cq, w_ck, g_ck, w_cv, w_co, g_ffn, w_pq, sub_keys, peer_u, peer_v):
    params = dict(g_mix=g_mix, w_in=w_in, g_q_lat=g_q_lat, w_uq=w_uq, g_qn=g_qn, g_qr=g_qr,
                  g_kv_lat=g_kv_lat, g_kr=g_kr, w_uk=w_uk, w_uv=w_uv, g_kn=g_kn, w_oa=w_oa, g_gm=g_gm,
                  w_s=w_s, b_s=b_s, w_ob=w_ob, w_o=w_o, g_xattn=g_xattn, g_mem=g_mem, w_cq=w_cq,
                  g_cq=g_cq, w_ck=w_ck, g_ck=g_ck, w_cv=w_cv, w_co=w_co, g_ffn=g_ffn, w_pq=w_pq,
                  sub_keys=sub_keys, peer_u=peer_u, peer_v=peer_v)
    bp, sp, d = x_prompt.shape
    bs, ss, _ = x_sample.shape
    depth = w_in.shape[0]
    past_len = cache_mla_ckv.shape[2]
    n_mem = mem_prompt.shape[1]
    tm = ROW_TILE
    assert (bp * sp) % tm == 0 and (bs * ss) % tm == 0 and sp % ATTN_TILE == 0
    pos_p = jnp.arange(sp)
    pos_s = past_len + jnp.arange(ss)
    xp = x_prompt.reshape(bp * sp, d)
    xs = x_sample.reshape(bs * ss, d)
    outs = [[] for _ in range(7)]
    for i in range(depth):
        w = _prep_weights(params, i)
        xp1, qcp, ckv_p, kr_p, _ = _token_mixer(xp, w, sp, pos_p, None, tm)
        xs1, qcs, ckv_s, kr_s, vg_s = _token_mixer(xs, w, ss, pos_s, (cache_mla_ckv[i], cache_mla_krope[i]), tm)
        mk_p, mv_p = _mem_kv(mem_prompt.reshape(bp * n_mem, d), w, min(tm, bp * n_mem))
        xp = _tail(xp1, qcp, mk_p.reshape(bp, n_mem, -1), mv_p.reshape(bp, n_mem, -1), w, bp, sp, tm)
        xs = _tail(xs1, qcs, cache_mem_k[i].reshape(bs, n_mem, -1), cache_mem_v[i].reshape(bs, n_mem, -1),
                   w, bs, ss, tm)
        outs[0].append(ckv_p.reshape(bp, sp, -1))
        outs[1].append(kr_p.reshape(bp, sp, -1))
        outs[2].append(mk_p.reshape(bp, n_mem, MEM_HEADS, MEM_HEAD_DIM))
        outs[3].append(mv_p.reshape(bp, n_mem, MEM_HEADS, MEM_HEAD_DIM))
        outs[4].append(ckv_s.reshape(bs, ss, -1))
        outs[5].append(kr_s.reshape(bs, ss, -1))
        outs[6].append(vg_s.reshape(bs, ss, -1))
    return (xp.reshape(bp, sp, d), xs.reshape(bs, ss, d)) + tuple(jnp.stack(o) for o in outs)
```

```python
import functools

import jax
import jax.numpy as jnp
import numpy as np
from jax import lax
from jax.experimental import pallas as pl
from jax.experimental.pallas import tpu as pltpu

CHUNK = 64
EPS = 1e-6
MLA_HEADS = 16
QK_NOPE = 64
QK_ROPE = 32
V_HEAD = 64
ROPE_THETA = 10000.0
MLA_SCALE = (QK_NOPE + QK_ROPE) ** -0.5
GM_CHUNK = 128
GM_GROUPS = 8
MEM_HEADS = 4
MEM_HEAD_DIM = 128
MEM_SCALE = MEM_HEAD_DIM ** -0.5
PEER_HEADS = 8
N_KEYS = 128
PEER_TOPK = 16
PEER_HALF = 128

LANES = 128
SLOT = LANES
ROPE_LO = QK_NOPE
ROPE_HALF = QK_ROPE // 2
ROW_TILE = 512
ATTN_TILE = 512
LOG2_E = float(np.log2(np.e))
STATIC_SHIFT_LIMIT = 48.0
BOUND_MARGIN = 1.02
PEER_ROWS = 512
PEER_EXPERTS = 1024
PEER_CHUNK = 512
MASK_PITCH = N_KEYS + 8
MASK_UNROLL = 32
VMEM_LIMIT = 48 * 1024 * 1024
PEER_VMEM_LIMIT = 60 * 1024 * 1024

F32 = jnp.float32
BF16 = jnp.bfloat16
_NT = (((1,), (1,)), ((), ()))


def _dot(a, b):
    return jnp.dot(a, b, preferred_element_type=F32)


def _dot_nt(a, b):
    return lax.dot_general(a, b, _NT, preferred_element_type=F32)


def _rms(xf, g):
    return xf * lax.rsqrt(jnp.mean(xf * xf, axis=-1, keepdims=True) + EPS) * g


def _gelu(x):
    return 0.5 * x * (1.0 + lax.erf(x * np.float32(np.sqrt(0.5))))


def _group_mean(sq, m_ref):
    hi = sq.astype(BF16)
    lo = (sq - hi.astype(F32)).astype(BF16)
    return _dot(jnp.concatenate([hi, lo], axis=1), m_ref[...])


def _rope_slot(y, cos, sin_up, sin_dn):
    return (y * cos + pltpu.roll(y, ROPE_HALF, 1) * sin_up
            + pltpu.roll(y, SLOT - ROPE_HALF, 1) * sin_dn)


def _const_spec(shape):
    nd = len(shape)
    return pl.BlockSpec(shape, lambda *_: (0,) * nd, pipeline_mode=pl.Buffered(1))


def _row_spec(tm, width):
    return pl.BlockSpec((tm, width), lambda i: (i, 0))


def _params(sem, limit=VMEM_LIMIT):
    return pltpu.CompilerParams(dimension_semantics=sem, vmem_limit_bytes=limit)


def _in_proj_kernel(x_ref, gmix_ref, wcq_ref, wckv_ref, wkr_ref, wu_ref, wv_ref, wga_ref, wgb_ref,
                    gq_ref, gkv_ref, gkr_ref, ggm_ref, cos_ref, sup_ref, sdn_ref,
                    cq_out, ckv_out, kr_out, u_out, vg_out, siga_out, sigb_out):
    h = _rms(x_ref[...], gmix_ref[...]).astype(BF16)
    cq_out[...] = _rms(_dot(h, wcq_ref[...]), gq_ref[...]).astype(BF16)
    ckv_out[...] = _rms(_dot(h, wckv_ref[...]), gkv_ref[...])
    kr = _dot(h, wkr_ref[...])
    ms = jnp.sum(kr * kr, axis=-1, keepdims=True) * (1.0 / QK_ROPE)
    kr = kr * lax.rsqrt(ms + EPS) * gkr_ref[...]
    kr_out[...] = _rope_slot(kr, cos_ref[...], sup_ref[...], sdn_ref[...])
    u_out[...] = _gelu(_dot(h, wu_ref[...])).astype(BF16)
    vg_out[...] = _rms(_gelu(_dot(h, wv_ref[...])), ggm_ref[...])
    siga_out[...] = jax.nn.sigmoid(_dot(h, wga_ref[...])).astype(BF16)
    sigb_out[...] = jax.nn.sigmoid(_dot(h, wgb_ref[...])).astype(BF16)


def _in_proj(x2d, w, tabs, tm):
    t, d = x2d.shape
    cos, sup, sdn = tabs
    ntab = cos.shape[0] // tm
    tab_spec = pl.BlockSpec((tm, SLOT), lambda i: (i % ntab, 0))
    consts = [w['g_mix'], w['w_cq'], w['w_ckv'], w['w_kr'], w['w_u'], w['w_v'], w['w_ga'], w['w_gb'],
              w['g_q_lat'], w['g_kv_lat'], w['g_kr'], w['g_gm']]
    gw = w['w_u'].shape[1]
    widths = [w['w_cq'].shape[1], w['w_ckv'].shape[1], SLOT, gw, gw, d, d]
    dtypes = [BF16, F32, F32, BF16, F32, BF16, BF16]
    return pl.pallas_call(
        _in_proj_kernel,
        grid=(t // tm,),
        in_specs=[_row_spec(tm, d)] + [_const_spec(c.shape) for c in consts] + [tab_spec] * 3,
        out_specs=[_row_spec(tm, n) for n in widths],
        out_shape=[jax.ShapeDtypeStruct((t, n), dt) for n, dt in zip(widths, dtypes)],
        compiler_params=_params(("parallel",)),
        name="in_proj",
    )(x2d, *consts, cos, sup, sdn)


def _gmlp_kernel(u_ref, vg_ref, sigb_ref, mix_ref, bias_ref, wob_ref, out_ref):
    vgb = vg_ref[...].astype(BF16)
    parts = []
    for g in range(GM_GROUPS):
        sl = slice(g * LANES, (g + 1) * LANES)
        mixed = _dot(mix_ref[g], vgb[:, sl]) + bias_ref[:, sl]
        parts.append((u_ref[:, sl].astype(F32) * mixed).astype(BF16))
    ob = _dot(jnp.concatenate(parts, axis=1), wob_ref[...])
    out_ref[...] = sigb_ref[...].astype(F32) * ob


def _gmlp(u, vg, sigb, mix, bias, w_ob, tm):
    t, gw = u.shape
    d = w_ob.shape[1]
    return pl.pallas_call(
        _gmlp_kernel,
        grid=(t // tm,),
        in_specs=[_row_spec(tm, gw), _row_spec(tm, gw), _row_spec(tm, d),
                  _const_spec(mix.shape), _const_spec(bias.shape), _const_spec(w_ob.shape)],
        out_specs=_row_spec(tm, d),
        out_shape=jax.ShapeDtypeStruct((t, d), F32),
        compiler_params=_params(("parallel",)),
        name="gmlp",
    )(u, vg, sigb, mix, bias, w_ob)


def _q_proj_kernel(cq_ref, wuq_ref, gq_ref, mavg_ref, cos_ref, sup_ref, sdn_ref, q_out):
    q = _dot(cq_ref[...], wuq_ref[...])
    cos, sup, sdn = cos_ref[...], sup_ref[...], sdn_ref[...]
    for h in range(MLA_HEADS):
        sl = slice(h * SLOT, (h + 1) * SLOT)
        qs = q[:, sl]
        y = qs * lax.rsqrt(_group_mean(qs * qs, mavg_ref) + EPS) * gq_ref[...]
        q_out[:, sl] = _rope_slot(y, cos, sup, sdn).astype(BF16)


def _q_proj(cq, w, tabs, tm):
    t, ql = cq.shape
    cos, sup, sdn = tabs
    ntab = cos.shape[0] // tm
    tab_spec = pl.BlockSpec((tm, SLOT), lambda i: (i % ntab, 0))
    n = MLA_HEADS * SLOT
    return pl.pallas_call(
        _q_proj_kernel,
        grid=(t // tm,),
        in_specs=[_row_spec(tm, ql), _const_spec(w['w_uq'].shape), _const_spec(w['g_q'].shape),
                  _const_spec(w['m_avg'].shape)] + [tab_spec] * 3,
        out_specs=_row_spec(tm, n),
        out_shape=jax.ShapeDtypeStruct((t, n), BF16),
        compiler_params=_params(("parallel",)),
        name="q_proj",
    )(cq, w['w_uq'], w['g_q'], w['m_avg'], cos, sup, sdn)


def _kv_proj_kernel(ckv_ref, kr_ref, wuk_ref, wuv_ref, gkn_ref, mavg_ref, k_out, v_out):
    c = ckv_ref[...].astype(BF16)
    kn = _dot(c, wuk_ref[...])
    kr = kr_ref[...]
    for h in range(MLA_HEADS):
        sl = slice(h * SLOT, (h + 1) * SLOT)
        ks = kn[:, sl]
        y = ks * lax.rsqrt(_group_mean(ks * ks, mavg_ref) + EPS) * gkn_ref[...]
        k_out[:, sl] = (y + kr).astype(BF16)
    v_out[...] = _dot(c, wuv_ref[...]).astype(BF16)


def _kv_proj(ckv, kr, w, tm):
    t, kl = ckv.shape
    n = MLA_HEADS * SLOT
    return pl.pallas_call(
        _kv_proj_kernel,
        grid=(t // tm,),
        in_specs=[_row_spec(tm, kl), _row_spec(tm, SLOT), _const_spec(w['w_uk'].shape),
                  _const_spec(w['w_uv'].shape), _const_spec(w['g_kn'].shape),
                  _const_spec(w['m_avg'].shape)],
        out_specs=[_row_spec(tm, n), _row_spec(tm, n)],
        out_shape=[jax.ShapeDtypeStruct((t, n), BF16)] * 2,
        compiler_params=_params(("parallel",)),
        name="kv_proj",
    )(ckv, kr, w['w_uk'], w['w_uv'], w['g_kn'], w['m_avg'])


def _attn_prompt_kernel(bound_ref, q_ref, k_ref, v_ref, o_ref, *, tile):
    n_tiles = q_ref.shape[1] // tile
    bound = bound_ref[0]
    first = lax.broadcasted_iota(jnp.int32, (1, LANES), 1) < V_HEAD
    row_chunk = lax.broadcasted_iota(jnp.int32, (tile, tile), 0) // CHUNK
    col_chunk = lax.broadcasted_iota(jnp.int32, (tile, tile), 1) // CHUNK
    visible = col_chunk <= row_chunk

    def static_step(q, kstart, carry, masked):
        sums, acc = carry
        kstart = pl.multiple_of(kstart, tile)
        k = k_ref[0, pl.ds(kstart, tile), :]
        v = v_ref[0, pl.ds(kstart, tile), :]
        new_sums = []
        for h in range(2):
            sl = slice(h * SLOT, (h + 1) * SLOT)
            s = _dot_nt(q[:, sl], k[:, sl]) * F32(MLA_SCALE * LOG2_E) - bound
            if masked:
                s = jnp.where(visible, s, F32(-1e30))
            p = jnp.exp2(s)
            part = sums[h]
            for c in range(tile // LANES):
                part = part + p[:, c * LANES:(c + 1) * LANES]
            new_sums.append(part)
            acc = acc + _dot(p.astype(BF16), v[:, sl])
        return tuple(new_sums), acc

    def static_tile(qi, _):
        rows = pl.ds(pl.multiple_of(qi * tile, tile), tile)
        q = q_ref[0, rows, :]
        zeros = jnp.zeros((tile, LANES), F32)

        def pair(i, c):
            c = static_step(q, 2 * i * tile, c, False)
            return static_step(q, (2 * i + 1) * tile, c, False)

        def odd_tail(c):
            return static_step(q, qi * tile, static_step(q, (qi - 1) * tile, c, False), True)

        carry = lax.fori_loop(0, qi // 2, pair, ((zeros, zeros), zeros))
        sums, acc = lax.cond(qi % 2 == 1, odd_tail, lambda c: static_step(q, qi * tile, c, True), carry)
        l0 = jnp.sum(sums[0], axis=-1, keepdims=True)
        l1 = jnp.sum(sums[1], axis=-1, keepdims=True)
        o_ref[0, rows, :] = (acc / jnp.where(first, l0, l1)).astype(BF16)
        return 0

    @pl.when(bound < STATIC_SHIFT_LIMIT)
    def _():
        lax.fori_loop(0, n_tiles, static_tile, 0)

    def step(q, kstart, carry, masked):
        ms, ls, acc = carry
        kstart = pl.multiple_of(kstart, tile)
        k = k_ref[0, pl.ds(kstart, tile), :]
        v = v_ref[0, pl.ds(kstart, tile), :]
        new_ms, new_ls, alphas, pvs = [], [], [], []
        for h in range(2):
            sl = slice(h * SLOT, (h + 1) * SLOT)
            s = _dot_nt(q[:, sl], k[:, sl]) * F32(MLA_SCALE * LOG2_E)
            if masked:
                s = jnp.where(visible, s, F32(-1e30))
            m_new = jnp.maximum(ms[h], jnp.max(s, axis=-1, keepdims=True))
            alpha = jnp.exp2(ms[h] - m_new)
            p = jnp.exp2(s - m_new)
            new_ls.append(alpha * ls[h] + jnp.sum(p, axis=-1, keepdims=True))
            new_ms.append(m_new)
            alphas.append(alpha)
            pvs.append(_dot(p.astype(BF16), v[:, sl]))
        acc = acc * jnp.where(first, alphas[0], alphas[1]) + pvs[0] + pvs[1]
        return tuple(new_ms), tuple(new_ls), acc

    def online_tile(qi, _):
        rows = pl.ds(pl.multiple_of(qi * tile, tile), tile)
        q = q_ref[0, rows, :]
        neg = jnp.full((tile, 1), -1e30, F32)
        zero = jnp.zeros((tile, 1), F32)
        init = ((neg, neg), (zero, zero), jnp.zeros((tile, LANES), F32))
        carry = lax.fori_loop(0, qi, lambda i, c: step(q, i * tile, c, False), init)
        _, ls, acc = step(q, qi * tile, carry, True)
        o_ref[0, rows, :] = (acc / jnp.where(first, ls[0], ls[1])).astype(BF16)
        return 0

    @pl.when(bound >= STATIC_SHIFT_LIMIT)
    def _():
        lax.fori_loop(0, n_tiles, online_tile, 0)


def _score_bound(w):
    gmax = lambda g, lo, n: jnp.max(jnp.abs(g[0, lo:lo + n]))
    nope = QK_NOPE * gmax(w['g_q'], 0, QK_NOPE) * gmax(w['g_kn'], 0, QK_NOPE)
    rope = QK_ROPE * gmax(w['g_q'], ROPE_LO, QK_ROPE) * gmax(w['g_kr'], ROPE_LO, QK_ROPE)
    return (BOUND_MARGIN * MLA_SCALE * LOG2_E * (nope + rope)).reshape(1).astype(F32)


def _attn_prompt(q, k, v, bound, tile):
    b, s, _ = q.shape
    pairs = MLA_HEADS // 2
    pair_block = pl.BlockSpec((1, s, 2 * SLOT), lambda bi, p: (bi, 0, p))
    return pl.pallas_call(
        functools.partial(_attn_prompt_kernel, tile=tile),
        grid=(b, pairs),
        in_specs=[pl.BlockSpec(memory_space=pltpu.SMEM), pair_block, pair_block, pair_block],
        out_specs=pl.BlockSpec((1, s, LANES), lambda bi, p: (bi, 0, p)),
        out_shape=jax.ShapeDtypeStruct((b, s, pairs * LANES), BF16),
        compiler_params=_params(("parallel", "parallel")),
        name="attn_prompt",
    )(bound, q, k, v)


def _attn_sample_kernel(q_ref, pckv_ref, pkr_ref, nckv_ref, nkr_ref, wuk_ref, wuv_ref, gkn_ref, place_ref,
                        o_ref):
    sq = q_ref.shape[1]
    rows = MLA_HEADS * sq
    gkn = gkn_ref[...]

    def keys_values(ckv, kr_slot):
        c = ckv.astype(BF16)
        kn = _dot(c, wuk_ref[...])
        parts = []
        for h in range(MLA_HEADS):
            ks = kn[:, h * SLOT:(h + 1) * SLOT]
            ms = jnp.sum(ks * ks, axis=-1, keepdims=True) * (1.0 / QK_NOPE)
            parts.append((ks * lax.rsqrt(ms + EPS) * gkn + kr_slot).astype(BF16))
        return jnp.concatenate(parts, axis=1), _dot(c, wuv_ref[...]).astype(BF16)

    past_kr = _dot(pkr_ref[0].astype(BF16), place_ref[...])
    k_p, v_p = keys_values(pckv_ref[0], past_kr)
    k_n, v_n = keys_values(nckv_ref[0], nkr_ref[0])
    k = jnp.concatenate([k_p, k_n], axis=0)
    v = jnp.concatenate([v_p, v_n], axis=0)

    qt = jnp.concatenate([q_ref[0].astype(F32)] * MLA_HEADS, axis=0)
    q_shape = (rows, MLA_HEADS * SLOT)
    own = (lax.broadcasted_iota(jnp.int32, q_shape, 0) // sq
           == lax.broadcasted_iota(jnp.int32, q_shape, 1) // SLOT)
    qbd = jnp.where(own, qt, 0.0).astype(BF16)
    s = _dot_nt(qbd, k) * MLA_SCALE
    p = jnp.exp(s - jnp.max(s, axis=-1, keepdims=True))
    p = p / jnp.sum(p, axis=-1, keepdims=True)
    o_all = _dot(p.astype(BF16), v)
    o_shape = o_all.shape
    own = (lax.broadcasted_iota(jnp.int32, o_shape, 0) // sq
           == lax.broadcasted_iota(jnp.int32, o_shape, 1) // V_HEAD)
    o_all = jnp.where(own, o_all, 0.0)
    o = o_all[0:sq]
    for h in range(1, MLA_HEADS):
        o = o + o_all[h * sq:(h + 1) * sq]
    o_ref[0] = o.astype(BF16)


def _attn_sample(q, past_ckv, past_kr, new_ckv, new_kr, w):
    b, sq, nq = q.shape
    npast, kl = past_ckv.shape[1:]
    nv = w['w_uv_plain'].shape[1]
    stream = lambda n, width: pl.BlockSpec((1, n, width), lambda bi: (bi, 0, 0))
    consts = [w['w_uk'], w['w_uv_plain'], w['g_kn'], w['rope_place']]
    return pl.pallas_call(
        _attn_sample_kernel,
        grid=(b,),
        in_specs=[stream(sq, nq), stream(npast, kl), stream(npast, QK_ROPE), stream(sq, kl), stream(sq, SLOT)]
        + [_const_spec(c.shape) for c in consts],
        out_specs=stream(sq, nv),
        out_shape=jax.ShapeDtypeStruct((b, sq, nv), BF16),
        compiler_params=_params(("parallel",)),
        name="attn_sample",
    )(q, past_ckv, past_kr, new_ckv, new_kr, *consts)


def _merge_kernel(x_ref, o_ref, siga_ref, gbob_ref, woa_ref, wo_ref, gx_ref, wcq_ref, gcq_ref,
                  x1_out, qc_out):
    oa = _dot(o_ref[...], woa_ref[...])
    merged = siga_ref[...].astype(F32) * oa + gbob_ref[...]
    x1 = x_ref[...] + _dot(merged.astype(BF16), wo_ref[...])
    x1_out[...] = x1
    qc = _dot(_rms(x1, gx_ref[...]).astype(BF16), wcq_ref[...])
    for h in range(MEM_HEADS):
        sl = slice(h * MEM_HEAD_DIM, (h + 1) * MEM_HEAD_DIM)
        qc_out[:, sl] = _rms(qc[:, sl], gcq_ref[...]).astype(BF16)


def _merge(x2d, o, siga, gbob, w, tm):
    t, d = x2d.shape
    consts = [w['w_oa'], w['w_o'], w['g_xattn'], w['w_cq_mem'], w['g_cq']]
    nq = w['w_cq_mem'].shape[1]
    return pl.pallas_call(
        _merge_kernel,
        grid=(t // tm,),
        in_specs=[_row_spec(tm, d), _row_spec(tm, o.shape[1]), _row_spec(tm, d), _row_spec(tm, d)]
        + [_const_spec(c.shape) for c in consts],
        out_specs=[_row_spec(tm, d), _row_spec(tm, nq)],
        out_shape=[jax.ShapeDtypeStruct((t, d), F32), jax.ShapeDtypeStruct((t, nq), BF16)],
        compiler_params=_params(("parallel",)),
        name="merge",
    )(x2d, o, siga, gbob, *consts)


def _mem_kv_kernel(mem_ref, gmem_ref, wck_ref, gck_ref, wcv_ref, k_out, v_out):
    m = _rms(mem_ref[...], gmem_ref[...]).astype(BF16)
    k = _dot(m, wck_ref[...])
    for h in range(MEM_HEADS):
        sl = slice(h * MEM_HEAD_DIM, (h + 1) * MEM_HEAD_DIM)
        k_out[:, sl] = _rms(k[:, sl], gck_ref[...])
    v_out[...] = _dot(m, wcv_ref[...])


def _mem_kv(mem2d, w, tm):
    t, d = mem2d.shape
    consts = [w['g_mem'], w['w_ck'], w['g_ck'], w['w_cv']]
    n = w['w_ck'].shape[1]
    return pl.pallas_call(
        _mem_kv_kernel,
        grid=(t // tm,),
        in_specs=[_row_spec(tm, d)] + [_const_spec(c.shape) for c in consts],
        out_specs=[_row_spec(tm, n)] * 2,
        out_shape=[jax.ShapeDtypeStruct((t, n), F32)] * 2,
        compiler_params=_params(("parallel",)),
        name="mem_kv",
    )(mem2d, *consts)


def _cross_kernel(q_ref, k_ref, v_ref, o_ref):
    q = q_ref[0]
    k = k_ref[0].astype(BF16)
    v = v_ref[0].astype(BF16)
    for h in range(MEM_HEADS):
        sl = slice(h * MEM_HEAD_DIM, (h + 1) * MEM_HEAD_DIM)
        s = _dot_nt(q[:, sl], k[:, sl]) * MEM_SCALE
        p = jnp.exp(s - jnp.max(s, axis=-1, keepdims=True))
        p = p / jnp.sum(p, axis=-1, keepdims=True)
        o_ref[0, :, sl] = _dot(p.astype(BF16), v[:, sl]).astype(BF16)


def _cross(qc, mk, mv, tc):
    b, s, n = qc.shape
    nm = mk.shape[1]
    return pl.pallas_call(
        _cross_kernel,
        grid=(b, s // tc),
        in_specs=[pl.BlockSpec((1, tc, n), lambda bi, i: (bi, i, 0)),
                  pl.BlockSpec((1, nm, n), lambda bi, i: (bi, 0, 0)),
                  pl.BlockSpec((1, nm, n), lambda bi, i: (bi, 0, 0))],
        out_specs=pl.BlockSpec((1, tc, n), lambda bi, i: (bi, i, 0)),
        out_shape=jax.ShapeDtypeStruct((b, s, n), BF16),
        compiler_params=_params(("parallel", "parallel")),
        name="cross_attn",
    )(qc, mk, mv)


def _cross_out_kernel(x1_ref, oc_ref, wco_ref, gffn_ref, wpq_ref, x2_out, h_out, q_out):
    x2 = x1_ref[...] + _dot(oc_ref[...], wco_ref[...])
    x2_out[...] = x2
    h = _rms(x2, gffn_ref[...]).astype(BF16)
    h_out[...] = h
    q_out[...] = _dot(h, wpq_ref[...]).astype(BF16)


def _cross_out(x1, oc, w, tm):
    t, d = x1.shape
    nq = w['w_pq'].shape[1]
    return pl.pallas_call(
        _cross_out_kernel,
        grid=(t // tm,),
        in_specs=[_row_spec(tm, d), _row_spec(tm, oc.shape[1]), _const_spec(w['w_co'].shape),
                  _const_spec(w['g_ffn'].shape), _const_spec(w['w_pq'].shape)],
        out_specs=[_row_spec(tm, d), _row_spec(tm, d), _row_spec(tm, nq)],
        out_shape=[jax.ShapeDtypeStruct((t, d), F32), jax.ShapeDtypeStruct((t, d), BF16),
                   jax.ShapeDtypeStruct((t, nq), BF16)],
        compiler_params=_params(("parallel",)),
        name="cross_out",
    )(x1, oc, w['w_co'], w['g_ffn'], w['w_pq'])


def _pair_blocks():
    k = PEER_TOPK
    blocks, flat = [], []
    n_row = sum(1 for a in range(k) if k // (a + 1) > 2)
    for a in range(n_row):
        nb = k if a == 0 else 8
        blocks.append(('row', a, nb))
        flat += [a * k + b if (a + 1) * (b + 1) <= k else -1 for b in range(nb)]
    for b in range(k // (n_row + 1)):
        na = k if b == 0 else 8
        blocks.append(('col', b, na))
        flat += [a * k + b if (a >= n_row and (a + 1) * (b + 1) <= k) else -1 for a in range(na)]
    return blocks, np.asarray(flat, np.float32)


def _pair_candidates(v0, v1, blocks, combine):
    parts = []
    for kind, fixed, n in blocks:
        if kind == 'row':
            parts.append(combine(v0[fixed:fixed + 1, :], v1[0:n, :]))
        else:
            parts.append(combine(v0[0:n, :], v1[fixed:fixed + 1, :]))
    return jnp.concatenate(parts, axis=0)


def _peer_kernel(h_ref, x_ref, q_ref, keys_ref, flat_ref, u_ref, v_ref, out_ref,
                 mask_scr, sv_scr, si_scr, ts_scr, te_scr, it_scr, jt_scr, gt_scr,
                 isel_scr, jsel_scr, gsel_scr):
    tm = h_ref.shape[0]
    te = v_ref.shape[0]
    r = pl.program_id(0)
    e = pl.program_id(1)
    n = pl.num_programs(0) - 2
    rows_per_step = te // N_KEYS
    tokens_per_head = tm // PEER_HEADS
    blocks, _ = _pair_blocks()
    neg_inf = F32(-jnp.inf)
    key_rows = lax.broadcasted_iota(jnp.int32, (N_KEYS, LANES), 0).astype(F32)
    swept = lax.rem(r, 2)
    ranked = 1 - swept

    def mask_token(t):
        irow = isel_scr[swept, pl.ds(t, 1), :]
        jrow = jsel_scr[swept, pl.ds(t, 1), :]
        grow = gsel_scr[swept, pl.ds(t, 1), :]
        rw = jnp.where(key_rows == irow, 0.5 * grow, 0.0).astype(BF16)
        cw = jnp.where(key_rows == jrow, 1.0, 0.0).astype(BF16)
        r0 = pl.multiple_of(t * MASK_PITCH, 8)
        mask_scr[pl.ds(r0, N_KEYS), :] = _dot_nt(rw, cw)

    def rank_pairs(hd):
        sv0, sv1 = sv_scr[2 * hd], sv_scr[2 * hd + 1]
        si0, si1 = si_scr[2 * hd], si_scr[2 * hd + 1]
        flat = flat_ref[...]
        cand = _pair_candidates(sv0, sv1, blocks, lambda x, y: x + y)
        cand = jnp.where(flat >= 0.0, cand, neg_inf)
        cidx = _pair_candidates(si0, si1, blocks, lambda x, y: x * F32(N_KEYS) + y)
        for kk in range(PEER_TOPK):
            m = jnp.max(cand, axis=0, keepdims=True)
            pos = jnp.min(jnp.where(cand == m, flat, F32(PEER_TOPK * PEER_TOPK)), axis=0, keepdims=True)
            sel = flat == pos
            ts_scr[kk:kk + 1, :] = m
            te_scr[kk:kk + 1, :] = jnp.max(jnp.where(sel, cidx, F32(-1.0)), axis=0, keepdims=True)
            cand = jnp.where(sel, neg_inf, cand)
        top = ts_scr[...]
        ex = jnp.exp(top - top[0:1, :])
        gate = ex / jnp.sum(ex, axis=0, keepdims=True)
        eidx = te_scr[...]
        irow = jnp.floor(eidx * F32(1.0 / N_KEYS))
        r0 = pl.multiple_of(hd * PEER_TOPK, PEER_TOPK)
        it_scr[pl.ds(r0, PEER_TOPK), :] = irow
        jt_scr[pl.ds(r0, PEER_TOPK), :] = eidx - irow * F32(N_KEYS)
        gt_scr[pl.ds(r0, PEER_TOPK), :] = gate

    def publish_ranked():
        isel_scr[ranked] = it_scr[...].T
        jsel_scr[ranked] = jt_scr[...].T
        gsel_scr[ranked] = gt_scr[...].T

    @pl.when(e == 0)
    def _():
        out_ref[...] = x_ref[...]

        @pl.when(r == 1)
        def _():
            def head(hd, _):
                rank_pairs(hd)
                return 0

            lax.fori_loop(0, PEER_HEADS, head, 0)
            publish_ranked()

        @pl.when(jnp.logical_and(r >= 2, r <= n))
        def _():
            def trip(i, _):
                for t in range(tokens_per_head):
                    mask_token(i * tokens_per_head + t)
                rank_pairs(i)
                return 0

            lax.fori_loop(0, PEER_HEADS, trip, 0)
            publish_ranked()

        @pl.when(r == n + 1)
        def _():
            def trip(i, _):
                for t in range(MASK_UNROLL):
                    mask_token(i * MASK_UNROLL + t)
                return 0

            lax.fori_loop(0, tm // MASK_UNROLL, trip, 0)

    def slot_scores():
        d0 = pl.multiple_of(e * PEER_HALF, PEER_HALF)
        return _dot_nt(q_ref[:, pl.ds(d0, PEER_HALF)], keys_ref[e])

    def rank_slot(scores):
        lane = lax.broadcasted_iota(jnp.int32, (tm, N_KEYS), 1)
        picked = jnp.zeros((tm, N_KEYS), jnp.int32)
        a = scores
        for kk in range(PEER_TOPK):
            idx = jnp.argmax(a, axis=-1, keepdims=True).astype(jnp.int32)
            a = jnp.where(lane == idx, neg_inf, a)
            picked = jnp.where(lane == kk, idx, picked)
        vals = jnp.take_along_axis(scores, picked, axis=1)
        sv_scr[e] = vals.T[0:PEER_TOPK, :]
        si_scr[e] = (N_KEYS - 1 - picked).astype(F32).T[0:PEER_TOPK, :]

    def expert_chunk():
        h = h_ref[...]
        row0 = e * rows_per_step
        weighted = []
        for c in range(te // PEER_CHUNK):
            rows = slice(c * PEER_CHUNK, (c + 1) * PEER_CHUNK)
            pre = _dot_nt(h, u_ref[rows, :])
            half_mask = jnp.concatenate(
                [mask_scr[pl.ds(row0 + c * (PEER_CHUNK // N_KEYS) + rr, tm, stride=MASK_PITCH), :]
                 for rr in range(PEER_CHUNK // N_KEYS)], axis=1)
            weighted.append((pre * (1.0 + lax.erf(pre * np.float32(np.sqrt(0.5)))) * half_mask).astype(BF16))
        out_ref[...] += _dot(jnp.concatenate(weighted, axis=1), v_ref[...])

    ranks = r < n
    sweeps = r >= 2

    @pl.when(jnp.logical_and(ranks, jnp.logical_not(sweeps)))
    def _():
        rank_slot(slot_scores())

    @pl.when(jnp.logical_and(ranks, sweeps))
    def _():
        a = slot_scores()
        expert_chunk()
        rank_slot(a)

    @pl.when(jnp.logical_and(jnp.logical_not(ranks), sweeps))
    def _():
        expert_chunk()


def _peer(h, x2, q_route, w, tm, te):
    t, d = h.shape
    n = t // tm
    nexp = w['peer_v'].shape[0]
    assert nexp // te == 2 * PEER_HEADS, "one half-key slot is ranked per expert step"
    assert tm % PEER_HEADS == 0 and tm % MASK_UNROLL == 0
    nsel = PEER_HEADS * PEER_TOPK
    _, flat = _pair_blocks()
    flat = jnp.asarray(np.repeat(flat[:, None], tm, axis=1))
    swept_tile = lambda r, e: (jnp.clip(r - 2, 0, n - 1), 0)
    dense_tile = lambda width: pl.BlockSpec((tm, width), swept_tile, pipeline_mode=pl.Buffered(1))
    expert_tile = pl.BlockSpec((te, d), lambda r, e: (jnp.where(r >= 2, e, 0), 0))
    return pl.pallas_call(
        _peer_kernel,
        grid=(n + 2, nexp // te),
        in_specs=[dense_tile(d), dense_tile(d),
                  pl.BlockSpec((tm, q_route.shape[1]), lambda r, e: (jnp.minimum(r, n - 1), 0),
                               pipeline_mode=pl.Buffered(1)),
                  _const_spec(w['sub_keys'].shape), _const_spec(flat.shape), expert_tile, expert_tile],
        out_specs=pl.BlockSpec((tm, d), swept_tile),
        out_shape=jax.ShapeDtypeStruct((t, d), F32),
        scratch_shapes=[pltpu.VMEM((tm * MASK_PITCH, LANES), F32),
                        pltpu.VMEM((2 * PEER_HEADS, PEER_TOPK, tm), F32),
                        pltpu.VMEM((2 * PEER_HEADS, PEER_TOPK, tm), F32),
                        pltpu.VMEM((PEER_TOPK, tm), F32),
                        pltpu.VMEM((PEER_TOPK, tm), F32),
                        pltpu.VMEM((nsel, tm), F32),
                        pltpu.VMEM((nsel, tm), F32),
                        pltpu.VMEM((nsel, tm), F32),
                        pltpu.VMEM((2, tm, nsel), F32),
                        pltpu.VMEM((2, tm, nsel), F32),
                        pltpu.VMEM((2, tm, nsel), F32)],
        compiler_params=_params(("arbitrary", "arbitrary"), PEER_VMEM_LIMIT),
        name="peer",
    )(h, x2, q_route, w['sub_keys'], flat, w['peer_u'], w['peer_v'])


def _slot_gain(parts):
    row = jnp.zeros((SLOT,), F32)
    for off, g in parts:
        row = lax.dynamic_update_slice(row, g.astype(F32), (off,))
    return row[None, :]


def _group_avg_matrix():
    m = np.zeros((SLOT, SLOT), np.float32)
    m[:QK_NOPE, :QK_NOPE] = 1.0 / QK_NOPE
    m[ROPE_LO:ROPE_LO + QK_ROPE, ROPE_LO:ROPE_LO + QK_ROPE] = 1.0 / QK_ROPE
    return jnp.asarray(np.concatenate([m, m], axis=0), BF16)


def _rope_tables(pos):
    inv = ROPE_THETA ** (-jnp.arange(ROPE_HALF, dtype=F32) / ROPE_HALF)
    ang = pos.astype(F32)[:, None] * inv[None, :]
    cos, sin = jnp.cos(ang), jnp.sin(ang)
    n = pos.shape[0]
    tail = SLOT - ROPE_LO - QK_ROPE
    cos_t = jnp.concatenate([jnp.ones((n, ROPE_LO), F32), cos, cos, jnp.zeros((n, tail), F32)], axis=1)
    sin_up = jnp.concatenate([jnp.zeros((n, ROPE_LO + ROPE_HALF), F32), sin, jnp.zeros((n, tail), F32)], axis=1)
    sin_dn = jnp.concatenate([jnp.zeros((n, ROPE_LO), F32), -sin, jnp.zeros((n, tail + ROPE_HALF), F32)], axis=1)
    return cos_t, sin_up, sin_dn


def _prep_weights(p, i):
    w_in = p['w_in'][i]
    q_lora = p['g_q_lat'].shape[1]
    kv_lora = p['g_kv_lat'].shape[1]
    gw = p['g_gm'].shape[1]
    d = w_in.shape[0]
    o = np.cumsum([0, q_lora, kv_lora, QK_ROPE, gw, gw, d, d])
    seg = lambda k: w_in[:, o[k]:o[k + 1]]
    row = lambda a: a[i][None, :].astype(F32)
    w = {}
    w['g_mix'] = row(p['g_mix'])
    w['w_cq'] = seg(0).astype(BF16)
    w['w_ckv'] = seg(1).astype(BF16)
    w['w_kr'] = jnp.pad(seg(2), ((0, 0), (ROPE_LO, SLOT - ROPE_LO - QK_ROPE))).astype(BF16)
    w['w_u'] = seg(3).astype(BF16)
    w['w_v'] = seg(4).astype(BF16)
    w['w_ga'] = seg(5).astype(BF16)
    w['w_gb'] = seg(6).astype(BF16)
    w['g_q_lat'] = row(p['g_q_lat'])
    w['g_kv_lat'] = row(p['g_kv_lat'])
    w['g_kr'] = _slot_gain([(ROPE_LO, p['g_kr'][i])])
    w['g_gm'] = row(p['g_gm'])
    head_dim = QK_NOPE + QK_ROPE
    w_uq = p['w_uq'][i].reshape(q_lora, MLA_HEADS, head_dim)
    w['w_uq'] = jnp.pad(w_uq, ((0, 0), (0, 0), (0, SLOT - head_dim))).reshape(q_lora, -1).astype(BF16)
    w['g_q'] = _slot_gain([(0, p['g_qn'][i]), (ROPE_LO, p['g_qr'][i])])
    w['m_avg'] = _group_avg_matrix()
    w_uk = p['w_uk'][i].reshape(kv_lora, MLA_HEADS, QK_NOPE)
    w['w_uk'] = jnp.pad(w_uk, ((0, 0), (0, 0), (0, SLOT - QK_NOPE))).reshape(kv_lora, -1).astype(BF16)
    w['g_kn'] = _slot_gain([(0, p['g_kn'][i])])
    w_uv = p['w_uv'][i].reshape(kv_lora, MLA_HEADS // 2, 2, V_HEAD)
    eye2 = jnp.eye(2, dtype=w_uv.dtype)
    w['w_uv'] = jnp.einsum('cpjd,jk->cpjkd', w_uv, eye2).reshape(kv_lora, -1).astype(BF16)
    w['w_uv_plain'] = p['w_uv'][i].astype(BF16)
    place = np.zeros((QK_ROPE, SLOT), np.float32)
    place[np.arange(QK_ROPE), ROPE_LO + np.arange(QK_ROPE)] = 1.0
    w['rope_place'] = jnp.asarray(place, BF16)
    w['w_oa'] = p['w_oa'][i].astype(BF16)
    w['w_ob'] = p['w_ob'][i].astype(BF16)
    w['w_o'] = p['w_o'][i].astype(BF16)
    w['g_xattn'] = row(p['g_xattn'])
    w['g_mem'] = row(p['g_mem'])
    w['w_cq_mem'] = p['w_cq'][i].astype(BF16)
    w['g_cq'] = row(p['g_cq'])
    w['w_ck'] = p['w_ck'][i].astype(BF16)
    w['g_ck'] = row(p['g_ck'])
    w['w_cv'] = p['w_cv'][i].astype(BF16)
    w['w_co'] = p['w_co'][i].astype(BF16)
    w['g_ffn'] = row(p['g_ffn'])
    w['w_pq'] = p['w_pq'][i].astype(BF16)
    w['sub_keys'] = p['sub_keys'][i].reshape(2 * PEER_HEADS, N_KEYS, PEER_HALF)[:, ::-1, :].astype(BF16)
    w['peer_u'] = p['peer_u'][i].astype(BF16)
    w['peer_v'] = p['peer_v'][i].astype(BF16)
    w['w_s'] = p['w_s'][i]
    w['b_s'] = p['b_s'][i]
    return w


def _spatial_operands(w, seq, tm):
    chunk = min(seq, GM_CHUNK)
    reps = tm // chunk
    w_mask = jnp.tril(w['w_s'][:, :chunk, :chunk])
    w_mask = w_mask.astype(BF16)
    mix = jnp.concatenate(
        [jnp.pad(w_mask, ((0, 0), (0, 0), (a * chunk, (reps - 1 - a) * chunk))) for a in range(reps)], axis=1)
    bias = jnp.tile(w['b_s'][:, :chunk].T, (reps, 1))
    bias = jnp.repeat(bias, LANES, axis=1)
    return mix.astype(BF16), bias.astype(F32)


def _token_mixer(x2d, w, seq, pos, past, tm):
    t, d = x2d.shape
    nb = t // seq
    period = max(seq, tm)
    tabs = _rope_tables(jnp.tile(pos, period // seq))
    cq, ckv, kr, u, vg, siga, sigb = _in_proj(x2d, w, tabs, tm)
    mix, bias = _spatial_operands(w, seq, tm)
    gbob = _gmlp(u, vg, sigb, mix, bias, w['w_ob'], tm)
    q = _q_proj(cq, w, tabs, tm)
    if past is None:
        k, v = _kv_proj(ckv, kr, w, tm)
        o = _attn_prompt(q.reshape(nb, seq, -1), k.reshape(nb, seq, -1), v.reshape(nb, seq, -1),
                         _score_bound(w), ATTN_TILE)
    else:
        past_ckv, past_kr = past
        o = _attn_sample(q.reshape(nb, seq, -1), past_ckv, past_kr, ckv.reshape(nb, seq, -1),
                         kr.reshape(nb, seq, SLOT), w)
    x1, qc = _merge(x2d, o.reshape(t, -1), siga, gbob, w, tm)
    return x1, qc, ckv, kr[:, ROPE_LO:ROPE_LO + QK_ROPE], vg


def _tail(x1, qc, mk, mv, w, nb, seq, tm):
    t, d = x1.shape
    oc = _cross(qc.reshape(nb, seq, -1), mk, mv, min(seq, tm))
    x2, h, q_route = _cross_out(x1, oc.reshape(t, -1), w, tm)
    return _peer(h, x2, q_route, w, PEER_ROWS, PEER_EXPERTS)


def kernel(x_prompt, x_sample, cache_mla_ckv, cache_mla_krope, cache_mem_k, cache_mem_v, mem_prompt, g_mix, w_in, g_q_lat, w_uq, g_qn, g_qr, g_kv_lat, g_kr, w_uk, w_uv, g_kn, w_oa, g_gm, w_s, b_s, w_ob, w_o, g_xattn, g_mem, w_cq, g_cq, w_ck, g_ck, w_cv, w_co, g_ffn, w_pq, sub_keys, peer_u, peer_v):
    params = dict(g_mix=g_mix, w_in=w_in, g_q_lat=g_q_lat, w_uq=w_uq, g_qn=g_qn, g_qr=g_qr,
                  g_kv_lat=g_kv_lat, g_kr=g_kr, w_uk=w_uk, w_uv=w_uv, g_kn=g_kn, w_oa=w_oa, g_gm=g_gm,
                  w_s=w_s, b_s=b_s, w_ob=w_ob, w_o=w_o, g_xattn=g_xattn, g_mem=g_mem, w_cq=w_cq,
                  g_cq=g_cq, w_ck=w_ck, g_ck=g_ck, w_cv=w_cv, w_co=w_co, g_ffn=g_ffn, w_pq=w_pq,
                  sub_keys=sub_keys, peer_u=peer_u, peer_v=peer_v)
    bp, sp, d = x_prompt.shape
    bs, ss, _ = x_sample.shape
    depth = w_in.shape[0]
    past_len = cache_mla_ckv.shape[2]
    n_mem = mem_prompt.shape[1]
    tm = ROW_TILE
    assert (bp * sp) % tm == 0 and (bs * ss) % tm == 0 and sp % ATTN_TILE == 0
    pos_p = jnp.arange(sp)
    pos_s = past_len + jnp.arange(ss)
    xp = x_prompt.reshape(bp * sp, d)
    xs = x_sample.reshape(bs * ss, d)
    outs = [[] for _ in range(7)]
    for i in range(depth):
        w = _prep_weights(params, i)
        xp1, qcp, ckv_p, kr_p, _ = _token_mixer(xp, w, sp, pos_p, None, tm)
        xs1, qcs, ckv_s, kr_s, vg_s = _token_mixer(xs, w, ss, pos_s, (cache_mla_ckv[i], cache_mla_krope[i]), tm)
        mk_p, mv_p = _mem_kv(mem_prompt.reshape(bp * n_mem, d), w, min(tm, bp * n_mem))
        xp = _tail(xp1, qcp, mk_p.reshape(bp, n_mem, -1), mv_p.reshape(bp, n_mem, -1), w, bp, sp, tm)
        xs = _tail(xs1, qcs, cache_mem_k[i].reshape(bs, n_mem, -1), cache_mem_v[i].reshape(bs, n_mem, -1),
                   w, bs, ss, tm)
        outs[0].append(ckv_p.reshape(bp, sp, -1))
        outs[1].append(kr_p.reshape(bp, sp, -1))
        outs[2].append(mk_p.reshape(bp, n_mem, MEM_HEADS, MEM_HEAD_DIM))
        outs[3].append(mv_p.reshape(bp, n_mem, MEM_HEADS, MEM_HEAD_DIM))
        outs[4].append(ckv_s.reshape(bs, ss, -1))
        outs[5].append(kr_s.reshape(bs, ss, -1))
        outs[6].append(vg_s.reshape(bs, ss, -1))
    return (xp.reshape(bp, sp, d), xs.reshape(bs, ss, d)) + tuple(jnp.stack(o) for o in outs)
```

```python
import functools

import jax
import jax.numpy as jnp
import numpy as np
from jax import lax
from jax.experimental import pallas as pl
from jax.experimental.pallas import tpu as pltpu

CHUNK = 64
EPS = 1e-6
MLA_HEADS = 16
QK_NOPE = 64
QK_ROPE = 32
V_HEAD = 64
ROPE_THETA = 10000.0
MLA_SCALE = (QK_NOPE + QK_ROPE) ** -0.5
GM_CHUNK = 128
GM_GROUPS = 8
MEM_HEADS = 4
MEM_HEAD_DIM = 128
MEM_SCALE = MEM_HEAD_DIM ** -0.5
PEER_HEADS = 8
N_KEYS = 128
PEER_TOPK = 16
PEER_HALF = 128

LANES = 128
SLOT = LANES
ROPE_LO = QK_NOPE
ROPE_HALF = QK_ROPE // 2
ROW_TILE = 512
WIDE_ROW_TILE = 1024
ATTN_TILE = 512
LOG2_E = float(np.log2(np.e))
STATIC_SHIFT_LIMIT = 48.0
BOUND_MARGIN = 1.02
PEER_ROWS = 512
PEER_EXPERTS = 1024
PEER_CHUNK = 512
MASK_PITCH = N_KEYS + 8
MASK_UNROLL = 32
VMEM_LIMIT = 48 * 1024 * 1024
PEER_VMEM_LIMIT = 60 * 1024 * 1024

F32 = jnp.float32
BF16 = jnp.bfloat16
_NT = (((1,), (1,)), ((), ()))


def _dot(a, b):
    return jnp.dot(a, b, preferred_element_type=F32)


def _dot_nt(a, b):
    return lax.dot_general(a, b, _NT, preferred_element_type=F32)


def _rms(xf, g):
    return xf * lax.rsqrt(jnp.mean(xf * xf, axis=-1, keepdims=True) + EPS) * g


def _gelu(x):
    return 0.5 * x * (1.0 + lax.erf(x * np.float32(np.sqrt(0.5))))


def _group_mean(sq, m_ref):
    hi = sq.astype(BF16)
    lo = (sq - hi.astype(F32)).astype(BF16)
    return _dot(jnp.concatenate([hi, lo], axis=1), m_ref[...])


def _rope_slot(y, cos, sin_up, sin_dn):
    return (y * cos + pltpu.roll(y, ROPE_HALF, 1) * sin_up
            + pltpu.roll(y, SLOT - ROPE_HALF, 1) * sin_dn)


def _const_spec(shape):
    nd = len(shape)
    return pl.BlockSpec(shape, lambda *_: (0,) * nd, pipeline_mode=pl.Buffered(1))


def _row_spec(tm, width):
    return pl.BlockSpec((tm, width), lambda i: (i, 0))


def _params(sem, limit=VMEM_LIMIT):
    return pltpu.CompilerParams(dimension_semantics=sem, vmem_limit_bytes=limit)


def _in_proj_kernel(x_ref, gmix_ref, wcq_ref, wckv_ref, wkr_ref, wu_ref, wv_ref, wga_ref, wgb_ref,
                    gq_ref, gkv_ref, gkr_ref, ggm_ref, cos_ref, sup_ref, sdn_ref,
                    cq_out, ckv_out, kr_out, u_out, vg_out, siga_out, sigb_out):
    h = _rms(x_ref[...], gmix_ref[...]).astype(BF16)
    cq_out[...] = _rms(_dot(h, wcq_ref[...]), gq_ref[...]).astype(BF16)
    ckv_out[...] = _rms(_dot(h, wckv_ref[...]), gkv_ref[...])
    kr = _dot(h, wkr_ref[...])
    ms = jnp.sum(kr * kr, axis=-1, keepdims=True) * (1.0 / QK_ROPE)
    kr = kr * lax.rsqrt(ms + EPS) * gkr_ref[...]
    kr_out[...] = _rope_slot(kr, cos_ref[...], sup_ref[...], sdn_ref[...])
    u_out[...] = _gelu(_dot(h, wu_ref[...])).astype(BF16)
    vg_out[...] = _rms(_gelu(_dot(h, wv_ref[...])), ggm_ref[...])
    siga_out[...] = jax.nn.sigmoid(_dot(h, wga_ref[...])).astype(BF16)
    sigb_out[...] = jax.nn.sigmoid(_dot(h, wgb_ref[...])).astype(BF16)


def _in_proj(x2d, w, tabs, tm):
    t, d = x2d.shape
    cos, sup, sdn = tabs
    ntab = cos.shape[0] // tm
    tab_spec = pl.BlockSpec((tm, SLOT), lambda i: (i % ntab, 0))
    consts = [w['g_mix'], w['w_cq'], w['w_ckv'], w['w_kr'], w['w_u'], w['w_v'], w['w_ga'], w['w_gb'],
              w['g_q_lat'], w['g_kv_lat'], w['g_kr'], w['g_gm']]
    gw = w['w_u'].shape[1]
    widths = [w['w_cq'].shape[1], w['w_ckv'].shape[1], SLOT, gw, gw, d, d]
    dtypes = [BF16, F32, F32, BF16, F32, BF16, BF16]
    return pl.pallas_call(
        _in_proj_kernel,
        grid=(t // tm,),
        in_specs=[_row_spec(tm, d)] + [_const_spec(c.shape) for c in consts] + [tab_spec] * 3,
        out_specs=[_row_spec(tm, n) for n in widths],
        out_shape=[jax.ShapeDtypeStruct((t, n), dt) for n, dt in zip(widths, dtypes)],
        compiler_params=_params(("parallel",)),
        name="in_proj",
    )(x2d, *consts, cos, sup, sdn)


def _gmlp_kernel(u_ref, vg_ref, sigb_ref, mix_ref, bias_ref, wob_ref, out_ref):
    vgb = vg_ref[...].astype(BF16)
    parts = []
    for g in range(GM_GROUPS):
        sl = slice(g * LANES, (g + 1) * LANES)
        mixed = _dot(mix_ref[g], vgb[:, sl]) + bias_ref[:, sl]
        parts.append((u_ref[:, sl].astype(F32) * mixed).astype(BF16))
    ob = _dot(jnp.concatenate(parts, axis=1), wob_ref[...])
    out_ref[...] = sigb_ref[...].astype(F32) * ob


def _gmlp(u, vg, sigb, mix, bias, w_ob, tm):
    t, gw = u.shape
    d = w_ob.shape[1]
    return pl.pallas_call(
        _gmlp_kernel,
        grid=(t // tm,),
        in_specs=[_row_spec(tm, gw), _row_spec(tm, gw), _row_spec(tm, d),
                  _const_spec(mix.shape), _const_spec(bias.shape), _const_spec(w_ob.shape)],
        out_specs=_row_spec(tm, d),
        out_shape=jax.ShapeDtypeStruct((t, d), F32),
        compiler_params=_params(("parallel",)),
        name="gmlp",
    )(u, vg, sigb, mix, bias, w_ob)


def _q_proj_kernel(cq_ref, wuq_ref, gq_ref, mavg_ref, cos_ref, sup_ref, sdn_ref, q_out):
    q = _dot(cq_ref[...], wuq_ref[...])
    cos, sup, sdn = cos_ref[...], sup_ref[...], sdn_ref[...]
    for h in range(MLA_HEADS):
        sl = slice(h * SLOT, (h + 1) * SLOT)
        qs = q[:, sl]
        y = qs * lax.rsqrt(_group_mean(qs * qs, mavg_ref) + EPS) * gq_ref[...]
        q_out[:, sl] = _rope_slot(y, cos, sup, sdn).astype(BF16)


def _q_proj(cq, w, tabs, tm):
    t, ql = cq.shape
    cos, sup, sdn = tabs
    ntab = cos.shape[0] // tm
    tab_spec = pl.BlockSpec((tm, SLOT), lambda i: (i % ntab, 0))
    n = MLA_HEADS * SLOT
    return pl.pallas_call(
        _q_proj_kernel,
        grid=(t // tm,),
        in_specs=[_row_spec(tm, ql), _const_spec(w['w_uq'].shape), _const_spec(w['g_q'].shape),
                  _const_spec(w['m_avg'].shape)] + [tab_spec] * 3,
        out_specs=_row_spec(tm, n),
        out_shape=jax.ShapeDtypeStruct((t, n), BF16),
        compiler_params=_params(("parallel",)),
        name="q_proj",
    )(cq, w['w_uq'], w['g_q'], w['m_avg'], cos, sup, sdn)


def _kv_proj_kernel(ckv_ref, kr_ref, wuk_ref, wuv_ref, gkn_ref, mavg_ref, k_out, v_out):
    c = ckv_ref[...].astype(BF16)
    kn = _dot(c, wuk_ref[...])
    kr = kr_ref[...]
    for h in range(MLA_HEADS):
        sl = slice(h * SLOT, (h + 1) * SLOT)
        ks = kn[:, sl]
        y = ks * lax.rsqrt(_group_mean(ks * ks, mavg_ref) + EPS) * gkn_ref[...]
        k_out[:, sl] = (y + kr).astype(BF16)
    v_out[...] = _dot(c, wuv_ref[...]).astype(BF16)


def _kv_proj(ckv, kr, w, tm):
    t, kl = ckv.shape
    n = MLA_HEADS * SLOT
    return pl.pallas_call(
        _kv_proj_kernel,
        grid=(t // tm,),
        in_specs=[_row_spec(tm, kl), _row_spec(tm, SLOT), _const_spec(w['w_uk'].shape),
                  _const_spec(w['w_uv'].shape), _const_spec(w['g_kn'].shape),
                  _const_spec(w['m_avg'].shape)],
        out_specs=[_row_spec(tm, n), _row_spec(tm, n)],
        out_shape=[jax.ShapeDtypeStruct((t, n), BF16)] * 2,
        compiler_params=_params(("parallel",)),
        name="kv_proj",
    )(ckv, kr, w['w_uk'], w['w_uv'], w['g_kn'], w['m_avg'])


def _attn_prompt_kernel(bound_ref, q_ref, k_ref, v_ref, o_ref, *, tile):
    n_tiles = q_ref.shape[1] // tile
    bound = bound_ref[0]
    first = lax.broadcasted_iota(jnp.int32, (1, LANES), 1) < V_HEAD
    row_chunk = lax.broadcasted_iota(jnp.int32, (tile, tile), 0) // CHUNK
    col_chunk = lax.broadcasted_iota(jnp.int32, (tile, tile), 1) // CHUNK
    visible = col_chunk <= row_chunk

    def static_step(q, kstart, carry, masked):
        sums, acc = carry
        kstart = pl.multiple_of(kstart, tile)
        k = k_ref[0, pl.ds(kstart, tile), :]
        v = v_ref[0, pl.ds(kstart, tile), :]
        new_sums = []
        for h in range(2):
            sl = slice(h * SLOT, (h + 1) * SLOT)
            s = _dot_nt(q[:, sl], k[:, sl]) * F32(MLA_SCALE * LOG2_E) - bound
            if masked:
                s = jnp.where(visible, s, F32(-1e30))
            p = jnp.exp2(s)
            part = sums[h]
            for c in range(tile // LANES):
                part = part + p[:, c * LANES:(c + 1) * LANES]
            new_sums.append(part)
            acc = acc + _dot(p.astype(BF16), v[:, sl])
        return tuple(new_sums), acc

    def static_tile(qi, _):
        rows = pl.ds(pl.multiple_of(qi * tile, tile), tile)
        q = q_ref[0, rows, :]
        zeros = jnp.zeros((tile, LANES), F32)

        def pair(i, c):
            c = static_step(q, 2 * i * tile, c, False)
            return static_step(q, (2 * i + 1) * tile, c, False)

        def odd_tail(c):
            return static_step(q, qi * tile, static_step(q, (qi - 1) * tile, c, False), True)

        carry = lax.fori_loop(0, qi // 2, pair, ((zeros, zeros), zeros))
        sums, acc = lax.cond(qi % 2 == 1, odd_tail, lambda c: static_step(q, qi * tile, c, True), carry)
        l0 = jnp.sum(sums[0], axis=-1, keepdims=True)
        l1 = jnp.sum(sums[1], axis=-1, keepdims=True)
        o_ref[0, rows, :] = (acc / jnp.where(first, l0, l1)).astype(BF16)
        return 0

    @pl.when(bound < STATIC_SHIFT_LIMIT)
    def _():
        lax.fori_loop(0, n_tiles, static_tile, 0)

    def step(q, kstart, carry, masked):
        ms, ls, acc = carry
        kstart = pl.multiple_of(kstart, tile)
        k = k_ref[0, pl.ds(kstart, tile), :]
        v = v_ref[0, pl.ds(kstart, tile), :]
        new_ms, new_ls, alphas, pvs = [], [], [], []
        for h in range(2):
            sl = slice(h * SLOT, (h + 1) * SLOT)
            s = _dot_nt(q[:, sl], k[:, sl]) * F32(MLA_SCALE * LOG2_E)
            if masked:
                s = jnp.where(visible, s, F32(-1e30))
            m_new = jnp.maximum(ms[h], jnp.max(s, axis=-1, keepdims=True))
            alpha = jnp.exp2(ms[h] - m_new)
            p = jnp.exp2(s - m_new)
            new_ls.append(alpha * ls[h] + jnp.sum(p, axis=-1, keepdims=True))
            new_ms.append(m_new)
            alphas.append(alpha)
            pvs.append(_dot(p.astype(BF16), v[:, sl]))
        acc = acc * jnp.where(first, alphas[0], alphas[1]) + pvs[0] + pvs[1]
        return tuple(new_ms), tuple(new_ls), acc

    def online_tile(qi, _):
        rows = pl.ds(pl.multiple_of(qi * tile, tile), tile)
        q = q_ref[0, rows, :]
        neg = jnp.full((tile, 1), -1e30, F32)
        zero = jnp.zeros((tile, 1), F32)
        init = ((neg, neg), (zero, zero), jnp.zeros((tile, LANES), F32))
        carry = lax.fori_loop(0, qi, lambda i, c: step(q, i * tile, c, False), init)
        _, ls, acc = step(q, qi * tile, carry, True)
        o_ref[0, rows, :] = (acc / jnp.where(first, ls[0], ls[1])).astype(BF16)
        return 0

    @pl.when(bound >= STATIC_SHIFT_LIMIT)
    def _():
        lax.fori_loop(0, n_tiles, online_tile, 0)


def _score_bound(w):
    gmax = lambda g, lo, n: jnp.max(jnp.abs(g[0, lo:lo + n]))
    nope = QK_NOPE * gmax(w['g_q'], 0, QK_NOPE) * gmax(w['g_kn'], 0, QK_NOPE)
    rope = QK_ROPE * gmax(w['g_q'], ROPE_LO, QK_ROPE) * gmax(w['g_kr'], ROPE_LO, QK_ROPE)
    return (BOUND_MARGIN * MLA_SCALE * LOG2_E * (nope + rope)).reshape(1).astype(F32)


def _attn_prompt(q, k, v, bound, tile):
    b, s, _ = q.shape
    pairs = MLA_HEADS // 2
    pair_block = pl.BlockSpec((1, s, 2 * SLOT), lambda bi, p: (bi, 0, p))
    return pl.pallas_call(
        functools.partial(_attn_prompt_kernel, tile=tile),
        grid=(b, pairs),
        in_specs=[pl.BlockSpec(memory_space=pltpu.SMEM), pair_block, pair_block, pair_block],
        out_specs=pl.BlockSpec((1, s, LANES), lambda bi, p: (bi, 0, p)),
        out_shape=jax.ShapeDtypeStruct((b, s, pairs * LANES), BF16),
        compiler_params=_params(("parallel", "parallel")),
        name="attn_prompt",
    )(bound, q, k, v)


def _attn_sample_kernel(q_ref, pckv_ref, pkr_ref, nckv_ref, nkr_ref, wuk_ref, wuv_ref, gkn_ref, place_ref,
                        o_ref):
    sq = q_ref.shape[1]
    rows = MLA_HEADS * sq
    gkn = gkn_ref[...]

    def keys_values(ckv, kr_slot):
        c = ckv.astype(BF16)
        kn = _dot(c, wuk_ref[...])
        parts = []
        for h in range(MLA_HEADS):
            ks = kn[:, h * SLOT:(h + 1) * SLOT]
            ms = jnp.sum(ks * ks, axis=-1, keepdims=True) * (1.0 / QK_NOPE)
            parts.append((ks * lax.rsqrt(ms + EPS) * gkn + kr_slot).astype(BF16))
        return jnp.concatenate(parts, axis=1), _dot(c, wuv_ref[...]).astype(BF16)

    past_kr = _dot(pkr_ref[0].astype(BF16), place_ref[...])
    k_p, v_p = keys_values(pckv_ref[0], past_kr)
    k_n, v_n = keys_values(nckv_ref[0], nkr_ref[0])
    k = jnp.concatenate([k_p, k_n], axis=0)
    v = jnp.concatenate([v_p, v_n], axis=0)

    qt = jnp.concatenate([q_ref[0].astype(F32)] * MLA_HEADS, axis=0)
    q_shape = (rows, MLA_HEADS * SLOT)
    own = (lax.broadcasted_iota(jnp.int32, q_shape, 0) // sq
           == lax.broadcasted_iota(jnp.int32, q_shape, 1) // SLOT)
    qbd = jnp.where(own, qt, 0.0).astype(BF16)
    s = _dot_nt(qbd, k) * MLA_SCALE
    p = jnp.exp(s - jnp.max(s, axis=-1, keepdims=True))
    p = p / jnp.sum(p, axis=-1, keepdims=True)
    o_all = _dot(p.astype(BF16), v)
    o_shape = o_all.shape
    own = (lax.broadcasted_iota(jnp.int32, o_shape, 0) // sq
           == lax.broadcasted_iota(jnp.int32, o_shape, 1) // V_HEAD)
    o_all = jnp.where(own, o_all, 0.0)
    o = o_all[0:sq]
    for h in range(1, MLA_HEADS):
        o = o + o_all[h * sq:(h + 1) * sq]
    o_ref[0] = o.astype(BF16)


def _attn_sample(q, past_ckv, past_kr, new_ckv, new_kr, w):
    b, sq, nq = q.shape
    npast, kl = past_ckv.shape[1:]
    nv = w['w_uv_plain'].shape[1]
    stream = lambda n, width: pl.BlockSpec((1, n, width), lambda bi: (bi, 0, 0))
    consts = [w['w_uk'], w['w_uv_plain'], w['g_kn'], w['rope_place']]
    return pl.pallas_call(
        _attn_sample_kernel,
        grid=(b,),
        in_specs=[stream(sq, nq), stream(npast, kl), stream(npast, QK_ROPE), stream(sq, kl), stream(sq, SLOT)]
        + [_const_spec(c.shape) for c in consts],
        out_specs=stream(sq, nv),
        out_shape=jax.ShapeDtypeStruct((b, sq, nv), BF16),
        compiler_params=_params(("parallel",)),
        name="attn_sample",
    )(q, past_ckv, past_kr, new_ckv, new_kr, *consts)


def _merge_kernel(x_ref, o_ref, siga_ref, gbob_ref, woa_ref, wo_ref, gx_ref, wcq_ref, gcq_ref,
                  x1_out, qc_out):
    oa = _dot(o_ref[...], woa_ref[...])
    merged = siga_ref[...].astype(F32) * oa + gbob_ref[...]
    x1 = x_ref[...] + _dot(merged.astype(BF16), wo_ref[...])
    x1_out[...] = x1
    qc = _dot(_rms(x1, gx_ref[...]).astype(BF16), wcq_ref[...])
    for h in range(MEM_HEADS):
        sl = slice(h * MEM_HEAD_DIM, (h + 1) * MEM_HEAD_DIM)
        qc_out[:, sl] = _rms(qc[:, sl], gcq_ref[...]).astype(BF16)


def _merge(x2d, o, siga, gbob, w, tm):
    t, d = x2d.shape
    consts = [w['w_oa'], w['w_o'], w['g_xattn'], w['w_cq_mem'], w['g_cq']]
    nq = w['w_cq_mem'].shape[1]
    return pl.pallas_call(
        _merge_kernel,
        grid=(t // tm,),
        in_specs=[_row_spec(tm, d), _row_spec(tm, o.shape[1]), _row_spec(tm, d), _row_spec(tm, d)]
        + [_const_spec(c.shape) for c in consts],
        out_specs=[_row_spec(tm, d), _row_spec(tm, nq)],
        out_shape=[jax.ShapeDtypeStruct((t, d), F32), jax.ShapeDtypeStruct((t, nq), BF16)],
        compiler_params=_params(("parallel",)),
        name="merge",
    )(x2d, o, siga, gbob, *consts)


def _mem_kv_kernel(mem_ref, gmem_ref, wck_ref, gck_ref, wcv_ref, k_out, v_out):
    m = _rms(mem_ref[...], gmem_ref[...]).astype(BF16)
    k = _dot(m, wck_ref[...])
    for h in range(MEM_HEADS):
        sl = slice(h * MEM_HEAD_DIM, (h + 1) * MEM_HEAD_DIM)
        k_out[:, sl] = _rms(k[:, sl], gck_ref[...])
    v_out[...] = _dot(m, wcv_ref[...])


def _mem_kv(mem2d, w, tm):
    t, d = mem2d.shape
    consts = [w['g_mem'], w['w_ck'], w['g_ck'], w['w_cv']]
    n = w['w_ck'].shape[1]
    return pl.pallas_call(
        _mem_kv_kernel,
        grid=(t // tm,),
        in_specs=[_row_spec(tm, d)] + [_const_spec(c.shape) for c in consts],
        out_specs=[_row_spec(tm, n)] * 2,
        out_shape=[jax.ShapeDtypeStruct((t, n), F32)] * 2,
        compiler_params=_params(("parallel",)),
        name="mem_kv",
    )(mem2d, *consts)


def _cross_kernel(q_ref, k_ref, v_ref, o_ref):
    q = q_ref[0]
    k = k_ref[0].astype(BF16)
    v = v_ref[0].astype(BF16)
    for h in range(MEM_HEADS):
        sl = slice(h * MEM_HEAD_DIM, (h + 1) * MEM_HEAD_DIM)
        s = _dot_nt(q[:, sl], k[:, sl]) * MEM_SCALE
        p = jnp.exp(s - jnp.max(s, axis=-1, keepdims=True))
        p = p / jnp.sum(p, axis=-1, keepdims=True)
        o_ref[0, :, sl] = _dot(p.astype(BF16), v[:, sl]).astype(BF16)


def _cross(qc, mk, mv, tc):
    b, s, n = qc.shape
    nm = mk.shape[1]
    return pl.pallas_call(
        _cross_kernel,
        grid=(b, s // tc),
        in_specs=[pl.BlockSpec((1, tc, n), lambda bi, i: (bi, i, 0)),
                  pl.BlockSpec((1, nm, n), lambda bi, i: (bi, 0, 0)),
                  pl.BlockSpec((1, nm, n), lambda bi, i: (bi, 0, 0))],
        out_specs=pl.BlockSpec((1, tc, n), lambda bi, i: (bi, i, 0)),
        out_shape=jax.ShapeDtypeStruct((b, s, n), BF16),
        compiler_params=_params(("parallel", "parallel")),
        name="cross_attn",
    )(qc, mk, mv)


def _cross_out_kernel(x1_ref, oc_ref, wco_ref, gffn_ref, wpq_ref, x2_out, h_out, q_out):
    x2 = x1_ref[...] + _dot(oc_ref[...], wco_ref[...])
    x2_out[...] = x2
    h = _rms(x2, gffn_ref[...]).astype(BF16)
    h_out[...] = h
    q_out[...] = _dot(h, wpq_ref[...]).astype(BF16)


def _cross_out(x1, oc, w, tm):
    t, d = x1.shape
    nq = w['w_pq'].shape[1]
    return pl.pallas_call(
        _cross_out_kernel,
        grid=(t // tm,),
        in_specs=[_row_spec(tm, d), _row_spec(tm, oc.shape[1]), _const_spec(w['w_co'].shape),
                  _const_spec(w['g_ffn'].shape), _const_spec(w['w_pq'].shape)],
        out_specs=[_row_spec(tm, d), _row_spec(tm, d), _row_spec(tm, nq)],
        out_shape=[jax.ShapeDtypeStruct((t, d), F32), jax.ShapeDtypeStruct((t, d), BF16),
                   jax.ShapeDtypeStruct((t, nq), BF16)],
        compiler_params=_params(("parallel",)),
        name="cross_out",
    )(x1, oc, w['w_co'], w['g_ffn'], w['w_pq'])


def _pair_blocks():
    k = PEER_TOPK
    blocks, flat = [], []
    n_row = sum(1 for a in range(k) if k // (a + 1) > 2)
    for a in range(n_row):
        nb = k if a == 0 else 8
        blocks.append(('row', a, nb))
        flat += [a * k + b if (a + 1) * (b + 1) <= k else -1 for b in range(nb)]
    for b in range(k // (n_row + 1)):
        na = k if b == 0 else 8
        blocks.append(('col', b, na))
        flat += [a * k + b if (a >= n_row and (a + 1) * (b + 1) <= k) else -1 for a in range(na)]
    return blocks, np.asarray(flat, np.float32)


def _pair_candidates(v0, v1, blocks, combine):
    parts = []
    for kind, fixed, n in blocks:
        if kind == 'row':
            parts.append(combine(v0[fixed:fixed + 1, :], v1[0:n, :]))
        else:
            parts.append(combine(v0[0:n, :], v1[fixed:fixed + 1, :]))
    return jnp.concatenate(parts, axis=0)


def _peer_kernel(h_ref, x_ref, q_ref, keys_ref, flat_ref, u_ref, v_ref, out_ref,
                 mask_scr, sv_scr, si_scr, ts_scr, te_scr, it_scr, jt_scr, gt_scr,
                 isel_scr, jsel_scr, gsel_scr):
    tm = h_ref.shape[0]
    te = v_ref.shape[0]
    r = pl.program_id(0)
    e = pl.program_id(1)
    n = pl.num_programs(0) - 2
    rows_per_step = te // N_KEYS
    tokens_per_head = tm // PEER_HEADS
    blocks, _ = _pair_blocks()
    neg_inf = F32(-jnp.inf)
    key_rows = lax.broadcasted_iota(jnp.int32, (N_KEYS, LANES), 0).astype(F32)
    swept = lax.rem(r, 2)
    ranked = 1 - swept

    def mask_token(t):
        irow = isel_scr[swept, pl.ds(t, 1), :]
        jrow = jsel_scr[swept, pl.ds(t, 1), :]
        grow = gsel_scr[swept, pl.ds(t, 1), :]
        rw = jnp.where(key_rows == irow, 0.5 * grow, 0.0).astype(BF16)
        cw = jnp.where(key_rows == jrow, 1.0, 0.0).astype(BF16)
        r0 = pl.multiple_of(t * MASK_PITCH, 8)
        mask_scr[pl.ds(r0, N_KEYS), :] = _dot_nt(rw, cw)

    def rank_pairs(hd):
        sv0, sv1 = sv_scr[2 * hd], sv_scr[2 * hd + 1]
        si0, si1 = si_scr[2 * hd], si_scr[2 * hd + 1]
        flat = flat_ref[...]
        cand = _pair_candidates(sv0, sv1, blocks, lambda x, y: x + y)
        cand = jnp.where(flat >= 0.0, cand, neg_inf)
        cidx = _pair_candidates(si0, si1, blocks, lambda x, y: x * F32(N_KEYS) + y)
        for kk in range(PEER_TOPK):
            m = jnp.max(cand, axis=0, keepdims=True)
            pos = jnp.min(jnp.where(cand == m, flat, F32(PEER_TOPK * PEER_TOPK)), axis=0, keepdims=True)
            sel = flat == pos
            ts_scr[kk:kk + 1, :] = m
            te_scr[kk:kk + 1, :] = jnp.max(jnp.where(sel, cidx, F32(-1.0)), axis=0, keepdims=True)
            cand = jnp.where(sel, neg_inf, cand)
        top = ts_scr[...]
        ex = jnp.exp(top - top[0:1, :])
        gate = ex / jnp.sum(ex, axis=0, keepdims=True)
        eidx = te_scr[...]
        irow = jnp.floor(eidx * F32(1.0 / N_KEYS))
        r0 = pl.multiple_of(hd * PEER_TOPK, PEER_TOPK)
        it_scr[pl.ds(r0, PEER_TOPK), :] = irow
        jt_scr[pl.ds(r0, PEER_TOPK), :] = eidx - irow * F32(N_KEYS)
        gt_scr[pl.ds(r0, PEER_TOPK), :] = gate

    def publish_ranked():
        isel_scr[ranked] = it_scr[...].T
        jsel_scr[ranked] = jt_scr[...].T
        gsel_scr[ranked] = gt_scr[...].T

    @pl.when(e == 0)
    def _():
        out_ref[...] = x_ref[...]

        @pl.when(r == 1)
        def _():
            def head(hd, _):
                rank_pairs(hd)
                return 0

            lax.fori_loop(0, PEER_HEADS, head, 0)
            publish_ranked()

        @pl.when(jnp.logical_and(r >= 2, r <= n))
        def _():
            def trip(i, _):
                for t in range(tokens_per_head):
                    mask_token(i * tokens_per_head + t)
                rank_pairs(i)
                return 0

            lax.fori_loop(0, PEER_HEADS, trip, 0)
            publish_ranked()

        @pl.when(r == n + 1)
        def _():
            def trip(i, _):
                for t in range(MASK_UNROLL):
                    mask_token(i * MASK_UNROLL + t)
                return 0

            lax.fori_loop(0, tm // MASK_UNROLL, trip, 0)

    def slot_scores():
        d0 = pl.multiple_of(e * PEER_HALF, PEER_HALF)
        return _dot_nt(q_ref[:, pl.ds(d0, PEER_HALF)], keys_ref[e])

    def rank_slot(scores):
        lane = lax.broadcasted_iota(jnp.int32, (tm, N_KEYS), 1)
        picked = jnp.zeros((tm, N_KEYS), jnp.int32)
        a = scores
        for kk in range(PEER_TOPK):
            idx = jnp.argmax(a, axis=-1, keepdims=True).astype(jnp.int32)
            a = jnp.where(lane == idx, neg_inf, a)
            picked = jnp.where(lane == kk, idx, picked)
        vals = jnp.take_along_axis(scores, picked, axis=1)
        sv_scr[e] = vals.T[0:PEER_TOPK, :]
        si_scr[e] = (N_KEYS - 1 - picked).astype(F32).T[0:PEER_TOPK, :]

    def expert_chunk():
        h = h_ref[...]
        row0 = e * rows_per_step
        weighted = []
        for c in range(te // PEER_CHUNK):
            rows = slice(c * PEER_CHUNK, (c + 1) * PEER_CHUNK)
            pre = _dot_nt(h, u_ref[rows, :])
            half_mask = jnp.concatenate(
                [mask_scr[pl.ds(row0 + c * (PEER_CHUNK // N_KEYS) + rr, tm, stride=MASK_PITCH), :]
                 for rr in range(PEER_CHUNK // N_KEYS)], axis=1)
            weighted.append((pre * (1.0 + lax.erf(pre * np.float32(np.sqrt(0.5)))) * half_mask).astype(BF16))
        out_ref[...] += _dot(jnp.concatenate(weighted, axis=1), v_ref[...])

    ranks = r < n
    sweeps = r >= 2

    @pl.when(jnp.logical_and(ranks, jnp.logical_not(sweeps)))
    def _():
        rank_slot(slot_scores())

    @pl.when(jnp.logical_and(ranks, sweeps))
    def _():
        a = slot_scores()
        expert_chunk()
        rank_slot(a)

    @pl.when(jnp.logical_and(jnp.logical_not(ranks), sweeps))
    def _():
        expert_chunk()


def _peer(h, x2, q_route, w, tm, te):
    t, d = h.shape
    n = t // tm
    nexp = w['peer_v'].shape[0]
    assert nexp // te == 2 * PEER_HEADS, "one half-key slot is ranked per expert step"
    assert tm % PEER_HEADS == 0 and tm % MASK_UNROLL == 0
    nsel = PEER_HEADS * PEER_TOPK
    _, flat = _pair_blocks()
    flat = jnp.asarray(np.repeat(flat[:, None], tm, axis=1))
    swept_tile = lambda r, e: (jnp.clip(r - 2, 0, n - 1), 0)
    dense_tile = lambda width: pl.BlockSpec((tm, width), swept_tile, pipeline_mode=pl.Buffered(1))
    expert_tile = pl.BlockSpec((te, d), lambda r, e: (jnp.where(r >= 2, e, 0), 0))
    return pl.pallas_call(
        _peer_kernel,
        grid=(n + 2, nexp // te),
        in_specs=[dense_tile(d), dense_tile(d),
                  pl.BlockSpec((tm, q_route.shape[1]), lambda r, e: (jnp.minimum(r, n - 1), 0),
                               pipeline_mode=pl.Buffered(1)),
                  _const_spec(w['sub_keys'].shape), _const_spec(flat.shape), expert_tile, expert_tile],
        out_specs=pl.BlockSpec((tm, d), swept_tile),
        out_shape=jax.ShapeDtypeStruct((t, d), F32),
        scratch_shapes=[pltpu.VMEM((tm * MASK_PITCH, LANES), F32),
                        pltpu.VMEM((2 * PEER_HEADS, PEER_TOPK, tm), F32),
                        pltpu.VMEM((2 * PEER_HEADS, PEER_TOPK, tm), F32),
                        pltpu.VMEM((PEER_TOPK, tm), F32),
                        pltpu.VMEM((PEER_TOPK, tm), F32),
                        pltpu.VMEM((nsel, tm), F32),
                        pltpu.VMEM((nsel, tm), F32),
                        pltpu.VMEM((nsel, tm), F32),
                        pltpu.VMEM((2, tm, nsel), F32),
                        pltpu.VMEM((2, tm, nsel), F32),
                        pltpu.VMEM((2, tm, nsel), F32)],
        compiler_params=_params(("arbitrary", "arbitrary"), PEER_VMEM_LIMIT),
        name="peer",
    )(h, x2, q_route, w['sub_keys'], flat, w['peer_u'], w['peer_v'])


def _slot_gain(parts):
    row = jnp.zeros((SLOT,), F32)
    for off, g in parts:
        row = lax.dynamic_update_slice(row, g.astype(F32), (off,))
    return row[None, :]


def _group_avg_matrix():
    m = np.zeros((SLOT, SLOT), np.float32)
    m[:QK_NOPE, :QK_NOPE] = 1.0 / QK_NOPE
    m[ROPE_LO:ROPE_LO + QK_ROPE, ROPE_LO:ROPE_LO + QK_ROPE] = 1.0 / QK_ROPE
    return jnp.asarray(np.concatenate([m, m], axis=0), BF16)


def _rope_tables(pos):
    inv = ROPE_THETA ** (-jnp.arange(ROPE_HALF, dtype=F32) / ROPE_HALF)
    ang = pos.astype(F32)[:, None] * inv[None, :]
    cos, sin = jnp.cos(ang), jnp.sin(ang)
    n = pos.shape[0]
    tail = SLOT - ROPE_LO - QK_ROPE
    cos_t = jnp.concatenate([jnp.ones((n, ROPE_LO), F32), cos, cos, jnp.zeros((n, tail), F32)], axis=1)
    sin_up = jnp.concatenate([jnp.zeros((n, ROPE_LO + ROPE_HALF), F32), sin, jnp.zeros((n, tail), F32)], axis=1)
    sin_dn = jnp.concatenate([jnp.zeros((n, ROPE_LO), F32), -sin, jnp.zeros((n, tail + ROPE_HALF), F32)], axis=1)
    return cos_t, sin_up, sin_dn


def _prep_weights(p, i):
    w_in = p['w_in'][i]
    q_lora = p['g_q_lat'].shape[1]
    kv_lora = p['g_kv_lat'].shape[1]
    gw = p['g_gm'].shape[1]
    d = w_in.shape[0]
    o = np.cumsum([0, q_lora, kv_lora, QK_ROPE, gw, gw, d, d])
    seg = lambda k: w_in[:, o[k]:o[k + 1]]
    row = lambda a: a[i][None, :].astype(F32)
    w = {}
    w['g_mix'] = row(p['g_mix'])
    w['w_cq'] = seg(0).astype(BF16)
    w['w_ckv'] = seg(1).astype(BF16)
    w['w_kr'] = jnp.pad(seg(2), ((0, 0), (ROPE_LO, SLOT - ROPE_LO - QK_ROPE))).astype(BF16)
    w['w_u'] = seg(3).astype(BF16)
    w['w_v'] = seg(4).astype(BF16)
    w['w_ga'] = seg(5).astype(BF16)
    w['w_gb'] = seg(6).astype(BF16)
    w['g_q_lat'] = row(p['g_q_lat'])
    w['g_kv_lat'] = row(p['g_kv_lat'])
    w['g_kr'] = _slot_gain([(ROPE_LO, p['g_kr'][i])])
    w['g_gm'] = row(p['g_gm'])
    head_dim = QK_NOPE + QK_ROPE
    w_uq = p['w_uq'][i].reshape(q_lora, MLA_HEADS, head_dim)
    w['w_uq'] = jnp.pad(w_uq, ((0, 0), (0, 0), (0, SLOT - head_dim))).reshape(q_lora, -1).astype(BF16)
    w['g_q'] = _slot_gain([(0, p['g_qn'][i]), (ROPE_LO, p['g_qr'][i])])
    w['m_avg'] = _group_avg_matrix()
    w_uk = p['w_uk'][i].reshape(kv_lora, MLA_HEADS, QK_NOPE)
    w['w_uk'] = jnp.pad(w_uk, ((0, 0), (0, 0), (0, SLOT - QK_NOPE))).reshape(kv_lora, -1).astype(BF16)
    w['g_kn'] = _slot_gain([(0, p['g_kn'][i])])
    w_uv = p['w_uv'][i].reshape(kv_lora, MLA_HEADS // 2, 2, V_HEAD)
    eye2 = jnp.eye(2, dtype=w_uv.dtype)
    w['w_uv'] = jnp.einsum('cpjd,jk->cpjkd', w_uv, eye2).reshape(kv_lora, -1).astype(BF16)
    w['w_uv_plain'] = p['w_uv'][i].astype(BF16)
    place = np.zeros((QK_ROPE, SLOT), np.float32)
    place[np.arange(QK_ROPE), ROPE_LO + np.arange(QK_ROPE)] = 1.0
    w['rope_place'] = jnp.asarray(place, BF16)
    w['w_oa'] = p['w_oa'][i].astype(BF16)
    w['w_ob'] = p['w_ob'][i].astype(BF16)
    w['w_o'] = p['w_o'][i].astype(BF16)
    w['g_xattn'] = row(p['g_xattn'])
    w['g_mem'] = row(p['g_mem'])
    w['w_cq_mem'] = p['w_cq'][i].astype(BF16)
    w['g_cq'] = row(p['g_cq'])
    w['w_ck'] = p['w_ck'][i].astype(BF16)
    w['g_ck'] = row(p['g_ck'])
    w['w_cv'] = p['w_cv'][i].astype(BF16)
    w['w_co'] = p['w_co'][i].astype(BF16)
    w['g_ffn'] = row(p['g_ffn'])
    w['w_pq'] = p['w_pq'][i].astype(BF16)
    w['sub_keys'] = p['sub_keys'][i].reshape(2 * PEER_HEADS, N_KEYS, PEER_HALF)[:, ::-1, :].astype(BF16)
    w['peer_u'] = p['peer_u'][i].astype(BF16)
    w['peer_v'] = p['peer_v'][i].astype(BF16)
    w['w_s'] = p['w_s'][i]
    w['b_s'] = p['b_s'][i]
    return w


def _spatial_operands(w, seq, tm):
    chunk = min(seq, GM_CHUNK)
    reps = tm // chunk
    w_mask = jnp.tril(w['w_s'][:, :chunk, :chunk])
    w_mask = w_mask.astype(BF16)
    mix = jnp.concatenate(
        [jnp.pad(w_mask, ((0, 0), (0, 0), (a * chunk, (reps - 1 - a) * chunk))) for a in range(reps)], axis=1)
    bias = jnp.tile(w['b_s'][:, :chunk].T, (reps, 1))
    bias = jnp.repeat(bias, LANES, axis=1)
    return mix.astype(BF16), bias.astype(F32)


def _token_mixer(x2d, w, seq, pos, past, tm):
    t, d = x2d.shape
    nb = t // seq
    period = max(seq, tm)
    tabs = _rope_tables(jnp.tile(pos, period // seq))
    cq, ckv, kr, u, vg, siga, sigb = _in_proj(x2d, w, tabs, tm)
    mix, bias = _spatial_operands(w, seq, tm)
    gbob = _gmlp(u, vg, sigb, mix, bias, w['w_ob'], tm)
    wide = min(WIDE_ROW_TILE, t)
    q = _q_proj(cq, w, tabs, wide)
    if past is None:
        k, v = _kv_proj(ckv, kr, w, wide)
        o = _attn_prompt(q.reshape(nb, seq, -1), k.reshape(nb, seq, -1), v.reshape(nb, seq, -1),
                         _score_bound(w), ATTN_TILE)
    else:
        past_ckv, past_kr = past
        o = _attn_sample(q.reshape(nb, seq, -1), past_ckv, past_kr, ckv.reshape(nb, seq, -1),
                         kr.reshape(nb, seq, SLOT), w)
    x1, qc = _merge(x2d, o.reshape(t, -1), siga, gbob, w, wide)
    return x1, qc, ckv, kr[:, ROPE_LO:ROPE_LO + QK_ROPE], vg


def _tail(x1, qc, mk, mv, w, nb, seq, tm):
    t, d = x1.shape
    oc = _cross(qc.reshape(nb, seq, -1), mk, mv, min(seq, tm))
    x2, h, q_route = _cross_out(x1, oc.reshape(t, -1), w, min(WIDE_ROW_TILE, t))
    return _peer(h, x2, q_route, w, PEER_ROWS, PEER_EXPERTS)


def kernel(x_prompt, x_sample, cache_mla_ckv, cache_mla_krope, cache_mem_k, cache_mem_v, mem_prompt, g_mix, w_in, g_q_lat, w_uq, g_qn, g_qr, g_kv_lat, g_kr, w_uk, w_uv, g_kn, w_oa, g_gm, w_s, b_s, w_ob, w_o, g_xattn, g_mem, w_cq, g_cq, w_ck, g_ck, w_cv, w_co, g_ffn, w_pq, sub_keys, peer_u, peer_v):
    params = dict(g_mix=g_mix, w_in=w_in, g_q_lat=g_q_lat, w_uq=w_uq, g_qn=g_qn, g_qr=g_qr,
                  g_kv_lat=g_kv_lat, g_kr=g_kr, w_uk=w_uk, w_uv=w_uv, g_kn=g_kn, w_oa=w_oa, g_gm=g_gm,
                  w_s=w_s, b_s=b_s, w_ob=w_ob, w_o=w_o, g_xattn=g_xattn, g_mem=g_mem, w_cq=w_cq,
                  g_cq=g_cq, w_ck=w_ck, g_ck=g_ck, w_cv=w_cv, w_co=w_co, g_ffn=g_ffn, w_pq=w_pq,
                  sub_keys=sub_keys, peer_u=peer_u, peer_v=peer_v)
    bp, sp, d = x_prompt.shape
    bs, ss, _ = x_sample.shape
    depth = w_in.shape[0]
    past_len = cache_mla_ckv.shape[2]
    n_mem = mem_prompt.shape[1]
    tm = ROW_TILE
    assert (bp * sp) % tm == 0 and (bs * ss) % tm == 0 and sp % ATTN_TILE == 0
    pos_p = jnp.arange(sp)
    pos_s = past_len + jnp.arange(ss)
    xp = x_prompt.reshape(bp * sp, d)
    xs = x_sample.reshape(bs * ss, d)
    outs = [[] for _ in range(7)]
    for i in range(depth):
        w = _prep_weights(params, i)
        xp1, qcp, ckv_p, kr_p, _ = _token_mixer(xp, w, sp, pos_p, None, tm)
        xs1, qcs, ckv_s, kr_s, vg_s = _token_mixer(xs, w, ss, pos_s, (cache_mla_ckv[i], cache_mla_krope[i]), tm)
        mk_p, mv_p = _mem_kv(mem_prompt.reshape(bp * n_mem, d), w, min(tm, bp * n_mem))
        xp = _tail(xp1, qcp, mk_p.reshape(bp, n_mem, -1), mv_p.reshape(bp, n_mem, -1), w, bp, sp, tm)
        xs = _tail(xs1, qcs, cache_mem_k[i].reshape(bs, n_mem, -1), cache_mem_v[i].reshape(bs, n_mem, -1),
                   w, bs, ss, tm)
        outs[0].append(ckv_p.reshape(bp, sp, -1))
        outs[1].append(kr_p.reshape(bp, sp, -1))
        outs[2].append(mk_p.reshape(bp, n_mem, MEM_HEADS, MEM_HEAD_DIM))
        outs[3].append(mv_p.reshape(bp, n_mem, MEM_HEADS, MEM_HEAD_DIM))
        outs[4].append(ckv_s.reshape(bs, ss, -1))
        outs[5].append(kr_s.reshape(bs, ss, -1))
        outs[6].append(vg_s.reshape(bs, ss, -1))
    return (xp.reshape(bp, sp, d), xs.reshape(bs, ss, d)) + tuple(jnp.stack(o) for o in outs)
```

```python
import functools

import jax
import jax.numpy as jnp
import numpy as np
from jax import lax
from jax.experimental import pallas as pl
from jax.experimental.pallas import tpu as pltpu

CHUNK = 64
EPS = 1e-6
MLA_HEADS = 16
QK_NOPE = 64
QK_ROPE = 32
V_HEAD = 64
ROPE_THETA = 10000.0
MLA_SCALE = (QK_NOPE + QK_ROPE) ** -0.5
GM_CHUNK = 128
GM_GROUPS = 8
MEM_HEADS = 4
MEM_HEAD_DIM = 128
MEM_SCALE = MEM_HEAD_DIM ** -0.5
PEER_HEADS = 8
N_KEYS = 128
PEER_TOPK = 16
PEER_HALF = 128

LANES = 128
SLOT = LANES
ROPE_LO = QK_NOPE
ROPE_HALF = QK_ROPE // 2
ROW_TILE = 512
WIDE_ROW_TILE = 1024
ATTN_TILE = 512
LOG2_E = float(np.log2(np.e))
STATIC_SHIFT_LIMIT = 48.0
BOUND_MARGIN = 1.02
PEER_ROWS = 512
PEER_EXPERTS = 1024
PEER_CHUNK = 512
MASK_PITCH = N_KEYS + 8
MASK_UNROLL = 32
VMEM_LIMIT = 48 * 1024 * 1024
PEER_VMEM_LIMIT = 60 * 1024 * 1024

F32 = jnp.float32
BF16 = jnp.bfloat16
_NT = (((1,), (1,)), ((), ()))


def _dot(a, b):
    return jnp.dot(a, b, preferred_element_type=F32)


def _dot_nt(a, b):
    return lax.dot_general(a, b, _NT, preferred_element_type=F32)


def _rms(xf, g):
    return xf * lax.rsqrt(jnp.mean(xf * xf, axis=-1, keepdims=True) + EPS) * g


def _gelu(x):
    return 0.5 * x * (1.0 + lax.erf(x * np.float32(np.sqrt(0.5))))


def _group_mean(sq, m_ref):
    hi = sq.astype(BF16)
    lo = (sq - hi.astype(F32)).astype(BF16)
    return _dot(jnp.concatenate([hi, lo], axis=1), m_ref[...])


def _rope_slot(y, cos, sin_up, sin_dn):
    return (y * cos + pltpu.roll(y, ROPE_HALF, 1) * sin_up
            + pltpu.roll(y, SLOT - ROPE_HALF, 1) * sin_dn)


def _const_spec(shape):
    nd = len(shape)
    return pl.BlockSpec(shape, lambda *_: (0,) * nd, pipeline_mode=pl.Buffered(1))


def _row_spec(tm, width):
    return pl.BlockSpec((tm, width), lambda i: (i, 0))


def _params(sem, limit=VMEM_LIMIT):
    return pltpu.CompilerParams(dimension_semantics=sem, vmem_limit_bytes=limit)


def _in_proj_kernel(x_ref, gmix_ref, wcq_ref, wckv_ref, wkr_ref, wu_ref, wv_ref, wga_ref, wgb_ref,
                    gq_ref, gkv_ref, gkr_ref, ggm_ref, cos_ref, sup_ref, sdn_ref,
                    cq_out, ckv_out, kr_out, u_out, vg_out, siga_out, sigb_out):
    h = _rms(x_ref[...], gmix_ref[...]).astype(BF16)
    cq_out[...] = _rms(_dot(h, wcq_ref[...]), gq_ref[...]).astype(BF16)
    ckv_out[...] = _rms(_dot(h, wckv_ref[...]), gkv_ref[...])
    kr = _dot(h, wkr_ref[...])
    ms = jnp.sum(kr * kr, axis=-1, keepdims=True) * (1.0 / QK_ROPE)
    kr = kr * lax.rsqrt(ms + EPS) * gkr_ref[...]
    kr_out[...] = _rope_slot(kr, cos_ref[...], sup_ref[...], sdn_ref[...])
    u_out[...] = _gelu(_dot(h, wu_ref[...])).astype(BF16)
    vg_out[...] = _rms(_gelu(_dot(h, wv_ref[...])), ggm_ref[...])
    siga_out[...] = jax.nn.sigmoid(_dot(h, wga_ref[...])).astype(BF16)
    sigb_out[...] = jax.nn.sigmoid(_dot(h, wgb_ref[...])).astype(BF16)


def _in_proj(x2d, w, tabs, tm):
    t, d = x2d.shape
    cos, sup, sdn = tabs
    ntab = cos.shape[0] // tm
    tab_spec = pl.BlockSpec((tm, SLOT), lambda i: (i % ntab, 0))
    consts = [w['g_mix'], w['w_cq'], w['w_ckv'], w['w_kr'], w['w_u'], w['w_v'], w['w_ga'], w['w_gb'],
              w['g_q_lat'], w['g_kv_lat'], w['g_kr'], w['g_gm']]
    gw = w['w_u'].shape[1]
    widths = [w['w_cq'].shape[1], w['w_ckv'].shape[1], SLOT, gw, gw, d, d]
    dtypes = [BF16, F32, F32, BF16, F32, BF16, BF16]
    return pl.pallas_call(
        _in_proj_kernel,
        grid=(t // tm,),
        in_specs=[_row_spec(tm, d)] + [_const_spec(c.shape) for c in consts] + [tab_spec] * 3,
        out_specs=[_row_spec(tm, n) for n in widths],
        out_shape=[jax.ShapeDtypeStruct((t, n), dt) for n, dt in zip(widths, dtypes)],
        compiler_params=_params(("parallel",), PEER_VMEM_LIMIT),
        name="in_proj",
    )(x2d, *consts, cos, sup, sdn)


def _gmlp_kernel(u_ref, vg_ref, sigb_ref, mix_ref, bias_ref, wob_ref, out_ref):
    vgb = vg_ref[...].astype(BF16)
    parts = []
    for g in range(GM_GROUPS):
        sl = slice(g * LANES, (g + 1) * LANES)
        mixed = _dot(mix_ref[g], vgb[:, sl]) + bias_ref[:, sl]
        parts.append((u_ref[:, sl].astype(F32) * mixed).astype(BF16))
    ob = _dot(jnp.concatenate(parts, axis=1), wob_ref[...])
    out_ref[...] = sigb_ref[...].astype(F32) * ob


def _gmlp(u, vg, sigb, mix, bias, w_ob, tm):
    t, gw = u.shape
    d = w_ob.shape[1]
    return pl.pallas_call(
        _gmlp_kernel,
        grid=(t // tm,),
        in_specs=[_row_spec(tm, gw), _row_spec(tm, gw), _row_spec(tm, d),
                  _const_spec(mix.shape), _const_spec(bias.shape), _const_spec(w_ob.shape)],
        out_specs=_row_spec(tm, d),
        out_shape=jax.ShapeDtypeStruct((t, d), F32),
        compiler_params=_params(("parallel",)),
        name="gmlp",
    )(u, vg, sigb, mix, bias, w_ob)


def _q_proj_kernel(cq_ref, wuq_ref, gq_ref, mavg_ref, cos_ref, sup_ref, sdn_ref, q_out):
    q = _dot(cq_ref[...], wuq_ref[...])
    cos, sup, sdn = cos_ref[...], sup_ref[...], sdn_ref[...]
    for h in range(MLA_HEADS):
        sl = slice(h * SLOT, (h + 1) * SLOT)
        qs = q[:, sl]
        y = qs * lax.rsqrt(_group_mean(qs * qs, mavg_ref) + EPS) * gq_ref[...]
        q_out[:, sl] = _rope_slot(y, cos, sup, sdn).astype(BF16)


def _q_proj(cq, w, tabs, tm):
    t, ql = cq.shape
    cos, sup, sdn = tabs
    ntab = cos.shape[0] // tm
    tab_spec = pl.BlockSpec((tm, SLOT), lambda i: (i % ntab, 0))
    n = MLA_HEADS * SLOT
    return pl.pallas_call(
        _q_proj_kernel,
        grid=(t // tm,),
        in_specs=[_row_spec(tm, ql), _const_spec(w['w_uq'].shape), _const_spec(w['g_q'].shape),
                  _const_spec(w['m_avg'].shape)] + [tab_spec] * 3,
        out_specs=_row_spec(tm, n),
        out_shape=jax.ShapeDtypeStruct((t, n), BF16),
        compiler_params=_params(("parallel",)),
        name="q_proj",
    )(cq, w['w_uq'], w['g_q'], w['m_avg'], cos, sup, sdn)


def _kv_proj_kernel(ckv_ref, kr_ref, wuk_ref, wuv_ref, gkn_ref, mavg_ref, k_out, v_out):
    c = ckv_ref[...].astype(BF16)
    kn = _dot(c, wuk_ref[...])
    kr = kr_ref[...]
    for h in range(MLA_HEADS):
        sl = slice(h * SLOT, (h + 1) * SLOT)
        ks = kn[:, sl]
        y = ks * lax.rsqrt(_group_mean(ks * ks, mavg_ref) + EPS) * gkn_ref[...]
        k_out[:, sl] = (y + kr).astype(BF16)
    v_out[...] = _dot(c, wuv_ref[...]).astype(BF16)


def _kv_proj(ckv, kr, w, tm):
    t, kl = ckv.shape
    n = MLA_HEADS * SLOT
    return pl.pallas_call(
        _kv_proj_kernel,
        grid=(t // tm,),
        in_specs=[_row_spec(tm, kl), _row_spec(tm, SLOT), _const_spec(w['w_uk'].shape),
                  _const_spec(w['w_uv'].shape), _const_spec(w['g_kn'].shape),
                  _const_spec(w['m_avg'].shape)],
        out_specs=[_row_spec(tm, n), _row_spec(tm, n)],
        out_shape=[jax.ShapeDtypeStruct((t, n), BF16)] * 2,
        compiler_params=_params(("parallel",)),
        name="kv_proj",
    )(ckv, kr, w['w_uk'], w['w_uv'], w['g_kn'], w['m_avg'])


def _attn_prompt_kernel(bound_ref, q_ref, k_ref, v_ref, o_ref, *, tile):
    n_tiles = q_ref.shape[1] // tile
    bound = bound_ref[0]
    first = lax.broadcasted_iota(jnp.int32, (1, LANES), 1) < V_HEAD
    row_chunk = lax.broadcasted_iota(jnp.int32, (tile, tile), 0) // CHUNK
    col_chunk = lax.broadcasted_iota(jnp.int32, (tile, tile), 1) // CHUNK
    visible = col_chunk <= row_chunk

    def static_step(q, kstart, carry, masked):
        sums, acc = carry
        kstart = pl.multiple_of(kstart, tile)
        k = k_ref[0, pl.ds(kstart, tile), :]
        v = v_ref[0, pl.ds(kstart, tile), :]
        new_sums = []
        for h in range(2):
            sl = slice(h * SLOT, (h + 1) * SLOT)
            s = _dot_nt(q[:, sl], k[:, sl]) * F32(MLA_SCALE * LOG2_E) - bound
            if masked:
                s = jnp.where(visible, s, F32(-1e30))
            p = jnp.exp2(s)
            part = sums[h]
            for c in range(tile // LANES):
                part = part + p[:, c * LANES:(c + 1) * LANES]
            new_sums.append(part)
            acc = acc + _dot(p.astype(BF16), v[:, sl])
        return tuple(new_sums), acc

    def static_tile(qi, _):
        rows = pl.ds(pl.multiple_of(qi * tile, tile), tile)
        q = q_ref[0, rows, :]
        zeros = jnp.zeros((tile, LANES), F32)

        def pair(i, c):
            c = static_step(q, 2 * i * tile, c, False)
            return static_step(q, (2 * i + 1) * tile, c, False)

        def odd_tail(c):
            return static_step(q, qi * tile, static_step(q, (qi - 1) * tile, c, False), True)

        carry = lax.fori_loop(0, qi // 2, pair, ((zeros, zeros), zeros))
        sums, acc = lax.cond(qi % 2 == 1, odd_tail, lambda c: static_step(q, qi * tile, c, True), carry)
        l0 = jnp.sum(sums[0], axis=-1, keepdims=True)
        l1 = jnp.sum(sums[1], axis=-1, keepdims=True)
        o_ref[0, rows, :] = (acc / jnp.where(first, l0, l1)).astype(BF16)
        return 0

    @pl.when(bound < STATIC_SHIFT_LIMIT)
    def _():
        lax.fori_loop(0, n_tiles, static_tile, 0)

    def step(q, kstart, carry, masked):
        ms, ls, acc = carry
        kstart = pl.multiple_of(kstart, tile)
        k = k_ref[0, pl.ds(kstart, tile), :]
        v = v_ref[0, pl.ds(kstart, tile), :]
        new_ms, new_ls, alphas, pvs = [], [], [], []
        for h in range(2):
            sl = slice(h * SLOT, (h + 1) * SLOT)
            s = _dot_nt(q[:, sl], k[:, sl]) * F32(MLA_SCALE * LOG2_E)
            if masked:
                s = jnp.where(visible, s, F32(-1e30))
            m_new = jnp.maximum(ms[h], jnp.max(s, axis=-1, keepdims=True))
            alpha = jnp.exp2(ms[h] - m_new)
            p = jnp.exp2(s - m_new)
            new_ls.append(alpha * ls[h] + jnp.sum(p, axis=-1, keepdims=True))
            new_ms.append(m_new)
            alphas.append(alpha)
            pvs.append(_dot(p.astype(BF16), v[:, sl]))
        acc = acc * jnp.where(first, alphas[0], alphas[1]) + pvs[0] + pvs[1]
        return tuple(new_ms), tuple(new_ls), acc

    def online_tile(qi, _):
        rows = pl.ds(pl.multiple_of(qi * tile, tile), tile)
        q = q_ref[0, rows, :]
        neg = jnp.full((tile, 1), -1e30, F32)
        zero = jnp.zeros((tile, 1), F32)
        init = ((neg, neg), (zero, zero), jnp.zeros((tile, LANES), F32))
        carry = lax.fori_loop(0, qi, lambda i, c: step(q, i * tile, c, False), init)
        _, ls, acc = step(q, qi * tile, carry, True)
        o_ref[0, rows, :] = (acc / jnp.where(first, ls[0], ls[1])).astype(BF16)
        return 0

    @pl.when(bound >= STATIC_SHIFT_LIMIT)
    def _():
        lax.fori_loop(0, n_tiles, online_tile, 0)


def _score_bound(w):
    gmax = lambda g, lo, n: jnp.max(jnp.abs(g[0, lo:lo + n]))
    nope = QK_NOPE * gmax(w['g_q'], 0, QK_NOPE) * gmax(w['g_kn'], 0, QK_NOPE)
    rope = QK_ROPE * gmax(w['g_q'], ROPE_LO, QK_ROPE) * gmax(w['g_kr'], ROPE_LO, QK_ROPE)
    return (BOUND_MARGIN * MLA_SCALE * LOG2_E * (nope + rope)).reshape(1).astype(F32)


def _attn_prompt(q, k, v, bound, tile):
    b, s, _ = q.shape
    pairs = MLA_HEADS // 2
    pair_block = pl.BlockSpec((1, s, 2 * SLOT), lambda bi, p: (bi, 0, p))
    return pl.pallas_call(
        functools.partial(_attn_prompt_kernel, tile=tile),
        grid=(b, pairs),
        in_specs=[pl.BlockSpec(memory_space=pltpu.SMEM), pair_block, pair_block, pair_block],
        out_specs=pl.BlockSpec((1, s, LANES), lambda bi, p: (bi, 0, p)),
        out_shape=jax.ShapeDtypeStruct((b, s, pairs * LANES), BF16),
        compiler_params=_params(("parallel", "parallel")),
        name="attn_prompt",
    )(bound, q, k, v)


def _attn_sample_kernel(q_ref, pckv_ref, pkr_ref, nckv_ref, nkr_ref, wuk_ref, wuv_ref, gkn_ref, place_ref,
                        o_ref):
    sq = q_ref.shape[1]
    rows = MLA_HEADS * sq
    gkn = gkn_ref[...]

    def keys_values(ckv, kr_slot):
        c = ckv.astype(BF16)
        kn = _dot(c, wuk_ref[...])
        parts = []
        for h in range(MLA_HEADS):
            ks = kn[:, h * SLOT:(h + 1) * SLOT]
            ms = jnp.sum(ks * ks, axis=-1, keepdims=True) * (1.0 / QK_NOPE)
            parts.append((ks * lax.rsqrt(ms + EPS) * gkn + kr_slot).astype(BF16))
        return jnp.concatenate(parts, axis=1), _dot(c, wuv_ref[...]).astype(BF16)

    past_kr = _dot(pkr_ref[0].astype(BF16), place_ref[...])
    k_p, v_p = keys_values(pckv_ref[0], past_kr)
    k_n, v_n = keys_values(nckv_ref[0], nkr_ref[0])
    k = jnp.concatenate([k_p, k_n], axis=0)
    v = jnp.concatenate([v_p, v_n], axis=0)

    qt = jnp.concatenate([q_ref[0].astype(F32)] * MLA_HEADS, axis=0)
    q_shape = (rows, MLA_HEADS * SLOT)
    own = (lax.broadcasted_iota(jnp.int32, q_shape, 0) // sq
           == lax.broadcasted_iota(jnp.int32, q_shape, 1) // SLOT)
    qbd = jnp.where(own, qt, 0.0).astype(BF16)
    s = _dot_nt(qbd, k) * MLA_SCALE
    p = jnp.exp(s - jnp.max(s, axis=-1, keepdims=True))
    p = p / jnp.sum(p, axis=-1, keepdims=True)
    o_all = _dot(p.astype(BF16), v)
    o_shape = o_all.shape
    own = (lax.broadcasted_iota(jnp.int32, o_shape, 0) // sq
           == lax.broadcasted_iota(jnp.int32, o_shape, 1) // V_HEAD)
    o_all = jnp.where(own, o_all, 0.0)
    o = o_all[0:sq]
    for h in range(1, MLA_HEADS):
        o = o + o_all[h * sq:(h + 1) * sq]
    o_ref[0] = o.astype(BF16)


def _attn_sample(q, past_ckv, past_kr, new_ckv, new_kr, w):
    b, sq, nq = q.shape
    npast, kl = past_ckv.shape[1:]
    nv = w['w_uv_plain'].shape[1]
    stream = lambda n, width: pl.BlockSpec((1, n, width), lambda bi: (bi, 0, 0))
    consts = [w['w_uk'], w['w_uv_plain'], w['g_kn'], w['rope_place']]
    return pl.pallas_call(
        _attn_sample_kernel,
        grid=(b,),
        in_specs=[stream(sq, nq), stream(npast, kl), stream(npast, QK_ROPE), stream(sq, kl), stream(sq, SLOT)]
        + [_const_spec(c.shape) for c in consts],
        out_specs=stream(sq, nv),
        out_shape=jax.ShapeDtypeStruct((b, sq, nv), BF16),
        compiler_params=_params(("parallel",)),
        name="attn_sample",
    )(q, past_ckv, past_kr, new_ckv, new_kr, *consts)


def _merge_kernel(x_ref, o_ref, siga_ref, gbob_ref, woa_ref, wo_ref, gx_ref, wcq_ref, gcq_ref,
                  x1_out, qc_out):
    oa = _dot(o_ref[...], woa_ref[...])
    merged = siga_ref[...].astype(F32) * oa + gbob_ref[...]
    x1 = x_ref[...] + _dot(merged.astype(BF16), wo_ref[...])
    x1_out[...] = x1
    qc = _dot(_rms(x1, gx_ref[...]).astype(BF16), wcq_ref[...])
    for h in range(MEM_HEADS):
        sl = slice(h * MEM_HEAD_DIM, (h + 1) * MEM_HEAD_DIM)
        qc_out[:, sl] = _rms(qc[:, sl], gcq_ref[...]).astype(BF16)


def _merge(x2d, o, siga, gbob, w, tm):
    t, d = x2d.shape
    consts = [w['w_oa'], w['w_o'], w['g_xattn'], w['w_cq_mem'], w['g_cq']]
    nq = w['w_cq_mem'].shape[1]
    return pl.pallas_call(
        _merge_kernel,
        grid=(t // tm,),
        in_specs=[_row_spec(tm, d), _row_spec(tm, o.shape[1]), _row_spec(tm, d), _row_spec(tm, d)]
        + [_const_spec(c.shape) for c in consts],
        out_specs=[_row_spec(tm, d), _row_spec(tm, nq)],
        out_shape=[jax.ShapeDtypeStruct((t, d), F32), jax.ShapeDtypeStruct((t, nq), BF16)],
        compiler_params=_params(("parallel",)),
        name="merge",
    )(x2d, o, siga, gbob, *consts)


def _mem_kv_kernel(mem_ref, gmem_ref, wck_ref, gck_ref, wcv_ref, k_out, v_out):
    m = _rms(mem_ref[...], gmem_ref[...]).astype(BF16)
    k = _dot(m, wck_ref[...])
    for h in range(MEM_HEADS):
        sl = slice(h * MEM_HEAD_DIM, (h + 1) * MEM_HEAD_DIM)
        k_out[:, sl] = _rms(k[:, sl], gck_ref[...])
    v_out[...] = _dot(m, wcv_ref[...])


def _mem_kv(mem2d, w, tm):
    t, d = mem2d.shape
    consts = [w['g_mem'], w['w_ck'], w['g_ck'], w['w_cv']]
    n = w['w_ck'].shape[1]
    return pl.pallas_call(
        _mem_kv_kernel,
        grid=(t // tm,),
        in_specs=[_row_spec(tm, d)] + [_const_spec(c.shape) for c in consts],
        out_specs=[_row_spec(tm, n)] * 2,
        out_shape=[jax.ShapeDtypeStruct((t, n), F32)] * 2,
        compiler_params=_params(("parallel",)),
        name="mem_kv",
    )(mem2d, *consts)


def _cross_kernel(q_ref, k_ref, v_ref, o_ref):
    q = q_ref[0]
    k = k_ref[0].astype(BF16)
    v = v_ref[0].astype(BF16)
    for h in range(MEM_HEADS):
        sl = slice(h * MEM_HEAD_DIM, (h + 1) * MEM_HEAD_DIM)
        s = _dot_nt(q[:, sl], k[:, sl]) * MEM_SCALE
        p = jnp.exp(s - jnp.max(s, axis=-1, keepdims=True))
        p = p / jnp.sum(p, axis=-1, keepdims=True)
        o_ref[0, :, sl] = _dot(p.astype(BF16), v[:, sl]).astype(BF16)


def _cross(qc, mk, mv, tc):
    b, s, n = qc.shape
    nm = mk.shape[1]
    return pl.pallas_call(
        _cross_kernel,
        grid=(b, s // tc),
        in_specs=[pl.BlockSpec((1, tc, n), lambda bi, i: (bi, i, 0)),
                  pl.BlockSpec((1, nm, n), lambda bi, i: (bi, 0, 0)),
                  pl.BlockSpec((1, nm, n), lambda bi, i: (bi, 0, 0))],
        out_specs=pl.BlockSpec((1, tc, n), lambda bi, i: (bi, i, 0)),
        out_shape=jax.ShapeDtypeStruct((b, s, n), BF16),
        compiler_params=_params(("parallel", "parallel")),
        name="cross_attn",
    )(qc, mk, mv)


def _cross_out_kernel(x1_ref, oc_ref, wco_ref, gffn_ref, wpq_ref, x2_out, h_out, q_out):
    x2 = x1_ref[...] + _dot(oc_ref[...], wco_ref[...])
    x2_out[...] = x2
    h = _rms(x2, gffn_ref[...]).astype(BF16)
    h_out[...] = h
    q_out[...] = _dot(h, wpq_ref[...]).astype(BF16)


def _cross_out(x1, oc, w, tm):
    t, d = x1.shape
    nq = w['w_pq'].shape[1]
    return pl.pallas_call(
        _cross_out_kernel,
        grid=(t // tm,),
        in_specs=[_row_spec(tm, d), _row_spec(tm, oc.shape[1]), _const_spec(w['w_co'].shape),
                  _const_spec(w['g_ffn'].shape), _const_spec(w['w_pq'].shape)],
        out_specs=[_row_spec(tm, d), _row_spec(tm, d), _row_spec(tm, nq)],
        out_shape=[jax.ShapeDtypeStruct((t, d), F32), jax.ShapeDtypeStruct((t, d), BF16),
                   jax.ShapeDtypeStruct((t, nq), BF16)],
        compiler_params=_params(("parallel",)),
        name="cross_out",
    )(x1, oc, w['w_co'], w['g_ffn'], w['w_pq'])


def _pair_blocks():
    k = PEER_TOPK
    blocks, flat = [], []
    n_row = sum(1 for a in range(k) if k // (a + 1) > 2)
    for a in range(n_row):
        nb = k if a == 0 else 8
        blocks.append(('row', a, nb))
        flat += [a * k + b if (a + 1) * (b + 1) <= k else -1 for b in range(nb)]
    for b in range(k // (n_row + 1)):
        na = k if b == 0 else 8
        blocks.append(('col', b, na))
        flat += [a * k + b if (a >= n_row and (a + 1) * (b + 1) <= k) else -1 for a in range(na)]
    return blocks, np.asarray(flat, np.float32)


def _pair_candidates(v0, v1, blocks, combine):
    parts = []
    for kind, fixed, n in blocks:
        if kind == 'row':
            parts.append(combine(v0[fixed:fixed + 1, :], v1[0:n, :]))
        else:
            parts.append(combine(v0[0:n, :], v1[fixed:fixed + 1, :]))
    return jnp.concatenate(parts, axis=0)


def _peer_kernel(h_ref, x_ref, q_ref, keys_ref, flat_ref, u_ref, v_ref, out_ref,
                 mask_scr, sv_scr, si_scr, ts_scr, te_scr, it_scr, jt_scr, gt_scr,
                 isel_scr, jsel_scr, gsel_scr):
    tm = h_ref.shape[0]
    te = v_ref.shape[0]
    r = pl.program_id(0)
    e = pl.program_id(1)
    n = pl.num_programs(0) - 2
    rows_per_step = te // N_KEYS
    tokens_per_head = tm // PEER_HEADS
    blocks, _ = _pair_blocks()
    neg_inf = F32(-jnp.inf)
    key_rows = lax.broadcasted_iota(jnp.int32, (N_KEYS, LANES), 0).astype(F32)
    swept = lax.rem(r, 2)
    ranked = 1 - swept

    def mask_token(t):
        irow = isel_scr[swept, pl.ds(t, 1), :]
        jrow = jsel_scr[swept, pl.ds(t, 1), :]
        grow = gsel_scr[swept, pl.ds(t, 1), :]
        rw = jnp.where(key_rows == irow, 0.5 * grow, 0.0).astype(BF16)
        cw = jnp.where(key_rows == jrow, 1.0, 0.0).astype(BF16)
        r0 = pl.multiple_of(t * MASK_PITCH, 8)
        mask_scr[pl.ds(r0, N_KEYS), :] = _dot_nt(rw, cw)

    def rank_pairs(hd):
        sv0, sv1 = sv_scr[2 * hd], sv_scr[2 * hd + 1]
        si0, si1 = si_scr[2 * hd], si_scr[2 * hd + 1]
        flat = flat_ref[...]
        cand = _pair_candidates(sv0, sv1, blocks, lambda x, y: x + y)
        cand = jnp.where(flat >= 0.0, cand, neg_inf)
        cidx = _pair_candidates(si0, si1, blocks, lambda x, y: x * F32(N_KEYS) + y)
        for kk in range(PEER_TOPK):
            m = jnp.max(cand, axis=0, keepdims=True)
            pos = jnp.min(jnp.where(cand == m, flat, F32(PEER_TOPK * PEER_TOPK)), axis=0, keepdims=True)
            sel = flat == pos
            ts_scr[kk:kk + 1, :] = m
            te_scr[kk:kk + 1, :] = jnp.max(jnp.where(sel, cidx, F32(-1.0)), axis=0, keepdims=True)
            cand = jnp.where(sel, neg_inf, cand)
        top = ts_scr[...]
        ex = jnp.exp(top - top[0:1, :])
        gate = ex / jnp.sum(ex, axis=0, keepdims=True)
        eidx = te_scr[...]
        irow = jnp.floor(eidx * F32(1.0 / N_KEYS))
        r0 = pl.multiple_of(hd * PEER_TOPK, PEER_TOPK)
        it_scr[pl.ds(r0, PEER_TOPK), :] = irow
        jt_scr[pl.ds(r0, PEER_TOPK), :] = eidx - irow * F32(N_KEYS)
        gt_scr[pl.ds(r0, PEER_TOPK), :] = gate

    def publish_ranked():
        isel_scr[ranked] = it_scr[...].T
        jsel_scr[ranked] = jt_scr[...].T
        gsel_scr[ranked] = gt_scr[...].T

    @pl.when(e == 0)
    def _():
        out_ref[...] = x_ref[...]

        @pl.when(r == 1)
        def _():
            def head(hd, _):
                rank_pairs(hd)
                return 0

            lax.fori_loop(0, PEER_HEADS, head, 0)
            publish_ranked()

        @pl.when(jnp.logical_and(r >= 2, r <= n))
        def _():
            def trip(i, _):
                for t in range(tokens_per_head):
                    mask_token(i * tokens_per_head + t)
                rank_pairs(i)
                return 0

            lax.fori_loop(0, PEER_HEADS, trip, 0)
            publish_ranked()

        @pl.when(r == n + 1)
        def _():
            def trip(i, _):
                for t in range(MASK_UNROLL):
                    mask_token(i * MASK_UNROLL + t)
                return 0

            lax.fori_loop(0, tm // MASK_UNROLL, trip, 0)

    def slot_scores():
        d0 = pl.multiple_of(e * PEER_HALF, PEER_HALF)
        return _dot_nt(q_ref[:, pl.ds(d0, PEER_HALF)], keys_ref[e])

    def rank_slot(scores):
        lane = lax.broadcasted_iota(jnp.int32, (tm, N_KEYS), 1)
        picked = jnp.zeros((tm, N_KEYS), jnp.int32)
        a = scores
        for kk in range(PEER_TOPK):
            idx = jnp.argmax(a, axis=-1, keepdims=True).astype(jnp.int32)
            a = jnp.where(lane == idx, neg_inf, a)
            picked = jnp.where(lane == kk, idx, picked)
        vals = jnp.take_along_axis(scores, picked, axis=1)
        sv_scr[e] = vals.T[0:PEER_TOPK, :]
        si_scr[e] = (N_KEYS - 1 - picked).astype(F32).T[0:PEER_TOPK, :]

    def expert_chunk():
        h = h_ref[...]
        row0 = e * rows_per_step
        weighted = []
        for c in range(te // PEER_CHUNK):
            rows = slice(c * PEER_CHUNK, (c + 1) * PEER_CHUNK)
            pre = _dot_nt(h, u_ref[rows, :])
            half_mask = jnp.concatenate(
                [mask_scr[pl.ds(row0 + c * (PEER_CHUNK // N_KEYS) + rr, tm, stride=MASK_PITCH), :]
                 for rr in range(PEER_CHUNK // N_KEYS)], axis=1)
            weighted.append((pre * (1.0 + lax.erf(pre * np.float32(np.sqrt(0.5)))) * half_mask).astype(BF16))
        out_ref[...] += _dot(jnp.concatenate(weighted, axis=1), v_ref[...])

    ranks = r < n
    sweeps = r >= 2

    @pl.when(jnp.logical_and(ranks, jnp.logical_not(sweeps)))
    def _():
        rank_slot(slot_scores())

    @pl.when(jnp.logical_and(ranks, sweeps))
    def _():
        a = slot_scores()
        expert_chunk()
        rank_slot(a)

    @pl.when(jnp.logical_and(jnp.logical_not(ranks), sweeps))
    def _():
        expert_chunk()


def _peer(h, x2, q_route, w, tm, te):
    t, d = h.shape
    n = t // tm
    nexp = w['peer_v'].shape[0]
    assert nexp // te == 2 * PEER_HEADS, "one half-key slot is ranked per expert step"
    assert tm % PEER_HEADS == 0 and tm % MASK_UNROLL == 0
    nsel = PEER_HEADS * PEER_TOPK
    _, flat = _pair_blocks()
    flat = jnp.asarray(np.repeat(flat[:, None], tm, axis=1))
    swept_tile = lambda r, e: (jnp.clip(r - 2, 0, n - 1), 0)
    dense_tile = lambda width: pl.BlockSpec((tm, width), swept_tile, pipeline_mode=pl.Buffered(1))
    expert_tile = pl.BlockSpec((te, d), lambda r, e: (jnp.where(r >= 2, e, 0), 0))
    return pl.pallas_call(
        _peer_kernel,
        grid=(n + 2, nexp // te),
        in_specs=[dense_tile(d), dense_tile(d),
                  pl.BlockSpec((tm, q_route.shape[1]), lambda r, e: (jnp.minimum(r, n - 1), 0),
                               pipeline_mode=pl.Buffered(1)),
                  _const_spec(w['sub_keys'].shape), _const_spec(flat.shape), expert_tile, expert_tile],
        out_specs=pl.BlockSpec((tm, d), swept_tile),
        out_shape=jax.ShapeDtypeStruct((t, d), F32),
        scratch_shapes=[pltpu.VMEM((tm * MASK_PITCH, LANES), F32),
                        pltpu.VMEM((2 * PEER_HEADS, PEER_TOPK, tm), F32),
                        pltpu.VMEM((2 * PEER_HEADS, PEER_TOPK, tm), F32),
                        pltpu.VMEM((PEER_TOPK, tm), F32),
                        pltpu.VMEM((PEER_TOPK, tm), F32),
                        pltpu.VMEM((nsel, tm), F32),
                        pltpu.VMEM((nsel, tm), F32),
                        pltpu.VMEM((nsel, tm), F32),
                        pltpu.VMEM((2, tm, nsel), F32),
                        pltpu.VMEM((2, tm, nsel), F32),
                        pltpu.VMEM((2, tm, nsel), F32)],
        compiler_params=_params(("arbitrary", "arbitrary"), PEER_VMEM_LIMIT),
        name="peer",
    )(h, x2, q_route, w['sub_keys'], flat, w['peer_u'], w['peer_v'])


def _slot_gain(parts):
    row = jnp.zeros((SLOT,), F32)
    for off, g in parts:
        row = lax.dynamic_update_slice(row, g.astype(F32), (off,))
    return row[None, :]


def _group_avg_matrix():
    m = np.zeros((SLOT, SLOT), np.float32)
    m[:QK_NOPE, :QK_NOPE] = 1.0 / QK_NOPE
    m[ROPE_LO:ROPE_LO + QK_ROPE, ROPE_LO:ROPE_LO + QK_ROPE] = 1.0 / QK_ROPE
    return jnp.asarray(np.concatenate([m, m], axis=0), BF16)


def _rope_tables(pos):
    inv = ROPE_THETA ** (-jnp.arange(ROPE_HALF, dtype=F32) / ROPE_HALF)
    ang = pos.astype(F32)[:, None] * inv[None, :]
    cos, sin = jnp.cos(ang), jnp.sin(ang)
    n = pos.shape[0]
    tail = SLOT - ROPE_LO - QK_ROPE
    cos_t = jnp.concatenate([jnp.ones((n, ROPE_LO), F32), cos, cos, jnp.zeros((n, tail), F32)], axis=1)
    sin_up = jnp.concatenate([jnp.zeros((n, ROPE_LO + ROPE_HALF), F32), sin, jnp.zeros((n, tail), F32)], axis=1)
    sin_dn = jnp.concatenate([jnp.zeros((n, ROPE_LO), F32), -sin, jnp.zeros((n, tail + ROPE_HALF), F32)], axis=1)
    return cos_t, sin_up, sin_dn


def _prep_weights(p, i):
    w_in = p['w_in'][i]
    q_lora = p['g_q_lat'].shape[1]
    kv_lora = p['g_kv_lat'].shape[1]
    gw = p['g_gm'].shape[1]
    d = w_in.shape[0]
    o = np.cumsum([0, q_lora, kv_lora, QK_ROPE, gw, gw, d, d])
    seg = lambda k: w_in[:, o[k]:o[k + 1]]
    row = lambda a: a[i][None, :].astype(F32)
    w = {}
    w['g_mix'] = row(p['g_mix'])
    w['w_cq'] = seg(0).astype(BF16)
    w['w_ckv'] = seg(1).astype(BF16)
    w['w_kr'] = jnp.pad(seg(2), ((0, 0), (ROPE_LO, SLOT - ROPE_LO - QK_ROPE))).astype(BF16)
    w['w_u'] = seg(3).astype(BF16)
    w['w_v'] = seg(4).astype(BF16)
    w['w_ga'] = seg(5).astype(BF16)
    w['w_gb'] = seg(6).astype(BF16)
    w['g_q_lat'] = row(p['g_q_lat'])
    w['g_kv_lat'] = row(p['g_kv_lat'])
    w['g_kr'] = _slot_gain([(ROPE_LO, p['g_kr'][i])])
    w['g_gm'] = row(p['g_gm'])
    head_dim = QK_NOPE + QK_ROPE
    w_uq = p['w_uq'][i].reshape(q_lora, MLA_HEADS, head_dim)
    w['w_uq'] = jnp.pad(w_uq, ((0, 0), (0, 0), (0, SLOT - head_dim))).reshape(q_lora, -1).astype(BF16)
    w['g_q'] = _slot_gain([(0, p['g_qn'][i]), (ROPE_LO, p['g_qr'][i])])
    w['m_avg'] = _group_avg_matrix()
    w_uk = p['w_uk'][i].reshape(kv_lora, MLA_HEADS, QK_NOPE)
    w['w_uk'] = jnp.pad(w_uk, ((0, 0), (0, 0), (0, SLOT - QK_NOPE))).reshape(kv_lora, -1).astype(BF16)
    w['g_kn'] = _slot_gain([(0, p['g_kn'][i])])
    w_uv = p['w_uv'][i].reshape(kv_lora, MLA_HEADS // 2, 2, V_HEAD)
    eye2 = jnp.eye(2, dtype=w_uv.dtype)
    w['w_uv'] = jnp.einsum('cpjd,jk->cpjkd', w_uv, eye2).reshape(kv_lora, -1).astype(BF16)
    w['w_uv_plain'] = p['w_uv'][i].astype(BF16)
    place = np.zeros((QK_ROPE, SLOT), np.float32)
    place[np.arange(QK_ROPE), ROPE_LO + np.arange(QK_ROPE)] = 1.0
    w['rope_place'] = jnp.asarray(place, BF16)
    w['w_oa'] = p['w_oa'][i].astype(BF16)
    w['w_ob'] = p['w_ob'][i].astype(BF16)
    w['w_o'] = p['w_o'][i].astype(BF16)
    w['g_xattn'] = row(p['g_xattn'])
    w['g_mem'] = row(p['g_mem'])
    w['w_cq_mem'] = p['w_cq'][i].astype(BF16)
    w['g_cq'] = row(p['g_cq'])
    w['w_ck'] = p['w_ck'][i].astype(BF16)
    w['g_ck'] = row(p['g_ck'])
    w['w_cv'] = p['w_cv'][i].astype(BF16)
    w['w_co'] = p['w_co'][i].astype(BF16)
    w['g_ffn'] = row(p['g_ffn'])
    w['w_pq'] = p['w_pq'][i].astype(BF16)
    w['sub_keys'] = p['sub_keys'][i].reshape(2 * PEER_HEADS, N_KEYS, PEER_HALF)[:, ::-1, :].astype(BF16)
    w['peer_u'] = p['peer_u'][i].astype(BF16)
    w['peer_v'] = p['peer_v'][i].astype(BF16)
    w['w_s'] = p['w_s'][i]
    w['b_s'] = p['b_s'][i]
    return w


def _spatial_operands(w, seq, tm):
    chunk = min(seq, GM_CHUNK)
    reps = tm // chunk
    w_mask = jnp.tril(w['w_s'][:, :chunk, :chunk])
    w_mask = w_mask.astype(BF16)
    mix = jnp.concatenate(
        [jnp.pad(w_mask, ((0, 0), (0, 0), (a * chunk, (reps - 1 - a) * chunk))) for a in range(reps)], axis=1)
    bias = jnp.tile(w['b_s'][:, :chunk].T, (reps, 1))
    bias = jnp.repeat(bias, LANES, axis=1)
    return mix.astype(BF16), bias.astype(F32)


def _token_mixer(x2d, w, seq, pos, past, tm):
    t, d = x2d.shape
    nb = t // seq
    period = max(seq, tm)
    tabs = _rope_tables(jnp.tile(pos, period // seq))
    wide = min(WIDE_ROW_TILE, t)
    cq, ckv, kr, u, vg, siga, sigb = _in_proj(x2d, w, tabs, wide)
    mix, bias = _spatial_operands(w, seq, tm)
    gbob = _gmlp(u, vg, sigb, mix, bias, w['w_ob'], tm)
    q = _q_proj(cq, w, tabs, wide)
    if past is None:
        k, v = _kv_proj(ckv, kr, w, wide)
        o = _attn_prompt(q.reshape(nb, seq, -1), k.reshape(nb, seq, -1), v.reshape(nb, seq, -1),
                         _score_bound(w), ATTN_TILE)
    else:
        past_ckv, past_kr = past
        o = _attn_sample(q.reshape(nb, seq, -1), past_ckv, past_kr, ckv.reshape(nb, seq, -1),
                         kr.reshape(nb, seq, SLOT), w)
    x1, qc = _merge(x2d, o.reshape(t, -1), siga, gbob, w, wide)
    return x1, qc, ckv, kr[:, ROPE_LO:ROPE_LO + QK_ROPE], vg


def _tail(x1, qc, mk, mv, w, nb, seq, tm):
    t, d = x1.shape
    oc = _cross(qc.reshape(nb, seq, -1), mk, mv, min(seq, tm))
    x2, h, q_route = _cross_out(x1, oc.reshape(t, -1), w, min(WIDE_ROW_TILE, t))
    return _peer(h, x2, q_route, w, PEER_ROWS, PEER_EXPERTS)


def kernel(x_prompt, x_sample, cache_mla_ckv, cache_mla_krope, cache_mem_k, cache_mem_v, mem_prompt, g_mix, w_in, g_q_lat, w_uq, g_qn, g_qr, g_kv_lat, g_kr, w_uk, w_uv, g_kn, w_oa, g_gm, w_s, b_s, w_ob, w_o, g_xattn, g_mem, w_cq, g_cq, w_ck, g_ck, w_cv, w_co, g_ffn, w_pq, sub_keys, peer_u, peer_v):
    params = dict(g_mix=g_mix, w_in=w_in, g_q_lat=g_q_lat, w_uq=w_uq, g_qn=g_qn, g_qr=g_qr,
                  g_kv_lat=g_kv_lat, g_kr=g_kr, w_uk=w_uk, w_uv=w_uv, g_kn=g_kn, w_oa=w_oa, g_gm=g_gm,
                  w_s=w_s, b_s=b_s, w_ob=w_ob, w_o=w_o, g_xattn=g_xattn, g_mem=g_mem, w_cq=w_cq,
                  g_cq=g_cq, w_ck=w_ck, g_ck=g_ck, w_cv=w_cv, w_co=w_co, g_ffn=g_ffn, w_pq=w_pq,
                  sub_keys=sub_keys, peer_u=peer_u, peer_v=peer_v)
    bp, sp, d = x_prompt.shape
    bs, ss, _ = x_sample.shape
    depth = w_in.shape[0]
    past_len = cache_mla_ckv.shape[2]
    n_mem = mem_prompt.shape[1]
    tm = ROW_TILE
    assert (bp * sp) % tm == 0 and (bs * ss) % tm == 0 and sp % ATTN_TILE == 0
    pos_p = jnp.arange(sp)
    pos_s = past_len + jnp.arange(ss)
    xp = x_prompt.reshape(bp * sp, d)
    xs = x_sample.reshape(bs * ss, d)
    outs = [[] for _ in range(7)]
    for i in range(depth):
        w = _prep_weights(params, i)
        xp1, qcp, ckv_p, kr_p, _ = _token_mixer(xp, w, sp, pos_p, None, tm)
        xs1, qcs, ckv_s, kr_s, vg_s = _token_mixer(xs, w, ss, pos_s, (cache_mla_ckv[i], cache_mla_krope[i]), tm)
        mk_p, mv_p = _mem_kv(mem_prompt.reshape(bp * n_mem, d), w, min(tm, bp * n_mem))
        xp = _tail(xp1, qcp, mk_p.reshape(bp, n_mem, -1), mv_p.reshape(bp, n_mem, -1), w, bp, sp, tm)
        xs = _tail(xs1, qcs, cache_mem_k[i].reshape(bs, n_mem, -1), cache_mem_v[i].reshape(bs, n_mem, -1),
                   w, bs, ss, tm)
        outs[0].append(ckv_p.reshape(bp, sp, -1))
        outs[1].append(kr_p.reshape(bp, sp, -1))
        outs[2].append(mk_p.reshape(bp, n_mem, MEM_HEADS, MEM_HEAD_DIM))
        outs[3].append(mv_p.reshape(bp, n_mem, MEM_HEADS, MEM_HEAD_DIM))
        outs[4].append(ckv_s.reshape(bs, ss, -1))
        outs[5].append(kr_s.reshape(bs, ss, -1))
        outs[6].append(vg_s.reshape(bs, ss, -1))
    return (xp.reshape(bp, sp, d), xs.reshape(bs, ss, d)) + tuple(jnp.stack(o) for o in outs)
```

```python
import functools

import jax
import jax.numpy as jnp
import numpy as np
from jax import lax
from jax.experimental import pallas as pl
from jax.experimental.pallas import tpu as pltpu

CHUNK = 64
EPS = 1e-6
MLA_HEADS = 16
QK_NOPE = 64
QK_ROPE = 32
V_HEAD = 64
ROPE_THETA = 10000.0
MLA_SCALE = (QK_NOPE + QK_ROPE) ** -0.5
GM_CHUNK = 128
GM_GROUPS = 8
MEM_HEADS = 4
MEM_HEAD_DIM = 128
MEM_SCALE = MEM_HEAD_DIM ** -0.5
PEER_HEADS = 8
N_KEYS = 128
PEER_TOPK = 16
PEER_HALF = 128

LANES = 128
SLOT = LANES
ROPE_LO = QK_NOPE
ROPE_HALF = QK_ROPE // 2
ROW_TILE = 512
WIDE_ROW_TILE = 1024
ATTN_TILE = 512
LOG2_E = float(np.log2(np.e))
STATIC_SHIFT_LIMIT = 48.0
BOUND_MARGIN = 1.02
PEER_ROWS = 512
PEER_EXPERTS = 1024
PEER_CHUNK = 512
MASK_PITCH = N_KEYS + 8
MASK_UNROLL = 32
VMEM_LIMIT = 48 * 1024 * 1024
PEER_VMEM_LIMIT = 60 * 1024 * 1024

F32 = jnp.float32
BF16 = jnp.bfloat16
_NT = (((1,), (1,)), ((), ()))


def _dot(a, b):
    return jnp.dot(a, b, preferred_element_type=F32)


def _dot_nt(a, b):
    return lax.dot_general(a, b, _NT, preferred_element_type=F32)


def _rms(xf, g):
    return xf * lax.rsqrt(jnp.mean(xf * xf, axis=-1, keepdims=True) + EPS) * g


def _gelu(x):
    return 0.5 * x * (1.0 + lax.erf(x * np.float32(np.sqrt(0.5))))


def _group_mean(sq, m_ref):
    hi = sq.astype(BF16)
    lo = (sq - hi.astype(F32)).astype(BF16)
    return _dot(jnp.concatenate([hi, lo], axis=1), m_ref[...])


def _rope_slot(y, cos, sin_up, sin_dn):
    return (y * cos + pltpu.roll(y, ROPE_HALF, 1) * sin_up
            + pltpu.roll(y, SLOT - ROPE_HALF, 1) * sin_dn)


def _const_spec(shape):
    nd = len(shape)
    return pl.BlockSpec(shape, lambda *_: (0,) * nd, pipeline_mode=pl.Buffered(1))


def _row_spec(tm, width):
    return pl.BlockSpec((tm, width), lambda i: (i, 0))


def _params(sem, limit=VMEM_LIMIT):
    return pltpu.CompilerParams(dimension_semantics=sem, vmem_limit_bytes=limit)


def _in_proj_kernel(x_ref, gmix_ref, wcq_ref, wckv_ref, wkr_ref, wu_ref, wv_ref, wga_ref, wgb_ref,
                    gq_ref, gkv_ref, gkr_ref, ggm_ref, cos_ref, sup_ref, sdn_ref,
                    cq_out, ckv_out, kr_out, u_out, vg_out, siga_out, sigb_out):
    h = _rms(x_ref[...], gmix_ref[...]).astype(BF16)
    cq_out[...] = _rms(_dot(h, wcq_ref[...]), gq_ref[...]).astype(BF16)
    ckv_out[...] = _rms(_dot(h, wckv_ref[...]), gkv_ref[...])
    kr = _dot(h, wkr_ref[...])
    ms = jnp.sum(kr * kr, axis=-1, keepdims=True) * (1.0 / QK_ROPE)
    kr = kr * lax.rsqrt(ms + EPS) * gkr_ref[...]
    kr_out[...] = _rope_slot(kr, cos_ref[...], sup_ref[...], sdn_ref[...])
    u_out[...] = _gelu(_dot(h, wu_ref[...])).astype(BF16)
    vg_out[...] = _rms(_gelu(_dot(h, wv_ref[...])), ggm_ref[...])
    siga_out[...] = jax.nn.sigmoid(_dot(h, wga_ref[...])).astype(BF16)
    sigb_out[...] = jax.nn.sigmoid(_dot(h, wgb_ref[...])).astype(BF16)


def _in_proj(x2d, w, tabs, tm):
    t, d = x2d.shape
    cos, sup, sdn = tabs
    ntab = cos.shape[0] // tm
    tab_spec = pl.BlockSpec((tm, SLOT), lambda i: (i % ntab, 0))
    consts = [w['g_mix'], w['w_cq'], w['w_ckv'], w['w_kr'], w['w_u'], w['w_v'], w['w_ga'], w['w_gb'],
              w['g_q_lat'], w['g_kv_lat'], w['g_kr'], w['g_gm']]
    gw = w['w_u'].shape[1]
    widths = [w['w_cq'].shape[1], w['w_ckv'].shape[1], SLOT, gw, gw, d, d]
    dtypes = [BF16, F32, F32, BF16, F32, BF16, BF16]
    return pl.pallas_call(
        _in_proj_kernel,
        grid=(t // tm,),
        in_specs=[_row_spec(tm, d)] + [_const_spec(c.shape) for c in consts] + [tab_spec] * 3,
        out_specs=[_row_spec(tm, n) for n in widths],
        out_shape=[jax.ShapeDtypeStruct((t, n), dt) for n, dt in zip(widths, dtypes)],
        compiler_params=_params(("parallel",)),
        name="in_proj",
    )(x2d, *consts, cos, sup, sdn)


def _gmlp_kernel(u_ref, vg_ref, sigb_ref, mix_ref, bias_ref, wob_ref, out_ref):
    vgb = vg_ref[...].astype(BF16)
    parts = []
    for g in range(GM_GROUPS):
        sl = slice(g * LANES, (g + 1) * LANES)
        mixed = _dot(mix_ref[g], vgb[:, sl]) + bias_ref[:, sl]
        parts.append((u_ref[:, sl].astype(F32) * mixed).astype(BF16))
    ob = _dot(jnp.concatenate(parts, axis=1), wob_ref[...])
    out_ref[...] = sigb_ref[...].astype(F32) * ob


def _gmlp(u, vg, sigb, mix, bias, w_ob, tm):
    t, gw = u.shape
    d = w_ob.shape[1]
    return pl.pallas_call(
        _gmlp_kernel,
        grid=(t // tm,),
        in_specs=[_row_spec(tm, gw), _row_spec(tm, gw), _row_spec(tm, d),
                  _const_spec(mix.shape), _const_spec(bias.shape), _const_spec(w_ob.shape)],
        out_specs=_row_spec(tm, d),
        out_shape=jax.ShapeDtypeStruct((t, d), F32),
        compiler_params=_params(("parallel",)),
        name="gmlp",
    )(u, vg, sigb, mix, bias, w_ob)


def _q_proj_kernel(cq_ref, wuq_ref, gq_ref, mavg_ref, cos_ref, sup_ref, sdn_ref, q_out):
    q = _dot(cq_ref[...], wuq_ref[...])
    cos, sup, sdn = cos_ref[...], sup_ref[...], sdn_ref[...]
    for h in range(MLA_HEADS):
        sl = slice(h * SLOT, (h + 1) * SLOT)
        qs = q[:, sl]
        y = qs * lax.rsqrt(_group_mean(qs * qs, mavg_ref) + EPS) * gq_ref[...]
        q_out[:, sl] = _rope_slot(y, cos, sup, sdn).astype(BF16)


def _q_proj(cq, w, tabs, tm):
    t, ql = cq.shape
    cos, sup, sdn = tabs
    ntab = cos.shape[0] // tm
    tab_spec = pl.BlockSpec((tm, SLOT), lambda i: (i % ntab, 0))
    n = MLA_HEADS * SLOT
    return pl.pallas_call(
        _q_proj_kernel,
        grid=(t // tm,),
        in_specs=[_row_spec(tm, ql), _const_spec(w['w_uq'].shape), _const_spec(w['g_q'].shape),
                  _const_spec(w['m_avg'].shape)] + [tab_spec] * 3,
        out_specs=_row_spec(tm, n),
        out_shape=jax.ShapeDtypeStruct((t, n), BF16),
        compiler_params=_params(("parallel",)),
        name="q_proj",
    )(cq, w['w_uq'], w['g_q'], w['m_avg'], cos, sup, sdn)


def _kv_proj_kernel(ckv_ref, kr_ref, wuk_ref, wuv_ref, gkn_ref, mavg_ref, k_out, v_out):
    c = ckv_ref[...].astype(BF16)
    kn = _dot(c, wuk_ref[...])
    kr = kr_ref[...]
    for h in range(MLA_HEADS):
        sl = slice(h * SLOT, (h + 1) * SLOT)
        ks = kn[:, sl]
        y = ks * lax.rsqrt(_group_mean(ks * ks, mavg_ref) + EPS) * gkn_ref[...]
        k_out[:, sl] = (y + kr).astype(BF16)
    v_out[...] = _dot(c, wuv_ref[...]).astype(BF16)


def _kv_proj(ckv, kr, w, tm):
    t, kl = ckv.shape
    n = MLA_HEADS * SLOT
    return pl.pallas_call(
        _kv_proj_kernel,
        grid=(t // tm,),
        in_specs=[_row_spec(tm, kl), _row_spec(tm, SLOT), _const_spec(w['w_uk'].shape),
                  _const_spec(w['w_uv'].shape), _const_spec(w['g_kn'].shape),
                  _const_spec(w['m_avg'].shape)],
        out_specs=[_row_spec(tm, n), _row_spec(tm, n)],
        out_shape=[jax.ShapeDtypeStruct((t, n), BF16)] * 2,
        compiler_params=_params(("parallel",)),
        name="kv_proj",
    )(ckv, kr, w['w_uk'], w['w_uv'], w['g_kn'], w['m_avg'])


def _attn_prompt_kernel(bound_ref, q_ref, k_ref, v_ref, o_ref, *, tile):
    n_tiles = q_ref.shape[1] // tile
    bound = bound_ref[0]
    first = lax.broadcasted_iota(jnp.int32, (1, LANES), 1) < V_HEAD
    row_chunk = lax.broadcasted_iota(jnp.int32, (tile, tile), 0) // CHUNK
    col_chunk = lax.broadcasted_iota(jnp.int32, (tile, tile), 1) // CHUNK
    visible = col_chunk <= row_chunk

    def static_step(q, kstart, carry, masked):
        sums, acc = carry
        kstart = pl.multiple_of(kstart, tile)
        k = k_ref[0, pl.ds(kstart, tile), :]
        v = v_ref[0, pl.ds(kstart, tile), :]
        new_sums = []
        for h in range(2):
            sl = slice(h * SLOT, (h + 1) * SLOT)
            s = _dot_nt(q[:, sl], k[:, sl]) * F32(MLA_SCALE * LOG2_E) - bound
            if masked:
                s = jnp.where(visible, s, F32(-1e30))
            p = jnp.exp2(s)
            part = sums[h]
            for c in range(tile // LANES):
                part = part + p[:, c * LANES:(c + 1) * LANES]
            new_sums.append(part)
            acc = acc + _dot(p.astype(BF16), v[:, sl])
        return tuple(new_sums), acc

    def static_tile(qi, _):
        rows = pl.ds(pl.multiple_of(qi * tile, tile), tile)
        q = q_ref[0, rows, :]
        zeros = jnp.zeros((tile, LANES), F32)

        def pair(i, c):
            c = static_step(q, 2 * i * tile, c, False)
            return static_step(q, (2 * i + 1) * tile, c, False)

        def odd_tail(c):
            return static_step(q, qi * tile, static_step(q, (qi - 1) * tile, c, False), True)

        carry = lax.fori_loop(0, qi // 2, pair, ((zeros, zeros), zeros))
        sums, acc = lax.cond(qi % 2 == 1, odd_tail, lambda c: static_step(q, qi * tile, c, True), carry)
        l0 = jnp.sum(sums[0], axis=-1, keepdims=True)
        l1 = jnp.sum(sums[1], axis=-1, keepdims=True)
        o_ref[0, rows, :] = (acc / jnp.where(first, l0, l1)).astype(BF16)
        return 0

    @pl.when(bound < STATIC_SHIFT_LIMIT)
    def _():
        lax.fori_loop(0, n_tiles, static_tile, 0)

    def step(q, kstart, carry, masked):
        ms, ls, acc = carry
        kstart = pl.multiple_of(kstart, tile)
        k = k_ref[0, pl.ds(kstart, tile), :]
        v = v_ref[0, pl.ds(kstart, tile), :]
        new_ms, new_ls, alphas, pvs = [], [], [], []
        for h in range(2):
            sl = slice(h * SLOT, (h + 1) * SLOT)
            s = _dot_nt(q[:, sl], k[:, sl]) * F32(MLA_SCALE * LOG2_E)
            if masked:
                s = jnp.where(visible, s, F32(-1e30))
            m_new = jnp.maximum(ms[h], jnp.max(s, axis=-1, keepdims=True))
            alpha = jnp.exp2(ms[h] - m_new)
            p = jnp.exp2(s - m_new)
            new_ls.append(alpha * ls[h] + jnp.sum(p, axis=-1, keepdims=True))
            new_ms.append(m_new)
            alphas.append(alpha)
            pvs.append(_dot(p.astype(BF16), v[:, sl]))
        acc = acc * jnp.where(first, alphas[0], alphas[1]) + pvs[0] + pvs[1]
        return tuple(new_ms), tuple(new_ls), acc

    def online_tile(qi, _):
        rows = pl.ds(pl.multiple_of(qi * tile, tile), tile)
        q = q_ref[0, rows, :]
        neg = jnp.full((tile, 1), -1e30, F32)
        zero = jnp.zeros((tile, 1), F32)
        init = ((neg, neg), (zero, zero), jnp.zeros((tile, LANES), F32))
        carry = lax.fori_loop(0, qi, lambda i, c: step(q, i * tile, c, False), init)
        _, ls, acc = step(q, qi * tile, carry, True)
        o_ref[0, rows, :] = (acc / jnp.where(first, ls[0], ls[1])).astype(BF16)
        return 0

    @pl.when(bound >= STATIC_SHIFT_LIMIT)
    def _():
        lax.fori_loop(0, n_tiles, online_tile, 0)


def _score_bound(w):
    gmax = lambda g, lo, n: jnp.max(jnp.abs(g[0, lo:lo + n]))
    nope = QK_NOPE * gmax(w['g_q'], 0, QK_NOPE) * gmax(w['g_kn'], 0, QK_NOPE)
    rope = QK_ROPE * gmax(w['g_q'], ROPE_LO, QK_ROPE) * gmax(w['g_kr'], ROPE_LO, QK_ROPE)
    return (BOUND_MARGIN * MLA_SCALE * LOG2_E * (nope + rope)).reshape(1).astype(F32)


def _attn_prompt(q, k, v, bound, tile):
    b, s, _ = q.shape
    pairs = MLA_HEADS // 2
    pair_block = pl.BlockSpec((1, s, 2 * SLOT), lambda bi, p: (bi, 0, p))
    return pl.pallas_call(
        functools.partial(_attn_prompt_kernel, tile=tile),
        grid=(b, pairs),
        in_specs=[pl.BlockSpec(memory_space=pltpu.SMEM), pair_block, pair_block, pair_block],
        out_specs=pl.BlockSpec((1, s, LANES), lambda bi, p: (bi, 0, p)),
        out_shape=jax.ShapeDtypeStruct((b, s, pairs * LANES), BF16),
        compiler_params=_params(("parallel", "parallel")),
        name="attn_prompt",
    )(bound, q, k, v)


def _attn_sample_kernel(q_ref, pckv_ref, pkr_ref, nckv_ref, nkr_ref, wuk_ref, wuv_ref, gkn_ref, place_ref,
                        o_ref):
    sq = q_ref.shape[1]
    rows = MLA_HEADS * sq
    gkn = gkn_ref[...]

    def keys_values(ckv, kr_slot):
        c = ckv.astype(BF16)
        kn = _dot(c, wuk_ref[...])
        parts = []
        for h in range(MLA_HEADS):
            ks = kn[:, h * SLOT:(h + 1) * SLOT]
            ms = jnp.sum(ks * ks, axis=-1, keepdims=True) * (1.0 / QK_NOPE)
            parts.append((ks * lax.rsqrt(ms + EPS) * gkn + kr_slot).astype(BF16))
        return jnp.concatenate(parts, axis=1), _dot(c, wuv_ref[...]).astype(BF16)

    past_kr = _dot(pkr_ref[0].astype(BF16), place_ref[...])
    k_p, v_p = keys_values(pckv_ref[0], past_kr)
    k_n, v_n = keys_values(nckv_ref[0], nkr_ref[0])
    k = jnp.concatenate([k_p, k_n], axis=0)
    v = jnp.concatenate([v_p, v_n], axis=0)

    qt = jnp.concatenate([q_ref[0].astype(F32)] * MLA_HEADS, axis=0)
    q_shape = (rows, MLA_HEADS * SLOT)
    own = (lax.broadcasted_iota(jnp.int32, q_shape, 0) // sq
           == lax.broadcasted_iota(jnp.int32, q_shape, 1) // SLOT)
    qbd = jnp.where(own, qt, 0.0).astype(BF16)
    s = _dot_nt(qbd, k) * MLA_SCALE
    p = jnp.exp(s - jnp.max(s, axis=-1, keepdims=True))
    p = p / jnp.sum(p, axis=-1, keepdims=True)
    o_all = _dot(p.astype(BF16), v)
    o_shape = o_all.shape
    own = (lax.broadcasted_iota(jnp.int32, o_shape, 0) // sq
           == lax.broadcasted_iota(jnp.int32, o_shape, 1) // V_HEAD)
    o_all = jnp.where(own, o_all, 0.0)
    o = o_all[0:sq]
    for h in range(1, MLA_HEADS):
        o = o + o_all[h * sq:(h + 1) * sq]
    o_ref[0] = o.astype(BF16)


def _attn_sample(q, past_ckv, past_kr, new_ckv, new_kr, w):
    b, sq, nq = q.shape
    npast, kl = past_ckv.shape[1:]
    nv = w['w_uv_plain'].shape[1]
    stream = lambda n, width: pl.BlockSpec((1, n, width), lambda bi: (bi, 0, 0))
    consts = [w['w_uk'], w['w_uv_plain'], w['g_kn'], w['rope_place']]
    return pl.pallas_call(
        _attn_sample_kernel,
        grid=(b,),
        in_specs=[stream(sq, nq), stream(npast, kl), stream(npast, QK_ROPE), stream(sq, kl), stream(sq, SLOT)]
        + [_const_spec(c.shape) for c in consts],
        out_specs=stream(sq, nv),
        out_shape=jax.ShapeDtypeStruct((b, sq, nv), BF16),
        compiler_params=_params(("parallel",)),
        name="attn_sample",
    )(q, past_ckv, past_kr, new_ckv, new_kr, *consts)


def _merge_kernel(x_ref, o_ref, siga_ref, gbob_ref, woa_ref, wo_ref, gx_ref, wcq_ref, gcq_ref,
                  x1_out, qc_out):
    oa = _dot(o_ref[...], woa_ref[...])
    merged = siga_ref[...].astype(F32) * oa + gbob_ref[...]
    x1 = x_ref[...] + _dot(merged.astype(BF16), wo_ref[...])
    x1_out[...] = x1
    qc = _dot(_rms(x1, gx_ref[...]).astype(BF16), wcq_ref[...])
    for h in range(MEM_HEADS):
        sl = slice(h * MEM_HEAD_DIM, (h + 1) * MEM_HEAD_DIM)
        qc_out[:, sl] = _rms(qc[:, sl], gcq_ref[...]).astype(BF16)


def _merge(x2d, o, siga, gbob, w, tm):
    t, d = x2d.shape
    consts = [w['w_oa'], w['w_o'], w['g_xattn'], w['w_cq_mem'], w['g_cq']]
    nq = w['w_cq_mem'].shape[1]
    return pl.pallas_call(
        _merge_kernel,
        grid=(t // tm,),
        in_specs=[_row_spec(tm, d), _row_spec(tm, o.shape[1]), _row_spec(tm, d), _row_spec(tm, d)]
        + [_const_spec(c.shape) for c in consts],
        out_specs=[_row_spec(tm, d), _row_spec(tm, nq)],
        out_shape=[jax.ShapeDtypeStruct((t, d), F32), jax.ShapeDtypeStruct((t, nq), BF16)],
        compiler_params=_params(("parallel",)),
        name="merge",
    )(x2d, o, siga, gbob, *consts)


def _mem_kv_kernel(mem_ref, gmem_ref, wck_ref, gck_ref, wcv_ref, k_out, v_out):
    m = _rms(mem_ref[...], gmem_ref[...]).astype(BF16)
    k = _dot(m, wck_ref[...])
    for h in range(MEM_HEADS):
        sl = slice(h * MEM_HEAD_DIM, (h + 1) * MEM_HEAD_DIM)
        k_out[:, sl] = _rms(k[:, sl], gck_ref[...])
    v_out[...] = _dot(m, wcv_ref[...])


def _mem_kv(mem2d, w, tm):
    t, d = mem2d.shape
    consts = [w['g_mem'], w['w_ck'], w['g_ck'], w['w_cv']]
    n = w['w_ck'].shape[1]
    return pl.pallas_call(
        _mem_kv_kernel,
        grid=(t // tm,),
        in_specs=[_row_spec(tm, d)] + [_const_spec(c.shape) for c in consts],
        out_specs=[_row_spec(tm, n)] * 2,
        out_shape=[jax.ShapeDtypeStruct((t, n), F32)] * 2,
        compiler_params=_params(("parallel",)),
        name="mem_kv",
    )(mem2d, *consts)


def _cross_kernel(q_ref, k_ref, v_ref, o_ref):
    q = q_ref[0]
    k = k_ref[0].astype(BF16)
    v = v_ref[0].astype(BF16)
    for h in range(MEM_HEADS):
        sl = slice(h * MEM_HEAD_DIM, (h + 1) * MEM_HEAD_DIM)
        s = _dot_nt(q[:, sl], k[:, sl]) * MEM_SCALE
        p = jnp.exp(s - jnp.max(s, axis=-1, keepdims=True))
        p = p / jnp.sum(p, axis=-1, keepdims=True)
        o_ref[0, :, sl] = _dot(p.astype(BF16), v[:, sl]).astype(BF16)


def _cross(qc, mk, mv, tc):
    b, s, n = qc.shape
    nm = mk.shape[1]
    return pl.pallas_call(
        _cross_kernel,
        grid=(b, s // tc),
        in_specs=[pl.BlockSpec((1, tc, n), lambda bi, i: (bi, i, 0)),
                  pl.BlockSpec((1, nm, n), lambda bi, i: (bi, 0, 0)),
                  pl.BlockSpec((1, nm, n), lambda bi, i: (bi, 0, 0))],
        out_specs=pl.BlockSpec((1, tc, n), lambda bi, i: (bi, i, 0)),
        out_shape=jax.ShapeDtypeStruct((b, s, n), BF16),
        compiler_params=_params(("parallel", "parallel")),
        name="cross_attn",
    )(qc, mk, mv)


def _cross_out_kernel(x1_ref, oc_ref, wco_ref, gffn_ref, wpq_ref, x2_out, h_out, q_out):
    x2 = x1_ref[...] + _dot(oc_ref[...], wco_ref[...])
    x2_out[...] = x2
    h = _rms(x2, gffn_ref[...]).astype(BF16)
    h_out[...] = h
    q_out[...] = _dot(h, wpq_ref[...]).astype(BF16)


def _cross_out(x1, oc, w, tm):
    t, d = x1.shape
    nq = w['w_pq'].shape[1]
    return pl.pallas_call(
        _cross_out_kernel,
        grid=(t // tm,),
        in_specs=[_row_spec(tm, d), _row_spec(tm, oc.shape[1]), _const_spec(w['w_co'].shape),
                  _const_spec(w['g_ffn'].shape), _const_spec(w['w_pq'].shape)],
        out_specs=[_row_spec(tm, d), _row_spec(tm, d), _row_spec(tm, nq)],
        out_shape=[jax.ShapeDtypeStruct((t, d), F32), jax.ShapeDtypeStruct((t, d), BF16),
                   jax.ShapeDtypeStruct((t, nq), BF16)],
        compiler_params=_params(("parallel",)),
        name="cross_out",
    )(x1, oc, w['w_co'], w['g_ffn'], w['w_pq'])


def _pair_blocks():
    k = PEER_TOPK
    blocks, flat = [], []
    n_row = sum(1 for a in range(k) if k // (a + 1) > 2)
    for a in range(n_row):
        nb = k if a == 0 else 8
        blocks.append(('row', a, nb))
        flat += [a * k + b if (a + 1) * (b + 1) <= k else -1 for b in range(nb)]
    for b in range(k // (n_row + 1)):
        na = k if b == 0 else 8
        blocks.append(('col', b, na))
        flat += [a * k + b if (a >= n_row and (a + 1) * (b + 1) <= k) else -1 for a in range(na)]
    return blocks, np.asarray(flat, np.float32)


def _pair_candidates(v0, v1, blocks, combine):
    parts = []
    for kind, fixed, n in blocks:
        if kind == 'row':
            parts.append(combine(v0[fixed:fixed + 1, :], v1[0:n, :]))
        else:
            parts.append(combine(v0[0:n, :], v1[fixed:fixed + 1, :]))
    return jnp.concatenate(parts, axis=0)


def _peer_kernel(h_ref, x_ref, q_ref, keys_ref, flat_ref, u_ref, v_ref, out_ref,
                 mask_scr, sv_scr, si_scr, ts_scr, te_scr, it_scr, jt_scr, gt_scr,
                 isel_scr, jsel_scr, gsel_scr):
    tm = h_ref.shape[0]
    te = v_ref.shape[0]
    r = pl.program_id(0)
    e = pl.program_id(1)
    n = pl.num_programs(0) - 2
    rows_per_step = te // N_KEYS
    tokens_per_head = tm // PEER_HEADS
    blocks, _ = _pair_blocks()
    neg_inf = F32(-jnp.inf)
    key_rows = lax.broadcasted_iota(jnp.int32, (N_KEYS, LANES), 0).astype(F32)
    swept = lax.rem(r, 2)
    ranked = 1 - swept

    def mask_token(t):
        irow = isel_scr[swept, pl.ds(t, 1), :]
        jrow = jsel_scr[swept, pl.ds(t, 1), :]
        grow = gsel_scr[swept, pl.ds(t, 1), :]
        rw = jnp.where(key_rows == irow, 0.5 * grow, 0.0).astype(BF16)
        cw = jnp.where(key_rows == jrow, 1.0, 0.0).astype(BF16)
        r0 = pl.multiple_of(t * MASK_PITCH, 8)
        mask_scr[pl.ds(r0, N_KEYS), :] = _dot_nt(rw, cw)

    def rank_pairs(hd):
        sv0, sv1 = sv_scr[2 * hd], sv_scr[2 * hd + 1]
        si0, si1 = si_scr[2 * hd], si_scr[2 * hd + 1]
        flat = flat_ref[...]
        cand = _pair_candidates(sv0, sv1, blocks, lambda x, y: x + y)
        cand = jnp.where(flat >= 0.0, cand, neg_inf)
        cidx = _pair_candidates(si0, si1, blocks, lambda x, y: x * F32(N_KEYS) + y)
        for kk in range(PEER_TOPK):
            m = jnp.max(cand, axis=0, keepdims=True)
            pos = jnp.min(jnp.where(cand == m, flat, F32(PEER_TOPK * PEER_TOPK)), axis=0, keepdims=True)
            sel = flat == pos
            ts_scr[kk:kk + 1, :] = m
            te_scr[kk:kk + 1, :] = jnp.max(jnp.where(sel, cidx, F32(-1.0)), axis=0, keepdims=True)
            cand = jnp.where(sel, neg_inf, cand)
        top = ts_scr[...]
        ex = jnp.exp(top - top[0:1, :])
        gate = ex / jnp.sum(ex, axis=0, keepdims=True)
        eidx = te_scr[...]
        irow = jnp.floor(eidx * F32(1.0 / N_KEYS))
        r0 = pl.multiple_of(hd * PEER_TOPK, PEER_TOPK)
        it_scr[pl.ds(r0, PEER_TOPK), :] = irow
        jt_scr[pl.ds(r0, PEER_TOPK), :] = eidx - irow * F32(N_KEYS)
        gt_scr[pl.ds(r0, PEER_TOPK), :] = gate

    def publish_ranked():
        isel_scr[ranked] = it_scr[...].T
        jsel_scr[ranked] = jt_scr[...].T
        gsel_scr[ranked] = gt_scr[...].T

    @pl.when(e == 0)
    def _():
        out_ref[...] = x_ref[...]

        @pl.when(r == 1)
        def _():
            def head(hd, _):
                rank_pairs(hd)
                return 0

            lax.fori_loop(0, PEER_HEADS, head, 0)
            publish_ranked()

        @pl.when(jnp.logical_and(r >= 2, r <= n))
        def _():
            def trip(i, _):
                for t in range(tokens_per_head):
                    mask_token(i * tokens_per_head + t)
                rank_pairs(i)
                return 0

            lax.fori_loop(0, PEER_HEADS, trip, 0)
            publish_ranked()

        @pl.when(r == n + 1)
        def _():
            def trip(i, _):
                for t in range(MASK_UNROLL):
                    mask_token(i * MASK_UNROLL + t)
                return 0

            lax.fori_loop(0, tm // MASK_UNROLL, trip, 0)

    def slot_scores():
        d0 = pl.multiple_of(e * PEER_HALF, PEER_HALF)
        return _dot_nt(q_ref[:, pl.ds(d0, PEER_HALF)], keys_ref[e])

    def rank_slot(scores):
        lane = lax.broadcasted_iota(jnp.int32, (tm, N_KEYS), 1)
        picked = jnp.zeros((tm, N_KEYS), jnp.int32)
        a = scores
        for kk in range(PEER_TOPK):
            idx = jnp.argmax(a, axis=-1, keepdims=True).astype(jnp.int32)
            a = jnp.where(lane == idx, neg_inf, a)
            picked = jnp.where(lane == kk, idx, picked)
        vals = jnp.take_along_axis(scores, picked, axis=1)
        sv_scr[e] = vals.T[0:PEER_TOPK, :]
        si_scr[e] = (N_KEYS - 1 - picked).astype(F32).T[0:PEER_TOPK, :]

    def expert_chunk():
        h = h_ref[...]
        row0 = e * rows_per_step
        weighted = []
        for c in range(te // PEER_CHUNK):
            rows = slice(c * PEER_CHUNK, (c + 1) * PEER_CHUNK)
            pre = _dot_nt(h, u_ref[rows, :])
            half_mask = jnp.concatenate(
                [mask_scr[pl.ds(row0 + c * (PEER_CHUNK // N_KEYS) + rr, tm, stride=MASK_PITCH), :]
                 for rr in range(PEER_CHUNK // N_KEYS)], axis=1)
            weighted.append((pre * (1.0 + lax.erf(pre * np.float32(np.sqrt(0.5)))) * half_mask).astype(BF16))
        out_ref[...] += _dot(jnp.concatenate(weighted, axis=1), v_ref[...])

    ranks = r < n
    sweeps = r >= 2

    @pl.when(jnp.logical_and(ranks, jnp.logical_not(sweeps)))
    def _():
        rank_slot(slot_scores())

    @pl.when(jnp.logical_and(ranks, sweeps))
    def _():
        a = slot_scores()
        expert_chunk()
        rank_slot(a)

    @pl.when(jnp.logical_and(jnp.logical_not(ranks), sweeps))
    def _():
        expert_chunk()


def _peer(h, x2, q_route, w, tm, te):
    t, d = h.shape
    n = t // tm
    nexp = w['peer_v'].shape[0]
    assert nexp // te == 2 * PEER_HEADS, "one half-key slot is ranked per expert step"
    assert tm % PEER_HEADS == 0 and tm % MASK_UNROLL == 0
    nsel = PEER_HEADS * PEER_TOPK
    _, flat = _pair_blocks()
    flat = jnp.asarray(np.repeat(flat[:, None], tm, axis=1))
    swept_tile = lambda r, e: (jnp.clip(r - 2, 0, n - 1), 0)
    dense_tile = lambda width: pl.BlockSpec((tm, width), swept_tile, pipeline_mode=pl.Buffered(1))
    expert_tile = pl.BlockSpec((te, d), lambda r, e: (jnp.where(r >= 2, e, 0), 0))
    return pl.pallas_call(
        _peer_kernel,
        grid=(n + 2, nexp // te),
        in_specs=[dense_tile(d), dense_tile(d),
                  pl.BlockSpec((tm, q_route.shape[1]), lambda r, e: (jnp.minimum(r, n - 1), 0),
                               pipeline_mode=pl.Buffered(1)),
                  _const_spec(w['sub_keys'].shape), _const_spec(flat.shape), expert_tile, expert_tile],
        out_specs=pl.BlockSpec((tm, d), swept_tile),
        out_shape=jax.ShapeDtypeStruct((t, d), F32),
        scratch_shapes=[pltpu.VMEM((tm * MASK_PITCH, LANES), F32),
                        pltpu.VMEM((2 * PEER_HEADS, PEER_TOPK, tm), F32),
                        pltpu.VMEM((2 * PEER_HEADS, PEER_TOPK, tm), F32),
                        pltpu.VMEM((PEER_TOPK, tm), F32),
                        pltpu.VMEM((PEER_TOPK, tm), F32),
                        pltpu.VMEM((nsel, tm), F32),
                        pltpu.VMEM((nsel, tm), F32),
                        pltpu.VMEM((nsel, tm), F32),
                        pltpu.VMEM((2, tm, nsel), F32),
                        pltpu.VMEM((2, tm, nsel), F32),
                        pltpu.VMEM((2, tm, nsel), F32)],
        compiler_params=_params(("arbitrary", "arbitrary"), PEER_VMEM_LIMIT),
        name="peer",
    )(h, x2, q_route, w['sub_keys'], flat, w['peer_u'], w['peer_v'])


def _slot_gain(parts):
    row = jnp.zeros((SLOT,), F32)
    for off, g in parts:
        row = lax.dynamic_update_slice(row, g.astype(F32), (off,))
    return row[None, :]


def _group_avg_matrix():
    m = np.zeros((SLOT, SLOT), np.float32)
    m[:QK_NOPE, :QK_NOPE] = 1.0 / QK_NOPE
    m[ROPE_LO:ROPE_LO + QK_ROPE, ROPE_LO:ROPE_LO + QK_ROPE] = 1.0 / QK_ROPE
    return jnp.asarray(np.concatenate([m, m], axis=0), BF16)


def _rope_tables(pos):
    inv = ROPE_THETA ** (-jnp.arange(ROPE_HALF, dtype=F32) / ROPE_HALF)
    ang = pos.astype(F32)[:, None] * inv[None, :]
    cos, sin = jnp.cos(ang), jnp.sin(ang)
    n = pos.shape[0]
    tail = SLOT - ROPE_LO - QK_ROPE
    cos_t = jnp.concatenate([jnp.ones((n, ROPE_LO), F32), cos, cos, jnp.zeros((n, tail), F32)], axis=1)
    sin_up = jnp.concatenate([jnp.zeros((n, ROPE_LO + ROPE_HALF), F32), sin, jnp.zeros((n, tail), F32)], axis=1)
    sin_dn = jnp.concatenate([jnp.zeros((n, ROPE_LO), F32), -sin, jnp.zeros((n, tail + ROPE_HALF), F32)], axis=1)
    return cos_t, sin_up, sin_dn


def _prep_weights(p, i):
    w_in = p['w_in'][i]
    q_lora = p['g_q_lat'].shape[1]
    kv_lora = p['g_kv_lat'].shape[1]
    gw = p['g_gm'].shape[1]
    d = w_in.shape[0]
    o = np.cumsum([0, q_lora, kv_lora, QK_ROPE, gw, gw, d, d])
    seg = lambda k: w_in[:, o[k]:o[k + 1]]
    row = lambda a: a[i][None, :].astype(F32)
    w = {}
    w['g_mix'] = row(p['g_mix'])
    w['w_cq'] = seg(0).astype(BF16)
    w['w_ckv'] = seg(1).astype(BF16)
    w['w_kr'] = jnp.pad(seg(2), ((0, 0), (ROPE_LO, SLOT - ROPE_LO - QK_ROPE))).astype(BF16)
    w['w_u'] = seg(3).astype(BF16)
    w['w_v'] = seg(4).astype(BF16)
    w['w_ga'] = seg(5).astype(BF16)
    w['w_gb'] = seg(6).astype(BF16)
    w['g_q_lat'] = row(p['g_q_lat'])
    w['g_kv_lat'] = row(p['g_kv_lat'])
    w['g_kr'] = _slot_gain([(ROPE_LO, p['g_kr'][i])])
    w['g_gm'] = row(p['g_gm'])
    head_dim = QK_NOPE + QK_ROPE
    w_uq = p['w_uq'][i].reshape(q_lora, MLA_HEADS, head_dim)
    w['w_uq'] = jnp.pad(w_uq, ((0, 0), (0, 0), (0, SLOT - head_dim))).reshape(q_lora, -1).astype(BF16)
    w['g_q'] = _slot_gain([(0, p['g_qn'][i]), (ROPE_LO, p['g_qr'][i])])
    w['m_avg'] = _group_avg_matrix()
    w_uk = p['w_uk'][i].reshape(kv_lora, MLA_HEADS, QK_NOPE)
    w['w_uk'] = jnp.pad(w_uk, ((0, 0), (0, 0), (0, SLOT - QK_NOPE))).reshape(kv_lora, -1).astype(BF16)
    w['g_kn'] = _slot_gain([(0, p['g_kn'][i])])
    w_uv = p['w_uv'][i].reshape(kv_lora, MLA_HEADS // 2, 2, V_HEAD)
    eye2 = jnp.eye(2, dtype=w_uv.dtype)
    w['w_uv'] = jnp.einsum('cpjd,jk->cpjkd', w_uv, eye2).reshape(kv_lora, -1).astype(BF16)
    w['w_uv_plain'] = p['w_uv'][i].astype(BF16)
    place = np.zeros((QK_ROPE, SLOT), np.float32)
    place[np.arange(QK_ROPE), ROPE_LO + np.arange(QK_ROPE)] = 1.0
    w['rope_place'] = jnp.asarray(place, BF16)
    w['w_oa'] = p['w_oa'][i].astype(BF16)
    w['w_ob'] = p['w_ob'][i].astype(BF16)
    w['w_o'] = p['w_o'][i].astype(BF16)
    w['g_xattn'] = row(p['g_xattn'])
    w['g_mem'] = row(p['g_mem'])
    w['w_cq_mem'] = p['w_cq'][i].astype(BF16)
    w['g_cq'] = row(p['g_cq'])
    w['w_ck'] = p['w_ck'][i].astype(BF16)
    w['g_ck'] = row(p['g_ck'])
    w['w_cv'] = p['w_cv'][i].astype(BF16)
    w['w_co'] = p['w_co'][i].astype(BF16)
    w['g_ffn'] = row(p['g_ffn'])
    w['w_pq'] = p['w_pq'][i].astype(BF16)
    w['sub_keys'] = p['sub_keys'][i].reshape(2 * PEER_HEADS, N_KEYS, PEER_HALF)[:, ::-1, :].astype(BF16)
    w['peer_u'] = p['peer_u'][i].astype(BF16)
    w['peer_v'] = p['peer_v'][i].astype(BF16)
    w['w_s'] = p['w_s'][i]
    w['b_s'] = p['b_s'][i]
    return w


def _spatial_operands(w, seq, tm):
    chunk = min(seq, GM_CHUNK)
    reps = tm // chunk
    w_mask = jnp.tril(w['w_s'][:, :chunk, :chunk])
    w_mask = w_mask.astype(BF16)
    mix = jnp.concatenate(
        [jnp.pad(w_mask, ((0, 0), (0, 0), (a * chunk, (reps - 1 - a) * chunk))) for a in range(reps)], axis=1)
    bias = jnp.tile(w['b_s'][:, :chunk].T, (reps, 1))
    bias = jnp.repeat(bias, LANES, axis=1)
    return mix.astype(BF16), bias.astype(F32)


def _token_mixer(x2d, w, seq, pos, past, tm):
    t, d = x2d.shape
    nb = t // seq
    period = max(seq, tm)
    tabs = _rope_tables(jnp.tile(pos, period // seq))
    cq, ckv, kr, u, vg, siga, sigb = _in_proj(x2d, w, tabs, tm)
    mix, bias = _spatial_operands(w, seq, tm)
    gbob = _gmlp(u, vg, sigb, mix, bias, w['w_ob'], tm)
    wide = min(WIDE_ROW_TILE, t)
    q = _q_proj(cq, w, tabs, wide)
    if past is None:
        k, v = _kv_proj(ckv, kr, w, wide)
        o = _attn_prompt(q.reshape(nb, seq, -1), k.reshape(nb, seq, -1), v.reshape(nb, seq, -1),
                         _score_bound(w), ATTN_TILE)
    else:
        past_ckv, past_kr = past
        o = _attn_sample(q.reshape(nb, seq, -1), past_ckv, past_kr, ckv.reshape(nb, seq, -1),
                         kr.reshape(nb, seq, SLOT), w)
    x1, qc = _merge(x2d, o.reshape(t, -1), siga, gbob, w, wide)
    return x1, qc, ckv, kr[:, ROPE_LO:ROPE_LO + QK_ROPE], vg


def _tail(x1, qc, mk, mv, w, nb, seq, tm):
    t, d = x1.shape
    oc = _cross(qc.reshape(nb, seq, -1), mk, mv, min(seq, WIDE_ROW_TILE))
    x2, h, q_route = _cross_out(x1, oc.reshape(t, -1), w, min(WIDE_ROW_TILE, t))
    return _peer(h, x2, q_route, w, PEER_ROWS, PEER_EXPERTS)


def kernel(x_prompt, x_sample, cache_mla_ckv, cache_mla_krope, cache_mem_k, cache_mem_v, mem_prompt, g_mix, w_in, g_q_lat, w_uq, g_qn, g_qr, g_kv_lat, g_kr, w_uk, w_uv, g_kn, w_oa, g_gm, w_s, b_s, w_ob, w_o, g_xattn, g_mem, w_cq, g_cq, w_ck, g_ck, w_cv, w_co, g_ffn, w_pq, sub_keys, peer_u, peer_v):
    params = dict(g_mix=g_mix, w_in=w_in, g_q_lat=g_q_lat, w_uq=w_uq, g_qn=g_qn, g_qr=g_qr,
                  g_kv_lat=g_kv_lat, g_kr=g_kr, w_uk=w_uk, w_uv=w_uv, g_kn=g_kn, w_oa=w_oa, g_gm=g_gm,
                  w_s=w_s, b_s=b_s, w_ob=w_ob, w_o=w_o, g_xattn=g_xattn, g_mem=g_mem, w_cq=w_cq,
                  g_cq=g_cq, w_ck=w_ck, g_ck=g_ck, w_cv=w_cv, w_co=w_co, g_ffn=g_ffn, w_pq=w_pq,
                  sub_keys=sub_keys, peer_u=peer_u, peer_v=peer_v)
    bp, sp, d = x_prompt.shape
    bs, ss, _ = x_sample.shape
    depth = w_in.shape[0]
    past_len = cache_mla_ckv.shape[2]
    n_mem = mem_prompt.shape[1]
    tm = ROW_TILE
    assert (bp * sp) % tm == 0 and (bs * ss) % tm == 0 and sp % ATTN_TILE == 0
    pos_p = jnp.arange(sp)
    pos_s = past_len + jnp.arange(ss)
    xp = x_prompt.reshape(bp * sp, d)
    xs = x_sample.reshape(bs * ss, d)
    outs = [[] for _ in range(7)]
    for i in range(depth):
        w = _prep_weights(params, i)
        xp1, qcp, ckv_p, kr_p, _ = _token_mixer(xp, w, sp, pos_p, None, tm)
        xs1, qcs, ckv_s, kr_s, vg_s = _token_mixer(xs, w, ss, pos_s, (cache_mla_ckv[i], cache_mla_krope[i]), tm)
        mk_p, mv_p = _mem_kv(mem_prompt.reshape(bp * n_mem, d), w, min(tm, bp * n_mem))
        xp = _tail(xp1, qcp, mk_p.reshape(bp, n_mem, -1), mv_p.reshape(bp, n_mem, -1), w, bp, sp, tm)
        xs = _tail(xs1, qcs, cache_mem_k[i].reshape(bs, n_mem, -1), cache_mem_v[i].reshape(bs, n_mem, -1),
                   w, bs, ss, tm)
        outs[0].append(ckv_p.reshape(bp, sp, -1))
        outs[1].append(kr_p.reshape(bp, sp, -1))
        outs[2].append(mk_p.reshape(bp, n_mem, MEM_HEADS, MEM_HEAD_DIM))
        outs[3].append(mv_p.reshape(bp, n_mem, MEM_HEADS, MEM_HEAD_DIM))
        outs[4].append(ckv_s.reshape(bs, ss, -1))
        outs[5].append(kr_s.reshape(bs, ss, -1))
        outs[6].append(vg_s.reshape(bs, ss, -1))
    return (xp.reshape(bp, sp, d), xs.reshape(bs, ss, d)) + tuple(jnp.stack(o) for o in outs)
```

```python
import functools

import jax
import jax.numpy as jnp
import numpy as np
from jax import lax
from jax.experimental import pallas as pl
from jax.experimental.pallas import tpu as pltpu

CHUNK = 64
EPS = 1e-6
MLA_HEADS = 16
QK_NOPE = 64
QK_ROPE = 32
V_HEAD = 64
ROPE_THETA = 10000.0
MLA_SCALE = (QK_NOPE + QK_ROPE) ** -0.5
GM_CHUNK = 128
GM_GROUPS = 8
MEM_HEADS = 4
MEM_HEAD_DIM = 128
MEM_SCALE = MEM_HEAD_DIM ** -0.5
PEER_HEADS = 8
N_KEYS = 128
PEER_TOPK = 16
PEER_HALF = 128

LANES = 128
SLOT = LANES
ROPE_LO = QK_NOPE
ROPE_HALF = QK_ROPE // 2
ROW_TILE = 512
WIDE_ROW_TILE = 1024
ATTN_TILE = 512
LOG2_E = float(np.log2(np.e))
STATIC_SHIFT_LIMIT = 48.0
BOUND_MARGIN = 1.02
PEER_ROWS = 512
PEER_EXPERTS = 1024
PEER_CHUNK = 512
MASK_PITCH = N_KEYS + 8
MASK_UNROLL = 32
VMEM_LIMIT = 48 * 1024 * 1024
PEER_VMEM_LIMIT = 60 * 1024 * 1024

F32 = jnp.float32
BF16 = jnp.bfloat16
_NT = (((1,), (1,)), ((), ()))


def _dot(a, b):
    return jnp.dot(a, b, preferred_element_type=F32)


def _dot_nt(a, b):
    return lax.dot_general(a, b, _NT, preferred_element_type=F32)


def _rms(xf, g):
    return xf * lax.rsqrt(jnp.mean(xf * xf, axis=-1, keepdims=True) + EPS) * g


def _gelu(x):
    return 0.5 * x * (1.0 + lax.erf(x * np.float32(np.sqrt(0.5))))


def _group_mean(sq, m_ref):
    hi = sq.astype(BF16)
    lo = (sq - hi.astype(F32)).astype(BF16)
    return _dot(jnp.concatenate([hi, lo], axis=1), m_ref[...])


def _rope_slot(y, cos, sin_up, sin_dn):
    return (y * cos + pltpu.roll(y, ROPE_HALF, 1) * sin_up
            + pltpu.roll(y, SLOT - ROPE_HALF, 1) * sin_dn)


def _const_spec(shape):
    nd = len(shape)
    return pl.BlockSpec(shape, lambda *_: (0,) * nd, pipeline_mode=pl.Buffered(1))


def _row_spec(tm, width):
    return pl.BlockSpec((tm, width), lambda i: (i, 0))


def _params(sem, limit=VMEM_LIMIT):
    return pltpu.CompilerParams(dimension_semantics=sem, vmem_limit_bytes=limit)


def _in_proj_kernel(x_ref, gmix_ref, wcq_ref, wckv_ref, wkr_ref, wu_ref, wv_ref, wga_ref, wgb_ref,
                    gq_ref, gkv_ref, gkr_ref, ggm_ref, cos_ref, sup_ref, sdn_ref,
                    cq_out, ckv_out, kr_out, u_out, vg_out, siga_out, sigb_out):
    h = _rms(x_ref[...], gmix_ref[...]).astype(BF16)
    cq_out[...] = _rms(_dot(h, wcq_ref[...]), gq_ref[...]).astype(BF16)
    ckv_out[...] = _rms(_dot(h, wckv_ref[...]), gkv_ref[...])
    kr = _dot(h, wkr_ref[...])
    ms = jnp.sum(kr * kr, axis=-1, keepdims=True) * (1.0 / QK_ROPE)
    kr = kr * lax.rsqrt(ms + EPS) * gkr_ref[...]
    kr_out[...] = _rope_slot(kr, cos_ref[...], sup_ref[...], sdn_ref[...])
    u_out[...] = _gelu(_dot(h, wu_ref[...])).astype(BF16)
    vg_out[...] = _rms(_gelu(_dot(h, wv_ref[...])), ggm_ref[...])
    siga_out[...] = jax.nn.sigmoid(_dot(h, wga_ref[...])).astype(BF16)
    sigb_out[...] = jax.nn.sigmoid(_dot(h, wgb_ref[...])).astype(BF16)


def _in_proj(x2d, w, tabs, tm):
    t, d = x2d.shape
    cos, sup, sdn = tabs
    ntab = cos.shape[0] // tm
    tab_spec = pl.BlockSpec((tm, SLOT), lambda i: (i % ntab, 0))
    consts = [w['g_mix'], w['w_cq'], w['w_ckv'], w['w_kr'], w['w_u'], w['w_v'], w['w_ga'], w['w_gb'],
              w['g_q_lat'], w['g_kv_lat'], w['g_kr'], w['g_gm']]
    gw = w['w_u'].shape[1]
    widths = [w['w_cq'].shape[1], w['w_ckv'].shape[1], SLOT, gw, gw, d, d]
    dtypes = [BF16, F32, F32, BF16, F32, BF16, BF16]
    return pl.pallas_call(
        _in_proj_kernel,
        grid=(t // tm,),
        in_specs=[_row_spec(tm, d)] + [_const_spec(c.shape) for c in consts] + [tab_spec] * 3,
        out_specs=[_row_spec(tm, n) for n in widths],
        out_shape=[jax.ShapeDtypeStruct((t, n), dt) for n, dt in zip(widths, dtypes)],
        compiler_params=_params(("parallel",)),
        name="in_proj",
    )(x2d, *consts, cos, sup, sdn)


def _gmlp_kernel(u_ref, vg_ref, sigb_ref, mix_ref, bias_ref, wob_ref, out_ref):
    vgb = vg_ref[...].astype(BF16)
    parts = []
    for g in range(GM_GROUPS):
        sl = slice(g * LANES, (g + 1) * LANES)
        mixed = _dot(mix_ref[g], vgb[:, sl]) + bias_ref[:, sl]
        parts.append((u_ref[:, sl].astype(F32) * mixed).astype(BF16))
    ob = _dot(jnp.concatenate(parts, axis=1), wob_ref[...])
    out_ref[...] = sigb_ref[...].astype(F32) * ob


def _gmlp(u, vg, sigb, mix, bias, w_ob, tm):
    t, gw = u.shape
    d = w_ob.shape[1]
    return pl.pallas_call(
        _gmlp_kernel,
        grid=(t // tm,),
        in_specs=[_row_spec(tm, gw), _row_spec(tm, gw), _row_spec(tm, d),
                  _const_spec(mix.shape), _const_spec(bias.shape), _const_spec(w_ob.shape)],
        out_specs=_row_spec(tm, d),
        out_shape=jax.ShapeDtypeStruct((t, d), F32),
        compiler_params=_params(("parallel",)),
        name="gmlp",
    )(u, vg, sigb, mix, bias, w_ob)


def _q_proj_kernel(cq_ref, wuq_ref, gq_ref, mavg_ref, cos_ref, sup_ref, sdn_ref, q_out):
    q = _dot(cq_ref[...], wuq_ref[...])
    cos, sup, sdn = cos_ref[...], sup_ref[...], sdn_ref[...]
    for h in range(MLA_HEADS):
        sl = slice(h * SLOT, (h + 1) * SLOT)
        qs = q[:, sl]
        y = qs * lax.rsqrt(_group_mean(qs * qs, mavg_ref) + EPS) * gq_ref[...]
        q_out[:, sl] = _rope_slot(y, cos, sup, sdn).astype(BF16)


def _q_proj(cq, w, tabs, tm):
    t, ql = cq.shape
    cos, sup, sdn = tabs
    ntab = cos.shape[0] // tm
    tab_spec = pl.BlockSpec((tm, SLOT), lambda i: (i % ntab, 0))
    n = MLA_HEADS * SLOT
    return pl.pallas_call(
        _q_proj_kernel,
        grid=(t // tm,),
        in_specs=[_row_spec(tm, ql), _const_spec(w['w_uq'].shape), _const_spec(w['g_q'].shape),
                  _const_spec(w['m_avg'].shape)] + [tab_spec] * 3,
        out_specs=_row_spec(tm, n),
        out_shape=jax.ShapeDtypeStruct((t, n), BF16),
        compiler_params=_params(("parallel",)),
        name="q_proj",
    )(cq, w['w_uq'], w['g_q'], w['m_avg'], cos, sup, sdn)


def _kv_proj_kernel(ckv_ref, kr_ref, wuk_ref, wuv_ref, gkn_ref, mavg_ref, k_out, v_out):
    c = ckv_ref[...].astype(BF16)
    kn = _dot(c, wuk_ref[...])
    kr = kr_ref[...]
    for h in range(MLA_HEADS):
        sl = slice(h * SLOT, (h + 1) * SLOT)
        ks = kn[:, sl]
        y = ks * lax.rsqrt(_group_mean(ks * ks, mavg_ref) + EPS) * gkn_ref[...]
        k_out[:, sl] = (y + kr).astype(BF16)
    v_out[...] = _dot(c, wuv_ref[...]).astype(BF16)


def _kv_proj(ckv, kr, w, tm):
    t, kl = ckv.shape
    n = MLA_HEADS * SLOT
    return pl.pallas_call(
        _kv_proj_kernel,
        grid=(t // tm,),
        in_specs=[_row_spec(tm, kl), _row_spec(tm, SLOT), _const_spec(w['w_uk'].shape),
                  _const_spec(w['w_uv'].shape), _const_spec(w['g_kn'].shape),
                  _const_spec(w['m_avg'].shape)],
        out_specs=[_row_spec(tm, n), _row_spec(tm, n)],
        out_shape=[jax.ShapeDtypeStruct((t, n), BF16)] * 2,
        compiler_params=_params(("parallel",)),
        name="kv_proj",
    )(ckv, kr, w['w_uk'], w['w_uv'], w['g_kn'], w['m_avg'])


def _attn_prompt_kernel(bound_ref, q_ref, k_ref, v_ref, o_ref, *, tile):
    n_tiles = q_ref.shape[1] // tile
    bound = bound_ref[0]
    first = lax.broadcasted_iota(jnp.int32, (1, LANES), 1) < V_HEAD
    row_chunk = lax.broadcasted_iota(jnp.int32, (tile, tile), 0) // CHUNK
    col_chunk = lax.broadcasted_iota(jnp.int32, (tile, tile), 1) // CHUNK
    visible = col_chunk <= row_chunk

    def static_step(q, kstart, carry, masked):
        sums, acc = carry
        kstart = pl.multiple_of(kstart, tile)
        k = k_ref[0, pl.ds(kstart, tile), :]
        v = v_ref[0, pl.ds(kstart, tile), :]
        new_sums = []
        for h in range(2):
            sl = slice(h * SLOT, (h + 1) * SLOT)
            s = _dot_nt(q[:, sl], k[:, sl]) * F32(MLA_SCALE * LOG2_E) - bound
            if masked:
                s = jnp.where(visible, s, F32(-1e30))
            p = jnp.exp2(s)
            part = sums[h]
            for c in range(tile // LANES):
                part = part + p[:, c * LANES:(c + 1) * LANES]
            new_sums.append(part)
            acc = acc + _dot(p.astype(BF16), v[:, sl])
        return tuple(new_sums), acc

    def static_tile(qi, _):
        rows = pl.ds(pl.multiple_of(qi * tile, tile), tile)
        q = q_ref[0, rows, :]
        zeros = jnp.zeros((tile, LANES), F32)

        def pair(i, c):
            c = static_step(q, 2 * i * tile, c, False)
            return static_step(q, (2 * i + 1) * tile, c, False)

        def odd_tail(c):
            return static_step(q, qi * tile, static_step(q, (qi - 1) * tile, c, False), True)

        carry = lax.fori_loop(0, qi // 2, pair, ((zeros, zeros), zeros))
        sums, acc = lax.cond(qi % 2 == 1, odd_tail, lambda c: static_step(q, qi * tile, c, True), carry)
        l0 = jnp.sum(sums[0], axis=-1, keepdims=True)
        l1 = jnp.sum(sums[1], axis=-1, keepdims=True)
        o_ref[0, rows, :] = (acc / jnp.where(first, l0, l1)).astype(BF16)
        return 0

    @pl.when(bound < STATIC_SHIFT_LIMIT)
    def _():
        lax.fori_loop(0, n_tiles, static_tile, 0)

    def step(q, kstart, carry, masked):
        ms, ls, acc = carry
        kstart = pl.multiple_of(kstart, tile)
        k = k_ref[0, pl.ds(kstart, tile), :]
        v = v_ref[0, pl.ds(kstart, tile), :]
        new_ms, new_ls, alphas, pvs = [], [], [], []
        for h in range(2):
            sl = slice(h * SLOT, (h + 1) * SLOT)
            s = _dot_nt(q[:, sl], k[:, sl]) * F32(MLA_SCALE * LOG2_E)
            if masked:
                s = jnp.where(visible, s, F32(-1e30))
            m_new = jnp.maximum(ms[h], jnp.max(s, axis=-1, keepdims=True))
            alpha = jnp.exp2(ms[h] - m_new)
            p = jnp.exp2(s - m_new)
            new_ls.append(alpha * ls[h] + jnp.sum(p, axis=-1, keepdims=True))
            new_ms.append(m_new)
            alphas.append(alpha)
            pvs.append(_dot(p.astype(BF16), v[:, sl]))
        acc = acc * jnp.where(first, alphas[0], alphas[1]) + pvs[0] + pvs[1]
        return tuple(new_ms), tuple(new_ls), acc

    def online_tile(qi, _):
        rows = pl.ds(pl.multiple_of(qi * tile, tile), tile)
        q = q_ref[0, rows, :]
        neg = jnp.full((tile, 1), -1e30, F32)
        zero = jnp.zeros((tile, 1), F32)
        init = ((neg, neg), (zero, zero), jnp.zeros((tile, LANES), F32))
        carry = lax.fori_loop(0, qi, lambda i, c: step(q, i * tile, c, False), init)
        _, ls, acc = step(q, qi * tile, carry, True)
        o_ref[0, rows, :] = (acc / jnp.where(first, ls[0], ls[1])).astype(BF16)
        return 0

    @pl.when(bound >= STATIC_SHIFT_LIMIT)
    def _():
        lax.fori_loop(0, n_tiles, online_tile, 0)


def _score_bound(w):
    gmax = lambda g, lo, n: jnp.max(jnp.abs(g[0, lo:lo + n]))
    nope = QK_NOPE * gmax(w['g_q'], 0, QK_NOPE) * gmax(w['g_kn'], 0, QK_NOPE)
    rope = QK_ROPE * gmax(w['g_q'], ROPE_LO, QK_ROPE) * gmax(w['g_kr'], ROPE_LO, QK_ROPE)
    return (BOUND_MARGIN * MLA_SCALE * LOG2_E * (nope + rope)).reshape(1).astype(F32)


def _attn_prompt(q, k, v, bound, tile):
    b, s, _ = q.shape
    pairs = MLA_HEADS // 2
    pair_block = pl.BlockSpec((1, s, 2 * SLOT), lambda bi, p: (bi, 0, p))
    return pl.pallas_call(
        functools.partial(_attn_prompt_kernel, tile=tile),
        grid=(b, pairs),
        in_specs=[pl.BlockSpec(memory_space=pltpu.SMEM), pair_block, pair_block, pair_block],
        out_specs=pl.BlockSpec((1, s, LANES), lambda bi, p: (bi, 0, p)),
        out_shape=jax.ShapeDtypeStruct((b, s, pairs * LANES), BF16),
        compiler_params=_params(("parallel", "parallel")),
        name="attn_prompt",
    )(bound, q, k, v)


def _attn_sample_kernel(q_ref, pckv_ref, pkr_ref, nckv_ref, nkr_ref, wuk_ref, wuv_ref, gkn_ref, place_ref,
                        o_ref):
    sq = q_ref.shape[1]
    rows = MLA_HEADS * sq
    gkn = gkn_ref[...]

    def keys_values(ckv, kr_slot):
        c = ckv.astype(BF16)
        kn = _dot(c, wuk_ref[...])
        parts = []
        for h in range(MLA_HEADS):
            ks = kn[:, h * SLOT:(h + 1) * SLOT]
            ms = jnp.sum(ks * ks, axis=-1, keepdims=True) * (1.0 / QK_NOPE)
            parts.append((ks * lax.rsqrt(ms + EPS) * gkn + kr_slot).astype(BF16))
        return jnp.concatenate(parts, axis=1), _dot(c, wuv_ref[...]).astype(BF16)

    past_kr = _dot(pkr_ref[0].astype(BF16), place_ref[...])
    k_p, v_p = keys_values(pckv_ref[0], past_kr)
    k_n, v_n = keys_values(nckv_ref[0], nkr_ref[0])
    k = jnp.concatenate([k_p, k_n], axis=0)
    v = jnp.concatenate([v_p, v_n], axis=0)

    qt = jnp.concatenate([q_ref[0].astype(F32)] * MLA_HEADS, axis=0)
    q_shape = (rows, MLA_HEADS * SLOT)
    own = (lax.broadcasted_iota(jnp.int32, q_shape, 0) // sq
           == lax.broadcasted_iota(jnp.int32, q_shape, 1) // SLOT)
    qbd = jnp.where(own, qt, 0.0).astype(BF16)
    s = _dot_nt(qbd, k) * MLA_SCALE
    p = jnp.exp(s - jnp.max(s, axis=-1, keepdims=True))
    p = p / jnp.sum(p, axis=-1, keepdims=True)
    o_all = _dot(p.astype(BF16), v)
    o_shape = o_all.shape
    own = (lax.broadcasted_iota(jnp.int32, o_shape, 0) // sq
           == lax.broadcasted_iota(jnp.int32, o_shape, 1) // V_HEAD)
    o_all = jnp.where(own, o_all, 0.0)
    o = o_all[0:sq]
    for h in range(1, MLA_HEADS):
        o = o + o_all[h * sq:(h + 1) * sq]
    o_ref[0] = o.astype(BF16)


def _attn_sample(q, past_ckv, past_kr, new_ckv, new_kr, w):
    b, sq, nq = q.shape
    npast, kl = past_ckv.shape[1:]
    nv = w['w_uv_plain'].shape[1]
    stream = lambda n, width: pl.BlockSpec((1, n, width), lambda bi: (bi, 0, 0))
    consts = [w['w_uk'], w['w_uv_plain'], w['g_kn'], w['rope_place']]
    return pl.pallas_call(
        _attn_sample_kernel,
        grid=(b,),
        in_specs=[stream(sq, nq), stream(npast, kl), stream(npast, QK_ROPE), stream(sq, kl), stream(sq, SLOT)]
        + [_const_spec(c.shape) for c in consts],
        out_specs=stream(sq, nv),
        out_shape=jax.ShapeDtypeStruct((b, sq, nv), BF16),
        compiler_params=_params(("parallel",)),
        name="attn_sample",
    )(q, past_ckv, past_kr, new_ckv, new_kr, *consts)


def _merge_kernel(x_ref, o_ref, siga_ref, gbob_ref, woa_ref, wo_ref, gx_ref, wcq_ref, gcq_ref,
                  x1_out, qc_out):
    oa = _dot(o_ref[...], woa_ref[...])
    merged = siga_ref[...].astype(F32) * oa + gbob_ref[...]
    x1 = x_ref[...] + _dot(merged.astype(BF16), wo_ref[...])
    x1_out[...] = x1
    qc = _dot(_rms(x1, gx_ref[...]).astype(BF16), wcq_ref[...])
    for h in range(MEM_HEADS):
        sl = slice(h * MEM_HEAD_DIM, (h + 1) * MEM_HEAD_DIM)
        qc_out[:, sl] = _rms(qc[:, sl], gcq_ref[...]).astype(BF16)


def _merge(x2d, o, siga, gbob, w, tm):
    t, d = x2d.shape
    consts = [w['w_oa'], w['w_o'], w['g_xattn'], w['w_cq_mem'], w['g_cq']]
    nq = w['w_cq_mem'].shape[1]
    return pl.pallas_call(
        _merge_kernel,
        grid=(t // tm,),
        in_specs=[_row_spec(tm, d), _row_spec(tm, o.shape[1]), _row_spec(tm, d), _row_spec(tm, d)]
        + [_const_spec(c.shape) for c in consts],
        out_specs=[_row_spec(tm, d), _row_spec(tm, nq)],
        out_shape=[jax.ShapeDtypeStruct((t, d), F32), jax.ShapeDtypeStruct((t, nq), BF16)],
        compiler_params=_params(("parallel",)),
        name="merge",
    )(x2d, o, siga, gbob, *consts)


def _mem_kv_kernel(mem_ref, gmem_ref, wck_ref, gck_ref, wcv_ref, k_out, v_out):
    m = _rms(mem_ref[...], gmem_ref[...]).astype(BF16)
    k = _dot(m, wck_ref[...])
    for h in range(MEM_HEADS):
        sl = slice(h * MEM_HEAD_DIM, (h + 1) * MEM_HEAD_DIM)
        k_out[:, sl] = _rms(k[:, sl], gck_ref[...])
    v_out[...] = _dot(m, wcv_ref[...])


def _mem_kv(mem2d, w, tm):
    t, d = mem2d.shape
    consts = [w['g_mem'], w['w_ck'], w['g_ck'], w['w_cv']]
    n = w['w_ck'].shape[1]
    return pl.pallas_call(
        _mem_kv_kernel,
        grid=(t // tm,),
        in_specs=[_row_spec(tm, d)] + [_const_spec(c.shape) for c in consts],
        out_specs=[_row_spec(tm, n)] * 2,
        out_shape=[jax.ShapeDtypeStruct((t, n), F32)] * 2,
        compiler_params=_params(("parallel",)),
        name="mem_kv",
    )(mem2d, *consts)


def _cross_kernel(q_ref, k_ref, v_ref, o_ref):
    q = q_ref[0]
    k = k_ref[0].astype(BF16)
    v = v_ref[0].astype(BF16)
    for h in range(MEM_HEADS):
        sl = slice(h * MEM_HEAD_DIM, (h + 1) * MEM_HEAD_DIM)
        s = _dot_nt(q[:, sl], k[:, sl]) * MEM_SCALE
        p = jnp.exp(s - jnp.max(s, axis=-1, keepdims=True))
        p = p / jnp.sum(p, axis=-1, keepdims=True)
        o_ref[0, :, sl] = _dot(p.astype(BF16), v[:, sl]).astype(BF16)


def _cross(qc, mk, mv, tc):
    b, s, n = qc.shape
    nm = mk.shape[1]
    return pl.pallas_call(
        _cross_kernel,
        grid=(b, s // tc),
        in_specs=[pl.BlockSpec((1, tc, n), lambda bi, i: (bi, i, 0)),
                  pl.BlockSpec((1, nm, n), lambda bi, i: (bi, 0, 0)),
                  pl.BlockSpec((1, nm, n), lambda bi, i: (bi, 0, 0))],
        out_specs=pl.BlockSpec((1, tc, n), lambda bi, i: (bi, i, 0)),
        out_shape=jax.ShapeDtypeStruct((b, s, n), BF16),
        compiler_params=_params(("parallel", "parallel")),
        name="cross_attn",
    )(qc, mk, mv)


def _cross_out_kernel(x1_ref, oc_ref, wco_ref, gffn_ref, wpq_ref, x2_out, h_out, q_out):
    x2 = x1_ref[...] + _dot(oc_ref[...], wco_ref[...])
    x2_out[...] = x2
    h = _rms(x2, gffn_ref[...]).astype(BF16)
    h_out[...] = h
    q_out[...] = _dot(h, wpq_ref[...]).astype(BF16)


def _cross_out(x1, oc, w, tm):
    t, d = x1.shape
    nq = w['w_pq'].shape[1]
    return pl.pallas_call(
        _cross_out_kernel,
        grid=(t // tm,),
        in_specs=[_row_spec(tm, d), _row_spec(tm, oc.shape[1]), _const_spec(w['w_co'].shape),
                  _const_spec(w['g_ffn'].shape), _const_spec(w['w_pq'].shape)],
        out_specs=[_row_spec(tm, d), _row_spec(tm, d), _row_spec(tm, nq)],
        out_shape=[jax.ShapeDtypeStruct((t, d), F32), jax.ShapeDtypeStruct((t, d), BF16),
                   jax.ShapeDtypeStruct((t, nq), BF16)],
        compiler_params=_params(("parallel",)),
        name="cross_out",
    )(x1, oc, w['w_co'], w['g_ffn'], w['w_pq'])


def _pair_blocks():
    k = PEER_TOPK
    lead = 2
    blocks, flat = [], []
    for a in range(lead):
        nb = k // (a + 1)
        blocks.append(('row', a, nb))
        flat += [a * k + b for b in range(nb)]
    for b in range(lead):
        na = k // (b + 1)
        blocks.append(('col', b, na))
        flat += [a * k + b if a >= lead else -1 for a in range(na)]
    cells = [(a, b) for a in range(lead, k) for b in range(lead, k) if (a + 1) * (b + 1) <= k]
    pad = -len(cells) % 8
    blocks.append(('cells', cells + [cells[0]] * pad, len(cells) + pad))
    flat += [a * k + b for a, b in cells] + [-1] * pad
    assert all(n % 8 == 0 for _, _, n in blocks)
    return blocks, np.asarray(flat, np.float32)


def _pick_rows(v, ranks):
    row = lax.broadcasted_iota(jnp.int32, (len(ranks), v.shape[1]), 0)
    out = jnp.broadcast_to(v[ranks[0]:ranks[0] + 1, :], row.shape)
    for i in range(1, len(ranks)):
        if ranks[i] != ranks[0]:
            out = jnp.where(row == i, v[ranks[i]:ranks[i] + 1, :], out)
    return out


def _pair_candidates(v0, v1, blocks, combine):
    parts = []
    for kind, spec, n in blocks:
        if kind == 'row':
            parts.append(combine(v0[spec:spec + 1, :], v1[0:n, :]))
        elif kind == 'col':
            parts.append(combine(v0[0:n, :], v1[spec:spec + 1, :]))
        else:
            parts.append(combine(_pick_rows(v0, [a for a, _ in spec]), _pick_rows(v1, [b for _, b in spec])))
    return jnp.concatenate(parts, axis=0)


def _peer_kernel(h_ref, x_ref, q_ref, keys_ref, flat_ref, u_ref, v_ref, out_ref,
                 mask_scr, sv_scr, si_scr, ts_scr, te_scr, it_scr, jt_scr, gt_scr,
                 isel_scr, jsel_scr, gsel_scr):
    tm = h_ref.shape[0]
    te = v_ref.shape[0]
    r = pl.program_id(0)
    e = pl.program_id(1)
    n = pl.num_programs(0) - 2
    rows_per_step = te // N_KEYS
    tokens_per_head = tm // PEER_HEADS
    blocks, _ = _pair_blocks()
    neg_inf = F32(-jnp.inf)
    key_rows = lax.broadcasted_iota(jnp.int32, (N_KEYS, LANES), 0).astype(F32)
    swept = lax.rem(r, 2)
    ranked = 1 - swept

    def mask_token(t):
        irow = isel_scr[swept, pl.ds(t, 1), :]
        jrow = jsel_scr[swept, pl.ds(t, 1), :]
        grow = gsel_scr[swept, pl.ds(t, 1), :]
        rw = jnp.where(key_rows == irow, 0.5 * grow, 0.0).astype(BF16)
        cw = jnp.where(key_rows == jrow, 1.0, 0.0).astype(BF16)
        r0 = pl.multiple_of(t * MASK_PITCH, 8)
        mask_scr[pl.ds(r0, N_KEYS), :] = _dot_nt(rw, cw)

    def rank_pairs(hd):
        sv0, sv1 = sv_scr[2 * hd], sv_scr[2 * hd + 1]
        si0, si1 = si_scr[2 * hd], si_scr[2 * hd + 1]
        flat = flat_ref[...]
        cand = _pair_candidates(sv0, sv1, blocks, lambda x, y: x + y)
        cand = jnp.where(flat >= 0.0, cand, neg_inf)
        cidx = _pair_candidates(si0, si1, blocks, lambda x, y: x * F32(N_KEYS) + y)
        for kk in range(PEER_TOPK):
            m = jnp.max(cand, axis=0, keepdims=True)
            pos = jnp.min(jnp.where(cand == m, flat, F32(PEER_TOPK * PEER_TOPK)), axis=0, keepdims=True)
            sel = flat == pos
            ts_scr[kk:kk + 1, :] = m
            te_scr[kk:kk + 1, :] = jnp.max(jnp.where(sel, cidx, F32(-1.0)), axis=0, keepdims=True)
            cand = jnp.where(sel, neg_inf, cand)
        top = ts_scr[...]
        ex = jnp.exp(top - top[0:1, :])
        gate = ex / jnp.sum(ex, axis=0, keepdims=True)
        eidx = te_scr[...]
        irow = jnp.floor(eidx * F32(1.0 / N_KEYS))
        r0 = pl.multiple_of(hd * PEER_TOPK, PEER_TOPK)
        it_scr[pl.ds(r0, PEER_TOPK), :] = irow
        jt_scr[pl.ds(r0, PEER_TOPK), :] = eidx - irow * F32(N_KEYS)
        gt_scr[pl.ds(r0, PEER_TOPK), :] = gate

    def publish_ranked():
        isel_scr[ranked] = it_scr[...].T
        jsel_scr[ranked] = jt_scr[...].T
        gsel_scr[ranked] = gt_scr[...].T

    @pl.when(e == 0)
    def _():
        out_ref[...] = x_ref[...]

        @pl.when(r == 1)
        def _():
            def head(hd, _):
                rank_pairs(hd)
                return 0

            lax.fori_loop(0, PEER_HEADS, head, 0)
            publish_ranked()

        @pl.when(jnp.logical_and(r >= 2, r <= n))
        def _():
            def trip(i, _):
                for t in range(tokens_per_head):
                    mask_token(i * tokens_per_head + t)
                rank_pairs(i)
                return 0

            lax.fori_loop(0, PEER_HEADS, trip, 0)
            publish_ranked()

        @pl.when(r == n + 1)
        def _():
            def trip(i, _):
                for t in range(MASK_UNROLL):
                    mask_token(i * MASK_UNROLL + t)
                return 0

            lax.fori_loop(0, tm // MASK_UNROLL, trip, 0)

    def slot_scores():
        d0 = pl.multiple_of(e * PEER_HALF, PEER_HALF)
        return _dot_nt(q_ref[:, pl.ds(d0, PEER_HALF)], keys_ref[e])

    def rank_slot(scores):
        lane = lax.broadcasted_iota(jnp.int32, (tm, N_KEYS), 1)
        picked = jnp.zeros((tm, N_KEYS), jnp.int32)
        a = scores
        for kk in range(PEER_TOPK):
            idx = jnp.argmax(a, axis=-1, keepdims=True).astype(jnp.int32)
            a = jnp.where(lane == idx, neg_inf, a)
            picked = jnp.where(lane == kk, idx, picked)
        vals = jnp.take_along_axis(scores, picked, axis=1)
        sv_scr[e] = vals.T[0:PEER_TOPK, :]
        si_scr[e] = (N_KEYS - 1 - picked).astype(F32).T[0:PEER_TOPK, :]

    def expert_chunk():
        h = h_ref[...]
        row0 = e * rows_per_step
        weighted = []
        for c in range(te // PEER_CHUNK):
            rows = slice(c * PEER_CHUNK, (c + 1) * PEER_CHUNK)
            pre = _dot_nt(h, u_ref[rows, :])
            half_mask = jnp.concatenate(
                [mask_scr[pl.ds(row0 + c * (PEER_CHUNK // N_KEYS) + rr, tm, stride=MASK_PITCH), :]
                 for rr in range(PEER_CHUNK // N_KEYS)], axis=1)
            weighted.append((pre * (1.0 + lax.erf(pre * np.float32(np.sqrt(0.5)))) * half_mask).astype(BF16))
        out_ref[...] += _dot(jnp.concatenate(weighted, axis=1), v_ref[...])

    ranks = r < n
    sweeps = r >= 2

    @pl.when(jnp.logical_and(ranks, jnp.logical_not(sweeps)))
    def _():
        rank_slot(slot_scores())

    @pl.when(jnp.logical_and(ranks, sweeps))
    def _():
        a = slot_scores()
        expert_chunk()
        rank_slot(a)

    @pl.when(jnp.logical_and(jnp.logical_not(ranks), sweeps))
    def _():
        expert_chunk()


def _peer(h, x2, q_route, w, tm, te):
    t, d = h.shape
    n = t // tm
    nexp = w['peer_v'].shape[0]
    assert nexp // te == 2 * PEER_HEADS, "one half-key slot is ranked per expert step"
    assert tm % PEER_HEADS == 0 and tm % MASK_UNROLL == 0
    nsel = PEER_HEADS * PEER_TOPK
    _, flat = _pair_blocks()
    flat = jnp.asarray(np.repeat(flat[:, None], tm, axis=1))
    swept_tile = lambda r, e: (jnp.clip(r - 2, 0, n - 1), 0)
    dense_tile = lambda width: pl.BlockSpec((tm, width), swept_tile, pipeline_mode=pl.Buffered(1))
    expert_tile = pl.BlockSpec((te, d), lambda r, e: (jnp.where(r >= 2, e, 0), 0))
    return pl.pallas_call(
        _peer_kernel,
        grid=(n + 2, nexp // te),
        in_specs=[dense_tile(d), dense_tile(d),
                  pl.BlockSpec((tm, q_route.shape[1]), lambda r, e: (jnp.minimum(r, n - 1), 0),
                               pipeline_mode=pl.Buffered(1)),
                  _const_spec(w['sub_keys'].shape), _const_spec(flat.shape), expert_tile, expert_tile],
        out_specs=pl.BlockSpec((tm, d), swept_tile),
        out_shape=jax.ShapeDtypeStruct((t, d), F32),
        scratch_shapes=[pltpu.VMEM((tm * MASK_PITCH, LANES), F32),
                        pltpu.VMEM((2 * PEER_HEADS, PEER_TOPK, tm), F32),
                        pltpu.VMEM((2 * PEER_HEADS, PEER_TOPK, tm), F32),
                        pltpu.VMEM((PEER_TOPK, tm), F32),
                        pltpu.VMEM((PEER_TOPK, tm), F32),
                        pltpu.VMEM((nsel, tm), F32),
                        pltpu.VMEM((nsel, tm), F32),
                        pltpu.VMEM((nsel, tm), F32),
                        pltpu.VMEM((2, tm, nsel), F32),
                        pltpu.VMEM((2, tm, nsel), F32),
                        pltpu.VMEM((2, tm, nsel), F32)],
        compiler_params=_params(("arbitrary", "arbitrary"), PEER_VMEM_LIMIT),
        name="peer",
    )(h, x2, q_route, w['sub_keys'], flat, w['peer_u'], w['peer_v'])


def _slot_gain(parts):
    row = jnp.zeros((SLOT,), F32)
    for off, g in parts:
        row = lax.dynamic_update_slice(row, g.astype(F32), (off,))
    return row[None, :]


def _group_avg_matrix():
    m = np.zeros((SLOT, SLOT), np.float32)
    m[:QK_NOPE, :QK_NOPE] = 1.0 / QK_NOPE
    m[ROPE_LO:ROPE_LO + QK_ROPE, ROPE_LO:ROPE_LO + QK_ROPE] = 1.0 / QK_ROPE
    return jnp.asarray(np.concatenate([m, m], axis=0), BF16)


def _rope_tables(pos):
    inv = ROPE_THETA ** (-jnp.arange(ROPE_HALF, dtype=F32) / ROPE_HALF)
    ang = pos.astype(F32)[:, None] * inv[None, :]
    cos, sin = jnp.cos(ang), jnp.sin(ang)
    n = pos.shape[0]
    tail = SLOT - ROPE_LO - QK_ROPE
    cos_t = jnp.concatenate([jnp.ones((n, ROPE_LO), F32), cos, cos, jnp.zeros((n, tail), F32)], axis=1)
    sin_up = jnp.concatenate([jnp.zeros((n, ROPE_LO + ROPE_HALF), F32), sin, jnp.zeros((n, tail), F32)], axis=1)
    sin_dn = jnp.concatenate([jnp.zeros((n, ROPE_LO), F32), -sin, jnp.zeros((n, tail + ROPE_HALF), F32)], axis=1)
    return cos_t, sin_up, sin_dn


def _prep_weights(p, i):
    w_in = p['w_in'][i]
    q_lora = p['g_q_lat'].shape[1]
    kv_lora = p['g_kv_lat'].shape[1]
    gw = p['g_gm'].shape[1]
    d = w_in.shape[0]
    o = np.cumsum([0, q_lora, kv_lora, QK_ROPE, gw, gw, d, d])
    seg = lambda k: w_in[:, o[k]:o[k + 1]]
    row = lambda a: a[i][None, :].astype(F32)
    w = {}
    w['g_mix'] = row(p['g_mix'])
    w['w_cq'] = seg(0).astype(BF16)
    w['w_ckv'] = seg(1).astype(BF16)
    w['w_kr'] = jnp.pad(seg(2), ((0, 0), (ROPE_LO, SLOT - ROPE_LO - QK_ROPE))).astype(BF16)
    w['w_u'] = seg(3).astype(BF16)
    w['w_v'] = seg(4).astype(BF16)
    w['w_ga'] = seg(5).astype(BF16)
    w['w_gb'] = seg(6).astype(BF16)
    w['g_q_lat'] = row(p['g_q_lat'])
    w['g_kv_lat'] = row(p['g_kv_lat'])
    w['g_kr'] = _slot_gain([(ROPE_LO, p['g_kr'][i])])
    w['g_gm'] = row(p['g_gm'])
    head_dim = QK_NOPE + QK_ROPE
    w_uq = p['w_uq'][i].reshape(q_lora, MLA_HEADS, head_dim)
    w['w_uq'] = jnp.pad(w_uq, ((0, 0), (0, 0), (0, SLOT - head_dim))).reshape(q_lora, -1).astype(BF16)
    w['g_q'] = _slot_gain([(0, p['g_qn'][i]), (ROPE_LO, p['g_qr'][i])])
    w['m_avg'] = _group_avg_matrix()
    w_uk = p['w_uk'][i].reshape(kv_lora, MLA_HEADS, QK_NOPE)
    w['w_uk'] = jnp.pad(w_uk, ((0, 0), (0, 0), (0, SLOT - QK_NOPE))).reshape(kv_lora, -1).astype(BF16)
    w['g_kn'] = _slot_gain([(0, p['g_kn'][i])])
    w_uv = p['w_uv'][i].reshape(kv_lora, MLA_HEADS // 2, 2, V_HEAD)
    eye2 = jnp.eye(2, dtype=w_uv.dtype)
    w['w_uv'] = jnp.einsum('cpjd,jk->cpjkd', w_uv, eye2).reshape(kv_lora, -1).astype(BF16)
    w['w_uv_plain'] = p['w_uv'][i].astype(BF16)
    place = np.zeros((QK_ROPE, SLOT), np.float32)
    place[np.arange(QK_ROPE), ROPE_LO + np.arange(QK_ROPE)] = 1.0
    w['rope_place'] = jnp.asarray(place, BF16)
    w['w_oa'] = p['w_oa'][i].astype(BF16)
    w['w_ob'] = p['w_ob'][i].astype(BF16)
    w['w_o'] = p['w_o'][i].astype(BF16)
    w['g_xattn'] = row(p['g_xattn'])
    w['g_mem'] = row(p['g_mem'])
    w['w_cq_mem'] = p['w_cq'][i].astype(BF16)
    w['g_cq'] = row(p['g_cq'])
    w['w_ck'] = p['w_ck'][i].astype(BF16)
    w['g_ck'] = row(p['g_ck'])
    w['w_cv'] = p['w_cv'][i].astype(BF16)
    w['w_co'] = p['w_co'][i].astype(BF16)
    w['g_ffn'] = row(p['g_ffn'])
    w['w_pq'] = p['w_pq'][i].astype(BF16)
    w['sub_keys'] = p['sub_keys'][i].reshape(2 * PEER_HEADS, N_KEYS, PEER_HALF)[:, ::-1, :].astype(BF16)
    w['peer_u'] = p['peer_u'][i].astype(BF16)
    w['peer_v'] = p['peer_v'][i].astype(BF16)
    w['w_s'] = p['w_s'][i]
    w['b_s'] = p['b_s'][i]
    return w


def _spatial_operands(w, seq, tm):
    chunk = min(seq, GM_CHUNK)
    reps = tm // chunk
    w_mask = jnp.tril(w['w_s'][:, :chunk, :chunk])
    w_mask = w_mask.astype(BF16)
    mix = jnp.concatenate(
        [jnp.pad(w_mask, ((0, 0), (0, 0), (a * chunk, (reps - 1 - a) * chunk))) for a in range(reps)], axis=1)
    bias = jnp.tile(w['b_s'][:, :chunk].T, (reps, 1))
    bias = jnp.repeat(bias, LANES, axis=1)
    return mix.astype(BF16), bias.astype(F32)


def _token_mixer(x2d, w, seq, pos, past, tm):
    t, d = x2d.shape
    nb = t // seq
    period = max(seq, tm)
    tabs = _rope_tables(jnp.tile(pos, period // seq))
    cq, ckv, kr, u, vg, siga, sigb = _in_proj(x2d, w, tabs, tm)
    mix, bias = _spatial_operands(w, seq, tm)
    gbob = _gmlp(u, vg, sigb, mix, bias, w['w_ob'], tm)
    wide = min(WIDE_ROW_TILE, t)
    q = _q_proj(cq, w, tabs, wide)
    if past is None:
        k, v = _kv_proj(ckv, kr, w, wide)
        o = _attn_prompt(q.reshape(nb, seq, -1), k.reshape(nb, seq, -1), v.reshape(nb, seq, -1),
                         _score_bound(w), ATTN_TILE)
    else:
        past_ckv, past_kr = past
        o = _attn_sample(q.reshape(nb, seq, -1), past_ckv, past_kr, ckv.reshape(nb, seq, -1),
                         kr.reshape(nb, seq, SLOT), w)
    x1, qc = _merge(x2d, o.reshape(t, -1), siga, gbob, w, wide)
    return x1, qc, ckv, kr[:, ROPE_LO:ROPE_LO + QK_ROPE], vg


def _tail(x1, qc, mk, mv, w, nb, seq, tm):
    t, d = x1.shape
    oc = _cross(qc.reshape(nb, seq, -1), mk, mv, min(seq, WIDE_ROW_TILE))
    x2, h, q_route = _cross_out(x1, oc.reshape(t, -1), w, min(WIDE_ROW_TILE, t))
    return _peer(h, x2, q_route, w, PEER_ROWS, PEER_EXPERTS)


def kernel(x_prompt, x_sample, cache_mla_ckv, cache_mla_krope, cache_mem_k, cache_mem_v, mem_prompt, g_mix, w_in, g_q_lat, w_uq, g_qn, g_qr, g_kv_lat, g_kr, w_uk, w_uv, g_kn, w_oa, g_gm, w_s, b_s, w_ob, w_o, g_xattn, g_mem, w_cq, g_cq, w_ck, g_ck, w_cv, w_co, g_ffn, w_pq, sub_keys, peer_u, peer_v):
    params = dict(g_mix=g_mix, w_in=w_in, g_q_lat=g_q_lat, w_uq=w_uq, g_qn=g_qn, g_qr=g_qr,
                  g_kv_lat=g_kv_lat, g_kr=g_kr, w_uk=w_uk, w_uv=w_uv, g_kn=g_kn, w_oa=w_oa, g_gm=g_gm,
                  w_s=w_s, b_s=b_s, w_ob=w_ob, w_o=w_o, g_xattn=g_xattn, g_mem=g_mem, w_cq=w_cq,
                  g_cq=g_cq, w_ck=w_ck, g_ck=g_ck, w_cv=w_cv, w_co=w_co, g_ffn=g_ffn, w_pq=w_pq,
                  sub_keys=sub_keys, peer_u=peer_u, peer_v=peer_v)
    bp, sp, d = x_prompt.shape
    bs, ss, _ = x_sample.shape
    depth = w_in.shape[0]
    past_len = cache_mla_ckv.shape[2]
    n_mem = mem_prompt.shape[1]
    tm = ROW_TILE
    assert (bp * sp) % tm == 0 and (bs * ss) % tm == 0 and sp % ATTN_TILE == 0
    pos_p = jnp.arange(sp)
    pos_s = past_len + jnp.arange(ss)
    xp = x_prompt.reshape(bp * sp, d)
    xs = x_sample.reshape(bs * ss, d)
    outs = [[] for _ in range(7)]
    for i in range(depth):
        w = _prep_weights(params, i)
        xp1, qcp, ckv_p, kr_p, _ = _token_mixer(xp, w, sp, pos_p, None, tm)
        xs1, qcs, ckv_s, kr_s, vg_s = _token_mixer(xs, w, ss, pos_s, (cache_mla_ckv[i], cache_mla_krope[i]), tm)
        mk_p, mv_p = _mem_kv(mem_prompt.reshape(bp * n_mem, d), w, min(tm, bp * n_mem))
        xp = _tail(xp1, qcp, mk_p.reshape(bp, n_mem, -1), mv_p.reshape(bp, n_mem, -1), w, bp, sp, tm)
        xs = _tail(xs1, qcs, cache_mem_k[i].reshape(bs, n_mem, -1), cache_mem_v[i].reshape(bs, n_mem, -1),
                   w, bs, ss, tm)
        outs[0].append(ckv_p.reshape(bp, sp, -1))
        outs[1].append(kr_p.reshape(bp, sp, -1))
        outs[2].append(mk_p.reshape(bp, n_mem, MEM_HEADS, MEM_HEAD_DIM))
        outs[3].append(mv_p.reshape(bp, n_mem, MEM_HEADS, MEM_HEAD_DIM))
        outs[4].append(ckv_s.reshape(bs, ss, -1))
        outs[5].append(kr_s.reshape(bs, ss, -1))
        outs[6].append(vg_s.reshape(bs, ss, -1))
    return (xp.reshape(bp, sp, d), xs.reshape(bs, ss, d)) + tuple(jnp.stack(o) for o in outs)
```

```python
import functools

import jax
import jax.numpy as jnp
import numpy as np
from jax import lax
from jax.experimental import pallas as pl
from jax.experimental.pallas import tpu as pltpu

CHUNK = 64
EPS = 1e-6
MLA_HEADS = 16
QK_NOPE = 64
QK_ROPE = 32
V_HEAD = 64
ROPE_THETA = 10000.0
MLA_SCALE = (QK_NOPE + QK_ROPE) ** -0.5
GM_CHUNK = 128
GM_GROUPS = 8
MEM_HEADS = 4
MEM_HEAD_DIM = 128
MEM_SCALE = MEM_HEAD_DIM ** -0.5
PEER_HEADS = 8
N_KEYS = 128
PEER_TOPK = 16
PEER_HALF = 128

LANES = 128
SLOT = LANES
ROPE_LO = QK_NOPE
ROPE_HALF = QK_ROPE // 2
ROW_TILE = 512
WIDE_ROW_TILE = 1024
ATTN_TILE = 512
LOG2_E = float(np.log2(np.e))
STATIC_SHIFT_LIMIT = 48.0
BOUND_MARGIN = 1.02
PEER_ROWS = 512
PEER_EXPERTS = 1024
PEER_CHUNK = 512
MASK_PITCH = N_KEYS + 8
MASK_UNROLL = 32
VMEM_LIMIT = 48 * 1024 * 1024
PEER_VMEM_LIMIT = 60 * 1024 * 1024

F32 = jnp.float32
BF16 = jnp.bfloat16
_NT = (((1,), (1,)), ((), ()))


def _dot(a, b):
    return jnp.dot(a, b, preferred_element_type=F32)


def _dot_nt(a, b):
    return lax.dot_general(a, b, _NT, preferred_element_type=F32)


def _rms(xf, g):
    return xf * lax.rsqrt(jnp.mean(xf * xf, axis=-1, keepdims=True) + EPS) * g


def _gelu(x):
    return 0.5 * x * (1.0 + lax.erf(x * np.float32(np.sqrt(0.5))))


def _group_mean(sq, m_ref):
    hi = sq.astype(BF16)
    lo = (sq - hi.astype(F32)).astype(BF16)
    return _dot(jnp.concatenate([hi, lo], axis=1), m_ref[...])


def _rope_slot(y, cos, sin_up, sin_dn):
    return (y * cos + pltpu.roll(y, ROPE_HALF, 1) * sin_up
            + pltpu.roll(y, SLOT - ROPE_HALF, 1) * sin_dn)


def _const_spec(shape):
    nd = len(shape)
    return pl.BlockSpec(shape, lambda *_: (0,) * nd, pipeline_mode=pl.Buffered(1))


def _row_spec(tm, width):
    return pl.BlockSpec((tm, width), lambda i: (i, 0))


def _params(sem, limit=VMEM_LIMIT):
    return pltpu.CompilerParams(dimension_semantics=sem, vmem_limit_bytes=limit)


def _in_proj_kernel(x_ref, gmix_ref, wcq_ref, wckv_ref, wkr_ref, wu_ref, wv_ref, wga_ref, wgb_ref,
                    gq_ref, gkv_ref, gkr_ref, ggm_ref, cos_ref, sup_ref, sdn_ref,
                    cq_out, ckv_out, kr_out, u_out, vg_out, siga_out, sigb_out):
    h = _rms(x_ref[...], gmix_ref[...]).astype(BF16)
    cq_out[...] = _rms(_dot(h, wcq_ref[...]), gq_ref[...]).astype(BF16)
    ckv_out[...] = _rms(_dot(h, wckv_ref[...]), gkv_ref[...])
    kr = _dot(h, wkr_ref[...])
    ms = jnp.sum(kr * kr, axis=-1, keepdims=True) * (1.0 / QK_ROPE)
    kr = kr * lax.rsqrt(ms + EPS) * gkr_ref[...]
    kr_out[...] = _rope_slot(kr, cos_ref[...], sup_ref[...], sdn_ref[...])
    u_out[...] = _gelu(_dot(h, wu_ref[...])).astype(BF16)
    vg_out[...] = _rms(_gelu(_dot(h, wv_ref[...])), ggm_ref[...])
    siga_out[...] = jax.nn.sigmoid(_dot(h, wga_ref[...])).astype(BF16)
    sigb_out[...] = jax.nn.sigmoid(_dot(h, wgb_ref[...])).astype(BF16)


def _in_proj(x2d, w, tabs, tm):
    t, d = x2d.shape
    cos, sup, sdn = tabs
    ntab = cos.shape[0] // tm
    tab_spec = pl.BlockSpec((tm, SLOT), lambda i: (i % ntab, 0))
    consts = [w['g_mix'], w['w_cq'], w['w_ckv'], w['w_kr'], w['w_u'], w['w_v'], w['w_ga'], w['w_gb'],
              w['g_q_lat'], w['g_kv_lat'], w['g_kr'], w['g_gm']]
    gw = w['w_u'].shape[1]
    widths = [w['w_cq'].shape[1], w['w_ckv'].shape[1], SLOT, gw, gw, d, d]
    dtypes = [BF16, F32, F32, BF16, F32, BF16, BF16]
    return pl.pallas_call(
        _in_proj_kernel,
        grid=(t // tm,),
        in_specs=[_row_spec(tm, d)] + [_const_spec(c.shape) for c in consts] + [tab_spec] * 3,
        out_specs=[_row_spec(tm, n) for n in widths],
        out_shape=[jax.ShapeDtypeStruct((t, n), dt) for n, dt in zip(widths, dtypes)],
        compiler_params=_params(("parallel",)),
        name="in_proj",
    )(x2d, *consts, cos, sup, sdn)


def _gmlp_kernel(u_ref, vg_ref, sigb_ref, mix_ref, bias_ref, wob_ref, out_ref):
    vgb = vg_ref[...].astype(BF16)
    parts = []
    for g in range(GM_GROUPS):
        sl = slice(g * LANES, (g + 1) * LANES)
        mixed = _dot(mix_ref[g], vgb[:, sl]) + bias_ref[:, sl]
        parts.append((u_ref[:, sl].astype(F32) * mixed).astype(BF16))
    ob = _dot(jnp.concatenate(parts, axis=1), wob_ref[...])
    out_ref[...] = sigb_ref[...].astype(F32) * ob


def _gmlp(u, vg, sigb, mix, bias, w_ob, tm):
    t, gw = u.shape
    d = w_ob.shape[1]
    return pl.pallas_call(
        _gmlp_kernel,
        grid=(t // tm,),
        in_specs=[_row_spec(tm, gw), _row_spec(tm, gw), _row_spec(tm, d),
                  _const_spec(mix.shape), _const_spec(bias.shape), _const_spec(w_ob.shape)],
        out_specs=_row_spec(tm, d),
        out_shape=jax.ShapeDtypeStruct((t, d), F32),
        compiler_params=_params(("parallel",)),
        name="gmlp",
    )(u, vg, sigb, mix, bias, w_ob)


def _q_proj_kernel(cq_ref, wuq_ref, gq_ref, mavg_ref, cos_ref, sup_ref, sdn_ref, q_out):
    q = _dot(cq_ref[...], wuq_ref[...])
    cos, sup, sdn = cos_ref[...], sup_ref[...], sdn_ref[...]
    for h in range(MLA_HEADS):
        sl = slice(h * SLOT, (h + 1) * SLOT)
        qs = q[:, sl]
        y = qs * lax.rsqrt(_group_mean(qs * qs, mavg_ref) + EPS) * gq_ref[...]
        q_out[:, sl] = _rope_slot(y, cos, sup, sdn).astype(BF16)


def _q_proj(cq, w, tabs, tm):
    t, ql = cq.shape
    cos, sup, sdn = tabs
    ntab = cos.shape[0] // tm
    tab_spec = pl.BlockSpec((tm, SLOT), lambda i: (i % ntab, 0))
    n = MLA_HEADS * SLOT
    return pl.pallas_call(
        _q_proj_kernel,
        grid=(t // tm,),
        in_specs=[_row_spec(tm, ql), _const_spec(w['w_uq'].shape), _const_spec(w['g_q'].shape),
                  _const_spec(w['m_avg'].shape)] + [tab_spec] * 3,
        out_specs=_row_spec(tm, n),
        out_shape=jax.ShapeDtypeStruct((t, n), BF16),
        compiler_params=_params(("parallel",)),
        name="q_proj",
    )(cq, w['w_uq'], w['g_q'], w['m_avg'], cos, sup, sdn)


def _kv_proj_kernel(ckv_ref, kr_ref, wuk_ref, wuv_ref, gkn_ref, mavg_ref, k_out, v_out):
    c = ckv_ref[...].astype(BF16)
    kn = _dot(c, wuk_ref[...])
    kr = kr_ref[...]
    for h in range(MLA_HEADS):
        sl = slice(h * SLOT, (h + 1) * SLOT)
        ks = kn[:, sl]
        y = ks * lax.rsqrt(_group_mean(ks * ks, mavg_ref) + EPS) * gkn_ref[...]
        k_out[:, sl] = (y + kr).astype(BF16)
    v_out[...] = _dot(c, wuv_ref[...]).astype(BF16)


def _kv_proj(ckv, kr, w, tm):
    t, kl = ckv.shape
    n = MLA_HEADS * SLOT
    return pl.pallas_call(
        _kv_proj_kernel,
        grid=(t // tm,),
        in_specs=[_row_spec(tm, kl), _row_spec(tm, SLOT), _const_spec(w['w_uk'].shape),
                  _const_spec(w['w_uv'].shape), _const_spec(w['g_kn'].shape),
                  _const_spec(w['m_avg'].shape)],
        out_specs=[_row_spec(tm, n), _row_spec(tm, n)],
        out_shape=[jax.ShapeDtypeStruct((t, n), BF16)] * 2,
        compiler_params=_params(("parallel",)),
        name="kv_proj",
    )(ckv, kr, w['w_uk'], w['w_uv'], w['g_kn'], w['m_avg'])


def _attn_prompt_kernel(bound_ref, q_ref, k_ref, v_ref, o_ref, *, tile):
    n_tiles = q_ref.shape[1] // tile
    bound = bound_ref[0]
    first = lax.broadcasted_iota(jnp.int32, (1, LANES), 1) < V_HEAD
    row_chunk = lax.broadcasted_iota(jnp.int32, (tile, tile), 0) // CHUNK
    col_chunk = lax.broadcasted_iota(jnp.int32, (tile, tile), 1) // CHUNK
    visible = col_chunk <= row_chunk

    def static_step(q, kstart, carry, masked):
        sums, acc = carry
        kstart = pl.multiple_of(kstart, tile)
        k = k_ref[0, pl.ds(kstart, tile), :]
        v = v_ref[0, pl.ds(kstart, tile), :]
        new_sums = []
        for h in range(2):
            sl = slice(h * SLOT, (h + 1) * SLOT)
            s = _dot_nt(q[:, sl], k[:, sl]) * F32(MLA_SCALE * LOG2_E) - bound
            if masked:
                s = jnp.where(visible, s, F32(-1e30))
            p = jnp.exp2(s)
            part = sums[h]
            for c in range(tile // LANES):
                part = part + p[:, c * LANES:(c + 1) * LANES]
            new_sums.append(part)
            acc = acc + _dot(p.astype(BF16), v[:, sl])
        return tuple(new_sums), acc

    def static_tile(qi, _):
        rows = pl.ds(pl.multiple_of(qi * tile, tile), tile)
        q = q_ref[0, rows, :]
        zeros = jnp.zeros((tile, LANES), F32)

        def pair(i, c):
            c = static_step(q, 2 * i * tile, c, False)
            return static_step(q, (2 * i + 1) * tile, c, False)

        def odd_tail(c):
            return static_step(q, qi * tile, static_step(q, (qi - 1) * tile, c, False), True)

        carry = lax.fori_loop(0, qi // 2, pair, ((zeros, zeros), zeros))
        sums, acc = lax.cond(qi % 2 == 1, odd_tail, lambda c: static_step(q, qi * tile, c, True), carry)
        l0 = jnp.sum(sums[0], axis=-1, keepdims=True)
        l1 = jnp.sum(sums[1], axis=-1, keepdims=True)
        o_ref[0, rows, :] = (acc / jnp.where(first, l0, l1)).astype(BF16)
        return 0

    @pl.when(bound < STATIC_SHIFT_LIMIT)
    def _():
        lax.fori_loop(0, n_tiles, static_tile, 0)

    def step(q, kstart, carry, masked):
        ms, ls, acc = carry
        kstart = pl.multiple_of(kstart, tile)
        k = k_ref[0, pl.ds(kstart, tile), :]
        v = v_ref[0, pl.ds(kstart, tile), :]
        new_ms, new_ls, alphas, pvs = [], [], [], []
        for h in range(2):
            sl = slice(h * SLOT, (h + 1) * SLOT)
            s = _dot_nt(q[:, sl], k[:, sl]) * F32(MLA_SCALE * LOG2_E)
            if masked:
                s = jnp.where(visible, s, F32(-1e30))
            m_new = jnp.maximum(ms[h], jnp.max(s, axis=-1, keepdims=True))
            alpha = jnp.exp2(ms[h] - m_new)
            p = jnp.exp2(s - m_new)
            new_ls.append(alpha * ls[h] + jnp.sum(p, axis=-1, keepdims=True))
            new_ms.append(m_new)
            alphas.append(alpha)
            pvs.append(_dot(p.astype(BF16), v[:, sl]))
        acc = acc * jnp.where(first, alphas[0], alphas[1]) + pvs[0] + pvs[1]
        return tuple(new_ms), tuple(new_ls), acc

    def online_tile(qi, _):
        rows = pl.ds(pl.multiple_of(qi * tile, tile), tile)
        q = q_ref[0, rows, :]
        neg = jnp.full((tile, 1), -1e30, F32)
        zero = jnp.zeros((tile, 1), F32)
        init = ((neg, neg), (zero, zero), jnp.zeros((tile, LANES), F32))
        carry = lax.fori_loop(0, qi, lambda i, c: step(q, i * tile, c, False), init)
        _, ls, acc = step(q, qi * tile, carry, True)
        o_ref[0, rows, :] = (acc / jnp.where(first, ls[0], ls[1])).astype(BF16)
        return 0

    @pl.when(bound >= STATIC_SHIFT_LIMIT)
    def _():
        lax.fori_loop(0, n_tiles, online_tile, 0)


def _score_bound(w):
    gmax = lambda g, lo, n: jnp.max(jnp.abs(g[0, lo:lo + n]))
    nope = QK_NOPE * gmax(w['g_q'], 0, QK_NOPE) * gmax(w['g_kn'], 0, QK_NOPE)
    rope = QK_ROPE * gmax(w['g_q'], ROPE_LO, QK_ROPE) * gmax(w['g_kr'], ROPE_LO, QK_ROPE)
    return (BOUND_MARGIN * MLA_SCALE * LOG2_E * (nope + rope)).reshape(1).astype(F32)


def _attn_prompt(q, k, v, bound, tile):
    b, s, _ = q.shape
    pairs = MLA_HEADS // 2
    pair_block = pl.BlockSpec((1, s, 2 * SLOT), lambda bi, p: (bi, 0, p))
    return pl.pallas_call(
        functools.partial(_attn_prompt_kernel, tile=tile),
        grid=(b, pairs),
        in_specs=[pl.BlockSpec(memory_space=pltpu.SMEM), pair_block, pair_block, pair_block],
        out_specs=pl.BlockSpec((1, s, LANES), lambda bi, p: (bi, 0, p)),
        out_shape=jax.ShapeDtypeStruct((b, s, pairs * LANES), BF16),
        compiler_params=_params(("parallel", "parallel")),
        name="attn_prompt",
    )(bound, q, k, v)


def _attn_sample_kernel(q_ref, pckv_ref, pkr_ref, nckv_ref, nkr_ref, wuk_ref, wuv_ref, gkn_ref, place_ref,
                        o_ref):
    sq = q_ref.shape[1]
    rows = MLA_HEADS * sq
    gkn = gkn_ref[...]

    def keys_values(ckv, kr_slot):
        c = ckv.astype(BF16)
        kn = _dot(c, wuk_ref[...])
        parts = []
        for h in range(MLA_HEADS):
            ks = kn[:, h * SLOT:(h + 1) * SLOT]
            ms = jnp.sum(ks * ks, axis=-1, keepdims=True) * (1.0 / QK_NOPE)
            parts.append((ks * lax.rsqrt(ms + EPS) * gkn + kr_slot).astype(BF16))
        return jnp.concatenate(parts, axis=1), _dot(c, wuv_ref[...]).astype(BF16)

    past_kr = _dot(pkr_ref[0].astype(BF16), place_ref[...])
    k_p, v_p = keys_values(pckv_ref[0], past_kr)
    k_n, v_n = keys_values(nckv_ref[0], nkr_ref[0])
    k = jnp.concatenate([k_p, k_n], axis=0)
    v = jnp.concatenate([v_p, v_n], axis=0)

    qt = jnp.concatenate([q_ref[0].astype(F32)] * MLA_HEADS, axis=0)
    q_shape = (rows, MLA_HEADS * SLOT)
    own = (lax.broadcasted_iota(jnp.int32, q_shape, 0) // sq
           == lax.broadcasted_iota(jnp.int32, q_shape, 1) // SLOT)
    qbd = jnp.where(own, qt, 0.0).astype(BF16)
    s = _dot_nt(qbd, k) * MLA_SCALE
    p = jnp.exp(s - jnp.max(s, axis=-1, keepdims=True))
    p = p / jnp.sum(p, axis=-1, keepdims=True)
    o_all = _dot(p.astype(BF16), v)
    o_shape = o_all.shape
    own = (lax.broadcasted_iota(jnp.int32, o_shape, 0) // sq
           == lax.broadcasted_iota(jnp.int32, o_shape, 1) // V_HEAD)
    o_all = jnp.where(own, o_all, 0.0)
    o = o_all[0:sq]
    for h in range(1, MLA_HEADS):
        o = o + o_all[h * sq:(h + 1) * sq]
    o_ref[0] = o.astype(BF16)


def _attn_sample(q, past_ckv, past_kr, new_ckv, new_kr, w):
    b, sq, nq = q.shape
    npast, kl = past_ckv.shape[1:]
    nv = w['w_uv_plain'].shape[1]
    stream = lambda n, width: pl.BlockSpec((1, n, width), lambda bi: (bi, 0, 0))
    consts = [w['w_uk'], w['w_uv_plain'], w['g_kn'], w['rope_place']]
    return pl.pallas_call(
        _attn_sample_kernel,
        grid=(b,),
        in_specs=[stream(sq, nq), stream(npast, kl), stream(npast, QK_ROPE), stream(sq, kl), stream(sq, SLOT)]
        + [_const_spec(c.shape) for c in consts],
        out_specs=stream(sq, nv),
        out_shape=jax.ShapeDtypeStruct((b, sq, nv), BF16),
        compiler_params=_params(("parallel",)),
        name="attn_sample",
    )(q, past_ckv, past_kr, new_ckv, new_kr, *consts)


def _merge_kernel(x_ref, o_ref, siga_ref, gbob_ref, woa_ref, wo_ref, gx_ref, wcq_ref, gcq_ref,
                  x1_out, qc_out):
    oa = _dot(o_ref[...], woa_ref[...])
    merged = siga_ref[...].astype(F32) * oa + gbob_ref[...]
    x1 = x_ref[...] + _dot(merged.astype(BF16), wo_ref[...])
    x1_out[...] = x1
    qc = _dot(_rms(x1, gx_ref[...]).astype(BF16), wcq_ref[...])
    for h in range(MEM_HEADS):
        sl = slice(h * MEM_HEAD_DIM, (h + 1) * MEM_HEAD_DIM)
        qc_out[:, sl] = _rms(qc[:, sl], gcq_ref[...]).astype(BF16)


def _merge(x2d, o, siga, gbob, w, tm):
    t, d = x2d.shape
    consts = [w['w_oa'], w['w_o'], w['g_xattn'], w['w_cq_mem'], w['g_cq']]
    nq = w['w_cq_mem'].shape[1]
    return pl.pallas_call(
        _merge_kernel,
        grid=(t // tm,),
        in_specs=[_row_spec(tm, d), _row_spec(tm, o.shape[1]), _row_spec(tm, d), _row_spec(tm, d)]
        + [_const_spec(c.shape) for c in consts],
        out_specs=[_row_spec(tm, d), _row_spec(tm, nq)],
        out_shape=[jax.ShapeDtypeStruct((t, d), F32), jax.ShapeDtypeStruct((t, nq), BF16)],
        compiler_params=_params(("parallel",)),
        name="merge",
    )(x2d, o, siga, gbob, *consts)


def _mem_kv_kernel(mem_ref, gmem_ref, wck_ref, gck_ref, wcv_ref, k_out, v_out):
    m = _rms(mem_ref[...], gmem_ref[...]).astype(BF16)
    k = _dot(m, wck_ref[...])
    for h in range(MEM_HEADS):
        sl = slice(h * MEM_HEAD_DIM, (h + 1) * MEM_HEAD_DIM)
        k_out[:, sl] = _rms(k[:, sl], gck_ref[...])
    v_out[...] = _dot(m, wcv_ref[...])


def _mem_kv(mem2d, w, tm):
    t, d = mem2d.shape
    consts = [w['g_mem'], w['w_ck'], w['g_ck'], w['w_cv']]
    n = w['w_ck'].shape[1]
    return pl.pallas_call(
        _mem_kv_kernel,
        grid=(t // tm,),
        in_specs=[_row_spec(tm, d)] + [_const_spec(c.shape) for c in consts],
        out_specs=[_row_spec(tm, n)] * 2,
        out_shape=[jax.ShapeDtypeStruct((t, n), F32)] * 2,
        compiler_params=_params(("parallel",)),
        name="mem_kv",
    )(mem2d, *consts)


def _cross_kernel(q_ref, k_ref, v_ref, o_ref):
    q = q_ref[0]
    k = k_ref[0].astype(BF16)
    v = v_ref[0].astype(BF16)
    for h in range(MEM_HEADS):
        sl = slice(h * MEM_HEAD_DIM, (h + 1) * MEM_HEAD_DIM)
        s = _dot_nt(q[:, sl], k[:, sl]) * MEM_SCALE
        p = jnp.exp(s - jnp.max(s, axis=-1, keepdims=True))
        p = p / jnp.sum(p, axis=-1, keepdims=True)
        o_ref[0, :, sl] = _dot(p.astype(BF16), v[:, sl]).astype(BF16)


def _cross(qc, mk, mv, tc):
    b, s, n = qc.shape
    nm = mk.shape[1]
    return pl.pallas_call(
        _cross_kernel,
        grid=(b, s // tc),
        in_specs=[pl.BlockSpec((1, tc, n), lambda bi, i: (bi, i, 0)),
                  pl.BlockSpec((1, nm, n), lambda bi, i: (bi, 0, 0)),
                  pl.BlockSpec((1, nm, n), lambda bi, i: (bi, 0, 0))],
        out_specs=pl.BlockSpec((1, tc, n), lambda bi, i: (bi, i, 0)),
        out_shape=jax.ShapeDtypeStruct((b, s, n), BF16),
        compiler_params=_params(("parallel", "parallel")),
        name="cross_attn",
    )(qc, mk, mv)


def _cross_out_kernel(x1_ref, oc_ref, wco_ref, gffn_ref, wpq_ref, x2_out, h_out, q_out):
    x2 = x1_ref[...] + _dot(oc_ref[...], wco_ref[...])
    x2_out[...] = x2
    h = _rms(x2, gffn_ref[...]).astype(BF16)
    h_out[...] = h
    q_out[...] = _dot(h, wpq_ref[...]).astype(BF16)


def _cross_out(x1, oc, w, tm):
    t, d = x1.shape
    nq = w['w_pq'].shape[1]
    return pl.pallas_call(
        _cross_out_kernel,
        grid=(t // tm,),
        in_specs=[_row_spec(tm, d), _row_spec(tm, oc.shape[1]), _const_spec(w['w_co'].shape),
                  _const_spec(w['g_ffn'].shape), _const_spec(w['w_pq'].shape)],
        out_specs=[_row_spec(tm, d), _row_spec(tm, d), _row_spec(tm, nq)],
        out_shape=[jax.ShapeDtypeStruct((t, d), F32), jax.ShapeDtypeStruct((t, d), BF16),
                   jax.ShapeDtypeStruct((t, nq), BF16)],
        compiler_params=_params(("parallel",)),
        name="cross_out",
    )(x1, oc, w['w_co'], w['g_ffn'], w['w_pq'])


def _pair_blocks():
    k = PEER_TOPK
    lead = 2
    blocks, flat = [], []
    for a in range(lead):
        nb = k // (a + 1)
        blocks.append(('row', a, nb))
        flat += [a * k + b for b in range(nb)]
    for b in range(lead):
        na = k // (b + 1)
        blocks.append(('col', b, na))
        flat += [a * k + b if a >= lead else -1 for a in range(na)]
    cells = [(a, b) for a in range(lead, k) for b in range(lead, k) if (a + 1) * (b + 1) <= k]
    pad = -len(cells) % 8
    blocks.append(('cells', cells + [cells[0]] * pad, len(cells) + pad))
    flat += [a * k + b for a, b in cells] + [-1] * pad
    assert all(n % 8 == 0 for _, _, n in blocks)
    return blocks, np.asarray(flat, np.float32)


def _pick_rows(v, ranks):
    row = lax.broadcasted_iota(jnp.int32, (len(ranks), v.shape[1]), 0)
    out = jnp.broadcast_to(v[ranks[0]:ranks[0] + 1, :], row.shape)
    for i in range(1, len(ranks)):
        if ranks[i] != ranks[0]:
            out = jnp.where(row == i, v[ranks[i]:ranks[i] + 1, :], out)
    return out


def _pair_candidates(v0, v1, blocks, combine):
    parts = []
    for kind, spec, n in blocks:
        if kind == 'row':
            parts.append(combine(v0[spec:spec + 1, :], v1[0:n, :]))
        elif kind == 'col':
            parts.append(combine(v0[0:n, :], v1[spec:spec + 1, :]))
        else:
            parts.append(combine(_pick_rows(v0, [a for a, _ in spec]), _pick_rows(v1, [b for _, b in spec])))
    return jnp.concatenate(parts, axis=0)


def _peer_kernel(h_ref, x_ref, q_ref, keys_ref, flat_ref, u_ref, v_ref, out_ref,
                 mask_scr, sv_scr, si_scr, ts_scr, te_scr, it_scr, jt_scr, gt_scr,
                 isel_scr, jsel_scr, gsel_scr):
    tm = h_ref.shape[0]
    te = v_ref.shape[0]
    r = pl.program_id(0)
    e = pl.program_id(1)
    n = pl.num_programs(0) - 2
    rows_per_step = te // N_KEYS
    tokens_per_head = tm // PEER_HEADS
    blocks, _ = _pair_blocks()
    neg_inf = F32(-jnp.inf)
    key_rows = lax.broadcasted_iota(jnp.int32, (N_KEYS, LANES), 0).astype(F32)
    swept = lax.rem(r, 2)
    ranked = 1 - swept

    def mask_token(t):
        irow = isel_scr[swept, pl.ds(t, 1), :]
        jrow = jsel_scr[swept, pl.ds(t, 1), :]
        grow = gsel_scr[swept, pl.ds(t, 1), :]
        rw = jnp.where(key_rows == irow, 0.5 * grow, 0.0).astype(BF16)
        cw = jnp.where(key_rows == jrow, 1.0, 0.0).astype(BF16)
        r0 = pl.multiple_of(t * MASK_PITCH, 8)
        mask_scr[pl.ds(r0, N_KEYS), :] = _dot_nt(rw, cw)

    def rank_pairs(hd):
        sv0, sv1 = sv_scr[2 * hd], sv_scr[2 * hd + 1]
        si0, si1 = si_scr[2 * hd], si_scr[2 * hd + 1]
        flat = flat_ref[...]
        cand = _pair_candidates(sv0, sv1, blocks, lambda x, y: x + y)
        cand = jnp.where(flat >= 0.0, cand, neg_inf)
        for kk in range(PEER_TOPK):
            m = jnp.max(cand, axis=0, keepdims=True)
            pos = jnp.min(jnp.where(cand == m, flat, F32(PEER_TOPK * PEER_TOPK)), axis=0, keepdims=True)
            ts_scr[kk:kk + 1, :] = m
            te_scr[kk:kk + 1, :] = pos
            cand = jnp.where(flat == pos, neg_inf, cand)
        top = ts_scr[...]
        ex = jnp.exp(top - top[0:1, :])
        gate = ex / jnp.sum(ex, axis=0, keepdims=True)
        pos = te_scr[...]
        rank_a = jnp.floor(pos * F32(1.0 / PEER_TOPK))
        rank_b = pos - rank_a * F32(PEER_TOPK)
        irow = jnp.zeros_like(pos)
        jrow = jnp.zeros_like(pos)
        for c in range(PEER_TOPK):
            irow = jnp.where(rank_a == F32(c), si0[c:c + 1, :], irow)
            jrow = jnp.where(rank_b == F32(c), si1[c:c + 1, :], jrow)
        r0 = pl.multiple_of(hd * PEER_TOPK, PEER_TOPK)
        it_scr[pl.ds(r0, PEER_TOPK), :] = irow
        jt_scr[pl.ds(r0, PEER_TOPK), :] = jrow
        gt_scr[pl.ds(r0, PEER_TOPK), :] = gate

    def publish_ranked():
        isel_scr[ranked] = it_scr[...].T
        jsel_scr[ranked] = jt_scr[...].T
        gsel_scr[ranked] = gt_scr[...].T

    @pl.when(e == 0)
    def _():
        out_ref[...] = x_ref[...]

        @pl.when(r == 1)
        def _():
            def head(hd, _):
                rank_pairs(hd)
                return 0

            lax.fori_loop(0, PEER_HEADS, head, 0)
            publish_ranked()

        @pl.when(jnp.logical_and(r >= 2, r <= n))
        def _():
            def trip(i, _):
                for t in range(tokens_per_head):
                    mask_token(i * tokens_per_head + t)
                rank_pairs(i)
                return 0

            lax.fori_loop(0, PEER_HEADS, trip, 0)
            publish_ranked()

        @pl.when(r == n + 1)
        def _():
            def trip(i, _):
                for t in range(MASK_UNROLL):
                    mask_token(i * MASK_UNROLL + t)
                return 0

            lax.fori_loop(0, tm // MASK_UNROLL, trip, 0)

    def slot_scores():
        d0 = pl.multiple_of(e * PEER_HALF, PEER_HALF)
        return _dot_nt(q_ref[:, pl.ds(d0, PEER_HALF)], keys_ref[e])

    def rank_slot(scores):
        lane = lax.broadcasted_iota(jnp.int32, (tm, N_KEYS), 1)
        picked = jnp.zeros((tm, N_KEYS), jnp.int32)
        a = scores
        for kk in range(PEER_TOPK):
            idx = jnp.argmax(a, axis=-1, keepdims=True).astype(jnp.int32)
            a = jnp.where(lane == idx, neg_inf, a)
            picked = jnp.where(lane == kk, idx, picked)
        vals = jnp.take_along_axis(scores, picked, axis=1)
        sv_scr[e] = vals.T[0:PEER_TOPK, :]
        si_scr[e] = (N_KEYS - 1 - picked).astype(F32).T[0:PEER_TOPK, :]

    def expert_chunk():
        h = h_ref[...]
        row0 = e * rows_per_step
        weighted = []
        for c in range(te // PEER_CHUNK):
            rows = slice(c * PEER_CHUNK, (c + 1) * PEER_CHUNK)
            pre = _dot_nt(h, u_ref[rows, :])
            half_mask = jnp.concatenate(
                [mask_scr[pl.ds(row0 + c * (PEER_CHUNK // N_KEYS) + rr, tm, stride=MASK_PITCH), :]
                 for rr in range(PEER_CHUNK // N_KEYS)], axis=1)
            weighted.append((pre * (1.0 + lax.erf(pre * np.float32(np.sqrt(0.5)))) * half_mask).astype(BF16))
        out_ref[...] += _dot(jnp.concatenate(weighted, axis=1), v_ref[...])

    ranks = r < n
    sweeps = r >= 2

    @pl.when(jnp.logical_and(ranks, jnp.logical_not(sweeps)))
    def _():
        rank_slot(slot_scores())

    @pl.when(jnp.logical_and(ranks, sweeps))
    def _():
        a = slot_scores()
        expert_chunk()
        rank_slot(a)

    @pl.when(jnp.logical_and(jnp.logical_not(ranks), sweeps))
    def _():
        expert_chunk()


def _peer(h, x2, q_route, w, tm, te):
    t, d = h.shape
    n = t // tm
    nexp = w['peer_v'].shape[0]
    assert nexp // te == 2 * PEER_HEADS, "one half-key slot is ranked per expert step"
    assert tm % PEER_HEADS == 0 and tm % MASK_UNROLL == 0
    nsel = PEER_HEADS * PEER_TOPK
    _, flat = _pair_blocks()
    flat = jnp.asarray(np.repeat(flat[:, None], tm, axis=1))
    swept_tile = lambda r, e: (jnp.clip(r - 2, 0, n - 1), 0)
    dense_tile = lambda width: pl.BlockSpec((tm, width), swept_tile, pipeline_mode=pl.Buffered(1))
    expert_tile = pl.BlockSpec((te, d), lambda r, e: (jnp.where(r >= 2, e, 0), 0))
    return pl.pallas_call(
        _peer_kernel,
        grid=(n + 2, nexp // te),
        in_specs=[dense_tile(d), dense_tile(d),
                  pl.BlockSpec((tm, q_route.shape[1]), lambda r, e: (jnp.minimum(r, n - 1), 0),
                               pipeline_mode=pl.Buffered(1)),
                  _const_spec(w['sub_keys'].shape), _const_spec(flat.shape), expert_tile, expert_tile],
        out_specs=pl.BlockSpec((tm, d), swept_tile),
        out_shape=jax.ShapeDtypeStruct((t, d), F32),
        scratch_shapes=[pltpu.VMEM((tm * MASK_PITCH, LANES), F32),
                        pltpu.VMEM((2 * PEER_HEADS, PEER_TOPK, tm), F32),
                        pltpu.VMEM((2 * PEER_HEADS, PEER_TOPK, tm), F32),
                        pltpu.VMEM((PEER_TOPK, tm), F32),
                        pltpu.VMEM((PEER_TOPK, tm), F32),
                        pltpu.VMEM((nsel, tm), F32),
                        pltpu.VMEM((nsel, tm), F32),
                        pltpu.VMEM((nsel, tm), F32),
                        pltpu.VMEM((2, tm, nsel), F32),
                        pltpu.VMEM((2, tm, nsel), F32),
                        pltpu.VMEM((2, tm, nsel), F32)],
        compiler_params=_params(("arbitrary", "arbitrary"), PEER_VMEM_LIMIT),
        name="peer",
    )(h, x2, q_route, w['sub_keys'], flat, w['peer_u'], w['peer_v'])


def _slot_gain(parts):
    row = jnp.zeros((SLOT,), F32)
    for off, g in parts:
        row = lax.dynamic_update_slice(row, g.astype(F32), (off,))
    return row[None, :]


def _group_avg_matrix():
    m = np.zeros((SLOT, SLOT), np.float32)
    m[:QK_NOPE, :QK_NOPE] = 1.0 / QK_NOPE
    m[ROPE_LO:ROPE_LO + QK_ROPE, ROPE_LO:ROPE_LO + QK_ROPE] = 1.0 / QK_ROPE
    return jnp.asarray(np.concatenate([m, m], axis=0), BF16)


def _rope_tables(pos):
    inv = ROPE_THETA ** (-jnp.arange(ROPE_HALF, dtype=F32) / ROPE_HALF)
    ang = pos.astype(F32)[:, None] * inv[None, :]
    cos, sin = jnp.cos(ang), jnp.sin(ang)
    n = pos.shape[0]
    tail = SLOT - ROPE_LO - QK_ROPE
    cos_t = jnp.concatenate([jnp.ones((n, ROPE_LO), F32), cos, cos, jnp.zeros((n, tail), F32)], axis=1)
    sin_up = jnp.concatenate([jnp.zeros((n, ROPE_LO + ROPE_HALF), F32), sin, jnp.zeros((n, tail), F32)], axis=1)
    sin_dn = jnp.concatenate([jnp.zeros((n, ROPE_LO), F32), -sin, jnp.zeros((n, tail + ROPE_HALF), F32)], axis=1)
    return cos_t, sin_up, sin_dn


def _prep_weights(p, i):
    w_in = p['w_in'][i]
    q_lora = p['g_q_lat'].shape[1]
    kv_lora = p['g_kv_lat'].shape[1]
    gw = p['g_gm'].shape[1]
    d = w_in.shape[0]
    o = np.cumsum([0, q_lora, kv_lora, QK_ROPE, gw, gw, d, d])
    seg = lambda k: w_in[:, o[k]:o[k + 1]]
    row = lambda a: a[i][None, :].astype(F32)
    w = {}
    w['g_mix'] = row(p['g_mix'])
    w['w_cq'] = seg(0).astype(BF16)
    w['w_ckv'] = seg(1).astype(BF16)
    w['w_kr'] = jnp.pad(seg(2), ((0, 0), (ROPE_LO, SLOT - ROPE_LO - QK_ROPE))).astype(BF16)
    w['w_u'] = seg(3).astype(BF16)
    w['w_v'] = seg(4).astype(BF16)
    w['w_ga'] = seg(5).astype(BF16)
    w['w_gb'] = seg(6).astype(BF16)
    w['g_q_lat'] = row(p['g_q_lat'])
    w['g_kv_lat'] = row(p['g_kv_lat'])
    w['g_kr'] = _slot_gain([(ROPE_LO, p['g_kr'][i])])
    w['g_gm'] = row(p['g_gm'])
    head_dim = QK_NOPE + QK_ROPE
    w_uq = p['w_uq'][i].reshape(q_lora, MLA_HEADS, head_dim)
    w['w_uq'] = jnp.pad(w_uq, ((0, 0), (0, 0), (0, SLOT - head_dim))).reshape(q_lora, -1).astype(BF16)
    w['g_q'] = _slot_gain([(0, p['g_qn'][i]), (ROPE_LO, p['g_qr'][i])])
    w['m_avg'] = _group_avg_matrix()
    w_uk = p['w_uk'][i].reshape(kv_lora, MLA_HEADS, QK_NOPE)
    w['w_uk'] = jnp.pad(w_uk, ((0, 0), (0, 0), (0, SLOT - QK_NOPE))).reshape(kv_lora, -1).astype(BF16)
    w['g_kn'] = _slot_gain([(0, p['g_kn'][i])])
    w_uv = p['w_uv'][i].reshape(kv_lora, MLA_HEADS // 2, 2, V_HEAD)
    eye2 = jnp.eye(2, dtype=w_uv.dtype)
    w['w_uv'] = jnp.einsum('cpjd,jk->cpjkd', w_uv, eye2).reshape(kv_lora, -1).astype(BF16)
    w['w_uv_plain'] = p['w_uv'][i].astype(BF16)
    place = np.zeros((QK_ROPE, SLOT), np.float32)
    place[np.arange(QK_ROPE), ROPE_LO + np.arange(QK_ROPE)] = 1.0
    w['rope_place'] = jnp.asarray(place, BF16)
    w['w_oa'] = p['w_oa'][i].astype(BF16)
    w['w_ob'] = p['w_ob'][i].astype(BF16)
    w['w_o'] = p['w_o'][i].astype(BF16)
    w['g_xattn'] = row(p['g_xattn'])
    w['g_mem'] = row(p['g_mem'])
    w['w_cq_mem'] = p['w_cq'][i].astype(BF16)
    w['g_cq'] = row(p['g_cq'])
    w['w_ck'] = p['w_ck'][i].astype(BF16)
    w['g_ck'] = row(p['g_ck'])
    w['w_cv'] = p['w_cv'][i].astype(BF16)
    w['w_co'] = p['w_co'][i].astype(BF16)
    w['g_ffn'] = row(p['g_ffn'])
    w['w_pq'] = p['w_pq'][i].astype(BF16)
    w['sub_keys'] = p['sub_keys'][i].reshape(2 * PEER_HEADS, N_KEYS, PEER_HALF)[:, ::-1, :].astype(BF16)
    w['peer_u'] = p['peer_u'][i].astype(BF16)
    w['peer_v'] = p['peer_v'][i].astype(BF16)
    w['w_s'] = p['w_s'][i]
    w['b_s'] = p['b_s'][i]
    return w


def _spatial_operands(w, seq, tm):
    chunk = min(seq, GM_CHUNK)
    reps = tm // chunk
    w_mask = jnp.tril(w['w_s'][:, :chunk, :chunk])
    w_mask = w_mask.astype(BF16)
    mix = jnp.concatenate(
        [jnp.pad(w_mask, ((0, 0), (0, 0), (a * chunk, (reps - 1 - a) * chunk))) for a in range(reps)], axis=1)
    bias = jnp.tile(w['b_s'][:, :chunk].T, (reps, 1))
    bias = jnp.repeat(bias, LANES, axis=1)
    return mix.astype(BF16), bias.astype(F32)


def _token_mixer(x2d, w, seq, pos, past, tm):
    t, d = x2d.shape
    nb = t // seq
    period = max(seq, tm)
    tabs = _rope_tables(jnp.tile(pos, period // seq))
    cq, ckv, kr, u, vg, siga, sigb = _in_proj(x2d, w, tabs, tm)
    mix, bias = _spatial_operands(w, seq, tm)
    gbob = _gmlp(u, vg, sigb, mix, bias, w['w_ob'], tm)
    wide = min(WIDE_ROW_TILE, t)
    q = _q_proj(cq, w, tabs, wide)
    if past is None:
        k, v = _kv_proj(ckv, kr, w, wide)
        o = _attn_prompt(q.reshape(nb, seq, -1), k.reshape(nb, seq, -1), v.reshape(nb, seq, -1),
                         _score_bound(w), ATTN_TILE)
    else:
        past_ckv, past_kr = past
        o = _attn_sample(q.reshape(nb, seq, -1), past_ckv, past_kr, ckv.reshape(nb, seq, -1),
                         kr.reshape(nb, seq, SLOT), w)
    x1, qc = _merge(x2d, o.reshape(t, -1), siga, gbob, w, wide)
    return x1, qc, ckv, kr[:, ROPE_LO:ROPE_LO + QK_ROPE], vg


def _tail(x1, qc, mk, mv, w, nb, seq, tm):
    t, d = x1.shape
    oc = _cross(qc.reshape(nb, seq, -1), mk, mv, min(seq, WIDE_ROW_TILE))
    x2, h, q_route = _cross_out(x1, oc.reshape(t, -1), w, min(WIDE_ROW_TILE, t))
    return _peer(h, x2, q_route, w, PEER_ROWS, PEER_EXPERTS)


def kernel(x_prompt, x_sample, cache_mla_ckv, cache_mla_krope, cache_mem_k, cache_mem_v, mem_prompt, g_mix, w_in, g_q_lat, w_uq, g_qn, g_qr, g_kv_lat, g_kr, w_uk, w_uv, g_kn, w_oa, g_gm, w_s, b_s, w_ob, w_o, g_xattn, g_mem, w_cq, g_cq, w_ck, g_ck, w_cv, w_co, g_ffn, w_pq, sub_keys, peer_u, peer_v):
    params = dict(g_mix=g_mix, w_in=w_in, g_q_lat=g_q_lat, w_uq=w_uq, g_qn=g_qn, g_qr=g_qr,
                  g_kv_lat=g_kv_lat, g_kr=g_kr, w_uk=w_uk, w_uv=w_uv, g_kn=g_kn, w_oa=w_oa, g_gm=g_gm,
                  w_s=w_s, b_s=b_s, w_ob=w_ob, w_o=w_o, g_xattn=g_xattn, g_mem=g_mem, w_cq=w_cq,
                  g_cq=g_cq, w_ck=w_ck, g_ck=g_ck, w_cv=w_cv, w_co=w_co, g_ffn=g_ffn, w_pq=w_pq,
                  sub_keys=sub_keys, peer_u=peer_u, peer_v=peer_v)
    bp, sp, d = x_prompt.shape
    bs, ss, _ = x_sample.shape
    depth = w_in.shape[0]
    past_len = cache_mla_ckv.shape[2]
    n_mem = mem_prompt.shape[1]
    tm = ROW_TILE
    assert (bp * sp) % tm == 0 and (bs * ss) % tm == 0 and sp % ATTN_TILE == 0
    pos_p = jnp.arange(sp)
    pos_s = past_len + jnp.arange(ss)
    xp = x_prompt.reshape(bp * sp, d)
    xs = x_sample.reshape(bs * ss, d)
    outs = [[] for _ in range(7)]
    for i in range(depth):
        w = _prep_weights(params, i)
        xp1, qcp, ckv_p, kr_p, _ = _token_mixer(xp, w, sp, pos_p, None, tm)
        xs1, qcs, ckv_s, kr_s, vg_s = _token_mixer(xs, w, ss, pos_s, (cache_mla_ckv[i], cache_mla_krope[i]), tm)
        mk_p, mv_p = _mem_kv(mem_prompt.reshape(bp * n_mem, d), w, min(tm, bp * n_mem))
        xp = _tail(xp1, qcp, mk_p.reshape(bp, n_mem, -1), mv_p.reshape(bp, n_mem, -1), w, bp, sp, tm)
        xs = _tail(xs1, qcs, cache_mem_k[i].reshape(bs, n_mem, -1), cache_mem_v[i].reshape(bs, n_mem, -1),
                   w, bs, ss, tm)
        outs[0].append(ckv_p.reshape(bp, sp, -1))
        outs[1].append(kr_p.reshape(bp, sp, -1))
        outs[2].append(mk_p.reshape(bp, n_mem, MEM_HEADS, MEM_HEAD_DIM))
        outs[3].append(mv_p.reshape(bp, n_mem, MEM_HEADS, MEM_HEAD_DIM))
        outs[4].append(ckv_s.reshape(bs, ss, -1))
        outs[5].append(kr_s.reshape(bs, ss, -1))
        outs[6].append(vg_s.reshape(bs, ss, -1))
    return (xp.reshape(bp, sp, d), xs.reshape(bs, ss, d)) + tuple(jnp.stack(o) for o in outs)
```

```python
import functools

import jax
import jax.numpy as jnp
import numpy as np
from jax import lax
from jax.experimental import pallas as pl
from jax.experimental.pallas import tpu as pltpu

CHUNK = 64
EPS = 1e-6
MLA_HEADS = 16
QK_NOPE = 64
QK_ROPE = 32
V_HEAD = 64
ROPE_THETA = 10000.0
MLA_SCALE = (QK_NOPE + QK_ROPE) ** -0.5
GM_CHUNK = 128
GM_GROUPS = 8
MEM_HEADS = 4
MEM_HEAD_DIM = 128
MEM_SCALE = MEM_HEAD_DIM ** -0.5
PEER_HEADS = 8
N_KEYS = 128
PEER_TOPK = 16
PEER_HALF = 128

LANES = 128
SLOT = LANES
ROPE_LO = QK_NOPE
ROPE_HALF = QK_ROPE // 2
ROW_TILE = 512
WIDE_ROW_TILE = 1024
ATTN_TILE = 512
LOG2_E = float(np.log2(np.e))
STATIC_SHIFT_LIMIT = 48.0
BOUND_MARGIN = 1.02
PEER_ROWS = 512
PEER_EXPERTS = 1024
PEER_CHUNK = 512
MASK_PITCH = N_KEYS + 8
MASK_UNROLL = 32
VMEM_LIMIT = 48 * 1024 * 1024
PEER_VMEM_LIMIT = 60 * 1024 * 1024

F32 = jnp.float32
BF16 = jnp.bfloat16
_NT = (((1,), (1,)), ((), ()))


def _dot(a, b):
    return jnp.dot(a, b, preferred_element_type=F32)


def _dot_nt(a, b):
    return lax.dot_general(a, b, _NT, preferred_element_type=F32)


def _rms(xf, g):
    return xf * lax.rsqrt(jnp.mean(xf * xf, axis=-1, keepdims=True) + EPS) * g


def _gelu(x):
    return 0.5 * x * (1.0 + lax.erf(x * np.float32(np.sqrt(0.5))))


def _group_mean(sq, m_ref):
    hi = sq.astype(BF16)
    lo = (sq - hi.astype(F32)).astype(BF16)
    return _dot(jnp.concatenate([hi, lo], axis=1), m_ref[...])


def _rope_slot(y, cos, sin_up, sin_dn):
    return (y * cos + pltpu.roll(y, ROPE_HALF, 1) * sin_up
            + pltpu.roll(y, SLOT - ROPE_HALF, 1) * sin_dn)


def _const_spec(shape):
    nd = len(shape)
    return pl.BlockSpec(shape, lambda *_: (0,) * nd, pipeline_mode=pl.Buffered(1))


def _row_spec(tm, width):
    return pl.BlockSpec((tm, width), lambda i: (i, 0))


def _params(sem, limit=VMEM_LIMIT):
    return pltpu.CompilerParams(dimension_semantics=sem, vmem_limit_bytes=limit)


def _in_proj_kernel(x_ref, gmix_ref, wcq_ref, wckv_ref, wkr_ref, wu_ref, wv_ref, wga_ref, wgb_ref,
                    gq_ref, gkv_ref, gkr_ref, ggm_ref, cos_ref, sup_ref, sdn_ref,
                    cq_out, ckv_out, kr_out, u_out, vg_out, siga_out, sigb_out):
    h = _rms(x_ref[...], gmix_ref[...]).astype(BF16)
    cq_out[...] = _rms(_dot(h, wcq_ref[...]), gq_ref[...]).astype(BF16)
    ckv_out[...] = _rms(_dot(h, wckv_ref[...]), gkv_ref[...])
    kr = _dot(h, wkr_ref[...])
    ms = jnp.sum(kr * kr, axis=-1, keepdims=True) * (1.0 / QK_ROPE)
    kr = kr * lax.rsqrt(ms + EPS) * gkr_ref[...]
    kr_out[...] = _rope_slot(kr, cos_ref[...], sup_ref[...], sdn_ref[...])
    u_out[...] = _gelu(_dot(h, wu_ref[...])).astype(BF16)
    vg_out[...] = _rms(_gelu(_dot(h, wv_ref[...])), ggm_ref[...])
    siga_out[...] = jax.nn.sigmoid(_dot(h, wga_ref[...])).astype(BF16)
    sigb_out[...] = jax.nn.sigmoid(_dot(h, wgb_ref[...])).astype(BF16)


def _in_proj(x2d, w, tabs, tm):
    t, d = x2d.shape
    cos, sup, sdn = tabs
    ntab = cos.shape[0] // tm
    tab_spec = pl.BlockSpec((tm, SLOT), lambda i: (i % ntab, 0))
    consts = [w['g_mix'], w['w_cq'], w['w_ckv'], w['w_kr'], w['w_u'], w['w_v'], w['w_ga'], w['w_gb'],
              w['g_q_lat'], w['g_kv_lat'], w['g_kr'], w['g_gm']]
    gw = w['w_u'].shape[1]
    widths = [w['w_cq'].shape[1], w['w_ckv'].shape[1], SLOT, gw, gw, d, d]
    dtypes = [BF16, F32, F32, BF16, F32, BF16, BF16]
    return pl.pallas_call(
        _in_proj_kernel,
        grid=(t // tm,),
        in_specs=[_row_spec(tm, d)] + [_const_spec(c.shape) for c in consts] + [tab_spec] * 3,
        out_specs=[_row_spec(tm, n) for n in widths],
        out_shape=[jax.ShapeDtypeStruct((t, n), dt) for n, dt in zip(widths, dtypes)],
        compiler_params=_params(("parallel",)),
        name="in_proj",
    )(x2d, *consts, cos, sup, sdn)


def _gmlp_kernel(u_ref, vg_ref, sigb_ref, mix_ref, bias_ref, wob_ref, out_ref):
    vgb = vg_ref[...].astype(BF16)
    block = mix_ref.shape[1]
    parts = []
    for g in range(GM_GROUPS):
        sl = slice(g * LANES, (g + 1) * LANES)
        mixed = jnp.concatenate([_dot(mix_ref[g], vgb[r0:r0 + block, sl])
                                 for r0 in range(0, vgb.shape[0], block)], axis=0) + bias_ref[:, sl]
        parts.append((u_ref[:, sl].astype(F32) * mixed).astype(BF16))
    ob = _dot(jnp.concatenate(parts, axis=1), wob_ref[...])
    out_ref[...] = sigb_ref[...].astype(F32) * ob


def _gmlp(u, vg, sigb, mix, bias, w_ob, tm):
    t, gw = u.shape
    d = w_ob.shape[1]
    return pl.pallas_call(
        _gmlp_kernel,
        grid=(t // tm,),
        in_specs=[_row_spec(tm, gw), _row_spec(tm, gw), _row_spec(tm, d),
                  _const_spec(mix.shape), _const_spec(bias.shape), _const_spec(w_ob.shape)],
        out_specs=_row_spec(tm, d),
        out_shape=jax.ShapeDtypeStruct((t, d), F32),
        compiler_params=_params(("parallel",)),
        name="gmlp",
    )(u, vg, sigb, mix, bias, w_ob)


def _q_proj_kernel(cq_ref, wuq_ref, gq_ref, mavg_ref, cos_ref, sup_ref, sdn_ref, q_out):
    q = _dot(cq_ref[...], wuq_ref[...])
    cos, sup, sdn = cos_ref[...], sup_ref[...], sdn_ref[...]
    for h in range(MLA_HEADS):
        sl = slice(h * SLOT, (h + 1) * SLOT)
        qs = q[:, sl]
        y = qs * lax.rsqrt(_group_mean(qs * qs, mavg_ref) + EPS) * gq_ref[...]
        q_out[:, sl] = _rope_slot(y, cos, sup, sdn).astype(BF16)


def _q_proj(cq, w, tabs, tm):
    t, ql = cq.shape
    cos, sup, sdn = tabs
    ntab = cos.shape[0] // tm
    tab_spec = pl.BlockSpec((tm, SLOT), lambda i: (i % ntab, 0))
    n = MLA_HEADS * SLOT
    return pl.pallas_call(
        _q_proj_kernel,
        grid=(t // tm,),
        in_specs=[_row_spec(tm, ql), _const_spec(w['w_uq'].shape), _const_spec(w['g_q'].shape),
                  _const_spec(w['m_avg'].shape)] + [tab_spec] * 3,
        out_specs=_row_spec(tm, n),
        out_shape=jax.ShapeDtypeStruct((t, n), BF16),
        compiler_params=_params(("parallel",)),
        name="q_proj",
    )(cq, w['w_uq'], w['g_q'], w['m_avg'], cos, sup, sdn)


def _kv_proj_kernel(ckv_ref, kr_ref, wuk_ref, wuv_ref, gkn_ref, mavg_ref, k_out, v_out):
    c = ckv_ref[...].astype(BF16)
    kn = _dot(c, wuk_ref[...])
    kr = kr_ref[...]
    for h in range(MLA_HEADS):
        sl = slice(h * SLOT, (h + 1) * SLOT)
        ks = kn[:, sl]
        y = ks * lax.rsqrt(_group_mean(ks * ks, mavg_ref) + EPS) * gkn_ref[...]
        k_out[:, sl] = (y + kr).astype(BF16)
    v_out[...] = _dot(c, wuv_ref[...]).astype(BF16)


def _kv_proj(ckv, kr, w, tm):
    t, kl = ckv.shape
    n = MLA_HEADS * SLOT
    return pl.pallas_call(
        _kv_proj_kernel,
        grid=(t // tm,),
        in_specs=[_row_spec(tm, kl), _row_spec(tm, SLOT), _const_spec(w['w_uk'].shape),
                  _const_spec(w['w_uv'].shape), _const_spec(w['g_kn'].shape),
                  _const_spec(w['m_avg'].shape)],
        out_specs=[_row_spec(tm, n), _row_spec(tm, n)],
        out_shape=[jax.ShapeDtypeStruct((t, n), BF16)] * 2,
        compiler_params=_params(("parallel",)),
        name="kv_proj",
    )(ckv, kr, w['w_uk'], w['w_uv'], w['g_kn'], w['m_avg'])


def _attn_prompt_kernel(bound_ref, q_ref, k_ref, v_ref, o_ref, *, tile):
    n_tiles = q_ref.shape[1] // tile
    bound = bound_ref[0]
    first = lax.broadcasted_iota(jnp.int32, (1, LANES), 1) < V_HEAD
    row_chunk = lax.broadcasted_iota(jnp.int32, (tile, tile), 0) // CHUNK
    col_chunk = lax.broadcasted_iota(jnp.int32, (tile, tile), 1) // CHUNK
    visible = col_chunk <= row_chunk

    def static_step(q, kstart, carry, masked):
        sums, acc = carry
        kstart = pl.multiple_of(kstart, tile)
        k = k_ref[0, pl.ds(kstart, tile), :]
        v = v_ref[0, pl.ds(kstart, tile), :]
        new_sums = []
        for h in range(2):
            sl = slice(h * SLOT, (h + 1) * SLOT)
            s = _dot_nt(q[:, sl], k[:, sl]) * F32(MLA_SCALE * LOG2_E) - bound
            if masked:
                s = jnp.where(visible, s, F32(-1e30))
            p = jnp.exp2(s)
            part = sums[h]
            for c in range(tile // LANES):
                part = part + p[:, c * LANES:(c + 1) * LANES]
            new_sums.append(part)
            acc = acc + _dot(p.astype(BF16), v[:, sl])
        return tuple(new_sums), acc

    def static_tile(qi, _):
        rows = pl.ds(pl.multiple_of(qi * tile, tile), tile)
        q = q_ref[0, rows, :]
        zeros = jnp.zeros((tile, LANES), F32)

        def pair(i, c):
            c = static_step(q, 2 * i * tile, c, False)
            return static_step(q, (2 * i + 1) * tile, c, False)

        def odd_tail(c):
            return static_step(q, qi * tile, static_step(q, (qi - 1) * tile, c, False), True)

        carry = lax.fori_loop(0, qi // 2, pair, ((zeros, zeros), zeros))
        sums, acc = lax.cond(qi % 2 == 1, odd_tail, lambda c: static_step(q, qi * tile, c, True), carry)
        l0 = jnp.sum(sums[0], axis=-1, keepdims=True)
        l1 = jnp.sum(sums[1], axis=-1, keepdims=True)
        o_ref[0, rows, :] = (acc / jnp.where(first, l0, l1)).astype(BF16)
        return 0

    @pl.when(bound < STATIC_SHIFT_LIMIT)
    def _():
        lax.fori_loop(0, n_tiles, static_tile, 0)

    def step(q, kstart, carry, masked):
        ms, ls, acc = carry
        kstart = pl.multiple_of(kstart, tile)
        k = k_ref[0, pl.ds(kstart, tile), :]
        v = v_ref[0, pl.ds(kstart, tile), :]
        new_ms, new_ls, alphas, pvs = [], [], [], []
        for h in range(2):
            sl = slice(h * SLOT, (h + 1) * SLOT)
            s = _dot_nt(q[:, sl], k[:, sl]) * F32(MLA_SCALE * LOG2_E)
            if masked:
                s = jnp.where(visible, s, F32(-1e30))
            m_new = jnp.maximum(ms[h], jnp.max(s, axis=-1, keepdims=True))
            alpha = jnp.exp2(ms[h] - m_new)
            p = jnp.exp2(s - m_new)
            new_ls.append(alpha * ls[h] + jnp.sum(p, axis=-1, keepdims=True))
            new_ms.append(m_new)
            alphas.append(alpha)
            pvs.append(_dot(p.astype(BF16), v[:, sl]))
        acc = acc * jnp.where(first, alphas[0], alphas[1]) + pvs[0] + pvs[1]
        return tuple(new_ms), tuple(new_ls), acc

    def online_tile(qi, _):
        rows = pl.ds(pl.multiple_of(qi * tile, tile), tile)
        q = q_ref[0, rows, :]
        neg = jnp.full((tile, 1), -1e30, F32)
        zero = jnp.zeros((tile, 1), F32)
        init = ((neg, neg), (zero, zero), jnp.zeros((tile, LANES), F32))
        carry = lax.fori_loop(0, qi, lambda i, c: step(q, i * tile, c, False), init)
        _, ls, acc = step(q, qi * tile, carry, True)
        o_ref[0, rows, :] = (acc / jnp.where(first, ls[0], ls[1])).astype(BF16)
        return 0

    @pl.when(bound >= STATIC_SHIFT_LIMIT)
    def _():
        lax.fori_loop(0, n_tiles, online_tile, 0)


def _score_bound(w):
    gmax = lambda g, lo, n: jnp.max(jnp.abs(g[0, lo:lo + n]))
    nope = QK_NOPE * gmax(w['g_q'], 0, QK_NOPE) * gmax(w['g_kn'], 0, QK_NOPE)
    rope = QK_ROPE * gmax(w['g_q'], ROPE_LO, QK_ROPE) * gmax(w['g_kr'], ROPE_LO, QK_ROPE)
    return (BOUND_MARGIN * MLA_SCALE * LOG2_E * (nope + rope)).reshape(1).astype(F32)


def _attn_prompt(q, k, v, bound, tile):
    b, s, _ = q.shape
    pairs = MLA_HEADS // 2
    pair_block = pl.BlockSpec((1, s, 2 * SLOT), lambda bi, p: (bi, 0, p))
    return pl.pallas_call(
        functools.partial(_attn_prompt_kernel, tile=tile),
        grid=(b, pairs),
        in_specs=[pl.BlockSpec(memory_space=pltpu.SMEM), pair_block, pair_block, pair_block],
        out_specs=pl.BlockSpec((1, s, LANES), lambda bi, p: (bi, 0, p)),
        out_shape=jax.ShapeDtypeStruct((b, s, pairs * LANES), BF16),
        compiler_params=_params(("parallel", "parallel")),
        name="attn_prompt",
    )(bound, q, k, v)


def _attn_sample_kernel(q_ref, pckv_ref, pkr_ref, nckv_ref, nkr_ref, wuk_ref, wuv_ref, gkn_ref, place_ref,
                        o_ref):
    sq = q_ref.shape[1]
    rows = MLA_HEADS * sq
    gkn = gkn_ref[...]

    def keys_values(ckv, kr_slot):
        c = ckv.astype(BF16)
        kn = _dot(c, wuk_ref[...])
        parts = []
        for h in range(MLA_HEADS):
            ks = kn[:, h * SLOT:(h + 1) * SLOT]
            ms = jnp.sum(ks * ks, axis=-1, keepdims=True) * (1.0 / QK_NOPE)
            parts.append((ks * lax.rsqrt(ms + EPS) * gkn + kr_slot).astype(BF16))
        return jnp.concatenate(parts, axis=1), _dot(c, wuv_ref[...]).astype(BF16)

    past_kr = _dot(pkr_ref[0].astype(BF16), place_ref[...])
    k_p, v_p = keys_values(pckv_ref[0], past_kr)
    k_n, v_n = keys_values(nckv_ref[0], nkr_ref[0])
    k = jnp.concatenate([k_p, k_n], axis=0)
    v = jnp.concatenate([v_p, v_n], axis=0)

    qt = jnp.concatenate([q_ref[0].astype(F32)] * MLA_HEADS, axis=0)
    q_shape = (rows, MLA_HEADS * SLOT)
    own = (lax.broadcasted_iota(jnp.int32, q_shape, 0) // sq
           == lax.broadcasted_iota(jnp.int32, q_shape, 1) // SLOT)
    qbd = jnp.where(own, qt, 0.0).astype(BF16)
    s = _dot_nt(qbd, k) * MLA_SCALE
    p = jnp.exp(s - jnp.max(s, axis=-1, keepdims=True))
    p = p / jnp.sum(p, axis=-1, keepdims=True)
    o_all = _dot(p.astype(BF16), v)
    o_shape = o_all.shape
    own = (lax.broadcasted_iota(jnp.int32, o_shape, 0) // sq
           == lax.broadcasted_iota(jnp.int32, o_shape, 1) // V_HEAD)
    o_all = jnp.where(own, o_all, 0.0)
    o = o_all[0:sq]
    for h in range(1, MLA_HEADS):
        o = o + o_all[h * sq:(h + 1) * sq]
    o_ref[0] = o.astype(BF16)


def _attn_sample(q, past_ckv, past_kr, new_ckv, new_kr, w):
    b, sq, nq = q.shape
    npast, kl = past_ckv.shape[1:]
    nv = w['w_uv_plain'].shape[1]
    stream = lambda n, width: pl.BlockSpec((1, n, width), lambda bi: (bi, 0, 0))
    consts = [w['w_uk'], w['w_uv_plain'], w['g_kn'], w['rope_place']]
    return pl.pallas_call(
        _attn_sample_kernel,
        grid=(b,),
        in_specs=[stream(sq, nq), stream(npast, kl), stream(npast, QK_ROPE), stream(sq, kl), stream(sq, SLOT)]
        + [_const_spec(c.shape) for c in consts],
        out_specs=stream(sq, nv),
        out_shape=jax.ShapeDtypeStruct((b, sq, nv), BF16),
        compiler_params=_params(("parallel",)),
        name="attn_sample",
    )(q, past_ckv, past_kr, new_ckv, new_kr, *consts)


def _merge_kernel(x_ref, o_ref, siga_ref, gbob_ref, woa_ref, wo_ref, gx_ref, wcq_ref, gcq_ref,
                  x1_out, qc_out):
    oa = _dot(o_ref[...], woa_ref[...])
    merged = siga_ref[...].astype(F32) * oa + gbob_ref[...]
    x1 = x_ref[...] + _dot(merged.astype(BF16), wo_ref[...])
    x1_out[...] = x1
    qc = _dot(_rms(x1, gx_ref[...]).astype(BF16), wcq_ref[...])
    for h in range(MEM_HEADS):
        sl = slice(h * MEM_HEAD_DIM, (h + 1) * MEM_HEAD_DIM)
        qc_out[:, sl] = _rms(qc[:, sl], gcq_ref[...]).astype(BF16)


def _merge(x2d, o, siga, gbob, w, tm):
    t, d = x2d.shape
    consts = [w['w_oa'], w['w_o'], w['g_xattn'], w['w_cq_mem'], w['g_cq']]
    nq = w['w_cq_mem'].shape[1]
    return pl.pallas_call(
        _merge_kernel,
        grid=(t // tm,),
        in_specs=[_row_spec(tm, d), _row_spec(tm, o.shape[1]), _row_spec(tm, d), _row_spec(tm, d)]
        + [_const_spec(c.shape) for c in consts],
        out_specs=[_row_spec(tm, d), _row_spec(tm, nq)],
        out_shape=[jax.ShapeDtypeStruct((t, d), F32), jax.ShapeDtypeStruct((t, nq), BF16)],
        compiler_params=_params(("parallel",)),
        name="merge",
    )(x2d, o, siga, gbob, *consts)


def _mem_kv_kernel(mem_ref, gmem_ref, wck_ref, gck_ref, wcv_ref, k_out, v_out):
    m = _rms(mem_ref[...], gmem_ref[...]).astype(BF16)
    k = _dot(m, wck_ref[...])
    for h in range(MEM_HEADS):
        sl = slice(h * MEM_HEAD_DIM, (h + 1) * MEM_HEAD_DIM)
        k_out[:, sl] = _rms(k[:, sl], gck_ref[...])
    v_out[...] = _dot(m, wcv_ref[...])


def _mem_kv(mem2d, w, tm):
    t, d = mem2d.shape
    consts = [w['g_mem'], w['w_ck'], w['g_ck'], w['w_cv']]
    n = w['w_ck'].shape[1]
    return pl.pallas_call(
        _mem_kv_kernel,
        grid=(t // tm,),
        in_specs=[_row_spec(tm, d)] + [_const_spec(c.shape) for c in consts],
        out_specs=[_row_spec(tm, n)] * 2,
        out_shape=[jax.ShapeDtypeStruct((t, n), F32)] * 2,
        compiler_params=_params(("parallel",)),
        name="mem_kv",
    )(mem2d, *consts)


def _cross_kernel(q_ref, k_ref, v_ref, o_ref):
    q = q_ref[0]
    k = k_ref[0].astype(BF16)
    v = v_ref[0].astype(BF16)
    for h in range(MEM_HEADS):
        sl = slice(h * MEM_HEAD_DIM, (h + 1) * MEM_HEAD_DIM)
        s = _dot_nt(q[:, sl], k[:, sl]) * MEM_SCALE
        p = jnp.exp(s - jnp.max(s, axis=-1, keepdims=True))
        p = p / jnp.sum(p, axis=-1, keepdims=True)
        o_ref[0, :, sl] = _dot(p.astype(BF16), v[:, sl]).astype(BF16)


def _cross(qc, mk, mv, tc):
    b, s, n = qc.shape
    nm = mk.shape[1]
    return pl.pallas_call(
        _cross_kernel,
        grid=(b, s // tc),
        in_specs=[pl.BlockSpec((1, tc, n), lambda bi, i: (bi, i, 0)),
                  pl.BlockSpec((1, nm, n), lambda bi, i: (bi, 0, 0)),
                  pl.BlockSpec((1, nm, n), lambda bi, i: (bi, 0, 0))],
        out_specs=pl.BlockSpec((1, tc, n), lambda bi, i: (bi, i, 0)),
        out_shape=jax.ShapeDtypeStruct((b, s, n), BF16),
        compiler_params=_params(("parallel", "parallel")),
        name="cross_attn",
    )(qc, mk, mv)


def _cross_out_kernel(x1_ref, oc_ref, wco_ref, gffn_ref, wpq_ref, x2_out, h_out, q_out):
    x2 = x1_ref[...] + _dot(oc_ref[...], wco_ref[...])
    x2_out[...] = x2
    h = _rms(x2, gffn_ref[...]).astype(BF16)
    h_out[...] = h
    q_out[...] = _dot(h, wpq_ref[...]).astype(BF16)


def _cross_out(x1, oc, w, tm):
    t, d = x1.shape
    nq = w['w_pq'].shape[1]
    return pl.pallas_call(
        _cross_out_kernel,
        grid=(t // tm,),
        in_specs=[_row_spec(tm, d), _row_spec(tm, oc.shape[1]), _const_spec(w['w_co'].shape),
                  _const_spec(w['g_ffn'].shape), _const_spec(w['w_pq'].shape)],
        out_specs=[_row_spec(tm, d), _row_spec(tm, d), _row_spec(tm, nq)],
        out_shape=[jax.ShapeDtypeStruct((t, d), F32), jax.ShapeDtypeStruct((t, d), BF16),
                   jax.ShapeDtypeStruct((t, nq), BF16)],
        compiler_params=_params(("parallel",)),
        name="cross_out",
    )(x1, oc, w['w_co'], w['g_ffn'], w['w_pq'])


def _pair_blocks():
    k = PEER_TOPK
    lead = 2
    blocks, flat = [], []
    for a in range(lead):
        nb = k // (a + 1)
        blocks.append(('row', a, nb))
        flat += [a * k + b for b in range(nb)]
    for b in range(lead):
        na = k // (b + 1)
        blocks.append(('col', b, na))
        flat += [a * k + b if a >= lead else -1 for a in range(na)]
    cells = [(a, b) for a in range(lead, k) for b in range(lead, k) if (a + 1) * (b + 1) <= k]
    pad = -len(cells) % 8
    blocks.append(('cells', cells + [cells[0]] * pad, len(cells) + pad))
    flat += [a * k + b for a, b in cells] + [-1] * pad
    assert all(n % 8 == 0 for _, _, n in blocks)
    return blocks, np.asarray(flat, np.float32)


def _pick_rows(v, ranks):
    row = lax.broadcasted_iota(jnp.int32, (len(ranks), v.shape[1]), 0)
    out = jnp.broadcast_to(v[ranks[0]:ranks[0] + 1, :], row.shape)
    for i in range(1, len(ranks)):
        if ranks[i] != ranks[0]:
            out = jnp.where(row == i, v[ranks[i]:ranks[i] + 1, :], out)
    return out


def _pair_candidates(v0, v1, blocks, combine):
    parts = []
    for kind, spec, n in blocks:
        if kind == 'row':
            parts.append(combine(v0[spec:spec + 1, :], v1[0:n, :]))
        elif kind == 'col':
            parts.append(combine(v0[0:n, :], v1[spec:spec + 1, :]))
        else:
            parts.append(combine(_pick_rows(v0, [a for a, _ in spec]), _pick_rows(v1, [b for _, b in spec])))
    return jnp.concatenate(parts, axis=0)


def _peer_kernel(h_ref, x_ref, q_ref, keys_ref, flat_ref, u_ref, v_ref, out_ref,
                 mask_scr, sv_scr, si_scr, ts_scr, te_scr, it_scr, jt_scr, gt_scr,
                 isel_scr, jsel_scr, gsel_scr):
    tm = h_ref.shape[0]
    te = v_ref.shape[0]
    r = pl.program_id(0)
    e = pl.program_id(1)
    n = pl.num_programs(0) - 2
    rows_per_step = te // N_KEYS
    tokens_per_head = tm // PEER_HEADS
    blocks, _ = _pair_blocks()
    neg_inf = F32(-jnp.inf)
    key_rows = lax.broadcasted_iota(jnp.int32, (N_KEYS, LANES), 0).astype(F32)
    swept = lax.rem(r, 2)
    ranked = 1 - swept

    def mask_token(t):
        irow = isel_scr[swept, pl.ds(t, 1), :]
        jrow = jsel_scr[swept, pl.ds(t, 1), :]
        grow = gsel_scr[swept, pl.ds(t, 1), :]
        rw = jnp.where(key_rows == irow, 0.5 * grow, 0.0).astype(BF16)
        cw = jnp.where(key_rows == jrow, 1.0, 0.0).astype(BF16)
        r0 = pl.multiple_of(t * MASK_PITCH, 8)
        mask_scr[pl.ds(r0, N_KEYS), :] = _dot_nt(rw, cw)

    def rank_pairs(hd):
        sv0, sv1 = sv_scr[2 * hd], sv_scr[2 * hd + 1]
        si0, si1 = si_scr[2 * hd], si_scr[2 * hd + 1]
        flat = flat_ref[...]
        cand = _pair_candidates(sv0, sv1, blocks, lambda x, y: x + y)
        cand = jnp.where(flat >= 0.0, cand, neg_inf)
        for kk in range(PEER_TOPK):
            m = jnp.max(cand, axis=0, keepdims=True)
            pos = jnp.min(jnp.where(cand == m, flat, F32(PEER_TOPK * PEER_TOPK)), axis=0, keepdims=True)
            ts_scr[kk:kk + 1, :] = m
            te_scr[kk:kk + 1, :] = pos
            cand = jnp.where(flat == pos, neg_inf, cand)
        top = ts_scr[...]
        ex = jnp.exp(top - top[0:1, :])
        gate = ex / jnp.sum(ex, axis=0, keepdims=True)
        pos = te_scr[...]
        rank_a = jnp.floor(pos * F32(1.0 / PEER_TOPK))
        rank_b = pos - rank_a * F32(PEER_TOPK)
        irow = jnp.zeros_like(pos)
        jrow = jnp.zeros_like(pos)
        for c in range(PEER_TOPK):
            irow = jnp.where(rank_a == F32(c), si0[c:c + 1, :], irow)
            jrow = jnp.where(rank_b == F32(c), si1[c:c + 1, :], jrow)
        r0 = pl.multiple_of(hd * PEER_TOPK, PEER_TOPK)
        it_scr[pl.ds(r0, PEER_TOPK), :] = irow
        jt_scr[pl.ds(r0, PEER_TOPK), :] = jrow
        gt_scr[pl.ds(r0, PEER_TOPK), :] = gate

    def publish_ranked():
        isel_scr[ranked] = it_scr[...].T
        jsel_scr[ranked] = jt_scr[...].T
        gsel_scr[ranked] = gt_scr[...].T

    @pl.when(e == 0)
    def _():
        out_ref[...] = x_ref[...]

        @pl.when(r == 1)
        def _():
            def head(hd, _):
                rank_pairs(hd)
                return 0

            lax.fori_loop(0, PEER_HEADS, head, 0)
            publish_ranked()

        @pl.when(jnp.logical_and(r >= 2, r <= n))
        def _():
            def trip(i, _):
                for t in range(tokens_per_head):
                    mask_token(i * tokens_per_head + t)
                rank_pairs(i)
                return 0

            lax.fori_loop(0, PEER_HEADS, trip, 0)
            publish_ranked()

        @pl.when(r == n + 1)
        def _():
            def trip(i, _):
                for t in range(MASK_UNROLL):
                    mask_token(i * MASK_UNROLL + t)
                return 0

            lax.fori_loop(0, tm // MASK_UNROLL, trip, 0)

    def slot_scores():
        d0 = pl.multiple_of(e * PEER_HALF, PEER_HALF)
        return _dot_nt(q_ref[:, pl.ds(d0, PEER_HALF)], keys_ref[e])

    def rank_slot(scores):
        lane = lax.broadcasted_iota(jnp.int32, (tm, N_KEYS), 1)
        picked = jnp.zeros((tm, N_KEYS), jnp.int32)
        a = scores
        for kk in range(PEER_TOPK):
            idx = jnp.argmax(a, axis=-1, keepdims=True).astype(jnp.int32)
            a = jnp.where(lane == idx, neg_inf, a)
            picked = jnp.where(lane == kk, idx, picked)
        vals = jnp.take_along_axis(scores, picked, axis=1)
        sv_scr[e] = vals.T[0:PEER_TOPK, :]
        si_scr[e] = (N_KEYS - 1 - picked).astype(F32).T[0:PEER_TOPK, :]

    def expert_chunk():
        h = h_ref[...]
        row0 = e * rows_per_step
        weighted = []
        for c in range(te // PEER_CHUNK):
            rows = slice(c * PEER_CHUNK, (c + 1) * PEER_CHUNK)
            pre = _dot_nt(h, u_ref[rows, :])
            half_mask = jnp.concatenate(
                [mask_scr[pl.ds(row0 + c * (PEER_CHUNK // N_KEYS) + rr, tm, stride=MASK_PITCH), :]
                 for rr in range(PEER_CHUNK // N_KEYS)], axis=1)
            weighted.append((pre * (1.0 + lax.erf(pre * np.float32(np.sqrt(0.5)))) * half_mask).astype(BF16))
        out_ref[...] += _dot(jnp.concatenate(weighted, axis=1), v_ref[...])

    ranks = r < n
    sweeps = r >= 2

    @pl.when(jnp.logical_and(ranks, jnp.logical_not(sweeps)))
    def _():
        rank_slot(slot_scores())

    @pl.when(jnp.logical_and(ranks, sweeps))
    def _():
        a = slot_scores()
        expert_chunk()
        rank_slot(a)

    @pl.when(jnp.logical_and(jnp.logical_not(ranks), sweeps))
    def _():
        expert_chunk()


def _peer(h, x2, q_route, w, tm, te):
    t, d = h.shape
    n = t // tm
    nexp = w['peer_v'].shape[0]
    assert nexp // te == 2 * PEER_HEADS, "one half-key slot is ranked per expert step"
    assert tm % PEER_HEADS == 0 and tm % MASK_UNROLL == 0
    nsel = PEER_HEADS * PEER_TOPK
    _, flat = _pair_blocks()
    flat = jnp.asarray(np.repeat(flat[:, None], tm, axis=1))
    swept_tile = lambda r, e: (jnp.clip(r - 2, 0, n - 1), 0)
    dense_tile = lambda width: pl.BlockSpec((tm, width), swept_tile, pipeline_mode=pl.Buffered(1))
    expert_tile = pl.BlockSpec((te, d), lambda r, e: (jnp.where(r >= 2, e, 0), 0))
    return pl.pallas_call(
        _peer_kernel,
        grid=(n + 2, nexp // te),
        in_specs=[dense_tile(d), dense_tile(d),
                  pl.BlockSpec((tm, q_route.shape[1]), lambda r, e: (jnp.minimum(r, n - 1), 0),
                               pipeline_mode=pl.Buffered(1)),
                  _const_spec(w['sub_keys'].shape), _const_spec(flat.shape), expert_tile, expert_tile],
        out_specs=pl.BlockSpec((tm, d), swept_tile),
        out_shape=jax.ShapeDtypeStruct((t, d), F32),
        scratch_shapes=[pltpu.VMEM((tm * MASK_PITCH, LANES), F32),
                        pltpu.VMEM((2 * PEER_HEADS, PEER_TOPK, tm), F32),
                        pltpu.VMEM((2 * PEER_HEADS, PEER_TOPK, tm), F32),
                        pltpu.VMEM((PEER_TOPK, tm), F32),
                        pltpu.VMEM((PEER_TOPK, tm), F32),
                        pltpu.VMEM((nsel, tm), F32),
                        pltpu.VMEM((nsel, tm), F32),
                        pltpu.VMEM((nsel, tm), F32),
                        pltpu.VMEM((2, tm, nsel), F32),
                        pltpu.VMEM((2, tm, nsel), F32),
                        pltpu.VMEM((2, tm, nsel), F32)],
        compiler_params=_params(("arbitrary", "arbitrary"), PEER_VMEM_LIMIT),
        name="peer",
    )(h, x2, q_route, w['sub_keys'], flat, w['peer_u'], w['peer_v'])


def _slot_gain(parts):
    row = jnp.zeros((SLOT,), F32)
    for off, g in parts:
        row = lax.dynamic_update_slice(row, g.astype(F32), (off,))
    return row[None, :]


def _group_avg_matrix():
    m = np.zeros((SLOT, SLOT), np.float32)
    m[:QK_NOPE, :QK_NOPE] = 1.0 / QK_NOPE
    m[ROPE_LO:ROPE_LO + QK_ROPE, ROPE_LO:ROPE_LO + QK_ROPE] = 1.0 / QK_ROPE
    return jnp.asarray(np.concatenate([m, m], axis=0), BF16)


def _rope_tables(pos):
    inv = ROPE_THETA ** (-jnp.arange(ROPE_HALF, dtype=F32) / ROPE_HALF)
    ang = pos.astype(F32)[:, None] * inv[None, :]
    cos, sin = jnp.cos(ang), jnp.sin(ang)
    n = pos.shape[0]
    tail = SLOT - ROPE_LO - QK_ROPE
    cos_t = jnp.concatenate([jnp.ones((n, ROPE_LO), F32), cos, cos, jnp.zeros((n, tail), F32)], axis=1)
    sin_up = jnp.concatenate([jnp.zeros((n, ROPE_LO + ROPE_HALF), F32), sin, jnp.zeros((n, tail), F32)], axis=1)
    sin_dn = jnp.concatenate([jnp.zeros((n, ROPE_LO), F32), -sin, jnp.zeros((n, tail + ROPE_HALF), F32)], axis=1)
    return cos_t, sin_up, sin_dn


def _prep_weights(p, i):
    w_in = p['w_in'][i]
    q_lora = p['g_q_lat'].shape[1]
    kv_lora = p['g_kv_lat'].shape[1]
    gw = p['g_gm'].shape[1]
    d = w_in.shape[0]
    o = np.cumsum([0, q_lora, kv_lora, QK_ROPE, gw, gw, d, d])
    seg = lambda k: w_in[:, o[k]:o[k + 1]]
    row = lambda a: a[i][None, :].astype(F32)
    w = {}
    w['g_mix'] = row(p['g_mix'])
    w['w_cq'] = seg(0).astype(BF16)
    w['w_ckv'] = seg(1).astype(BF16)
    w['w_kr'] = jnp.pad(seg(2), ((0, 0), (ROPE_LO, SLOT - ROPE_LO - QK_ROPE))).astype(BF16)
    w['w_u'] = seg(3).astype(BF16)
    w['w_v'] = seg(4).astype(BF16)
    w['w_ga'] = seg(5).astype(BF16)
    w['w_gb'] = seg(6).astype(BF16)
    w['g_q_lat'] = row(p['g_q_lat'])
    w['g_kv_lat'] = row(p['g_kv_lat'])
    w['g_kr'] = _slot_gain([(ROPE_LO, p['g_kr'][i])])
    w['g_gm'] = row(p['g_gm'])
    head_dim = QK_NOPE + QK_ROPE
    w_uq = p['w_uq'][i].reshape(q_lora, MLA_HEADS, head_dim)
    w['w_uq'] = jnp.pad(w_uq, ((0, 0), (0, 0), (0, SLOT - head_dim))).reshape(q_lora, -1).astype(BF16)
    w['g_q'] = _slot_gain([(0, p['g_qn'][i]), (ROPE_LO, p['g_qr'][i])])
    w['m_avg'] = _group_avg_matrix()
    w_uk = p['w_uk'][i].reshape(kv_lora, MLA_HEADS, QK_NOPE)
    w['w_uk'] = jnp.pad(w_uk, ((0, 0), (0, 0), (0, SLOT - QK_NOPE))).reshape(kv_lora, -1).astype(BF16)
    w['g_kn'] = _slot_gain([(0, p['g_kn'][i])])
    w_uv = p['w_uv'][i].reshape(kv_lora, MLA_HEADS // 2, 2, V_HEAD)
    eye2 = jnp.eye(2, dtype=w_uv.dtype)
    w['w_uv'] = jnp.einsum('cpjd,jk->cpjkd', w_uv, eye2).reshape(kv_lora, -1).astype(BF16)
    w['w_uv_plain'] = p['w_uv'][i].astype(BF16)
    place = np.zeros((QK_ROPE, SLOT), np.float32)
    place[np.arange(QK_ROPE), ROPE_LO + np.arange(QK_ROPE)] = 1.0
    w['rope_place'] = jnp.asarray(place, BF16)
    w['w_oa'] = p['w_oa'][i].astype(BF16)
    w['w_ob'] = p['w_ob'][i].astype(BF16)
    w['w_o'] = p['w_o'][i].astype(BF16)
    w['g_xattn'] = row(p['g_xattn'])
    w['g_mem'] = row(p['g_mem'])
    w['w_cq_mem'] = p['w_cq'][i].astype(BF16)
    w['g_cq'] = row(p['g_cq'])
    w['w_ck'] = p['w_ck'][i].astype(BF16)
    w['g_ck'] = row(p['g_ck'])
    w['w_cv'] = p['w_cv'][i].astype(BF16)
    w['w_co'] = p['w_co'][i].astype(BF16)
    w['g_ffn'] = row(p['g_ffn'])
    w['w_pq'] = p['w_pq'][i].astype(BF16)
    w['sub_keys'] = p['sub_keys'][i].reshape(2 * PEER_HEADS, N_KEYS, PEER_HALF)[:, ::-1, :].astype(BF16)
    w['peer_u'] = p['peer_u'][i].astype(BF16)
    w['peer_v'] = p['peer_v'][i].astype(BF16)
    w['w_s'] = p['w_s'][i]
    w['b_s'] = p['b_s'][i]
    return w


def _spatial_operands(w, seq, tm):
    chunk = min(seq, GM_CHUNK)
    reps = GM_CHUNK // chunk
    w_mask = jnp.tril(w['w_s'][:, :chunk, :chunk])
    w_mask = w_mask.astype(BF16)
    mix = jnp.concatenate(
        [jnp.pad(w_mask, ((0, 0), (0, 0), (a * chunk, (reps - 1 - a) * chunk))) for a in range(reps)], axis=1)
    bias = jnp.tile(w['b_s'][:, :chunk].T, (tm // chunk, 1))
    bias = jnp.repeat(bias, LANES, axis=1)
    return mix.astype(BF16), bias.astype(F32)


def _token_mixer(x2d, w, seq, pos, past, tm):
    t, d = x2d.shape
    nb = t // seq
    period = max(seq, tm)
    tabs = _rope_tables(jnp.tile(pos, period // seq))
    cq, ckv, kr, u, vg, siga, sigb = _in_proj(x2d, w, tabs, tm)
    mix, bias = _spatial_operands(w, seq, tm)
    gbob = _gmlp(u, vg, sigb, mix, bias, w['w_ob'], tm)
    wide = min(WIDE_ROW_TILE, t)
    q = _q_proj(cq, w, tabs, wide)
    if past is None:
        k, v = _kv_proj(ckv, kr, w, wide)
        o = _attn_prompt(q.reshape(nb, seq, -1), k.reshape(nb, seq, -1), v.reshape(nb, seq, -1),
                         _score_bound(w), ATTN_TILE)
    else:
        past_ckv, past_kr = past
        o = _attn_sample(q.reshape(nb, seq, -1), past_ckv, past_kr, ckv.reshape(nb, seq, -1),
                         kr.reshape(nb, seq, SLOT), w)
    x1, qc = _merge(x2d, o.reshape(t, -1), siga, gbob, w, wide)
    return x1, qc, ckv, kr[:, ROPE_LO:ROPE_LO + QK_ROPE], vg


def _tail(x1, qc, mk, mv, w, nb, seq, tm):
    t, d = x1.shape
    oc = _cross(qc.reshape(nb, seq, -1), mk, mv, min(seq, WIDE_ROW_TILE))
    x2, h, q_route = _cross_out(x1, oc.reshape(t, -1), w, min(WIDE_ROW_TILE, t))
    return _peer(h, x2, q_route, w, PEER_ROWS, PEER_EXPERTS)


def kernel(x_prompt, x_sample, cache_mla_ckv, cache_mla_krope, cache_mem_k, cache_mem_v, mem_prompt, g_mix, w_in, g_q_lat, w_uq, g_qn, g_qr, g_kv_lat, g_kr, w_uk, w_uv, g_kn, w_oa, g_gm, w_s, b_s, w_ob, w_o, g_xattn, g_mem, w_cq, g_cq, w_ck, g_ck, w_cv, w_co, g_ffn, w_pq, sub_keys, peer_u, peer_v):
    params = dict(g_mix=g_mix, w_in=w_in, g_q_lat=g_q_lat, w_uq=w_uq, g_qn=g_qn, g_qr=g_qr,
                  g_kv_lat=g_kv_lat, g_kr=g_kr, w_uk=w_uk, w_uv=w_uv, g_kn=g_kn, w_oa=w_oa, g_gm=g_gm,
                  w_s=w_s, b_s=b_s, w_ob=w_ob, w_o=w_o, g_xattn=g_xattn, g_mem=g_mem, w_cq=w_cq,
                  g_cq=g_cq, w_ck=w_ck, g_ck=g_ck, w_cv=w_cv, w_co=w_co, g_ffn=g_ffn, w_pq=w_pq,
                  sub_keys=sub_keys, peer_u=peer_u, peer_v=peer_v)
    bp, sp, d = x_prompt.shape
    bs, ss, _ = x_sample.shape
    depth = w_in.shape[0]
    past_len = cache_mla_ckv.shape[2]
    n_mem = mem_prompt.shape[1]
    tm = ROW_TILE
    assert (bp * sp) % tm == 0 and (bs * ss) % tm == 0 and sp % ATTN_TILE == 0
    pos_p = jnp.arange(sp)
    pos_s = past_len + jnp.arange(ss)
    xp = x_prompt.reshape(bp * sp, d)
    xs = x_sample.reshape(bs * ss, d)
    outs = [[] for _ in range(7)]
    for i in range(depth):
        w = _prep_weights(params, i)
        xp1, qcp, ckv_p, kr_p, _ = _token_mixer(xp, w, sp, pos_p, None, tm)
        xs1, qcs, ckv_s, kr_s, vg_s = _token_mixer(xs, w, ss, pos_s, (cache_mla_ckv[i], cache_mla_krope[i]), tm)
        mk_p, mv_p = _mem_kv(mem_prompt.reshape(bp * n_mem, d), w, min(tm, bp * n_mem))
        xp = _tail(xp1, qcp, mk_p.reshape(bp, n_mem, -1), mv_p.reshape(bp, n_mem, -1), w, bp, sp, tm)
        xs = _tail(xs1, qcs, cache_mem_k[i].reshape(bs, n_mem, -1), cache_mem_v[i].reshape(bs, n_mem, -1),
                   w, bs, ss, tm)
        outs[0].append(ckv_p.reshape(bp, sp, -1))
        outs[1].append(kr_p.reshape(bp, sp, -1))
        outs[2].append(mk_p.reshape(bp, n_mem, MEM_HEADS, MEM_HEAD_DIM))
        outs[3].append(mv_p.reshape(bp, n_mem, MEM_HEADS, MEM_HEAD_DIM))
        outs[4].append(ckv_s.reshape(bs, ss, -1))
        outs[5].append(kr_s.reshape(bs, ss, -1))
        outs[6].append(vg_s.reshape(bs, ss, -1))
    return (xp.reshape(bp, sp, d), xs.reshape(bs, ss, d)) + tuple(jnp.stack(o) for o in outs)
```

```python
import functools

import jax
import jax.numpy as jnp
import numpy as np
from jax import lax
from jax.experimental import pallas as pl
from jax.experimental.pallas import tpu as pltpu

CHUNK = 64
EPS = 1e-6
MLA_HEADS = 16
QK_NOPE = 64
QK_ROPE = 32
V_HEAD = 64
ROPE_THETA = 10000.0
MLA_SCALE = (QK_NOPE + QK_ROPE) ** -0.5
GM_CHUNK = 128
GM_GROUPS = 8
MEM_HEADS = 4
MEM_HEAD_DIM = 128
MEM_SCALE = MEM_HEAD_DIM ** -0.5
PEER_HEADS = 8
N_KEYS = 128
PEER_TOPK = 16
PEER_HALF = 128

LANES = 128
SLOT = LANES
ROPE_LO = QK_NOPE
ROPE_HALF = QK_ROPE // 2
ROW_TILE = 512
WIDE_ROW_TILE = 1024
ATTN_TILE = 512
LOG2_E = float(np.log2(np.e))
STATIC_SHIFT_LIMIT = 48.0
BOUND_MARGIN = 1.02
PEER_ROWS = 512
PEER_EXPERTS = 1024
PEER_CHUNK = 512
MASK_PITCH = N_KEYS + 8
MASK_UNROLL = 32
VMEM_LIMIT = 48 * 1024 * 1024
PEER_VMEM_LIMIT = 60 * 1024 * 1024

F32 = jnp.float32
BF16 = jnp.bfloat16
_NT = (((1,), (1,)), ((), ()))


def _dot(a, b):
    return jnp.dot(a, b, preferred_element_type=F32)


def _dot_nt(a, b):
    return lax.dot_general(a, b, _NT, preferred_element_type=F32)


def _rms(xf, g):
    return xf * lax.rsqrt(jnp.mean(xf * xf, axis=-1, keepdims=True) + EPS) * g


def _gelu(x):
    return 0.5 * x * (1.0 + lax.erf(x * np.float32(np.sqrt(0.5))))


def _group_mean(sq, m_ref):
    hi = sq.astype(BF16)
    lo = (sq - hi.astype(F32)).astype(BF16)
    return _dot(jnp.concatenate([hi, lo], axis=1), m_ref[...])


def _rope_slot(y, cos, sin_up, sin_dn):
    return (y * cos + pltpu.roll(y, ROPE_HALF, 1) * sin_up
            + pltpu.roll(y, SLOT - ROPE_HALF, 1) * sin_dn)


def _const_spec(shape):
    nd = len(shape)
    return pl.BlockSpec(shape, lambda *_: (0,) * nd, pipeline_mode=pl.Buffered(1))


def _row_spec(tm, width):
    return pl.BlockSpec((tm, width), lambda i: (i, 0))


def _params(sem, limit=VMEM_LIMIT):
    return pltpu.CompilerParams(dimension_semantics=sem, vmem_limit_bytes=limit)


def _in_proj_kernel(x_ref, gmix_ref, wcq_ref, wckv_ref, wkr_ref, wu_ref, wv_ref, wga_ref, wgb_ref,
                    gq_ref, gkv_ref, gkr_ref, ggm_ref, cos_ref, sup_ref, sdn_ref,
                    cq_out, ckv_out, kr_out, u_out, vg_out, siga_out, sigb_out):
    h = _rms(x_ref[...], gmix_ref[...]).astype(BF16)
    cq_out[...] = _rms(_dot(h, wcq_ref[...]), gq_ref[...]).astype(BF16)
    ckv_out[...] = _rms(_dot(h, wckv_ref[...]), gkv_ref[...])
    kr = _dot(h, wkr_ref[...])
    ms = jnp.sum(kr * kr, axis=-1, keepdims=True) * (1.0 / QK_ROPE)
    kr = kr * lax.rsqrt(ms + EPS) * gkr_ref[...]
    kr_out[...] = _rope_slot(kr, cos_ref[...], sup_ref[...], sdn_ref[...])
    u_out[...] = _gelu(_dot(h, wu_ref[...])).astype(BF16)
    vg_out[...] = _rms(_gelu(_dot(h, wv_ref[...])), ggm_ref[...]).astype(vg_out.dtype)
    siga_out[...] = jax.nn.sigmoid(_dot(h, wga_ref[...])).astype(BF16)
    sigb_out[...] = jax.nn.sigmoid(_dot(h, wgb_ref[...])).astype(BF16)


def _in_proj(x2d, w, tabs, tm, vg_dtype):
    t, d = x2d.shape
    cos, sup, sdn = tabs
    ntab = cos.shape[0] // tm
    tab_spec = pl.BlockSpec((tm, SLOT), lambda i: (i % ntab, 0))
    consts = [w['g_mix'], w['w_cq'], w['w_ckv'], w['w_kr'], w['w_u'], w['w_v'], w['w_ga'], w['w_gb'],
              w['g_q_lat'], w['g_kv_lat'], w['g_kr'], w['g_gm']]
    gw = w['w_u'].shape[1]
    widths = [w['w_cq'].shape[1], w['w_ckv'].shape[1], SLOT, gw, gw, d, d]
    dtypes = [BF16, F32, F32, BF16, vg_dtype, BF16, BF16]
    return pl.pallas_call(
        _in_proj_kernel,
        grid=(t // tm,),
        in_specs=[_row_spec(tm, d)] + [_const_spec(c.shape) for c in consts] + [tab_spec] * 3,
        out_specs=[_row_spec(tm, n) for n in widths],
        out_shape=[jax.ShapeDtypeStruct((t, n), dt) for n, dt in zip(widths, dtypes)],
        compiler_params=_params(("parallel",)),
        name="in_proj",
    )(x2d, *consts, cos, sup, sdn)


def _gmlp_kernel(u_ref, vg_ref, sigb_ref, mix_ref, bias_ref, wob_ref, out_ref):
    vgb = vg_ref[...].astype(BF16)
    block = mix_ref.shape[1]
    parts = []
    for g in range(GM_GROUPS):
        sl = slice(g * LANES, (g + 1) * LANES)
        mixed = jnp.concatenate([_dot(mix_ref[g], vgb[r0:r0 + block, sl])
                                 for r0 in range(0, vgb.shape[0], block)], axis=0) + bias_ref[:, sl]
        parts.append((u_ref[:, sl].astype(F32) * mixed).astype(BF16))
    ob = _dot(jnp.concatenate(parts, axis=1), wob_ref[...])
    out_ref[...] = (sigb_ref[...].astype(F32) * ob).astype(BF16)


def _gmlp(u, vg, sigb, mix, bias, w_ob, tm):
    t, gw = u.shape
    d = w_ob.shape[1]
    return pl.pallas_call(
        _gmlp_kernel,
        grid=(t // tm,),
        in_specs=[_row_spec(tm, gw), _row_spec(tm, gw), _row_spec(tm, d),
                  _const_spec(mix.shape), _const_spec(bias.shape), _const_spec(w_ob.shape)],
        out_specs=_row_spec(tm, d),
        out_shape=jax.ShapeDtypeStruct((t, d), BF16),
        compiler_params=_params(("parallel",)),
        name="gmlp",
    )(u, vg, sigb, mix, bias, w_ob)


def _q_proj_kernel(cq_ref, wuq_ref, gq_ref, mavg_ref, cos_ref, sup_ref, sdn_ref, q_out):
    q = _dot(cq_ref[...], wuq_ref[...])
    cos, sup, sdn = cos_ref[...], sup_ref[...], sdn_ref[...]
    for h in range(MLA_HEADS):
        sl = slice(h * SLOT, (h + 1) * SLOT)
        qs = q[:, sl]
        y = qs * lax.rsqrt(_group_mean(qs * qs, mavg_ref) + EPS) * gq_ref[...]
        q_out[:, sl] = _rope_slot(y, cos, sup, sdn).astype(BF16)


def _q_proj(cq, w, tabs, tm):
    t, ql = cq.shape
    cos, sup, sdn = tabs
    ntab = cos.shape[0] // tm
    tab_spec = pl.BlockSpec((tm, SLOT), lambda i: (i % ntab, 0))
    n = MLA_HEADS * SLOT
    return pl.pallas_call(
        _q_proj_kernel,
        grid=(t // tm,),
        in_specs=[_row_spec(tm, ql), _const_spec(w['w_uq'].shape), _const_spec(w['g_q'].shape),
                  _const_spec(w['m_avg'].shape)] + [tab_spec] * 3,
        out_specs=_row_spec(tm, n),
        out_shape=jax.ShapeDtypeStruct((t, n), BF16),
        compiler_params=_params(("parallel",)),
        name="q_proj",
    )(cq, w['w_uq'], w['g_q'], w['m_avg'], cos, sup, sdn)


def _kv_proj_kernel(ckv_ref, kr_ref, wuk_ref, wuv_ref, gkn_ref, mavg_ref, k_out, v_out):
    c = ckv_ref[...].astype(BF16)
    kn = _dot(c, wuk_ref[...])
    kr = kr_ref[...]
    for h in range(MLA_HEADS):
        sl = slice(h * SLOT, (h + 1) * SLOT)
        ks = kn[:, sl]
        y = ks * lax.rsqrt(_group_mean(ks * ks, mavg_ref) + EPS) * gkn_ref[...]
        k_out[:, sl] = (y + kr).astype(BF16)
    v_out[...] = _dot(c, wuv_ref[...]).astype(BF16)


def _kv_proj(ckv, kr, w, tm):
    t, kl = ckv.shape
    n = MLA_HEADS * SLOT
    return pl.pallas_call(
        _kv_proj_kernel,
        grid=(t // tm,),
        in_specs=[_row_spec(tm, kl), _row_spec(tm, SLOT), _const_spec(w['w_uk'].shape),
                  _const_spec(w['w_uv'].shape), _const_spec(w['g_kn'].shape),
                  _const_spec(w['m_avg'].shape)],
        out_specs=[_row_spec(tm, n), _row_spec(tm, n)],
        out_shape=[jax.ShapeDtypeStruct((t, n), BF16)] * 2,
        compiler_params=_params(("parallel",)),
        name="kv_proj",
    )(ckv, kr, w['w_uk'], w['w_uv'], w['g_kn'], w['m_avg'])


def _attn_prompt_kernel(bound_ref, q_ref, k_ref, v_ref, o_ref, *, tile):
    n_tiles = q_ref.shape[1] // tile
    bound = bound_ref[0]
    first = lax.broadcasted_iota(jnp.int32, (1, LANES), 1) < V_HEAD
    row_chunk = lax.broadcasted_iota(jnp.int32, (tile, tile), 0) // CHUNK
    col_chunk = lax.broadcasted_iota(jnp.int32, (tile, tile), 1) // CHUNK
    visible = col_chunk <= row_chunk

    def static_step(q, kstart, carry, masked):
        sums, acc = carry
        kstart = pl.multiple_of(kstart, tile)
        k = k_ref[0, pl.ds(kstart, tile), :]
        v = v_ref[0, pl.ds(kstart, tile), :]
        new_sums = []
        for h in range(2):
            sl = slice(h * SLOT, (h + 1) * SLOT)
            s = _dot_nt(q[:, sl], k[:, sl]) * F32(MLA_SCALE * LOG2_E) - bound
            if masked:
                s = jnp.where(visible, s, F32(-1e30))
            p = jnp.exp2(s)
            part = sums[h]
            for c in range(tile // LANES):
                part = part + p[:, c * LANES:(c + 1) * LANES]
            new_sums.append(part)
            acc = acc + _dot(p.astype(BF16), v[:, sl])
        return tuple(new_sums), acc

    def static_tile(qi, _):
        rows = pl.ds(pl.multiple_of(qi * tile, tile), tile)
        q = q_ref[0, rows, :]
        zeros = jnp.zeros((tile, LANES), F32)

        def pair(i, c):
            c = static_step(q, 2 * i * tile, c, False)
            return static_step(q, (2 * i + 1) * tile, c, False)

        def odd_tail(c):
            return static_step(q, qi * tile, static_step(q, (qi - 1) * tile, c, False), True)

        carry = lax.fori_loop(0, qi // 2, pair, ((zeros, zeros), zeros))
        sums, acc = lax.cond(qi % 2 == 1, odd_tail, lambda c: static_step(q, qi * tile, c, True), carry)
        l0 = jnp.sum(sums[0], axis=-1, keepdims=True)
        l1 = jnp.sum(sums[1], axis=-1, keepdims=True)
        o_ref[0, rows, :] = (acc / jnp.where(first, l0, l1)).astype(BF16)
        return 0

    @pl.when(bound < STATIC_SHIFT_LIMIT)
    def _():
        lax.fori_loop(0, n_tiles, static_tile, 0)

    def step(q, kstart, carry, masked):
        ms, ls, acc = carry
        kstart = pl.multiple_of(kstart, tile)
        k = k_ref[0, pl.ds(kstart, tile), :]
        v = v_ref[0, pl.ds(kstart, tile), :]
        new_ms, new_ls, alphas, pvs = [], [], [], []
        for h in range(2):
            sl = slice(h * SLOT, (h + 1) * SLOT)
            s = _dot_nt(q[:, sl], k[:, sl]) * F32(MLA_SCALE * LOG2_E)
            if masked:
                s = jnp.where(visible, s, F32(-1e30))
            m_new = jnp.maximum(ms[h], jnp.max(s, axis=-1, keepdims=True))
            alpha = jnp.exp2(ms[h] - m_new)
            p = jnp.exp2(s - m_new)
            new_ls.append(alpha * ls[h] + jnp.sum(p, axis=-1, keepdims=True))
            new_ms.append(m_new)
            alphas.append(alpha)
            pvs.append(_dot(p.astype(BF16), v[:, sl]))
        acc = acc * jnp.where(first, alphas[0], alphas[1]) + pvs[0] + pvs[1]
        return tuple(new_ms), tuple(new_ls), acc

    def online_tile(qi, _):
        rows = pl.ds(pl.multiple_of(qi * tile, tile), tile)
        q = q_ref[0, rows, :]
        neg = jnp.full((tile, 1), -1e30, F32)
        zero = jnp.zeros((tile, 1), F32)
        init = ((neg, neg), (zero, zero), jnp.zeros((tile, LANES), F32))
        carry = lax.fori_loop(0, qi, lambda i, c: step(q, i * tile, c, False), init)
        _, ls, acc = step(q, qi * tile, carry, True)
        o_ref[0, rows, :] = (acc / jnp.where(first, ls[0], ls[1])).astype(BF16)
        return 0

    @pl.when(bound >= STATIC_SHIFT_LIMIT)
    def _():
        lax.fori_loop(0, n_tiles, online_tile, 0)


def _score_bound(w):
    gmax = lambda g, lo, n: jnp.max(jnp.abs(g[0, lo:lo + n]))
    nope = QK_NOPE * gmax(w['g_q'], 0, QK_NOPE) * gmax(w['g_kn'], 0, QK_NOPE)
    rope = QK_ROPE * gmax(w['g_q'], ROPE_LO, QK_ROPE) * gmax(w['g_kr'], ROPE_LO, QK_ROPE)
    return (BOUND_MARGIN * MLA_SCALE * LOG2_E * (nope + rope)).reshape(1).astype(F32)


def _attn_prompt(q, k, v, bound, tile):
    b, s, _ = q.shape
    pairs = MLA_HEADS // 2
    pair_block = pl.BlockSpec((1, s, 2 * SLOT), lambda bi, p: (bi, 0, p))
    return pl.pallas_call(
        functools.partial(_attn_prompt_kernel, tile=tile),
        grid=(b, pairs),
        in_specs=[pl.BlockSpec(memory_space=pltpu.SMEM), pair_block, pair_block, pair_block],
        out_specs=pl.BlockSpec((1, s, LANES), lambda bi, p: (bi, 0, p)),
        out_shape=jax.ShapeDtypeStruct((b, s, pairs * LANES), BF16),
        compiler_params=_params(("parallel", "parallel")),
        name="attn_prompt",
    )(bound, q, k, v)


def _attn_sample_kernel(q_ref, pckv_ref, pkr_ref, nckv_ref, nkr_ref, wuk_ref, wuv_ref, gkn_ref, place_ref,
                        o_ref):
    sq = q_ref.shape[1]
    rows = MLA_HEADS * sq
    gkn = gkn_ref[...]

    def keys_values(ckv, kr_slot):
        c = ckv.astype(BF16)
        kn = _dot(c, wuk_ref[...])
        parts = []
        for h in range(MLA_HEADS):
            ks = kn[:, h * SLOT:(h + 1) * SLOT]
            ms = jnp.sum(ks * ks, axis=-1, keepdims=True) * (1.0 / QK_NOPE)
            parts.append((ks * lax.rsqrt(ms + EPS) * gkn + kr_slot).astype(BF16))
        return jnp.concatenate(parts, axis=1), _dot(c, wuv_ref[...]).astype(BF16)

    past_kr = _dot(pkr_ref[0].astype(BF16), place_ref[...])
    k_p, v_p = keys_values(pckv_ref[0], past_kr)
    k_n, v_n = keys_values(nckv_ref[0], nkr_ref[0])
    k = jnp.concatenate([k_p, k_n], axis=0)
    v = jnp.concatenate([v_p, v_n], axis=0)

    qt = jnp.concatenate([q_ref[0].astype(F32)] * MLA_HEADS, axis=0)
    q_shape = (rows, MLA_HEADS * SLOT)
    own = (lax.broadcasted_iota(jnp.int32, q_shape, 0) // sq
           == lax.broadcasted_iota(jnp.int32, q_shape, 1) // SLOT)
    qbd = jnp.where(own, qt, 0.0).astype(BF16)
    s = _dot_nt(qbd, k) * MLA_SCALE
    p = jnp.exp(s - jnp.max(s, axis=-1, keepdims=True))
    p = p / jnp.sum(p, axis=-1, keepdims=True)
    o_all = _dot(p.astype(BF16), v)
    o_shape = o_all.shape
    own = (lax.broadcasted_iota(jnp.int32, o_shape, 0) // sq
           == lax.broadcasted_iota(jnp.int32, o_shape, 1) // V_HEAD)
    o_all = jnp.where(own, o_all, 0.0)
    o = o_all[0:sq]
    for h in range(1, MLA_HEADS):
        o = o + o_all[h * sq:(h + 1) * sq]
    o_ref[0] = o.astype(BF16)


def _attn_sample(q, past_ckv, past_kr, new_ckv, new_kr, w):
    b, sq, nq = q.shape
    npast, kl = past_ckv.shape[1:]
    nv = w['w_uv_plain'].shape[1]
    stream = lambda n, width: pl.BlockSpec((1, n, width), lambda bi: (bi, 0, 0))
    consts = [w['w_uk'], w['w_uv_plain'], w['g_kn'], w['rope_place']]
    return pl.pallas_call(
        _attn_sample_kernel,
        grid=(b,),
        in_specs=[stream(sq, nq), stream(npast, kl), stream(npast, QK_ROPE), stream(sq, kl), stream(sq, SLOT)]
        + [_const_spec(c.shape) for c in consts],
        out_specs=stream(sq, nv),
        out_shape=jax.ShapeDtypeStruct((b, sq, nv), BF16),
        compiler_params=_params(("parallel",)),
        name="attn_sample",
    )(q, past_ckv, past_kr, new_ckv, new_kr, *consts)


def _merge_kernel(x_ref, o_ref, siga_ref, gbob_ref, woa_ref, wo_ref, gx_ref, wcq_ref, gcq_ref,
                  x1_out, qc_out):
    oa = _dot(o_ref[...], woa_ref[...])
    merged = siga_ref[...].astype(F32) * oa + gbob_ref[...].astype(F32)
    x1 = x_ref[...] + _dot(merged.astype(BF16), wo_ref[...])
    x1_out[...] = x1
    qc = _dot(_rms(x1, gx_ref[...]).astype(BF16), wcq_ref[...])
    for h in range(MEM_HEADS):
        sl = slice(h * MEM_HEAD_DIM, (h + 1) * MEM_HEAD_DIM)
        qc_out[:, sl] = _rms(qc[:, sl], gcq_ref[...]).astype(BF16)


def _merge(x2d, o, siga, gbob, w, tm):
    t, d = x2d.shape
    consts = [w['w_oa'], w['w_o'], w['g_xattn'], w['w_cq_mem'], w['g_cq']]
    nq = w['w_cq_mem'].shape[1]
    return pl.pallas_call(
        _merge_kernel,
        grid=(t // tm,),
        in_specs=[_row_spec(tm, d), _row_spec(tm, o.shape[1]), _row_spec(tm, d), _row_spec(tm, d)]
        + [_const_spec(c.shape) for c in consts],
        out_specs=[_row_spec(tm, d), _row_spec(tm, nq)],
        out_shape=[jax.ShapeDtypeStruct((t, d), F32), jax.ShapeDtypeStruct((t, nq), BF16)],
        compiler_params=_params(("parallel",)),
        name="merge",
    )(x2d, o, siga, gbob, *consts)


def _mem_kv_kernel(mem_ref, gmem_ref, wck_ref, gck_ref, wcv_ref, k_out, v_out):
    m = _rms(mem_ref[...], gmem_ref[...]).astype(BF16)
    k = _dot(m, wck_ref[...])
    for h in range(MEM_HEADS):
        sl = slice(h * MEM_HEAD_DIM, (h + 1) * MEM_HEAD_DIM)
        k_out[:, sl] = _rms(k[:, sl], gck_ref[...])
    v_out[...] = _dot(m, wcv_ref[...])


def _mem_kv(mem2d, w, tm):
    t, d = mem2d.shape
    consts = [w['g_mem'], w['w_ck'], w['g_ck'], w['w_cv']]
    n = w['w_ck'].shape[1]
    return pl.pallas_call(
        _mem_kv_kernel,
        grid=(t // tm,),
        in_specs=[_row_spec(tm, d)] + [_const_spec(c.shape) for c in consts],
        out_specs=[_row_spec(tm, n)] * 2,
        out_shape=[jax.ShapeDtypeStruct((t, n), F32)] * 2,
        compiler_params=_params(("parallel",)),
        name="mem_kv",
    )(mem2d, *consts)


def _cross_kernel(q_ref, k_ref, v_ref, o_ref):
    q = q_ref[0]
    k = k_ref[0].astype(BF16)
    v = v_ref[0].astype(BF16)
    for h in range(MEM_HEADS):
        sl = slice(h * MEM_HEAD_DIM, (h + 1) * MEM_HEAD_DIM)
        s = _dot_nt(q[:, sl], k[:, sl]) * MEM_SCALE
        p = jnp.exp(s - jnp.max(s, axis=-1, keepdims=True))
        p = p / jnp.sum(p, axis=-1, keepdims=True)
        o_ref[0, :, sl] = _dot(p.astype(BF16), v[:, sl]).astype(BF16)


def _cross(qc, mk, mv, tc):
    b, s, n = qc.shape
    nm = mk.shape[1]
    return pl.pallas_call(
        _cross_kernel,
        grid=(b, s // tc),
        in_specs=[pl.BlockSpec((1, tc, n), lambda bi, i: (bi, i, 0)),
                  pl.BlockSpec((1, nm, n), lambda bi, i: (bi, 0, 0)),
                  pl.BlockSpec((1, nm, n), lambda bi, i: (bi, 0, 0))],
        out_specs=pl.BlockSpec((1, tc, n), lambda bi, i: (bi, i, 0)),
        out_shape=jax.ShapeDtypeStruct((b, s, n), BF16),
        compiler_params=_params(("parallel", "parallel")),
        name="cross_attn",
    )(qc, mk, mv)


def _cross_out_kernel(x1_ref, oc_ref, wco_ref, gffn_ref, wpq_ref, x2_out, h_out, q_out):
    x2 = x1_ref[...] + _dot(oc_ref[...], wco_ref[...])
    x2_out[...] = x2
    h = _rms(x2, gffn_ref[...]).astype(BF16)
    h_out[...] = h
    q_out[...] = _dot(h, wpq_ref[...]).astype(BF16)


def _cross_out(x1, oc, w, tm):
    t, d = x1.shape
    nq = w['w_pq'].shape[1]
    return pl.pallas_call(
        _cross_out_kernel,
        grid=(t // tm,),
        in_specs=[_row_spec(tm, d), _row_spec(tm, oc.shape[1]), _const_spec(w['w_co'].shape),
                  _const_spec(w['g_ffn'].shape), _const_spec(w['w_pq'].shape)],
        out_specs=[_row_spec(tm, d), _row_spec(tm, d), _row_spec(tm, nq)],
        out_shape=[jax.ShapeDtypeStruct((t, d), F32), jax.ShapeDtypeStruct((t, d), BF16),
                   jax.ShapeDtypeStruct((t, nq), BF16)],
        compiler_params=_params(("parallel",)),
        name="cross_out",
    )(x1, oc, w['w_co'], w['g_ffn'], w['w_pq'])


def _pair_blocks():
    k = PEER_TOPK
    lead = 2
    blocks, flat = [], []
    for a in range(lead):
        nb = k // (a + 1)
        blocks.append(('row', a, nb))
        flat += [a * k + b for b in range(nb)]
    for b in range(lead):
        na = k // (b + 1)
        blocks.append(('col', b, na))
        flat += [a * k + b if a >= lead else -1 for a in range(na)]
    cells = [(a, b) for a in range(lead, k) for b in range(lead, k) if (a + 1) * (b + 1) <= k]
    pad = -len(cells) % 8
    blocks.append(('cells', cells + [cells[0]] * pad, len(cells) + pad))
    flat += [a * k + b for a, b in cells] + [-1] * pad
    assert all(n % 8 == 0 for _, _, n in blocks)
    return blocks, np.asarray(flat, np.float32)


def _pick_rows(v, ranks):
    row = lax.broadcasted_iota(jnp.int32, (len(ranks), v.shape[1]), 0)
    out = jnp.broadcast_to(v[ranks[0]:ranks[0] + 1, :], row.shape)
    for i in range(1, len(ranks)):
        if ranks[i] != ranks[0]:
            out = jnp.where(row == i, v[ranks[i]:ranks[i] + 1, :], out)
    return out


def _pair_candidates(v0, v1, blocks, combine):
    parts = []
    for kind, spec, n in blocks:
        if kind == 'row':
            parts.append(combine(v0[spec:spec + 1, :], v1[0:n, :]))
        elif kind == 'col':
            parts.append(combine(v0[0:n, :], v1[spec:spec + 1, :]))
        else:
            parts.append(combine(_pick_rows(v0, [a for a, _ in spec]), _pick_rows(v1, [b for _, b in spec])))
    return jnp.concatenate(parts, axis=0)


def _peer_kernel(h_ref, x_ref, q_ref, keys_ref, flat_ref, u_ref, v_ref, out_ref,
                 mask_scr, sv_scr, si_scr, ts_scr, te_scr, it_scr, jt_scr, gt_scr,
                 isel_scr, jsel_scr, gsel_scr):
    tm = h_ref.shape[0]
    te = v_ref.shape[0]
    r = pl.program_id(0)
    e = pl.program_id(1)
    n = pl.num_programs(0) - 2
    rows_per_step = te // N_KEYS
    tokens_per_head = tm // PEER_HEADS
    blocks, _ = _pair_blocks()
    neg_inf = F32(-jnp.inf)
    key_rows = lax.broadcasted_iota(jnp.int32, (N_KEYS, LANES), 0).astype(F32)
    swept = lax.rem(r, 2)
    ranked = 1 - swept

    def mask_token(t):
        irow = isel_scr[swept, pl.ds(t, 1), :]
        jrow = jsel_scr[swept, pl.ds(t, 1), :]
        grow = gsel_scr[swept, pl.ds(t, 1), :]
        rw = jnp.where(key_rows == irow, 0.5 * grow, 0.0).astype(BF16)
        cw = jnp.where(key_rows == jrow, 1.0, 0.0).astype(BF16)
        r0 = pl.multiple_of(t * MASK_PITCH, 8)
        mask_scr[pl.ds(r0, N_KEYS), :] = _dot_nt(rw, cw)

    def rank_pairs(hd):
        sv0, sv1 = sv_scr[2 * hd], sv_scr[2 * hd + 1]
        si0, si1 = si_scr[2 * hd], si_scr[2 * hd + 1]
        flat = flat_ref[...]
        cand = _pair_candidates(sv0, sv1, blocks, lambda x, y: x + y)
        cand = jnp.where(flat >= 0.0, cand, neg_inf)
        for kk in range(PEER_TOPK):
            m = jnp.max(cand, axis=0, keepdims=True)
            pos = jnp.min(jnp.where(cand == m, flat, F32(PEER_TOPK * PEER_TOPK)), axis=0, keepdims=True)
            ts_scr[kk:kk + 1, :] = m
            te_scr[kk:kk + 1, :] = pos
            cand = jnp.where(flat == pos, neg_inf, cand)
        top = ts_scr[...]
        ex = jnp.exp(top - top[0:1, :])
        gate = ex / jnp.sum(ex, axis=0, keepdims=True)
        pos = te_scr[...]
        rank_a = jnp.floor(pos * F32(1.0 / PEER_TOPK))
        rank_b = pos - rank_a * F32(PEER_TOPK)
        irow = jnp.zeros_like(pos)
        jrow = jnp.zeros_like(pos)
        for c in range(PEER_TOPK):
            irow = jnp.where(rank_a == F32(c), si0[c:c + 1, :], irow)
            jrow = jnp.where(rank_b == F32(c), si1[c:c + 1, :], jrow)
        r0 = pl.multiple_of(hd * PEER_TOPK, PEER_TOPK)
        it_scr[pl.ds(r0, PEER_TOPK), :] = irow
        jt_scr[pl.ds(r0, PEER_TOPK), :] = jrow
        gt_scr[pl.ds(r0, PEER_TOPK), :] = gate

    def publish_ranked():
        isel_scr[ranked] = it_scr[...].T
        jsel_scr[ranked] = jt_scr[...].T
        gsel_scr[ranked] = gt_scr[...].T

    @pl.when(e == 0)
    def _():
        out_ref[...] = x_ref[...]

        @pl.when(r == 1)
        def _():
            def head(hd, _):
                rank_pairs(hd)
                return 0

            lax.fori_loop(0, PEER_HEADS, head, 0)
            publish_ranked()

        @pl.when(jnp.logical_and(r >= 2, r <= n))
        def _():
            def trip(i, _):
                for t in range(tokens_per_head):
                    mask_token(i * tokens_per_head + t)
                rank_pairs(i)
                return 0

            lax.fori_loop(0, PEER_HEADS, trip, 0)
            publish_ranked()

        @pl.when(r == n + 1)
        def _():
            def trip(i, _):
                for t in range(MASK_UNROLL):
                    mask_token(i * MASK_UNROLL + t)
                return 0

            lax.fori_loop(0, tm // MASK_UNROLL, trip, 0)

    def slot_scores():
        d0 = pl.multiple_of(e * PEER_HALF, PEER_HALF)
        return _dot_nt(q_ref[:, pl.ds(d0, PEER_HALF)], keys_ref[e])

    def rank_slot(scores):
        lane = lax.broadcasted_iota(jnp.int32, (tm, N_KEYS), 1)
        picked = jnp.zeros((tm, N_KEYS), jnp.int32)
        a = scores
        for kk in range(PEER_TOPK):
            idx = jnp.argmax(a, axis=-1, keepdims=True).astype(jnp.int32)
            a = jnp.where(lane == idx, neg_inf, a)
            picked = jnp.where(lane == kk, idx, picked)
        vals = jnp.take_along_axis(scores, picked, axis=1)
        sv_scr[e] = vals.T[0:PEER_TOPK, :]
        si_scr[e] = (N_KEYS - 1 - picked).astype(F32).T[0:PEER_TOPK, :]

    def expert_chunk():
        h = h_ref[...]
        row0 = e * rows_per_step
        weighted = []
        for c in range(te // PEER_CHUNK):
            rows = slice(c * PEER_CHUNK, (c + 1) * PEER_CHUNK)
            pre = _dot_nt(h, u_ref[rows, :])
            half_mask = jnp.concatenate(
                [mask_scr[pl.ds(row0 + c * (PEER_CHUNK // N_KEYS) + rr, tm, stride=MASK_PITCH), :]
                 for rr in range(PEER_CHUNK // N_KEYS)], axis=1)
            weighted.append((pre * (1.0 + lax.erf(pre * np.float32(np.sqrt(0.5)))) * half_mask).astype(BF16))
        out_ref[...] += _dot(jnp.concatenate(weighted, axis=1), v_ref[...])

    ranks = r < n
    sweeps = r >= 2

    @pl.when(jnp.logical_and(ranks, jnp.logical_not(sweeps)))
    def _():
        rank_slot(slot_scores())

    @pl.when(jnp.logical_and(ranks, sweeps))
    def _():
        a = slot_scores()
        expert_chunk()
        rank_slot(a)

    @pl.when(jnp.logical_and(jnp.logical_not(ranks), sweeps))
    def _():
        expert_chunk()


def _peer(h, x2, q_route, w, tm, te):
    t, d = h.shape
    n = t // tm
    nexp = w['peer_v'].shape[0]
    assert nexp // te == 2 * PEER_HEADS, "one half-key slot is ranked per expert step"
    assert tm % PEER_HEADS == 0 and tm % MASK_UNROLL == 0
    nsel = PEER_HEADS * PEER_TOPK
    _, flat = _pair_blocks()
    flat = jnp.asarray(np.repeat(flat[:, None], tm, axis=1))
    swept_tile = lambda r, e: (jnp.clip(r - 2, 0, n - 1), 0)
    dense_tile = lambda width: pl.BlockSpec((tm, width), swept_tile, pipeline_mode=pl.Buffered(1))
    expert_tile = pl.BlockSpec((te, d), lambda r, e: (jnp.where(r >= 2, e, 0), 0))
    return pl.pallas_call(
        _peer_kernel,
        grid=(n + 2, nexp // te),
        in_specs=[dense_tile(d), dense_tile(d),
                  pl.BlockSpec((tm, q_route.shape[1]), lambda r, e: (jnp.minimum(r, n - 1), 0),
                               pipeline_mode=pl.Buffered(1)),
                  _const_spec(w['sub_keys'].shape), _const_spec(flat.shape), expert_tile, expert_tile],
        out_specs=pl.BlockSpec((tm, d), swept_tile),
        out_shape=jax.ShapeDtypeStruct((t, d), F32),
        scratch_shapes=[pltpu.VMEM((tm * MASK_PITCH, LANES), F32),
                        pltpu.VMEM((2 * PEER_HEADS, PEER_TOPK, tm), F32),
                        pltpu.VMEM((2 * PEER_HEADS, PEER_TOPK, tm), F32),
                        pltpu.VMEM((PEER_TOPK, tm), F32),
                        pltpu.VMEM((PEER_TOPK, tm), F32),
                        pltpu.VMEM((nsel, tm), F32),
                        pltpu.VMEM((nsel, tm), F32),
                        pltpu.VMEM((nsel, tm), F32),
                        pltpu.VMEM((2, tm, nsel), F32),
                        pltpu.VMEM((2, tm, nsel), F32),
                        pltpu.VMEM((2, tm, nsel), F32)],
        compiler_params=_params(("arbitrary", "arbitrary"), PEER_VMEM_LIMIT),
        name="peer",
    )(h, x2, q_route, w['sub_keys'], flat, w['peer_u'], w['peer_v'])


def _slot_gain(parts):
    row = jnp.zeros((SLOT,), F32)
    for off, g in parts:
        row = lax.dynamic_update_slice(row, g.astype(F32), (off,))
    return row[None, :]


def _group_avg_matrix():
    m = np.zeros((SLOT, SLOT), np.float32)
    m[:QK_NOPE, :QK_NOPE] = 1.0 / QK_NOPE
    m[ROPE_LO:ROPE_LO + QK_ROPE, ROPE_LO:ROPE_LO + QK_ROPE] = 1.0 / QK_ROPE
    return jnp.asarray(np.concatenate([m, m], axis=0), BF16)


def _rope_tables(pos):
    inv = ROPE_THETA ** (-jnp.arange(ROPE_HALF, dtype=F32) / ROPE_HALF)
    ang = pos.astype(F32)[:, None] * inv[None, :]
    cos, sin = jnp.cos(ang), jnp.sin(ang)
    n = pos.shape[0]
    tail = SLOT - ROPE_LO - QK_ROPE
    cos_t = jnp.concatenate([jnp.ones((n, ROPE_LO), F32), cos, cos, jnp.zeros((n, tail), F32)], axis=1)
    sin_up = jnp.concatenate([jnp.zeros((n, ROPE_LO + ROPE_HALF), F32), sin, jnp.zeros((n, tail), F32)], axis=1)
    sin_dn = jnp.concatenate([jnp.zeros((n, ROPE_LO), F32), -sin, jnp.zeros((n, tail + ROPE_HALF), F32)], axis=1)
    return cos_t, sin_up, sin_dn


def _prep_weights(p, i):
    w_in = p['w_in'][i]
    q_lora = p['g_q_lat'].shape[1]
    kv_lora = p['g_kv_lat'].shape[1]
    gw = p['g_gm'].shape[1]
    d = w_in.shape[0]
    o = np.cumsum([0, q_lora, kv_lora, QK_ROPE, gw, gw, d, d])
    seg = lambda k: w_in[:, o[k]:o[k + 1]]
    row = lambda a: a[i][None, :].astype(F32)
    w = {}
    w['g_mix'] = row(p['g_mix'])
    w['w_cq'] = seg(0).astype(BF16)
    w['w_ckv'] = seg(1).astype(BF16)
    w['w_kr'] = jnp.pad(seg(2), ((0, 0), (ROPE_LO, SLOT - ROPE_LO - QK_ROPE))).astype(BF16)
    w['w_u'] = seg(3).astype(BF16)
    w['w_v'] = seg(4).astype(BF16)
    w['w_ga'] = seg(5).astype(BF16)
    w['w_gb'] = seg(6).astype(BF16)
    w['g_q_lat'] = row(p['g_q_lat'])
    w['g_kv_lat'] = row(p['g_kv_lat'])
    w['g_kr'] = _slot_gain([(ROPE_LO, p['g_kr'][i])])
    w['g_gm'] = row(p['g_gm'])
    head_dim = QK_NOPE + QK_ROPE
    w_uq = p['w_uq'][i].reshape(q_lora, MLA_HEADS, head_dim)
    w['w_uq'] = jnp.pad(w_uq, ((0, 0), (0, 0), (0, SLOT - head_dim))).reshape(q_lora, -1).astype(BF16)
    w['g_q'] = _slot_gain([(0, p['g_qn'][i]), (ROPE_LO, p['g_qr'][i])])
    w['m_avg'] = _group_avg_matrix()
    w_uk = p['w_uk'][i].reshape(kv_lora, MLA_HEADS, QK_NOPE)
    w['w_uk'] = jnp.pad(w_uk, ((0, 0), (0, 0), (0, SLOT - QK_NOPE))).reshape(kv_lora, -1).astype(BF16)
    w['g_kn'] = _slot_gain([(0, p['g_kn'][i])])
    w_uv = p['w_uv'][i].reshape(kv_lora, MLA_HEADS // 2, 2, V_HEAD)
    eye2 = jnp.eye(2, dtype=w_uv.dtype)
    w['w_uv'] = jnp.einsum('cpjd,jk->cpjkd', w_uv, eye2).reshape(kv_lora, -1).astype(BF16)
    w['w_uv_plain'] = p['w_uv'][i].astype(BF16)
    place = np.zeros((QK_ROPE, SLOT), np.float32)
    place[np.arange(QK_ROPE), ROPE_LO + np.arange(QK_ROPE)] = 1.0
    w['rope_place'] = jnp.asarray(place, BF16)
    w['w_oa'] = p['w_oa'][i].astype(BF16)
    w['w_ob'] = p['w_ob'][i].astype(BF16)
    w['w_o'] = p['w_o'][i].astype(BF16)
    w['g_xattn'] = row(p['g_xattn'])
    w['g_mem'] = row(p['g_mem'])
    w['w_cq_mem'] = p['w_cq'][i].astype(BF16)
    w['g_cq'] = row(p['g_cq'])
    w['w_ck'] = p['w_ck'][i].astype(BF16)
    w['g_ck'] = row(p['g_ck'])
    w['w_cv'] = p['w_cv'][i].astype(BF16)
    w['w_co'] = p['w_co'][i].astype(BF16)
    w['g_ffn'] = row(p['g_ffn'])
    w['w_pq'] = p['w_pq'][i].astype(BF16)
    w['sub_keys'] = p['sub_keys'][i].reshape(2 * PEER_HEADS, N_KEYS, PEER_HALF)[:, ::-1, :].astype(BF16)
    w['peer_u'] = p['peer_u'][i].astype(BF16)
    w['peer_v'] = p['peer_v'][i].astype(BF16)
    w['w_s'] = p['w_s'][i]
    w['b_s'] = p['b_s'][i]
    return w


def _spatial_operands(w, seq, tm):
    chunk = min(seq, GM_CHUNK)
    reps = GM_CHUNK // chunk
    w_mask = jnp.tril(w['w_s'][:, :chunk, :chunk])
    w_mask = w_mask.astype(BF16)
    mix = jnp.concatenate(
        [jnp.pad(w_mask, ((0, 0), (0, 0), (a * chunk, (reps - 1 - a) * chunk))) for a in range(reps)], axis=1)
    bias = jnp.tile(w['b_s'][:, :chunk].T, (tm // chunk, 1))
    bias = jnp.repeat(bias, LANES, axis=1)
    return mix.astype(BF16), bias.astype(F32)


def _token_mixer(x2d, w, seq, pos, past, tm):
    t, d = x2d.shape
    nb = t // seq
    period = max(seq, tm)
    tabs = _rope_tables(jnp.tile(pos, period // seq))
    cq, ckv, kr, u, vg, siga, sigb = _in_proj(x2d, w, tabs, tm, BF16 if past is None else F32)
    mix, bias = _spatial_operands(w, seq, tm)
    gbob = _gmlp(u, vg, sigb, mix, bias, w['w_ob'], tm)
    wide = min(WIDE_ROW_TILE, t)
    q = _q_proj(cq, w, tabs, wide)
    if past is None:
        k, v = _kv_proj(ckv, kr, w, wide)
        o = _attn_prompt(q.reshape(nb, seq, -1), k.reshape(nb, seq, -1), v.reshape(nb, seq, -1),
                         _score_bound(w), ATTN_TILE)
    else:
        past_ckv, past_kr = past
        o = _attn_sample(q.reshape(nb, seq, -1), past_ckv, past_kr, ckv.reshape(nb, seq, -1),
                         kr.reshape(nb, seq, SLOT), w)
    x1, qc = _merge(x2d, o.reshape(t, -1), siga, gbob, w, wide)
    return x1, qc, ckv, kr[:, ROPE_LO:ROPE_LO + QK_ROPE], vg


def _tail(x1, qc, mk, mv, w, nb, seq, tm):
    t, d = x1.shape
    oc = _cross(qc.reshape(nb, seq, -1), mk, mv, min(seq, WIDE_ROW_TILE))
    x2, h, q_route = _cross_out(x1, oc.reshape(t, -1), w, min(WIDE_ROW_TILE, t))
    return _peer(h, x2, q_route, w, PEER_ROWS, PEER_EXPERTS)


def kernel(x_prompt, x_sample, cache_mla_ckv, cache_mla_krope, cache_mem_k, cache_mem_v, mem_prompt, g_mix, w_in, g_q_lat, w_uq, g_qn, g_qr, g_kv_lat, g_kr, w_uk, w_uv, g_kn, w_oa, g_gm, w_s, b_s, w_ob, w_o, g_xattn, g_mem, w_cq, g_cq, w_ck, g_ck, w_cv, w_co, g_ffn, w_pq, sub_keys, peer_u, peer_v):
    params = dict(g_mix=g_mix, w_in=w_in, g_q_lat=g_q_lat, w_uq=w_uq, g_qn=g_qn, g_qr=g_qr,
                  g_kv_lat=g_kv_lat, g_kr=g_kr, w_uk=w_uk, w_uv=w_uv, g_kn=g_kn, w_oa=w_oa, g_gm=g_gm,
                  w_s=w_s, b_s=b_s, w_ob=w_ob, w_o=w_o, g_xattn=g_xattn, g_mem=g_mem, w_cq=w_cq,
                  g_cq=g_cq, w_ck=w_ck, g_ck=g_ck, w_cv=w_cv, w_co=w_co, g_ffn=g_ffn, w_pq=w_pq,
                  sub_keys=sub_keys, peer_u=peer_u, peer_v=peer_v)
    bp, sp, d = x_prompt.shape
    bs, ss, _ = x_sample.shape
    depth = w_in.shape[0]
    past_len = cache_mla_ckv.shape[2]
    n_mem = mem_prompt.shape[1]
    tm = ROW_TILE
    assert (bp * sp) % tm == 0 and (bs * ss) % tm == 0 and sp % ATTN_TILE == 0
    pos_p = jnp.arange(sp)
    pos_s = past_len + jnp.arange(ss)
    xp = x_prompt.reshape(bp * sp, d)
    xs = x_sample.reshape(bs * ss, d)
    outs = [[] for _ in range(7)]
    for i in range(depth):
        w = _prep_weights(params, i)
        xp1, qcp, ckv_p, kr_p, _ = _token_mixer(xp, w, sp, pos_p, None, tm)
        xs1, qcs, ckv_s, kr_s, vg_s = _token_mixer(xs, w, ss, pos_s, (cache_mla_ckv[i], cache_mla_krope[i]), tm)
        mk_p, mv_p = _mem_kv(mem_prompt.reshape(bp * n_mem, d), w, min(tm, bp * n_mem))
        xp = _tail(xp1, qcp, mk_p.reshape(bp, n_mem, -1), mv_p.reshape(bp, n_mem, -1), w, bp, sp, tm)
        xs = _tail(xs1, qcs, cache_mem_k[i].reshape(bs, n_mem, -1), cache_mem_v[i].reshape(bs, n_mem, -1),
                   w, bs, ss, tm)
        outs[0].append(ckv_p.reshape(bp, sp, -1))
        outs[1].append(kr_p.reshape(bp, sp, -1))
        outs[2].append(mk_p.reshape(bp, n_mem, MEM_HEADS, MEM_HEAD_DIM))
        outs[3].append(mv_p.reshape(bp, n_mem, MEM_HEADS, MEM_HEAD_DIM))
        outs[4].append(ckv_s.reshape(bs, ss, -1))
        outs[5].append(kr_s.reshape(bs, ss, -1))
        outs[6].append(vg_s.reshape(bs, ss, -1))
    return (xp.reshape(bp, sp, d), xs.reshape(bs, ss, d)) + tuple(jnp.stack(o) for o in outs)
```
